```python
import jax, jax.numpy as jnp
from jax import lax
import numpy as np

D_MODEL = 1024
BATCH = 1
SEQ = 16384
DEPTH = 1
DEC_BATCH = 16
DEC_SEQ = 16
PAST_LEN = 1024

CHUNK = 64
RWKV_DIM = 512
HEAD_SIZE = 64
RWKV_HEADS = RWKV_DIM // HEAD_SIZE
CONV_DIM = 512
CONV_WIDTH = 31
DECAY_LORA = 64
AAA_LORA = 64
GATE_LORA = 128
PLE_DIM = 256
N_EXPERT_GROUPS = 4
EXPERTS_PER_GROUP = 8
N_EXPERTS = N_EXPERT_GROUPS * EXPERTS_PER_GROUP
TOP_K = 2
EXPERT_FF = 512
MOE_BLOCK = 128
RMS_EPS = 1e-6
LN_EPS = 1e-5
GN_EPS = 64e-5
IN_COLS = 3 * RWKV_DIM + 2 * CONV_DIM

kernel_name = 'hymba_rwkv7_conformer_hmoe_stream'


def rmsnorm(x, g):
    xf = x.astype(jnp.float32)
    y = xf * lax.rsqrt(jnp.mean(xf * xf, axis=-1, keepdims=True) + RMS_EPS)
    return (y * g.astype(jnp.float32)).astype(x.dtype)


def layernorm(x, w, b):
    xf = x.astype(jnp.float32)
    mu = jnp.mean(xf, axis=-1, keepdims=True)
    var = jnp.mean(jnp.square(xf - mu), axis=-1, keepdims=True)
    y = (xf - mu) * lax.rsqrt(var + LN_EPS)
    return (y * w.astype(jnp.float32) + b.astype(jnp.float32)).astype(x.dtype)


def wkv_recurrence(s0, r, w, k, v, a, b):
    def step(s, inp):
        r_t, w_t, k_t, v_t, a_t, b_t = inp
        sa = jnp.einsum('bhij,bhj->bhi', s, a_t)
        s = s * w_t[:, :, None, :] + sa[..., None] * b_t[:, :, None, :] + v_t[..., None] * k_t[:, :, None, :]
        y = jnp.einsum('bhij,bhj->bhi', s, r_t)
        return s, y
    xs = tuple(jnp.moveaxis(t, 1, 0) for t in (r, w, k, v, a, b))
    s_final, ys = lax.scan(step, s0, xs)
    return jnp.moveaxis(ys, 0, 1), s_final


def rwkv7_mix(xn, h_rkv, shift_prev, wkv_prev, lp):
    B, T, _ = xn.shape
    H, N = RWKV_HEADS, HEAD_SIZE
    f32 = lambda t: t.astype(jnp.float32)
    heads = lambda t: t.reshape(B, T, H, N)
    xprev = jnp.concatenate([shift_prev[:, None, :].astype(xn.dtype), xn[:, :-1]], axis=1)
    dx = xprev - xn
    h_prev0 = shift_prev.astype(xn.dtype) @ lp['w_in'][:, :3 * RWKV_DIM]
    hprev = jnp.concatenate([h_prev0[:, None, :], h_rkv[:, :-1]], axis=1)
    rkv = h_rkv + (hprev - h_rkv) * lp['mu_rkv']
    r, k, v = jnp.split(f32(rkv), 3, axis=-1)
    xw = xn + dx * lp['mu_w']
    xa = xn + dx * lp['mu_a']
    xg = xn + dx * lp['mu_g']
    w_log = -jax.nn.softplus(-(f32(lp['w0']) + f32(jnp.tanh(xw @ lp['w1']) @ lp['w2']))) - 0.5
    decay = jnp.exp(-jnp.exp(w_log))
    a = jax.nn.sigmoid(f32(lp['a0']) + f32((xa @ lp['a1']) @ lp['a2']))
    g = f32(jax.nn.sigmoid(xg @ lp['g1']) @ lp['g2'])
    kk = heads(k * f32(lp['k_k']))
    kk = kk / jnp.maximum(jnp.sqrt(jnp.sum(kk * kk, axis=-1, keepdims=True)), 1e-12)
    k = k * (1.0 + (a - 1.0) * f32(lp['k_a']))
    rh, kh, vh = heads(r), heads(k), heads(v)
    y, s_new = wkv_recurrence(f32(wkv_prev), rh, heads(decay), kh, vh, -kk, kk * heads(a))
    mu = jnp.mean(y, axis=-1, keepdims=True)
    var = jnp.mean(jnp.square(y - mu), axis=-1, keepdims=True)
    y = (y - mu) * lax.rsqrt(var + GN_EPS) * f32(lp['ln_x_w']).reshape(H, N) + f32(lp['ln_x_b']).reshape(H, N)
    y = y + jnp.sum(rh * kh * f32(lp['r_k']), axis=-1, keepdims=True) * vh
    out = y.reshape(B, T, RWKV_DIM) * g
    return out.astype(xn.dtype), xn[:, -1], s_new.astype(wkv_prev.dtype)


def conformer_conv_mix(h_glu, conv_prev, lp):
    ga, gb = jnp.split(h_glu, 2, axis=-1)
    u = ga * jax.nn.sigmoid(gb)
    up = jnp.concatenate([conv_prev.astype(u.dtype), u], axis=1)
    z = lax.conv_general_dilated(up, lp['dw_w'][:, None, :].astype(up.dtype), window_strides=(1,), padding='VALID',
                                 dimension_numbers=('NWC', 'WIO', 'NWC'), feature_group_count=CONV_DIM)
    z = z + lp['dw_b']
    z = jax.nn.silu(layernorm(z, lp['cln_w'], lp['cln_b']))
    return z, up[:, -(CONV_WIDTH - 1):]


def hier_moe(x, lp):
    B, T, D = x.shape
    M = B * T
    xt = x.reshape(M, D)
    glog = (xt @ lp['w_router_group']).astype(jnp.float32)
    gprob = jax.nn.softmax(glog, axis=-1)
    gsel = jnp.argmax(glog, axis=-1).astype(jnp.int32)
    gp = jnp.max(gprob, axis=-1, keepdims=True)
    elog = (xt @ lp['w_router_expert']).astype(jnp.float32).reshape(M, N_EXPERT_GROUPS, EXPERTS_PER_GROUP)
    elog_sel = elog[jnp.arange(M), gsel]
    eprob = jax.nn.softmax(elog_sel, axis=-1)
    topv, topi = lax.top_k(eprob, TOP_K)
    gate = gp * topv / jnp.sum(topv, axis=-1, keepdims=True)
    eid = (gsel[:, None] * EXPERTS_PER_GROUP + topi.astype(jnp.int32)).reshape(-1)
    tok = jnp.repeat(jnp.arange(M, dtype=jnp.int32), TOP_K)
    gate_f = gate.reshape(-1)
    A = M * TOP_K
    order = jnp.argsort(eid)
    eid_s, tok_s, gate_s = eid[order], tok[order], gate_f[order]
    counts = jnp.bincount(eid, length=N_EXPERTS).astype(jnp.int32)
    start = jnp.cumsum(counts) - counts
    pcounts = (counts + MOE_BLOCK - 1) // MOE_BLOCK * MOE_BLOCK
    pend = jnp.cumsum(pcounts)
    pstart = pend - pcounts
    dest = pstart[eid_s] + jnp.arange(A, dtype=jnp.int32) - start[eid_s]
    n_blocks = (A + N_EXPERTS * (MOE_BLOCK - 1) + MOE_BLOCK - 1) // MOE_BLOCK
    P = n_blocks * MOE_BLOCK
    row_tok = jnp.full((P,), M, jnp.int32).at[dest].set(tok_s)
    row_gate = jnp.zeros((P,), jnp.float32).at[dest].set(gate_s)
    block_expert = jnp.minimum(jnp.searchsorted(pend, jnp.arange(n_blocks, dtype=jnp.int32) * MOE_BLOCK, side='right'),
                               N_EXPERTS - 1)
    x_pad = jnp.concatenate([xt, jnp.zeros((1, D), xt.dtype)], axis=0)
    xb = x_pad[row_tok].reshape(n_blocks, MOE_BLOCK, D)

    def expert_block(args):
        xblk, e = args
        hg = xblk @ lp['w_exp_gate'][e]
        hu = xblk @ lp['w_exp_up'][e]
        return (jax.nn.silu(hg) * hu) @ lp['w_exp_down'][e]

    yb = lax.map(expert_block, (xb, block_expert)).reshape(P, D)
    y = jnp.zeros((M + 1, D), yb.dtype).at[row_tok].add(yb * row_gate[:, None].astype(yb.dtype))
    return y[:M].reshape(B, T, D).astype(x.dtype)


def encoder_layer(h, p_i, shift_prev, wkv_prev, conv_prev, lp):
    xn = rmsnorm(h, lp['norm_mix'])
    hin = xn @ lp['w_in']
    y_a, shift_new, wkv_new = rwkv7_mix(xn, hin[..., :3 * RWKV_DIM], shift_prev, wkv_prev, lp)
    y_b, conv_new = conformer_conv_mix(hin[..., 3 * RWKV_DIM:], conv_prev, lp)
    h = h + jnp.concatenate([y_a, y_b], axis=-1) @ lp['w_out']
    h = h + hier_moe(rmsnorm(h, lp['norm_ffn']), lp)
    ple_gate = jax.nn.sigmoid(rmsnorm(h, lp['norm_ple']) @ lp['w_ple_gate'])
    h = h + ple_gate * (p_i.astype(h.dtype) @ lp['w_ple_proj'])
    return h, shift_new, wkv_new, conv_new


def run_trunk(x, p, shift0, wkv0, conv0, layers, norm_final):
    h = x
    shifts, wkvs, convs = [], [], []
    for i in range(DEPTH):
        h, s, w, c = encoder_layer(h, p[i], shift0[i], wkv0[i], conv0[i], layers[i])
        shifts.append(s)
        wkvs.append(w)
        convs.append(c)
    return rmsnorm(h, norm_final), jnp.stack(shifts), jnp.stack(wkvs), jnp.stack(convs)


def setup_inputs(seed: int = 0) -> dict:
    key = jax.random.key(seed)
    ks = jax.random.split(key, 48)
    f32 = jnp.float32

    def nrm(i, shape, scale):
        return jax.random.normal(ks[i], shape, f32) * scale

    def uni(i, shape, lo, hi):
        return jax.random.uniform(ks[i], shape, f32, lo, hi)

    L, D, C, Cc, H, N = DEPTH, D_MODEL, RWKV_DIM, CONV_DIM, RWKV_HEADS, HEAD_SIZE
    return {
        'x_prompt': nrm(0, (BATCH, SEQ, D), 1.0),
        'x_sample': nrm(1, (DEC_BATCH, DEC_SEQ, D), 1.0),
        'state_shift': nrm(2, (L, DEC_BATCH, D), 1.0),
        'state_wkv': nrm(3, (L, DEC_BATCH, H, N, N), 0.3),
        'cache_conv': nrm(4, (L, DEC_BATCH, CONV_WIDTH - 1, Cc), 0.5),
        'p_prompt': nrm(5, (L, BATCH, SEQ, PLE_DIM), 1.0),
        'p_sample': nrm(6, (L, DEC_BATCH, DEC_SEQ, PLE_DIM), 1.0),
        'norm_mix': 1.0 + nrm(7, (L, D), 0.01),
        'w_in': nrm(8, (L, D, IN_COLS), D ** -0.5),
        'mu_rkv': uni(9, (L, 3 * C), 0.0, 1.0),
        'mu_w': uni(10, (L, D), 0.0, 1.0),
        'mu_a': uni(11, (L, D), 0.0, 1.0),
        'mu_g': uni(12, (L, D), 0.0, 1.0),
        'w0': uni(13, (L, C), -5.0, 1.0),
        'w1': nrm(14, (L, D, DECAY_LORA), D ** -0.5),
        'w2': nrm(15, (L, DECAY_LORA, C), 0.5 * DECAY_LORA ** -0.5),
        'a0': nrm(16, (L, C), 0.3),
        'a1': nrm(17, (L, D, AAA_LORA), D ** -0.5),
        'a2': nrm(18, (L, AAA_LORA, C), 0.5 * AAA_LORA ** -0.5),
        'g1': nrm(19, (L, D, GATE_LORA), D ** -0.5),
        'g2': nrm(20, (L, GATE_LORA, C), GATE_LORA ** -0.5),
        'k_k': 0.85 + nrm(21, (L, C), 0.05),
        'k_a': 1.0 + nrm(22, (L, C), 0.05),
        'r_k': nrm(23, (L, H, N), 0.1),
        'ln_x_w': 1.0 + nrm(24, (L, C), 0.01),
        'ln_x_b': nrm(25, (L, C), 0.01),
        'dw_w': nrm(26, (L, CONV_WIDTH, Cc), CONV_WIDTH ** -0.5),
        'dw_b': nrm(27, (L, Cc), 0.01),
        'cln_w': 1.0 + nrm(28, (L, Cc), 0.01),
        'cln_b': nrm(29, (L, Cc), 0.01),
        'w_out': nrm(30, (L, C + Cc, D), (C + Cc) ** -0.5),
        'norm_ffn': 1.0 + nrm(31, (L, D), 0.01),
        'w_router_group': nrm(32, (L, D, N_EXPERT_GROUPS), D ** -0.5),
        'w_router_expert': nrm(33, (L, D, N_EXPERTS), D ** -0.5),
        'w_exp_gate': nrm(34, (L, N_EXPERTS, D, EXPERT_FF), D ** -0.5),
        'w_exp_up': nrm(35, (L, N_EXPERTS, D, EXPERT_FF), D ** -0.5),
        'w_exp_down': nrm(36, (L, N_EXPERTS, EXPERT_FF, D), EXPERT_FF ** -0.5),
        'norm_ple': 1.0 + nrm(37, (L, D), 0.01),
        'w_ple_gate': nrm(38, (L, D, D), D ** -0.5),
        'w_ple_proj': nrm(39, (L, PLE_DIM, D), PLE_DIM ** -0.5),
        'norm_final': 1.0 + nrm(40, (D,), 0.01),
    }


def reference(x_prompt, x_sample, state_shift, state_wkv, cache_conv, p_prompt, p_sample,
              norm_mix, w_in, mu_rkv, mu_w, mu_a, mu_g, w0, w1, w2, a0, a1, a2, g1, g2,
              k_k, k_a, r_k, ln_x_w, ln_x_b, dw_w, dw_b, cln_w, cln_b, w_out, norm_ffn,
              w_router_group, w_router_expert, w_exp_gate, w_exp_up, w_exp_down,
              norm_ple, w_ple_gate, w_ple_proj, norm_final):
    def layer_params(i):
        return dict(norm_mix=norm_mix[i], w_in=w_in[i], mu_rkv=mu_rkv[i], mu_w=mu_w[i], mu_a=mu_a[i], mu_g=mu_g[i],
                    w0=w0[i], w1=w1[i], w2=w2[i], a0=a0[i], a1=a1[i], a2=a2[i], g1=g1[i], g2=g2[i],
                    k_k=k_k[i], k_a=k_a[i], r_k=r_k[i], ln_x_w=ln_x_w[i], ln_x_b=ln_x_b[i],
                    dw_w=dw_w[i], dw_b=dw_b[i], cln_w=cln_w[i], cln_b=cln_b[i], w_out=w_out[i],
                    norm_ffn=norm_ffn[i], w_router_group=w_router_group[i], w_router_expert=w_router_expert[i],
                    w_exp_gate=w_exp_gate[i], w_exp_up=w_exp_up[i], w_exp_down=w_exp_down[i],
                    norm_ple=norm_ple[i], w_ple_gate=w_ple_gate[i], w_ple_proj=w_ple_proj[i])

    layers = [layer_params(i) for i in range(DEPTH)]
    b = x_prompt.shape[0]
    shift0 = jnp.zeros((DEPTH, b, D_MODEL), x_prompt.dtype)
    wkv0 = jnp.zeros((DEPTH, b, RWKV_HEADS, HEAD_SIZE, HEAD_SIZE), state_wkv.dtype)
    conv0 = jnp.zeros((DEPTH, b, CONV_WIDTH - 1, CONV_DIM), x_prompt.dtype)
    y_prompt, shift_prompt, wkv_prompt, conv_prompt = run_trunk(x_prompt, p_prompt, shift0, wkv0, conv0, layers, norm_final)
    y_sample, shift_sample, wkv_sample, conv_sample = run_trunk(x_sample, p_sample, state_shift, state_wkv, cache_conv,
                                                                layers, norm_final)
    return (y_prompt, y_sample, shift_prompt, wkv_prompt, conv_prompt, shift_sample, wkv_sample, conv_sample)
```

```python
import functools

import jax
import jax.numpy as jnp
from jax import lax
from jax.experimental import pallas as pl
from jax.experimental.pallas import tpu as pltpu

F32 = jnp.float32
BF16 = jnp.bfloat16
I32 = jnp.int32

HEAD_SIZE = 64
CONV_WIDTH = 31
CONV_CARRY = CONV_WIDTH - 1
CARRY_PAD = 32
N_EXPERT_GROUPS = 4
EXPERTS_PER_GROUP = 8
N_EXPERTS = N_EXPERT_GROUPS * EXPERTS_PER_GROUP
ROUTER_LANES = 128
RMS_EPS = 1e-6
LN_EPS = 1e-5
GN_EPS = 64e-5
INV_BASE = 16
VMEM_LIMIT = 56 * 1024 * 1024

NN = ((1,), (0,))
NT = ((1,), (1,))
TN = ((0,), (0,))


def _dg(a, b, dims=NN):
    return lax.dot_general(a, b, (dims, ((), ())), preferred_element_type=F32)


def _split2(x):
    hi = x.astype(BF16)
    lo = (x - hi.astype(F32)).astype(BF16)
    return hi, lo


def _split3(x):
    hi = x.astype(BF16)
    r1 = x - hi.astype(F32)
    mid = r1.astype(BF16)
    lo = (r1 - mid.astype(F32)).astype(BF16)
    return hi, mid, lo


def _dot3(a, b, dims=NN):
    ah, al = _split2(a)
    bh, bl = _split2(b)
    return _dg(ah, bh, dims) + (_dg(al, bh, dims) + _dg(ah, bl, dims))


def _mask_dot(mask_bf16, x):
    h, m, l = _split3(x)
    return _dg(mask_bf16, h) + (_dg(mask_bf16, m) + _dg(mask_bf16, l))


def _seg_sum(x, seg_bf16):
    h, m, l = _split3(x)
    return _dg(h, seg_bf16) + (_dg(m, seg_bf16) + _dg(l, seg_bf16))


def _rms(x, g):
    return x * lax.rsqrt(jnp.mean(x * x, axis=-1, keepdims=True) + RMS_EPS) * g


def _sigmoid(x):
    return 1.0 / (1.0 + jnp.exp(-x))


def _softplus(x):
    return jnp.maximum(x, 0.0) + jnp.log(1.0 + jnp.exp(-jnp.abs(x)))


def _full(shape):
    n = len(shape)
    return pl.BlockSpec(shape, lambda *_: (0,) * n)


def _mix_in_kernel(x_ref, shift_ref, conv_ref, nm_ref, win_ref, murkv_ref, muw_ref, mua_ref, mug_ref,
                   w0_ref, w1_ref, w2_ref, a0_ref, a1_ref, a2_ref, g1_ref, g2_ref,
                   kk_ref, ka_ref, rk_ref, seg_ref, dww_ref, dwb_ref, clnw_ref, clnb_ref,
                   r_out, lw_out, k_out, v_out, a_out, b_out, g_out, bonus_out, yb_out, shift_out, conv_out,
                   xn_last, h_last, up_ext, *, tm, c):
    i = pl.program_id(1)

    @pl.when(i == 0)
    def _():
        sp = shift_ref[0]
        xn_last[...] = sp
        sp8 = jnp.broadcast_to(sp, (8, sp.shape[1])).astype(BF16)
        h_last[...] = _dg(sp8, win_ref[:, :3 * c])[0:1]
        up_ext[CARRY_PAD - CONV_CARRY:CARRY_PAD, :] = conv_ref[0]

    x = x_ref[0]
    xn = _rms(x, nm_ref[...])
    hin = _dg(xn.astype(BF16), win_ref[...])
    first = lax.broadcasted_iota(I32, (tm, 1), 0) == 0
    xprev = jnp.where(first, xn_last[...], pltpu.roll(xn, 1, 0))
    dx = xprev - xn
    h_rkv = hin[:, :3 * c]
    hprev = jnp.where(first, h_last[...], pltpu.roll(h_rkv, 1, 0))
    rkv = h_rkv + (hprev - h_rkv) * murkv_ref[...]
    r = rkv[:, :c]
    k = rkv[:, c:2 * c]
    v = rkv[:, 2 * c:]
    xw = (xn + dx * muw_ref[...]).astype(BF16)
    xa = (xn + dx * mua_ref[...]).astype(BF16)
    xg = (xn + dx * mug_ref[...]).astype(BF16)
    zw = w0_ref[...] + _dg(jnp.tanh(_dg(xw, w1_ref[...])).astype(BF16), w2_ref[...])
    w_log = -_softplus(-zw) - 0.5
    a = _sigmoid(a0_ref[...] + _dg(_dg(xa, a1_ref[...]).astype(BF16), a2_ref[...]))
    g = _dg(_sigmoid(_dg(xg, g1_ref[...])).astype(BF16), g2_ref[...])
    seg = seg_ref[...]
    kk = k * kk_ref[...]
    kk = kk / jnp.maximum(jnp.sqrt(_seg_sum(kk * kk, seg)), 1e-12)
    k2 = k * (1.0 + (a - 1.0) * ka_ref[...])
    r_out[0] = r
    lw_out[0] = -jnp.exp(w_log)
    k_out[0] = k2
    v_out[0] = v
    a_out[0] = -kk
    b_out[0] = kk * a
    g_out[0] = g
    bonus_out[0] = _seg_sum(r * k2 * rk_ref[...], seg) * v

    u = hin[:, 3 * c:4 * c] * _sigmoid(hin[:, 4 * c:])
    up_ext[CARRY_PAD:CARRY_PAD + tm, :] = u
    z = jnp.zeros_like(u) + dwb_ref[...]
    for j in range(CONV_WIDTH):
        z = z + dww_ref[j:j + 1, :] * up_ext[pl.ds(CARRY_PAD - CONV_CARRY + j, tm), :]
    mu = jnp.mean(z, axis=-1, keepdims=True)
    zc = z - mu
    var = jnp.mean(zc * zc, axis=-1, keepdims=True)
    zn = zc * lax.rsqrt(var + LN_EPS) * clnw_ref[...] + clnb_ref[...]
    yb_out[0] = zn * _sigmoid(zn)

    tail = up_ext[pl.ds(tm + CARRY_PAD - CONV_CARRY, CONV_CARRY), :]
    up_ext[CARRY_PAD - CONV_CARRY:CARRY_PAD, :] = tail
    conv_out[0] = tail
    xn_last[...] = xn[tm - 1:tm]
    h_last[...] = h_rkv[tm - 1:tm]
    shift_out[0] = xn[tm - 1:tm]


def _mix_in(x, shift0, conv0, wd, tm):
    bsz, t, d = x.shape
    c = wd['w0'].shape[1]
    grid = (bsz, t // tm)
    tok = lambda w: pl.BlockSpec((1, tm, w), lambda b, i: (b, i, 0))
    per_seq = lambda rows, w: pl.BlockSpec((1, rows, w), lambda b, i: (b, 0, 0))
    weights = [wd[n] for n in ('norm_mix', 'w_in', 'mu_rkv', 'mu_w', 'mu_a', 'mu_g', 'w0', 'w1', 'w2', 'a0', 'a1',
                               'a2', 'g1', 'g2', 'k_k', 'k_a', 'r_k', 'seg', 'dw_w', 'dw_b', 'cln_w', 'cln_b')]
    out_tok = jax.ShapeDtypeStruct((bsz, t, c), F32)
    outs = pl.pallas_call(
        functools.partial(_mix_in_kernel, tm=tm, c=c),
        grid=grid,
        in_specs=[tok(d), per_seq(1, d), per_seq(CONV_CARRY, c)] + [_full(w.shape) for w in weights],
        out_specs=[tok(c)] * 9 + [per_seq(1, d), per_seq(CONV_CARRY, c)],
        out_shape=[out_tok] * 9 + [jax.ShapeDtypeStruct((bsz, 1, d), F32),
                                   jax.ShapeDtypeStruct((bsz, CONV_CARRY, c), F32)],
        scratch_shapes=[pltpu.VMEM((1, d), F32), pltpu.VMEM((1, 3 * c), F32),
                        pltpu.VMEM((tm + CARRY_PAD, c), F32)],
        compiler_params=pltpu.CompilerParams(dimension_semantics=("arbitrary", "arbitrary"),
                                             vmem_limit_bytes=VMEM_LIMIT),
        name="mix_in",
    )(x, shift0, conv0, *weights)
    return outs


def _tri_inverse(n_strict, row, col, eye, lg_chunk):
    lg_base = INV_BASE.bit_length() - 1
    same = lambda sh: (row >> sh) == (col >> sh)
    lg0 = min(lg_base, lg_chunk)
    p = jnp.where(same(lg0), n_strict, 0.0)
    t = jnp.where(eye, 1.0, 0.0) + p
    for _ in range(lg0 - 1):
        p = _dot3(p, p)
        t = t + _dot3(t, p)
    for lg in range(lg0, lg_chunk):
        off = jnp.where(same(lg + 1) & jnp.logical_not(same(lg)), n_strict, 0.0)
        t = t + _dot3(_dot3(t, off), t)
    return t


def _wkv_kernel(r_ref, lw_ref, k_ref, v_ref, a_ref, b_ref, s0_ref, y_ref, s_out, s_scr, *, tt, chunk, chained):
    ti = pl.program_id(2)
    n_chunks = tt // chunk
    lg_chunk = chunk.bit_length() - 1
    row = lax.broadcasted_iota(I32, (tt, tt), 0)
    col = lax.broadcasted_iota(I32, (tt, tt), 1)
    in_chunk = (row >> lg_chunk) == (col >> lg_chunk)
    eye = row == col
    tri_incl = in_chunk & (col <= row)
    tri_strict = in_chunk & (col < row)
    m_cum = jnp.where(tri_incl, 1.0, 0.0).astype(BF16)
    m_tot = jnp.where(in_chunk, 1.0, 0.0).astype(BF16)
    row_h = lax.broadcasted_iota(I32, (HEAD_SIZE, HEAD_SIZE), 0)
    col_h = lax.broadcasted_iota(I32, (HEAD_SIZE, HEAD_SIZE), 1)
    eye_h = row_h == col_h

    if chained:
        @pl.when(ti == 0)
        def _():
            s_scr[...] = s0_ref[0]

    for hh in range(2):
        sl = slice(HEAD_SIZE * hh, HEAD_SIZE * (hh + 1))
        r = r_ref[0, :, sl]
        lw = lw_ref[0, :, sl]
        k = k_ref[0, :, sl]
        v = v_ref[0, :, sl]
        ah = a_ref[0, :, sl]
        bh = b_ref[0, :, sl]
        cum = _mask_dot(m_cum, lw)
        tot = _mask_dot(m_tot, lw)
        e_neg = jnp.exp(-cum)
        rt = r * jnp.exp(cum)
        at = ah * jnp.exp(cum - lw)
        kt = k * e_neg
        bt = bh * e_neg
        e_end = jnp.exp(tot - cum)
        bd = bh * e_end
        kd = k * e_end
        g_end = jnp.exp(tot)

        m_ab = jnp.where(tri_strict, _dot3(at, bt, NT), 0.0)
        m_ak = jnp.where(tri_strict, _dot3(at, kt, NT), 0.0)
        m_rb = jnp.where(tri_incl, _dot3(rt, bt, NT), 0.0)
        m_rk = jnp.where(tri_incl, _dot3(rt, kt, NT), 0.0)
        tinv = _tri_inverse(m_ab, row, col, eye, lg_chunk)
        w1 = _dot3(tinv, at)
        w2 = _dot3(tinv, _dot3(m_ak, v))
        q = rt + _dot3(m_rb, w1)
        y0 = _dot3(m_rb, w2) + _dot3(m_rk, v)

        if chained:
            s = s_scr[hh]
        for ci in range(n_chunks):
            cs = slice(ci * chunk, (ci + 1) * chunk)
            if not chained:
                s = s0_ref[ci, hh]
            gdiag = jnp.where(eye_h, jnp.broadcast_to(g_end[ci * chunk:ci * chunk + 1], (HEAD_SIZE, HEAD_SIZE)), 0.0)
            gm = gdiag + _dot3(bd[cs], w1[cs], TN)
            hm = _dot3(bd[cs], w2[cs], TN) + _dot3(kd[cs], v[cs], TN)
            y_ref[0, cs, sl] = _dot3(q[cs], s) + y0[cs]
            s = _dot3(gm, s) + hm
            if not chained:
                s_out[ci, hh] = s
        if chained:
            s_scr[hh] = s
            s_out[0, hh] = s


def _wkv(r, lw, k, v, a, b, s0t, tt, chunk, chained):
    bsz, t, c = r.shape
    n_pairs = c // (2 * HEAD_SIZE)
    hs = HEAD_SIZE
    if chained:
        grid = (bsz, n_pairs, t // tt)
        tok = pl.BlockSpec((1, tt, 2 * hs), lambda bi, p, ti: (bi, ti, p))
        st = pl.BlockSpec((1, 2, hs, hs), lambda bi, p, ti: (bi, p, 0, 0))
        args = (r, lw, k, v, a, b)
        y_shape = (bsz, t, c)
    else:
        assert t == chunk and (bsz * t) % tt == 0
        n_seq = tt // chunk
        grid = (1, n_pairs, bsz * t // tt)
        tok = pl.BlockSpec((1, tt, 2 * hs), lambda bi, p, ti: (0, ti, p))
        st = pl.BlockSpec((n_seq, 2, hs, hs), lambda bi, p, ti: (ti, p, 0, 0))
        args = tuple(z.reshape(1, bsz * t, c) for z in (r, lw, k, v, a, b))
        y_shape = (1, bsz * t, c)
    y, s_new = pl.pallas_call(
        functools.partial(_wkv_kernel, tt=tt, chunk=chunk, chained=chained),
        grid=grid,
        in_specs=[tok] * 6 + [st],
        out_specs=[tok, st],
        out_shape=[jax.ShapeDtypeStruct(y_shape, F32), jax.ShapeDtypeStruct(s0t.shape, F32)],
        scratch_shapes=[pltpu.VMEM((2, hs, hs), F32)],
        compiler_params=pltpu.CompilerParams(dimension_semantics=("arbitrary",) * 3,
                                             vmem_limit_bytes=VMEM_LIMIT),
        name="wkv",
    )(*args, s0t)
    return y.reshape(bsz, t, c), s_new


def _mix_out_kernel(x_ref, y_ref, bonus_ref, g_ref, yb_ref, lnw_ref, lnb_ref, seg_ref, wout_ref, nffn_ref, wr_ref,
                    h1_out, xn_out, eid_out, gate_out, *, c):
    seg = seg_ref[...]
    y = y_ref[...]
    inv_n = 1.0 / HEAD_SIZE
    mu = _seg_sum(y, seg) * inv_n
    yc = y - mu
    var = _seg_sum(yc * yc, seg) * inv_n
    yn = yc * lax.rsqrt(var + GN_EPS) * lnw_ref[...] + lnb_ref[...]
    ya = (yn + bonus_ref[...]) * g_ref[...]
    mix = _dg(ya.astype(BF16), wout_ref[:c, :]) + _dg(yb_ref[...].astype(BF16), wout_ref[c:, :])
    h1 = x_ref[...] + mix
    h1_out[...] = h1
    xn = _rms(h1, nffn_ref[...])
    xn_out[...] = xn

    logits = _dot3(xn, wr_ref[...])
    lane = lax.broadcasted_iota(I32, logits.shape, 1)
    neg = jnp.float32(-jnp.inf)
    is_g = (lane >= N_EXPERTS) & (lane < N_EXPERTS + N_EXPERT_GROUPS)
    glog = jnp.where(is_g, logits, neg)
    gmax = jnp.max(glog, axis=-1, keepdims=True)
    gsel = jnp.min(jnp.where(glog == gmax, lane, 4 * ROUTER_LANES), axis=-1, keepdims=True) - N_EXPERTS
    gp = 1.0 / jnp.sum(jnp.where(is_g, jnp.exp(glog - gmax), 0.0), axis=-1, keepdims=True)
    in_grp = (lane >= gsel * EXPERTS_PER_GROUP) & (lane < (gsel + 1) * EXPERTS_PER_GROUP)
    elog = jnp.where(in_grp, logits, neg)
    emax = jnp.max(elog, axis=-1, keepdims=True)
    ex = jnp.where(in_grp, jnp.exp(elog - emax), 0.0)
    eprob = ex / jnp.sum(ex, axis=-1, keepdims=True)
    eprob = jnp.where(in_grp, eprob, -1.0)
    v1 = jnp.max(eprob, axis=-1, keepdims=True)
    i1 = jnp.min(jnp.where(eprob == v1, lane, 4 * ROUTER_LANES), axis=-1, keepdims=True)
    rest = jnp.where(lane == i1, -1.0, eprob)
    v2 = jnp.max(rest, axis=-1, keepdims=True)
    i2 = jnp.min(jnp.where(rest == v2, lane, 4 * ROUTER_LANES), axis=-1, keepdims=True)
    denom = v1 + v2
    eid_out[...] = jnp.where(lane == 0, i1, jnp.where(lane == 1, i2, 0))
    gate_out[...] = jnp.where(lane == 0, gp * v1 / denom, jnp.where(lane == 1, gp * v2 / denom, 0.0))


def _mix_out(x2, y2, bonus2, g2, yb2, wd, tm):
    m, d = x2.shape
    c = y2.shape[1]
    tokd = pl.BlockSpec((tm, d), lambda i: (i, 0))
    tokc = pl.BlockSpec((tm, c), lambda i: (i, 0))
    tokr = pl.BlockSpec((tm, ROUTER_LANES), lambda i: (i, 0))
    weights = [wd[n] for n in ('ln_x_w', 'ln_x_b', 'seg', 'w_out', 'norm_ffn', 'w_router')]
    return pl.pallas_call(
        functools.partial(_mix_out_kernel, c=c),
        grid=(m // tm,),
        in_specs=[tokd, tokc, tokc, tokc, tokc] + [_full(w.shape) for w in weights],
        out_specs=[tokd, tokd, tokr, tokr],
        out_shape=[jax.ShapeDtypeStruct((m, d), F32), jax.ShapeDtypeStruct((m, d), F32),
                   jax.ShapeDtypeStruct((m, ROUTER_LANES), I32), jax.ShapeDtypeStruct((m, ROUTER_LANES), F32)],
        compiler_params=pltpu.CompilerParams(dimension_semantics=("arbitrary",), vmem_limit_bytes=VMEM_LIMIT),
        name="mix_out",
    )(x2, y2, bonus2, g2, yb2, *weights)


def _dispatch_kernel(pos_ref, xn_ref, xs_in, xs_out, sem, *, tm):
    del xs_in
    base = pl.program_id(0) * tm * 2

    def row_copy(j):
        return pltpu.make_async_copy(xn_ref.at[pl.ds(j // 2, 1)], xs_out.at[pl.ds(pos_ref[base + j], 1)], sem)

    def start(j, carry):
        row_copy(j).start()
        return carry

    def wait(j, carry):
        row_copy(j).wait()
        return carry

    lax.fori_loop(0, 2 * tm, start, 0)
    lax.fori_loop(0, 2 * tm, wait, 0)


def _dispatch(pos, xn2, xs, tm):
    m, d = xn2.shape
    return pl.pallas_call(
        functools.partial(_dispatch_kernel, tm=tm),
        grid_spec=pltpu.PrefetchScalarGridSpec(
            num_scalar_prefetch=1,
            grid=(m // tm,),
            in_specs=[pl.BlockSpec((tm, d), lambda i, pos: (i, 0)), pl.BlockSpec(memory_space=pl.ANY)],
            out_specs=pl.BlockSpec(memory_space=pl.ANY),
            scratch_shapes=[pltpu.SemaphoreType.DMA],
        ),
        out_shape=jax.ShapeDtypeStruct(xs.shape, xs.dtype),
        input_output_aliases={2: 0},
        compiler_params=pltpu.CompilerParams(dimension_semantics=("arbitrary",), vmem_limit_bytes=VMEM_LIMIT),
        name="moe_dispatch",
    )(pos, xn2, xs)


def _experts_kernel(be_ref, nu_ref, xs_ref, wg_ref, wu_ref, wd_ref, yb_ref):
    @pl.when(pl.program_id(0) < nu_ref[0])
    def _():
        xb = xs_ref[...].astype(BF16)
        hg = _dg(xb, wg_ref[0])
        hu = _dg(xb, wu_ref[0])
        act = (hg * _sigmoid(hg) * hu).astype(BF16)
        yb_ref[...] = _dg(act, wd_ref[0])

    @pl.when(pl.program_id(0) >= nu_ref[0])
    def _():
        yb_ref[...] = jnp.zeros_like(yb_ref)


def _experts(block_expert, n_used, xs, wg, wu, wdn, bm):
    p, d = xs.shape
    ff = wg.shape[2]
    n_blocks = p // bm
    rows = lambda b, be, nu: (jnp.minimum(b, nu[0] - 1), 0)
    return pl.pallas_call(
        _experts_kernel,
        grid_spec=pltpu.PrefetchScalarGridSpec(
            num_scalar_prefetch=2,
            grid=(n_blocks,),
            in_specs=[pl.BlockSpec((bm, d), rows),
                      pl.BlockSpec((1, d, ff), lambda b, be, nu: (be[b], 0, 0)),
                      pl.BlockSpec((1, d, ff), lambda b, be, nu: (be[b], 0, 0)),
                      pl.BlockSpec((1, ff, d), lambda b, be, nu: (be[b], 0, 0))],
            out_specs=pl.BlockSpec((bm, d), lambda b, be, nu: (b, 0)),
        ),
        out_shape=jax.ShapeDtypeStruct((p, d), F32),
        compiler_params=pltpu.CompilerParams(dimension_semantics=("arbitrary",), vmem_limit_bytes=VMEM_LIMIT),
        name="moe_experts",
    )(block_expert, n_used, xs, wg, wu, wdn)


def _final_kernel(pos_ref, h1_ref, gate_ref, p_ref, yb_hbm, nple_ref, wpg_ref, wpp_ref, nfin_ref, y_out,
                  rows, sem, *, tm):
    base = pl.program_id(0) * tm * 2

    def row_copy(j):
        return pltpu.make_async_copy(yb_hbm.at[pl.ds(pos_ref[base + j], 1)], rows.at[j % 2, pl.ds(j // 2, 1)], sem)

    def start(j, carry):
        row_copy(j).start()
        return carry

    def wait(j, carry):
        row_copy(j).wait()
        return carry

    lax.fori_loop(0, 2 * tm, start, 0)
    lax.fori_loop(0, 2 * tm, wait, 0)
    gate = gate_ref[...]
    h2 = h1_ref[...] + (gate[:, 0:1] * rows[0] + gate[:, 1:2] * rows[1])
    pg = _sigmoid(_dg(_rms(h2, nple_ref[...]).astype(BF16), wpg_ref[...]))
    h3 = h2 + pg * _dg(p_ref[...].astype(BF16), wpp_ref[...])
    y_out[...] = _rms(h3, nfin_ref[...])


def _final(pos, h1, gate, p2, yb, wd, tm):
    m, d = h1.shape
    pd = p2.shape[1]
    weights = [wd[n] for n in ('norm_ple', 'w_ple_gate', 'w_ple_proj', 'norm_final')]
    return pl.pallas_call(
        functools.partial(_final_kernel, tm=tm),
        grid_spec=pltpu.PrefetchScalarGridSpec(
            num_scalar_prefetch=1,
            grid=(m // tm,),
            in_specs=[pl.BlockSpec((tm, d), lambda i, pos: (i, 0)),
                      pl.BlockSpec((tm, ROUTER_LANES), lambda i, pos: (i, 0)),
                      pl.BlockSpec((tm, pd), lambda i, pos: (i, 0)),
                      pl.BlockSpec(memory_space=pl.ANY)] +
                     [pl.BlockSpec(w.shape, lambda i, pos, n=len(w.shape): (0,) * n) for w in weights],
            out_specs=pl.BlockSpec((tm, d), lambda i, pos: (i, 0)),
            scratch_shapes=[pltpu.VMEM((2, tm, d), F32), pltpu.SemaphoreType.DMA],
        ),
        out_shape=jax.ShapeDtypeStruct((m, d), F32),
        compiler_params=pltpu.CompilerParams(dimension_semantics=("arbitrary",), vmem_limit_bytes=VMEM_LIMIT),
        name="moe_final",
    )(pos, h1, gate, p2, yb, *weights)


def _route_positions(eids, bm, n_blocks):
    eid_f = jnp.concatenate([e.reshape(-1) for e in eids])
    onehot = (eid_f[:, None] == jnp.arange(N_EXPERTS, dtype=I32)[None, :]).astype(I32)
    csum = jnp.cumsum(onehot, axis=0)
    rank = jnp.sum(csum * onehot, axis=1) - 1
    counts = csum[-1]
    pcounts = (counts + bm - 1) // bm * bm
    pend = jnp.cumsum(pcounts)
    pstart = pend - pcounts
    pos = (jnp.sum(pstart[None, :] * onehot, axis=1) + rank).astype(I32)
    block_start = jnp.arange(n_blocks, dtype=I32) * bm
    block_expert = jnp.minimum(jnp.sum((pend[None, :] <= block_start[:, None]).astype(I32), axis=1), N_EXPERTS - 1)
    n_used = (pend[-1] // bm).astype(I32).reshape(1)
    return pos, block_expert.astype(I32), n_used


def _layer_front(x, shift0, wkv0, conv0, wd, tm_in, tm_tok, wkv_tile, wkv_chunk, chained):
    bsz, t, d = x.shape
    r, lw, k2, v, ah, bh, g, bonus, yb, shift_new, conv_new = _mix_in(x, shift0, conv0, wd, tm_in)
    y, s_new = _wkv(r, lw, k2, v, ah, bh, jnp.swapaxes(wkv0, -1, -2), wkv_tile, wkv_chunk, chained)
    flat = lambda z: z.reshape(bsz * t, z.shape[-1])
    h1, xn2, eid, gate = _mix_out(flat(x), flat(y), flat(bonus), flat(g), flat(yb), wd, tm_tok)
    return h1, xn2, eid[:, :2], gate, shift_new.reshape(bsz, d), jnp.swapaxes(s_new, -1, -2), conv_new


def kernel(x_prompt, x_sample, state_shift, state_wkv, cache_conv, p_prompt, p_sample, norm_mix, w_in, mu_rkv, mu_w, mu_a, mu_g, w0, w1, w2, a0, a1, a2, g1, g2, k_k, k_a, r_k, ln_x_w, ln_x_b, dw_w, dw_b, cln_w, cln_b, w_out, norm_ffn, w_router_group, w_router_expert, w_exp_gate, w_exp_up, w_exp_down, norm_ple, w_ple_gate, w_ple_proj, norm_final):
    depth = norm_mix.shape[0]
    assert depth == 1
    d = x_prompt.shape[-1]
    c = w0.shape[-1]
    row = lambda z: z[0].reshape(1, -1).astype(F32)
    lane = jnp.arange(c, dtype=I32) // HEAD_SIZE
    wd = dict(
        norm_mix=row(norm_mix), w_in=w_in[0].astype(BF16), mu_rkv=row(mu_rkv), mu_w=row(mu_w), mu_a=row(mu_a),
        mu_g=row(mu_g), w0=row(w0), w1=w1[0].astype(BF16), w2=w2[0].astype(BF16), a0=row(a0),
        a1=a1[0].astype(BF16), a2=a2[0].astype(BF16), g1=g1[0].astype(BF16), g2=g2[0].astype(BF16),
        k_k=row(k_k), k_a=row(k_a), r_k=row(r_k), ln_x_w=row(ln_x_w), ln_x_b=row(ln_x_b),
        seg=(lane[:, None] == lane[None, :]).astype(BF16),
        dw_w=dw_w[0].astype(F32), dw_b=row(dw_b), cln_w=row(cln_w), cln_b=row(cln_b),
        w_out=w_out[0].astype(BF16), norm_ffn=row(norm_ffn),
        w_router=jnp.concatenate([w_router_expert[0], w_router_group[0],
                                  jnp.zeros((d, ROUTER_LANES - N_EXPERTS - N_EXPERT_GROUPS), F32)], axis=1),
        norm_ple=row(norm_ple), w_ple_gate=w_ple_gate[0].astype(BF16), w_ple_proj=w_ple_proj[0].astype(BF16),
        norm_final=norm_final.reshape(1, -1).astype(F32),
    )
    bp, tp, _ = x_prompt.shape
    bs, ts, _ = x_sample.shape
    mp, ms = bp * tp, bs * ts
    tm_p = min(256, tp)
    tm_s = min(256, ms)

    zeros = lambda *s: jnp.zeros(s, F32)
    h1_p, xn_p, eid_p, gate_p, shift_p, wkv_p, conv_p = _layer_front(
        x_prompt, zeros(bp, 1, d), zeros(bp, c // HEAD_SIZE, HEAD_SIZE, HEAD_SIZE), zeros(bp, CONV_CARRY, c),
        wd, tm_p, tm_p, min(128, tp), min(64, tp), True)
    h1_s, xn_s, eid_s, gate_s, shift_s, wkv_s, conv_s = _layer_front(
        x_sample, state_shift[0][:, None, :], state_wkv[0], cache_conv[0],
        wd, ts, tm_s, min(128, ms), ts, False)

    bm = 256
    n_assign = 2 * (mp + ms)
    n_blocks = (n_assign + N_EXPERTS * (bm - 1) + bm - 1) // bm
    pos, block_expert, n_used = _route_positions([eid_p, eid_s], bm, n_blocks)
    pos_p, pos_s = pos[:2 * mp], pos[2 * mp:]
    xs = jnp.zeros((n_blocks * bm, d), F32)
    xs = _dispatch(pos_p, xn_p, xs, tm_p)
    xs = _dispatch(pos_s, xn_s, xs, tm_s)
    yb = _experts(block_expert, n_used, xs, w_exp_gate[0].astype(BF16), w_exp_up[0].astype(BF16),
                  w_exp_down[0].astype(BF16), bm)
    y_p = _final(pos_p, h1_p, gate_p, p_prompt[0].reshape(mp, -1), yb, wd, tm_p)
    y_s = _final(pos_s, h1_s, gate_s, p_sample[0].reshape(ms, -1), yb, wd, tm_s)
    return (y_p.reshape(x_prompt.shape), y_s.reshape(x_sample.shape), shift_p[None], wkv_p[None], conv_p[None],
            shift_s[None], wkv_s[None], conv_s[None])
```

```python
import functools

import jax
import jax.numpy as jnp
from jax import lax
from jax.experimental import pallas as pl
from jax.experimental.pallas import tpu as pltpu

F32 = jnp.float32
BF16 = jnp.bfloat16
I32 = jnp.int32

HEAD_SIZE = 64
CONV_WIDTH = 31
CONV_CARRY = CONV_WIDTH - 1
CARRY_PAD = 32
N_EXPERT_GROUPS = 4
EXPERTS_PER_GROUP = 8
N_EXPERTS = N_EXPERT_GROUPS * EXPERTS_PER_GROUP
ROUTER_LANES = 128
RMS_EPS = 1e-6
LN_EPS = 1e-5
GN_EPS = 64e-5
INV_BASE = 16
VMEM_LIMIT = 56 * 1024 * 1024

NN = ((1,), (0,))
NT = ((1,), (1,))
TN = ((0,), (0,))


def _dg(a, b, dims=NN):
    return lax.dot_general(a, b, (dims, ((), ())), preferred_element_type=F32)


def _split2(x):
    hi = x.astype(BF16)
    lo = (x - hi.astype(F32)).astype(BF16)
    return hi, lo


def _split3(x):
    hi = x.astype(BF16)
    r1 = x - hi.astype(F32)
    mid = r1.astype(BF16)
    lo = (r1 - mid.astype(F32)).astype(BF16)
    return hi, mid, lo


def _dot3(a, b, dims=NN):
    ah, al = _split2(a)
    bh, bl = _split2(b)
    return _dg(ah, bh, dims) + (_dg(al, bh, dims) + _dg(ah, bl, dims))


def _bdot(a, b, dims=NN):
    return _dg(a.astype(BF16), b.astype(BF16), dims)


def _mask_dot(mask_bf16, x):
    h, m, l = _split3(x)
    return _dg(mask_bf16, h) + (_dg(mask_bf16, m) + _dg(mask_bf16, l))


def _seg_sum(x, seg_bf16):
    h, m, l = _split3(x)
    return _dg(h, seg_bf16) + (_dg(m, seg_bf16) + _dg(l, seg_bf16))


def _rms(x, g):
    return x * lax.rsqrt(jnp.mean(x * x, axis=-1, keepdims=True) + RMS_EPS) * g


def _sigmoid(x):
    return 1.0 / (1.0 + jnp.exp(-x))


def _softplus(x):
    return jnp.maximum(x, 0.0) + jnp.log(1.0 + jnp.exp(-jnp.abs(x)))


def _full(shape):
    n = len(shape)
    return pl.BlockSpec(shape, lambda *_: (0,) * n)


def _mix_in_kernel(x_ref, shift_ref, conv_ref, nm_ref, win_ref, murkv_ref, muw_ref, mua_ref, mug_ref,
                   w0_ref, w1_ref, w2_ref, a0_ref, a1_ref, a2_ref, g1_ref, g2_ref,
                   kk_ref, ka_ref, rk_ref, seg_ref, dww_ref, dwb_ref, clnw_ref, clnb_ref,
                   r_out, lw_out, k_out, v_out, a_out, b_out, g_out, bonus_out, yb_out, shift_out, conv_out,
                   xn_last, h_last, up_ext, *, tm, c):
    i = pl.program_id(1)

    @pl.when(i == 0)
    def _():
        sp = shift_ref[0]
        xn_last[...] = sp
        sp8 = jnp.broadcast_to(sp, (8, sp.shape[1])).astype(BF16)
        h_last[...] = _dg(sp8, win_ref[:, :3 * c])[0:1]
        up_ext[CARRY_PAD - CONV_CARRY:CARRY_PAD, :] = conv_ref[0]

    x = x_ref[0]
    xn = _rms(x, nm_ref[...])
    hin = _dg(xn.astype(BF16), win_ref[...])
    first = lax.broadcasted_iota(I32, (tm, 1), 0) == 0
    xprev = jnp.where(first, xn_last[...], pltpu.roll(xn, 1, 0))
    dx = xprev - xn
    h_rkv = hin[:, :3 * c]
    hprev = jnp.where(first, h_last[...], pltpu.roll(h_rkv, 1, 0))
    rkv = h_rkv + (hprev - h_rkv) * murkv_ref[...]
    r = rkv[:, :c]
    k = rkv[:, c:2 * c]
    v = rkv[:, 2 * c:]
    xw = (xn + dx * muw_ref[...]).astype(BF16)
    xa = (xn + dx * mua_ref[...]).astype(BF16)
    xg = (xn + dx * mug_ref[...]).astype(BF16)
    zw = w0_ref[...] + _dg(jnp.tanh(_dg(xw, w1_ref[...])).astype(BF16), w2_ref[...])
    w_log = -_softplus(-zw) - 0.5
    a = _sigmoid(a0_ref[...] + _dg(_dg(xa, a1_ref[...]).astype(BF16), a2_ref[...]))
    g = _dg(_sigmoid(_dg(xg, g1_ref[...])).astype(BF16), g2_ref[...])
    seg = seg_ref[...]
    kk = k * kk_ref[...]
    kk = kk / jnp.maximum(jnp.sqrt(_seg_sum(kk * kk, seg)), 1e-12)
    k2 = k * (1.0 + (a - 1.0) * ka_ref[...])
    r_out[0] = r
    lw_out[0] = -jnp.exp(w_log)
    k_out[0] = k2
    v_out[0] = v
    a_out[0] = -kk
    b_out[0] = kk * a
    g_out[0] = g
    bonus_out[0] = _seg_sum(r * k2 * rk_ref[...], seg) * v

    u = hin[:, 3 * c:4 * c] * _sigmoid(hin[:, 4 * c:])
    up_ext[CARRY_PAD:CARRY_PAD + tm, :] = u
    z = jnp.zeros_like(u) + dwb_ref[...]
    for j in range(CONV_WIDTH):
        z = z + dww_ref[j:j + 1, :] * up_ext[pl.ds(CARRY_PAD - CONV_CARRY + j, tm), :]
    mu = jnp.mean(z, axis=-1, keepdims=True)
    zc = z - mu
    var = jnp.mean(zc * zc, axis=-1, keepdims=True)
    zn = zc * lax.rsqrt(var + LN_EPS) * clnw_ref[...] + clnb_ref[...]
    yb_out[0] = zn * _sigmoid(zn)

    tail = up_ext[pl.ds(tm + CARRY_PAD - CONV_CARRY, CONV_CARRY), :]
    up_ext[CARRY_PAD - CONV_CARRY:CARRY_PAD, :] = tail
    conv_out[0] = tail
    xn_last[...] = xn[tm - 1:tm]
    h_last[...] = h_rkv[tm - 1:tm]
    shift_out[0] = xn[tm - 1:tm]


def _mix_in(x, shift0, conv0, wd, tm):
    bsz, t, d = x.shape
    c = wd['w0'].shape[1]
    grid = (bsz, t // tm)
    tok = lambda w: pl.BlockSpec((1, tm, w), lambda b, i: (b, i, 0))
    per_seq = lambda rows, w: pl.BlockSpec((1, rows, w), lambda b, i: (b, 0, 0))
    weights = [wd[n] for n in ('norm_mix', 'w_in', 'mu_rkv', 'mu_w', 'mu_a', 'mu_g', 'w0', 'w1', 'w2', 'a0', 'a1',
                               'a2', 'g1', 'g2', 'k_k', 'k_a', 'r_k', 'seg', 'dw_w', 'dw_b', 'cln_w', 'cln_b')]
    out_tok = jax.ShapeDtypeStruct((bsz, t, c), F32)
    outs = pl.pallas_call(
        functools.partial(_mix_in_kernel, tm=tm, c=c),
        grid=grid,
        in_specs=[tok(d), per_seq(1, d), per_seq(CONV_CARRY, c)] + [_full(w.shape) for w in weights],
        out_specs=[tok(c)] * 9 + [per_seq(1, d), per_seq(CONV_CARRY, c)],
        out_shape=[out_tok] * 9 + [jax.ShapeDtypeStruct((bsz, 1, d), F32),
                                   jax.ShapeDtypeStruct((bsz, CONV_CARRY, c), F32)],
        scratch_shapes=[pltpu.VMEM((1, d), F32), pltpu.VMEM((1, 3 * c), F32),
                        pltpu.VMEM((tm + CARRY_PAD, c), F32)],
        compiler_params=pltpu.CompilerParams(dimension_semantics=("arbitrary", "arbitrary"),
                                             vmem_limit_bytes=VMEM_LIMIT),
        name="mix_in",
    )(x, shift0, conv0, *weights)
    return outs


def _tri_inverse(n_strict, row, col, eye, lg_chunk):
    lg_base = INV_BASE.bit_length() - 1
    same = lambda sh: (row >> sh) == (col >> sh)
    lg0 = min(lg_base, lg_chunk)
    p = [jnp.where(same(lg0), n, 0.0) for n in n_strict]
    t = [jnp.where(eye, 1.0, 0.0) + x for x in p]
    for _ in range(lg0 - 1):
        p = [_bdot(x, x) for x in p]
        t = [x + _bdot(x, y) for x, y in zip(t, p)]
    for lg in range(lg0, lg_chunk):
        off_mask = same(lg + 1) & jnp.logical_not(same(lg))
        u = [_bdot(x, jnp.where(off_mask, n, 0.0)) for x, n in zip(t, n_strict)]
        t = [x + _bdot(y, x) for x, y in zip(t, u)]
    return t


def _wkv_kernel(r_ref, lw_ref, k_ref, v_ref, a_ref, b_ref, s0_ref, y_ref, s_out, s_scr, *, tt, chunk, chained):
    ti = pl.program_id(1)
    n_heads = r_ref.shape[2] // HEAD_SIZE
    n_chunks = tt // chunk
    lg_chunk = chunk.bit_length() - 1
    row = lax.broadcasted_iota(I32, (tt, tt), 0)
    col = lax.broadcasted_iota(I32, (tt, tt), 1)
    in_chunk = (row >> lg_chunk) == (col >> lg_chunk)
    eye = row == col
    tri_incl = in_chunk & (col <= row)
    tri_strict = in_chunk & (col < row)
    m_cum = jnp.where(tri_incl, 1.0, 0.0).astype(BF16)
    m_tot = jnp.where(in_chunk, 1.0, 0.0).astype(BF16)
    row_h = lax.broadcasted_iota(I32, (HEAD_SIZE, HEAD_SIZE), 0)
    col_h = lax.broadcasted_iota(I32, (HEAD_SIZE, HEAD_SIZE), 1)
    eye_h = row_h == col_h

    if chained:
        @pl.when(ti == 0)
        def _():
            s_scr[...] = s0_ref[0]

    lw_all = lw_ref[0]
    k_all = k_ref[0]
    b_all = b_ref[0]
    cum = _mask_dot(m_cum, lw_all)
    tot = _mask_dot(m_tot, lw_all)
    e_neg = jnp.exp(-cum)
    e_end = jnp.exp(tot - cum)
    rt_all = r_ref[0] * jnp.exp(cum)
    at_all = a_ref[0] * jnp.exp(cum - lw_all)
    kt_all = k_all * e_neg
    bt_all = b_all * e_neg
    bd_all = b_all * e_end
    kd_all = k_all * e_end
    g_end_all = jnp.exp(tot)

    heads = range(n_heads)
    hsl = [slice(HEAD_SIZE * h, HEAD_SIZE * (h + 1)) for h in heads]
    cut = lambda z: [z[:, s_].astype(BF16) for s_ in hsl]
    v, rt, at, kt, bt, bd, kd = (cut(z) for z in (v_ref[0], rt_all, at_all, kt_all, bt_all, bd_all, kd_all))
    m_ab = [jnp.where(tri_strict, _dg(at[h], bt[h], NT), 0.0) for h in heads]
    m_ak = [jnp.where(tri_strict, _dg(at[h], kt[h], NT), 0.0).astype(BF16) for h in heads]
    m_rb = [jnp.where(tri_incl, _dg(rt[h], bt[h], NT), 0.0).astype(BF16) for h in heads]
    m_rk = [jnp.where(tri_incl, _dg(rt[h], kt[h], NT), 0.0).astype(BF16) for h in heads]
    tinv = [x.astype(BF16) for x in _tri_inverse(m_ab, row, col, eye, lg_chunk)]
    akv = [_dg(m_ak[h], v[h]).astype(BF16) for h in heads]
    w1 = [_dg(tinv[h], at[h]).astype(BF16) for h in heads]
    w2 = [_dg(tinv[h], akv[h]).astype(BF16) for h in heads]
    q = [(rt_all[:, hsl[h]] + _dg(m_rb[h], w1[h])).astype(BF16) for h in heads]
    y0 = [_dg(m_rb[h], w2[h]) + _dg(m_rk[h], v[h]) for h in heads]

    if chained:
        s = [s_scr[h] for h in heads]
    for ci in range(n_chunks):
        cs = slice(ci * chunk, (ci + 1) * chunk)
        if not chained:
            s = [s0_ref[ci, h] for h in heads]
        g_row = g_end_all[ci * chunk:ci * chunk + 1]
        gm = [jnp.where(eye_h, jnp.broadcast_to(g_row[:, hsl[h]], (HEAD_SIZE, HEAD_SIZE)), 0.0)
              + _dg(bd[h][cs], w1[h][cs], TN) for h in heads]
        hm = [_dg(bd[h][cs], w2[h][cs], TN) + _dg(kd[h][cs], v[h][cs], TN) for h in heads]
        for h in heads:
            y_ref[0, cs, hsl[h]] = _bdot(q[h][cs], s[h]) + y0[h][cs]
        s = [_dot3(gm[h], s[h]) + hm[h] for h in heads]
        if not chained:
            for h in heads:
                s_out[ci, h] = s[h]
    if chained:
        for h in heads:
            s_scr[h] = s[h]
            s_out[0, h] = s[h]


def _wkv(r, lw, k, v, a, b, s0t, tt, chunk, chained):
    bsz, t, c = r.shape
    n_heads = c // HEAD_SIZE
    hs = HEAD_SIZE
    if chained:
        grid = (bsz, t // tt)
        tok = pl.BlockSpec((1, tt, c), lambda bi, ti: (bi, ti, 0))
        st = pl.BlockSpec((1, n_heads, hs, hs), lambda bi, ti: (bi, 0, 0, 0))
        args = (r, lw, k, v, a, b)
        y_shape = (bsz, t, c)
    else:
        assert t == chunk and (bsz * t) % tt == 0
        n_seq = tt // chunk
        grid = (1, bsz * t // tt)
        tok = pl.BlockSpec((1, tt, c), lambda bi, ti: (0, ti, 0))
        st = pl.BlockSpec((n_seq, n_heads, hs, hs), lambda bi, ti: (ti, 0, 0, 0))
        args = tuple(z.reshape(1, bsz * t, c) for z in (r, lw, k, v, a, b))
        y_shape = (1, bsz * t, c)
    y, s_new = pl.pallas_call(
        functools.partial(_wkv_kernel, tt=tt, chunk=chunk, chained=chained),
        grid=grid,
        in_specs=[tok] * 6 + [st],
        out_specs=[tok, st],
        out_shape=[jax.ShapeDtypeStruct(y_shape, F32), jax.ShapeDtypeStruct(s0t.shape, F32)],
        scratch_shapes=[pltpu.VMEM((n_heads, hs, hs), F32)],
        compiler_params=pltpu.CompilerParams(dimension_semantics=("arbitrary",) * 2,
                                             vmem_limit_bytes=VMEM_LIMIT),
        name="wkv",
    )(*args, s0t)
    return y.reshape(bsz, t, c), s_new


def _mix_out_kernel(x_ref, y_ref, bonus_ref, g_ref, yb_ref, lnw_ref, lnb_ref, seg_ref, wout_ref, nffn_ref, wr_ref,
                    h1_out, xn_out, eid_out, gate_out, *, c):
    seg = seg_ref[...]
    y = y_ref[...]
    inv_n = 1.0 / HEAD_SIZE
    mu = _seg_sum(y, seg) * inv_n
    yc = y - mu
    var = _seg_sum(yc * yc, seg) * inv_n
    yn = yc * lax.rsqrt(var + GN_EPS) * lnw_ref[...] + lnb_ref[...]
    ya = (yn + bonus_ref[...]) * g_ref[...]
    mix = _dg(ya.astype(BF16), wout_ref[:c, :]) + _dg(yb_ref[...].astype(BF16), wout_ref[c:, :])
    h1 = x_ref[...] + mix
    h1_out[...] = h1
    xn = _rms(h1, nffn_ref[...])
    xn_out[...] = xn

    logits = _dot3(xn, wr_ref[...])
    lane = lax.broadcasted_iota(I32, logits.shape, 1)
    neg = jnp.float32(-jnp.inf)
    is_g = (lane >= N_EXPERTS) & (lane < N_EXPERTS + N_EXPERT_GROUPS)
    glog = jnp.where(is_g, logits, neg)
    gmax = jnp.max(glog, axis=-1, keepdims=True)
    gsel = jnp.min(jnp.where(glog == gmax, lane, 4 * ROUTER_LANES), axis=-1, keepdims=True) - N_EXPERTS
    gp = 1.0 / jnp.sum(jnp.where(is_g, jnp.exp(glog - gmax), 0.0), axis=-1, keepdims=True)
    in_grp = (lane >= gsel * EXPERTS_PER_GROUP) & (lane < (gsel + 1) * EXPERTS_PER_GROUP)
    elog = jnp.where(in_grp, logits, neg)
    emax = jnp.max(elog, axis=-1, keepdims=True)
    ex = jnp.where(in_grp, jnp.exp(elog - emax), 0.0)
    eprob = ex / jnp.sum(ex, axis=-1, keepdims=True)
    eprob = jnp.where(in_grp, eprob, -1.0)
    v1 = jnp.max(eprob, axis=-1, keepdims=True)
    i1 = jnp.min(jnp.where(eprob == v1, lane, 4 * ROUTER_LANES), axis=-1, keepdims=True)
    rest = jnp.where(lane == i1, -1.0, eprob)
    v2 = jnp.max(rest, axis=-1, keepdims=True)
    i2 = jnp.min(jnp.where(rest == v2, lane, 4 * ROUTER_LANES), axis=-1, keepdims=True)
    denom = v1 + v2
    eid_out[...] = jnp.where(lane == 0, i1, jnp.where(lane == 1, i2, 0))
    gate_out[...] = jnp.where(lane == 0, gp * v1 / denom, jnp.where(lane == 1, gp * v2 / denom, 0.0))


def _mix_out(x2, y2, bonus2, g2, yb2, wd, tm):
    m, d = x2.shape
    c = y2.shape[1]
    tokd = pl.BlockSpec((tm, d), lambda i: (i, 0))
    tokc = pl.BlockSpec((tm, c), lambda i: (i, 0))
    tokr = pl.BlockSpec((tm, ROUTER_LANES), lambda i: (i, 0))
    weights = [wd[n] for n in ('ln_x_w', 'ln_x_b', 'seg', 'w_out', 'norm_ffn', 'w_router')]
    return pl.pallas_call(
        functools.partial(_mix_out_kernel, c=c),
        grid=(m // tm,),
        in_specs=[tokd, tokc, tokc, tokc, tokc] + [_full(w.shape) for w in weights],
        out_specs=[tokd, tokd, tokr, tokr],
        out_shape=[jax.ShapeDtypeStruct((m, d), F32), jax.ShapeDtypeStruct((m, d), F32),
                   jax.ShapeDtypeStruct((m, ROUTER_LANES), I32), jax.ShapeDtypeStruct((m, ROUTER_LANES), F32)],
        compiler_params=pltpu.CompilerParams(dimension_semantics=("arbitrary",), vmem_limit_bytes=VMEM_LIMIT),
        name="mix_out",
    )(x2, y2, bonus2, g2, yb2, *weights)


def _dispatch_kernel(pos_ref, xn_ref, xs_in, xs_out, sem, *, tm):
    del xs_in
    base = pl.program_id(0) * tm * 2

    def row_copy(j):
        return pltpu.make_async_copy(xn_ref.at[pl.ds(j // 2, 1)], xs_out.at[pl.ds(pos_ref[base + j], 1)], sem)

    def start(j, carry):
        row_copy(j).start()
        return carry

    def wait(j, carry):
        row_copy(j).wait()
        return carry

    lax.fori_loop(0, 2 * tm, start, 0)
    lax.fori_loop(0, 2 * tm, wait, 0)


def _dispatch(pos, xn2, xs, tm):
    m, d = xn2.shape
    return pl.pallas_call(
        functools.partial(_dispatch_kernel, tm=tm),
        grid_spec=pltpu.PrefetchScalarGridSpec(
            num_scalar_prefetch=1,
            grid=(m // tm,),
            in_specs=[pl.BlockSpec((tm, d), lambda i, pos: (i, 0)), pl.BlockSpec(memory_space=pl.ANY)],
            out_specs=pl.BlockSpec(memory_space=pl.ANY),
            scratch_shapes=[pltpu.SemaphoreType.DMA],
        ),
        out_shape=jax.ShapeDtypeStruct(xs.shape, xs.dtype),
        input_output_aliases={2: 0},
        compiler_params=pltpu.CompilerParams(dimension_semantics=("arbitrary",), vmem_limit_bytes=VMEM_LIMIT),
        name="moe_dispatch",
    )(pos, xn2, xs)


def _experts_kernel(be_ref, nu_ref, xs_ref, wg_ref, wu_ref, wd_ref, yb_ref):
    @pl.when(pl.program_id(0) < nu_ref[0])
    def _():
        xb = xs_ref[...].astype(BF16)
        hg = _dg(xb, wg_ref[0])
        hu = _dg(xb, wu_ref[0])
        act = (hg * _sigmoid(hg) * hu).astype(BF16)
        yb_ref[...] = _dg(act, wd_ref[0])

    @pl.when(pl.program_id(0) >= nu_ref[0])
    def _():
        yb_ref[...] = jnp.zeros_like(yb_ref)


def _experts(block_expert, n_used, xs, wg, wu, wdn, bm):
    p, d = xs.shape
    ff = wg.shape[2]
    n_blocks = p // bm
    rows = lambda b, be, nu: (jnp.minimum(b, nu[0] - 1), 0)
    return pl.pallas_call(
        _experts_kernel,
        grid_spec=pltpu.PrefetchScalarGridSpec(
            num_scalar_prefetch=2,
            grid=(n_blocks,),
            in_specs=[pl.BlockSpec((bm, d), rows),
                      pl.BlockSpec((1, d, ff), lambda b, be, nu: (be[b], 0, 0)),
                      pl.BlockSpec((1, d, ff), lambda b, be, nu: (be[b], 0, 0)),
                      pl.BlockSpec((1, ff, d), lambda b, be, nu: (be[b], 0, 0))],
            out_specs=pl.BlockSpec((bm, d), lambda b, be, nu: (b, 0)),
        ),
        out_shape=jax.ShapeDtypeStruct((p, d), F32),
        compiler_params=pltpu.CompilerParams(dimension_semantics=("arbitrary",), vmem_limit_bytes=VMEM_LIMIT),
        name="moe_experts",
    )(block_expert, n_used, xs, wg, wu, wdn)


def _final_kernel(pos_ref, h1_ref, gate_ref, p_ref, yb_hbm, nple_ref, wpg_ref, wpp_ref, nfin_ref, y_out,
                  rows, sem, *, tm):
    base = pl.program_id(0) * tm * 2

    def row_copy(j):
        return pltpu.make_async_copy(yb_hbm.at[pl.ds(pos_ref[base + j], 1)], rows.at[j % 2, pl.ds(j // 2, 1)], sem)

    def start(j, carry):
        row_copy(j).start()
        return carry

    def wait(j, carry):
        row_copy(j).wait()
        return carry

    lax.fori_loop(0, 2 * tm, start, 0)
    lax.fori_loop(0, 2 * tm, wait, 0)
    gate = gate_ref[...]
    h2 = h1_ref[...] + (gate[:, 0:1] * rows[0] + gate[:, 1:2] * rows[1])
    pg = _sigmoid(_dg(_rms(h2, nple_ref[...]).astype(BF16), wpg_ref[...]))
    h3 = h2 + pg * _dg(p_ref[...].astype(BF16), wpp_ref[...])
    y_out[...] = _rms(h3, nfin_ref[...])


def _final(pos, h1, gate, p2, yb, wd, tm):
    m, d = h1.shape
    pd = p2.shape[1]
    weights = [wd[n] for n in ('norm_ple', 'w_ple_gate', 'w_ple_proj', 'norm_final')]
    return pl.pallas_call(
        functools.partial(_final_kernel, tm=tm),
        grid_spec=pltpu.PrefetchScalarGridSpec(
            num_scalar_prefetch=1,
            grid=(m // tm,),
            in_specs=[pl.BlockSpec((tm, d), lambda i, pos: (i, 0)),
                      pl.BlockSpec((tm, ROUTER_LANES), lambda i, pos: (i, 0)),
                      pl.BlockSpec((tm, pd), lambda i, pos: (i, 0)),
                      pl.BlockSpec(memory_space=pl.ANY)] +
                     [pl.BlockSpec(w.shape, lambda i, pos, n=len(w.shape): (0,) * n) for w in weights],
            out_specs=pl.BlockSpec((tm, d), lambda i, pos: (i, 0)),
            scratch_shapes=[pltpu.VMEM((2, tm, d), F32), pltpu.SemaphoreType.DMA],
        ),
        out_shape=jax.ShapeDtypeStruct((m, d), F32),
        compiler_params=pltpu.CompilerParams(dimension_semantics=("arbitrary",), vmem_limit_bytes=VMEM_LIMIT),
        name="moe_final",
    )(pos, h1, gate, p2, yb, *weights)


def _route_positions(eids, bm, n_blocks):
    eid_f = jnp.concatenate([e.reshape(-1) for e in eids])
    onehot = (eid_f[:, None] == jnp.arange(N_EXPERTS, dtype=I32)[None, :]).astype(I32)
    csum = jnp.cumsum(onehot, axis=0)
    rank = jnp.sum(csum * onehot, axis=1) - 1
    counts = csum[-1]
    pcounts = (counts + bm - 1) // bm * bm
    pend = jnp.cumsum(pcounts)
    pstart = pend - pcounts
    pos = (jnp.sum(pstart[None, :] * onehot, axis=1) + rank).astype(I32)
    block_start = jnp.arange(n_blocks, dtype=I32) * bm
    block_expert = jnp.minimum(jnp.sum((pend[None, :] <= block_start[:, None]).astype(I32), axis=1), N_EXPERTS - 1)
    n_used = (pend[-1] // bm).astype(I32).reshape(1)
    return pos, block_expert.astype(I32), n_used


def _layer_front(x, shift0, wkv0, conv0, wd, tm_in, tm_tok, wkv_tile, wkv_chunk, chained):
    bsz, t, d = x.shape
    r, lw, k2, v, ah, bh, g, bonus, yb, shift_new, conv_new = _mix_in(x, shift0, conv0, wd, tm_in)
    y, s_new = _wkv(r, lw, k2, v, ah, bh, jnp.swapaxes(wkv0, -1, -2), wkv_tile, wkv_chunk, chained)
    flat = lambda z: z.reshape(bsz * t, z.shape[-1])
    h1, xn2, eid, gate = _mix_out(flat(x), flat(y), flat(bonus), flat(g), flat(yb), wd, tm_tok)
    return h1, xn2, eid[:, :2], gate, shift_new.reshape(bsz, d), jnp.swapaxes(s_new, -1, -2), conv_new


def kernel(x_prompt, x_sample, state_shift, state_wkv, cache_conv, p_prompt, p_sample, norm_mix, w_in, mu_rkv, mu_w, mu_a, mu_g, w0, w1, w2, a0, a1, a2, g1, g2, k_k, k_a, r_k, ln_x_w, ln_x_b, dw_w, dw_b, cln_w, cln_b, w_out, norm_ffn, w_router_group, w_router_expert, w_exp_gate, w_exp_up, w_exp_down, norm_ple, w_ple_gate, w_ple_proj, norm_final):
    depth = norm_mix.shape[0]
    assert depth == 1
    d = x_prompt.shape[-1]
    c = w0.shape[-1]
    row = lambda z: z[0].reshape(1, -1).astype(F32)
    lane = jnp.arange(c, dtype=I32) // HEAD_SIZE
    wd = dict(
        norm_mix=row(norm_mix), w_in=w_in[0].astype(BF16), mu_rkv=row(mu_rkv), mu_w=row(mu_w), mu_a=row(mu_a),
        mu_g=row(mu_g), w0=row(w0), w1=w1[0].astype(BF16), w2=w2[0].astype(BF16), a0=row(a0),
        a1=a1[0].astype(BF16), a2=a2[0].astype(BF16), g1=g1[0].astype(BF16), g2=g2[0].astype(BF16),
        k_k=row(k_k), k_a=row(k_a), r_k=row(r_k), ln_x_w=row(ln_x_w), ln_x_b=row(ln_x_b),
        seg=(lane[:, None] == lane[None, :]).astype(BF16),
        dw_w=dw_w[0].astype(F32), dw_b=row(dw_b), cln_w=row(cln_w), cln_b=row(cln_b),
        w_out=w_out[0].astype(BF16), norm_ffn=row(norm_ffn),
        w_router=jnp.concatenate([w_router_expert[0], w_router_group[0],
                                  jnp.zeros((d, ROUTER_LANES - N_EXPERTS - N_EXPERT_GROUPS), F32)], axis=1),
        norm_ple=row(norm_ple), w_ple_gate=w_ple_gate[0].astype(BF16), w_ple_proj=w_ple_proj[0].astype(BF16),
        norm_final=norm_final.reshape(1, -1).astype(F32),
    )
    bp, tp, _ = x_prompt.shape
    bs, ts, _ = x_sample.shape
    mp, ms = bp * tp, bs * ts
    tm_p = min(256, tp)
    tm_s = min(256, ms)

    zeros = lambda *s: jnp.zeros(s, F32)
    h1_p, xn_p, eid_p, gate_p, shift_p, wkv_p, conv_p = _layer_front(
        x_prompt, zeros(bp, 1, d), zeros(bp, c // HEAD_SIZE, HEAD_SIZE, HEAD_SIZE), zeros(bp, CONV_CARRY, c),
        wd, tm_p, tm_p, min(128, tp), min(64, tp), True)
    h1_s, xn_s, eid_s, gate_s, shift_s, wkv_s, conv_s = _layer_front(
        x_sample, state_shift[0][:, None, :], state_wkv[0], cache_conv[0],
        wd, ts, tm_s, min(128, ms), ts, False)

    bm = 256
    n_assign = 2 * (mp + ms)
    n_blocks = (n_assign + N_EXPERTS * (bm - 1) + bm - 1) // bm
    pos, block_expert, n_used = _route_positions([eid_p, eid_s], bm, n_blocks)
    pos_p, pos_s = pos[:2 * mp], pos[2 * mp:]
    xs = jnp.zeros((n_blocks * bm, d), F32)
    xs = _dispatch(pos_p, xn_p, xs, tm_p)
    xs = _dispatch(pos_s, xn_s, xs, tm_s)
    yb = _experts(block_expert, n_used, xs, w_exp_gate[0].astype(BF16), w_exp_up[0].astype(BF16),
                  w_exp_down[0].astype(BF16), bm)
    y_p = _final(pos_p, h1_p, gate_p, p_prompt[0].reshape(mp, -1), yb, wd, tm_p)
    y_s = _final(pos_s, h1_s, gate_s, p_sample[0].reshape(ms, -1), yb, wd, tm_s)
    return (y_p.reshape(x_prompt.shape), y_s.reshape(x_sample.shape), shift_p[None], wkv_p[None], conv_p[None],
            shift_s[None], wkv_s[None], conv_s[None])
```

```python
import functools

import jax
import jax.numpy as jnp
from jax import lax
from jax.experimental import pallas as pl
from jax.experimental.pallas import tpu as pltpu

F32 = jnp.float32
BF16 = jnp.bfloat16
I32 = jnp.int32

HEAD_SIZE = 64
CONV_WIDTH = 31
CONV_CARRY = CONV_WIDTH - 1
CARRY_PAD = 32
N_EXPERT_GROUPS = 4
EXPERTS_PER_GROUP = 8
N_EXPERTS = N_EXPERT_GROUPS * EXPERTS_PER_GROUP
ROUTER_LANES = 128
RMS_EPS = 1e-6
LN_EPS = 1e-5
GN_EPS = 64e-5
INV_BASE = 16
RUN_ALIGN = 8
LOCAL_PAD = N_EXPERTS * RUN_ALIGN
VMEM_LIMIT = 56 * 1024 * 1024

NN = ((1,), (0,))
NT = ((1,), (1,))
TN = ((0,), (0,))


def _dg(a, b, dims=NN):
    return lax.dot_general(a, b, (dims, ((), ())), preferred_element_type=F32)


def _split2(x):
    hi = x.astype(BF16)
    lo = (x - hi.astype(F32)).astype(BF16)
    return hi, lo


def _split3(x):
    hi = x.astype(BF16)
    r1 = x - hi.astype(F32)
    mid = r1.astype(BF16)
    lo = (r1 - mid.astype(F32)).astype(BF16)
    return hi, mid, lo


def _dot3(a, b, dims=NN):
    ah, al = _split2(a)
    bh, bl = _split2(b)
    return _dg(ah, bh, dims) + (_dg(al, bh, dims) + _dg(ah, bl, dims))


def _bdot(a, b, dims=NN):
    return _dg(a.astype(BF16), b.astype(BF16), dims)


def _mask_dot(mask_bf16, x):
    h, m, l = _split3(x)
    return _dg(mask_bf16, h) + (_dg(mask_bf16, m) + _dg(mask_bf16, l))


def _seg_sum(x, seg_bf16):
    h, m, l = _split3(x)
    return _dg(h, seg_bf16) + (_dg(m, seg_bf16) + _dg(l, seg_bf16))


def _rms(x, g):
    return x * lax.rsqrt(jnp.mean(x * x, axis=-1, keepdims=True) + RMS_EPS) * g


def _sigmoid(x):
    return 1.0 / (1.0 + jnp.exp(-x))


def _softplus(x):
    return jnp.maximum(x, 0.0) + jnp.log(1.0 + jnp.exp(-jnp.abs(x)))


def _full(shape):
    n = len(shape)
    return pl.BlockSpec(shape, lambda *_: (0,) * n)


def _mix_in_kernel(x_ref, shift_ref, conv_ref, nm_ref, win_ref, murkv_ref, muw_ref, mua_ref, mug_ref,
                   w0_ref, w1_ref, w2_ref, a0_ref, a1_ref, a2_ref, g1_ref, g2_ref,
                   kk_ref, ka_ref, rk_ref, seg_ref, dww_ref, dwb_ref, clnw_ref, clnb_ref,
                   r_out, lw_out, k_out, v_out, a_out, b_out, g_out, bonus_out, yb_out, shift_out, conv_out,
                   xn_last, h_last, up_ext, *, tm, c):
    i = pl.program_id(1)

    @pl.when(i == 0)
    def _():
        sp = shift_ref[0]
        xn_last[...] = sp
        sp8 = jnp.broadcast_to(sp, (8, sp.shape[1])).astype(BF16)
        h_last[...] = _dg(sp8, win_ref[:, :3 * c])[0:1]
        up_ext[CARRY_PAD - CONV_CARRY:CARRY_PAD, :] = conv_ref[0]

    x = x_ref[0]
    xn = _rms(x, nm_ref[...])
    hin = _dg(xn.astype(BF16), win_ref[...])
    first = lax.broadcasted_iota(I32, (tm, 1), 0) == 0
    xprev = jnp.where(first, xn_last[...], pltpu.roll(xn, 1, 0))
    dx = xprev - xn
    h_rkv = hin[:, :3 * c]
    hprev = jnp.where(first, h_last[...], pltpu.roll(h_rkv, 1, 0))
    rkv = h_rkv + (hprev - h_rkv) * murkv_ref[...]
    r = rkv[:, :c]
    k = rkv[:, c:2 * c]
    v = rkv[:, 2 * c:]
    xw = (xn + dx * muw_ref[...]).astype(BF16)
    xa = (xn + dx * mua_ref[...]).astype(BF16)
    xg = (xn + dx * mug_ref[...]).astype(BF16)
    zw = w0_ref[...] + _dg(jnp.tanh(_dg(xw, w1_ref[...])).astype(BF16), w2_ref[...])
    w_log = -_softplus(-zw) - 0.5
    a = _sigmoid(a0_ref[...] + _dg(_dg(xa, a1_ref[...]).astype(BF16), a2_ref[...]))
    g = _dg(_sigmoid(_dg(xg, g1_ref[...])).astype(BF16), g2_ref[...])
    seg = seg_ref[...]
    kk = k * kk_ref[...]
    kk = kk / jnp.maximum(jnp.sqrt(_seg_sum(kk * kk, seg)), 1e-12)
    k2 = k * (1.0 + (a - 1.0) * ka_ref[...])
    r_out[0] = r
    lw_out[0] = -jnp.exp(w_log)
    k_out[0] = k2
    v_out[0] = v
    a_out[0] = -kk
    b_out[0] = kk * a
    g_out[0] = g
    bonus_out[0] = _seg_sum(r * k2 * rk_ref[...], seg) * v

    u = hin[:, 3 * c:4 * c] * _sigmoid(hin[:, 4 * c:])
    up_ext[CARRY_PAD:CARRY_PAD + tm, :] = u
    z = jnp.zeros_like(u) + dwb_ref[...]
    for j in range(CONV_WIDTH):
        z = z + dww_ref[j:j + 1, :] * up_ext[pl.ds(CARRY_PAD - CONV_CARRY + j, tm), :]
    mu = jnp.mean(z, axis=-1, keepdims=True)
    zc = z - mu
    var = jnp.mean(zc * zc, axis=-1, keepdims=True)
    zn = zc * lax.rsqrt(var + LN_EPS) * clnw_ref[...] + clnb_ref[...]
    yb_out[0] = zn * _sigmoid(zn)

    tail = up_ext[pl.ds(tm + CARRY_PAD - CONV_CARRY, CONV_CARRY), :]
    up_ext[CARRY_PAD - CONV_CARRY:CARRY_PAD, :] = tail
    conv_out[0] = tail
    xn_last[...] = xn[tm - 1:tm]
    h_last[...] = h_rkv[tm - 1:tm]
    shift_out[0] = xn[tm - 1:tm]


def _mix_in(x, shift0, conv0, wd, tm):
    bsz, t, d = x.shape
    c = wd['w0'].shape[1]
    grid = (bsz, t // tm)
    tok = lambda w: pl.BlockSpec((1, tm, w), lambda b, i: (b, i, 0))
    per_seq = lambda rows, w: pl.BlockSpec((1, rows, w), lambda b, i: (b, 0, 0))
    weights = [wd[n] for n in ('norm_mix', 'w_in', 'mu_rkv', 'mu_w', 'mu_a', 'mu_g', 'w0', 'w1', 'w2', 'a0', 'a1',
                               'a2', 'g1', 'g2', 'k_k', 'k_a', 'r_k', 'seg', 'dw_w', 'dw_b', 'cln_w', 'cln_b')]
    out_tok = jax.ShapeDtypeStruct((bsz, t, c), F32)
    outs = pl.pallas_call(
        functools.partial(_mix_in_kernel, tm=tm, c=c),
        grid=grid,
        in_specs=[tok(d), per_seq(1, d), per_seq(CONV_CARRY, c)] + [_full(w.shape) for w in weights],
        out_specs=[tok(c)] * 9 + [per_seq(1, d), per_seq(CONV_CARRY, c)],
        out_shape=[out_tok] * 9 + [jax.ShapeDtypeStruct((bsz, 1, d), F32),
                                   jax.ShapeDtypeStruct((bsz, CONV_CARRY, c), F32)],
        scratch_shapes=[pltpu.VMEM((1, d), F32), pltpu.VMEM((1, 3 * c), F32),
                        pltpu.VMEM((tm + CARRY_PAD, c), F32)],
        compiler_params=pltpu.CompilerParams(dimension_semantics=("arbitrary", "arbitrary"),
                                             vmem_limit_bytes=VMEM_LIMIT),
        name="mix_in",
    )(x, shift0, conv0, *weights)
    return outs


def _tri_inverse(n_strict, row, col, eye, lg_chunk):
    lg_base = INV_BASE.bit_length() - 1
    same = lambda sh: (row >> sh) == (col >> sh)
    lg0 = min(lg_base, lg_chunk)
    p = [jnp.where(same(lg0), n, 0.0) for n in n_strict]
    t = [jnp.where(eye, 1.0, 0.0) + x for x in p]
    for _ in range(lg0 - 1):
        p = [_bdot(x, x) for x in p]
        t = [x + _bdot(x, y) for x, y in zip(t, p)]
    for lg in range(lg0, lg_chunk):
        off_mask = same(lg + 1) & jnp.logical_not(same(lg))
        u = [_bdot(x, jnp.where(off_mask, n, 0.0)) for x, n in zip(t, n_strict)]
        t = [x + _bdot(y, x) for x, y in zip(t, u)]
    return t


def _wkv_kernel(r_ref, lw_ref, k_ref, v_ref, a_ref, b_ref, s0_ref, y_ref, s_out, s_scr, *, tt, chunk, chained):
    ti = pl.program_id(1)
    n_heads = r_ref.shape[2] // HEAD_SIZE
    n_chunks = tt // chunk
    lg_chunk = chunk.bit_length() - 1
    row = lax.broadcasted_iota(I32, (tt, tt), 0)
    col = lax.broadcasted_iota(I32, (tt, tt), 1)
    in_chunk = (row >> lg_chunk) == (col >> lg_chunk)
    eye = row == col
    tri_incl = in_chunk & (col <= row)
    tri_strict = in_chunk & (col < row)
    m_cum = jnp.where(tri_incl, 1.0, 0.0).astype(BF16)
    m_tot = jnp.where(in_chunk, 1.0, 0.0).astype(BF16)
    row_h = lax.broadcasted_iota(I32, (HEAD_SIZE, HEAD_SIZE), 0)
    col_h = lax.broadcasted_iota(I32, (HEAD_SIZE, HEAD_SIZE), 1)
    eye_h = row_h == col_h

    if chained:
        @pl.when(ti == 0)
        def _():
            s_scr[...] = s0_ref[0]

    lw_all = lw_ref[0]
    k_all = k_ref[0]
    b_all = b_ref[0]
    cum = _mask_dot(m_cum, lw_all)
    tot = _mask_dot(m_tot, lw_all)
    e_neg = jnp.exp(-cum)
    e_end = jnp.exp(tot - cum)
    rt_all = r_ref[0] * jnp.exp(cum)
    at_all = a_ref[0] * jnp.exp(cum - lw_all)
    kt_all = k_all * e_neg
    bt_all = b_all * e_neg
    bd_all = b_all * e_end
    kd_all = k_all * e_end
    g_end_all = jnp.exp(tot)

    heads = range(n_heads)
    hsl = [slice(HEAD_SIZE * h, HEAD_SIZE * (h + 1)) for h in heads]
    cut = lambda z: [z[:, s_].astype(BF16) for s_ in hsl]
    v, rt, at, kt, bt, bd, kd = (cut(z) for z in (v_ref[0], rt_all, at_all, kt_all, bt_all, bd_all, kd_all))
    m_ab = [jnp.where(tri_strict, _dg(at[h], bt[h], NT), 0.0) for h in heads]
    m_ak = [jnp.where(tri_strict, _dg(at[h], kt[h], NT), 0.0).astype(BF16) for h in heads]
    m_rb = [jnp.where(tri_incl, _dg(rt[h], bt[h], NT), 0.0).astype(BF16) for h in heads]
    m_rk = [jnp.where(tri_incl, _dg(rt[h], kt[h], NT), 0.0).astype(BF16) for h in heads]
    tinv = [x.astype(BF16) for x in _tri_inverse(m_ab, row, col, eye, lg_chunk)]
    akv = [_dg(m_ak[h], v[h]).astype(BF16) for h in heads]
    w1 = [_dg(tinv[h], at[h]).astype(BF16) for h in heads]
    w2 = [_dg(tinv[h], akv[h]).astype(BF16) for h in heads]
    q = [(rt_all[:, hsl[h]] + _dg(m_rb[h], w1[h])).astype(BF16) for h in heads]
    y0 = [_dg(m_rb[h], w2[h]) + _dg(m_rk[h], v[h]) for h in heads]

    if chained:
        s = [s_scr[h] for h in heads]
    for ci in range(n_chunks):
        cs = slice(ci * chunk, (ci + 1) * chunk)
        if not chained:
            s = [s0_ref[ci, h] for h in heads]
        g_row = g_end_all[ci * chunk:ci * chunk + 1]
        gm = [jnp.where(eye_h, jnp.broadcast_to(g_row[:, hsl[h]], (HEAD_SIZE, HEAD_SIZE)), 0.0)
              + _dg(bd[h][cs], w1[h][cs], TN) for h in heads]
        hm = [_dg(bd[h][cs], w2[h][cs], TN) + _dg(kd[h][cs], v[h][cs], TN) for h in heads]
        for h in heads:
            y_ref[0, cs, hsl[h]] = _bdot(q[h][cs], s[h]) + y0[h][cs]
        s = [_dot3(gm[h], s[h]) + hm[h] for h in heads]
        if not chained:
            for h in heads:
                s_out[ci, h] = s[h]
    if chained:
        for h in heads:
            s_scr[h] = s[h]
            s_out[0, h] = s[h]


def _wkv(r, lw, k, v, a, b, s0t, tt, chunk, chained):
    bsz, t, c = r.shape
    n_heads = c // HEAD_SIZE
    hs = HEAD_SIZE
    if chained:
        grid = (bsz, t // tt)
        tok = pl.BlockSpec((1, tt, c), lambda bi, ti: (bi, ti, 0))
        st = pl.BlockSpec((1, n_heads, hs, hs), lambda bi, ti: (bi, 0, 0, 0))
        args = (r, lw, k, v, a, b)
        y_shape = (bsz, t, c)
    else:
        assert t == chunk and (bsz * t) % tt == 0
        n_seq = tt // chunk
        grid = (1, bsz * t // tt)
        tok = pl.BlockSpec((1, tt, c), lambda bi, ti: (0, ti, 0))
        st = pl.BlockSpec((n_seq, n_heads, hs, hs), lambda bi, ti: (ti, 0, 0, 0))
        args = tuple(z.reshape(1, bsz * t, c) for z in (r, lw, k, v, a, b))
        y_shape = (1, bsz * t, c)
    y, s_new = pl.pallas_call(
        functools.partial(_wkv_kernel, tt=tt, chunk=chunk, chained=chained),
        grid=grid,
        in_specs=[tok] * 6 + [st],
        out_specs=[tok, st],
        out_shape=[jax.ShapeDtypeStruct(y_shape, F32), jax.ShapeDtypeStruct(s0t.shape, F32)],
        scratch_shapes=[pltpu.VMEM((n_heads, hs, hs), F32)],
        compiler_params=pltpu.CompilerParams(dimension_semantics=("arbitrary",) * 2,
                                             vmem_limit_bytes=VMEM_LIMIT),
        name="wkv",
    )(*args, s0t)
    return y.reshape(bsz, t, c), s_new


def _mix_out_kernel(x_ref, y_ref, bonus_ref, g_ref, yb_ref, lnw_ref, lnb_ref, seg_ref, wout_ref, nffn_ref, wr_ref,
                    h1_out, xn_out, eid_out, gate_out, *, c):
    seg = seg_ref[...]
    y = y_ref[...]
    inv_n = 1.0 / HEAD_SIZE
    mu = _seg_sum(y, seg) * inv_n
    yc = y - mu
    var = _seg_sum(yc * yc, seg) * inv_n
    yn = yc * lax.rsqrt(var + GN_EPS) * lnw_ref[...] + lnb_ref[...]
    ya = (yn + bonus_ref[...]) * g_ref[...]
    mix = _dg(ya.astype(BF16), wout_ref[:c, :]) + _dg(yb_ref[...].astype(BF16), wout_ref[c:, :])
    h1 = x_ref[...] + mix
    h1_out[...] = h1
    xn = _rms(h1, nffn_ref[...])
    xn_out[...] = xn.astype(BF16)

    logits = _dot3(xn, wr_ref[...])
    lane = lax.broadcasted_iota(I32, logits.shape, 1)
    neg = jnp.float32(-jnp.inf)
    is_g = (lane >= N_EXPERTS) & (lane < N_EXPERTS + N_EXPERT_GROUPS)
    glog = jnp.where(is_g, logits, neg)
    gmax = jnp.max(glog, axis=-1, keepdims=True)
    gsel = jnp.min(jnp.where(glog == gmax, lane, 4 * ROUTER_LANES), axis=-1, keepdims=True) - N_EXPERTS
    gp = 1.0 / jnp.sum(jnp.where(is_g, jnp.exp(glog - gmax), 0.0), axis=-1, keepdims=True)
    in_grp = (lane >= gsel * EXPERTS_PER_GROUP) & (lane < (gsel + 1) * EXPERTS_PER_GROUP)
    elog = jnp.where(in_grp, logits, neg)
    emax = jnp.max(elog, axis=-1, keepdims=True)
    ex = jnp.where(in_grp, jnp.exp(elog - emax), 0.0)
    eprob = ex / jnp.sum(ex, axis=-1, keepdims=True)
    eprob = jnp.where(in_grp, eprob, -1.0)
    v1 = jnp.max(eprob, axis=-1, keepdims=True)
    i1 = jnp.min(jnp.where(eprob == v1, lane, 4 * ROUTER_LANES), axis=-1, keepdims=True)
    rest = jnp.where(lane == i1, -1.0, eprob)
    v2 = jnp.max(rest, axis=-1, keepdims=True)
    i2 = jnp.min(jnp.where(rest == v2, lane, 4 * ROUTER_LANES), axis=-1, keepdims=True)
    denom = v1 + v2
    eid_out[...] = jnp.where(lane == 0, i1, jnp.where(lane == 1, i2, 0))
    gate_out[...] = jnp.where(lane == 0, gp * v1 / denom, jnp.where(lane == 1, gp * v2 / denom, 0.0))


def _mix_out(x2, y2, bonus2, g2, yb2, wd, tm):
    m, d = x2.shape
    c = y2.shape[1]
    tokd = pl.BlockSpec((tm, d), lambda i: (i, 0))
    tokc = pl.BlockSpec((tm, c), lambda i: (i, 0))
    tokr = pl.BlockSpec((tm, ROUTER_LANES), lambda i: (i, 0))
    weights = [wd[n] for n in ('ln_x_w', 'ln_x_b', 'seg', 'w_out', 'norm_ffn', 'w_router')]
    return pl.pallas_call(
        functools.partial(_mix_out_kernel, c=c),
        grid=(m // tm,),
        in_specs=[tokd, tokc, tokc, tokc, tokc] + [_full(w.shape) for w in weights],
        out_specs=[tokd, tokd, tokr, tokr],
        out_shape=[jax.ShapeDtypeStruct((m, d), F32), jax.ShapeDtypeStruct((m, d), BF16),
                   jax.ShapeDtypeStruct((m, ROUTER_LANES), I32), jax.ShapeDtypeStruct((m, ROUTER_LANES), F32)],
        compiler_params=pltpu.CompilerParams(dimension_semantics=("arbitrary",), vmem_limit_bytes=VMEM_LIMIT),
        name="mix_out",
    )(x2, y2, bonus2, g2, yb2, *weights)


def _for_each_run_chunk(nch_ref, base, fn):
    def per_expert(e, carry):
        def per_chunk(c, carry2):
            fn(e, c)
            return carry2
        return lax.fori_loop(0, nch_ref[base + e], per_chunk, carry)
    lax.fori_loop(0, N_EXPERTS, per_expert, 0)


def _dispatch_kernel(nch_ref, off_ref, loff_ref, eid_ref, loffc_ref, xn_ref, xs_in, xs_out, buf, sem,
                     *, tm, lbuf, tile0):
    del xs_in
    e_rows = eid_ref[...]
    sub = lax.broadcasted_iota(I32, (N_EXPERTS, tm), 0)
    e1 = jnp.where(sub == e_rows[0:1], 1.0, 0.0)
    e2 = jnp.where(sub == e_rows[1:2], 1.0, 0.0)
    before = lax.broadcasted_iota(I32, (tm, tm), 0) < lax.broadcasted_iota(I32, (tm, tm), 1)
    slot = _dg((e1 + e2).astype(BF16), jnp.where(before, 1.0, 0.0).astype(BF16)) + loffc_ref[0]
    l1 = jnp.sum(slot * e1, axis=0, keepdims=True).astype(I32)
    l2 = jnp.sum(slot * e2, axis=0, keepdims=True).astype(I32)
    rows = lax.broadcasted_iota(I32, (lbuf, tm), 0)
    perm = jnp.where((rows == l1) | (rows == l2), 1.0, 0.0).astype(BF16)
    buf[...] = _dg(perm, xn_ref[...])

    base = (tile0 + pl.program_id(0)) * N_EXPERTS

    def chunk_copy(e, c):
        src = buf.at[pl.ds(pl.multiple_of(loff_ref[base + e] + c * RUN_ALIGN, RUN_ALIGN), RUN_ALIGN)]
        dst = xs_out.at[pl.ds(pl.multiple_of(off_ref[base + e] + c * RUN_ALIGN, RUN_ALIGN), RUN_ALIGN)]
        return pltpu.make_async_copy(src, dst, sem)

    _for_each_run_chunk(nch_ref, base, lambda e, c: chunk_copy(e, c).start())
    _for_each_run_chunk(nch_ref, base, lambda e, c: chunk_copy(e, c).wait())


def _dispatch(plan, eid_t, xn2, xs, tm, tile0):
    m, d = xn2.shape
    lbuf = 2 * tm + LOCAL_PAD
    imap = lambda f: (lambda i, *_: f(i))
    return pl.pallas_call(
        functools.partial(_dispatch_kernel, tm=tm, lbuf=lbuf, tile0=tile0),
        grid_spec=pltpu.PrefetchScalarGridSpec(
            num_scalar_prefetch=3,
            grid=(m // tm,),
            in_specs=[pl.BlockSpec((2, tm), imap(lambda i: (0, i))),
                      pl.BlockSpec((1, N_EXPERTS, 1), imap(lambda i: (tile0 + i, 0, 0))),
                      pl.BlockSpec((tm, d), imap(lambda i: (i, 0))),
                      pl.BlockSpec(memory_space=pl.ANY)],
            out_specs=pl.BlockSpec(memory_space=pl.ANY),
            scratch_shapes=[pltpu.VMEM((lbuf, d), F32), pltpu.SemaphoreType.DMA],
        ),
        out_shape=jax.ShapeDtypeStruct(xs.shape, xs.dtype),
        input_output_aliases={6: 0},
        compiler_params=pltpu.CompilerParams(dimension_semantics=("arbitrary",), vmem_limit_bytes=VMEM_LIMIT),
        name="moe_dispatch",
    )(plan['nch'], plan['off'], plan['loff'], eid_t, plan['loff_col'], xn2, xs)


def _experts_kernel(be_ref, nu_ref, xs_ref, wg_ref, wu_ref, wd_ref, yb_ref):
    @pl.when(pl.program_id(0) < nu_ref[0])
    def _():
        xb = xs_ref[...].astype(BF16)
        hg = _dg(xb, wg_ref[0])
        hu = _dg(xb, wu_ref[0])
        act = (hg * _sigmoid(hg) * hu).astype(BF16)
        yb_ref[...] = _dg(act, wd_ref[0])

    @pl.when(pl.program_id(0) >= nu_ref[0])
    def _():
        yb_ref[...] = jnp.zeros_like(yb_ref)


def _experts(block_expert, n_used, xs, wg, wu, wdn, bm):
    p, d = xs.shape
    ff = wg.shape[2]
    n_blocks = p // bm
    rows = lambda b, be, nu: (jnp.minimum(b, nu[0] - 1), 0)
    return pl.pallas_call(
        _experts_kernel,
        grid_spec=pltpu.PrefetchScalarGridSpec(
            num_scalar_prefetch=2,
            grid=(n_blocks,),
            in_specs=[pl.BlockSpec((bm, d), rows),
                      pl.BlockSpec((1, d, ff), lambda b, be, nu: (be[b], 0, 0)),
                      pl.BlockSpec((1, d, ff), lambda b, be, nu: (be[b], 0, 0)),
                      pl.BlockSpec((1, ff, d), lambda b, be, nu: (be[b], 0, 0))],
            out_specs=pl.BlockSpec((bm, d), lambda b, be, nu: (b, 0)),
        ),
        out_shape=jax.ShapeDtypeStruct((p, d), F32),
        compiler_params=pltpu.CompilerParams(dimension_semantics=("arbitrary",), vmem_limit_bytes=VMEM_LIMIT),
        name="moe_experts",
    )(block_expert, n_used, xs, wg, wu, wdn)


def _final_kernel(nch_ref, off_ref, loff_ref, h1_ref, eid_ref, gate_ref, loffr_ref, p_ref, yb_hbm,
                  nple_ref, wpg_ref, wpp_ref, nfin_ref, y_out, buf, sem, *, tm, lbuf, tile0):
    @pl.when(pl.program_id(0) == 0)
    def _():
        buf[...] = jnp.zeros_like(buf)

    base = (tile0 + pl.program_id(0)) * N_EXPERTS

    def chunk_copy(e, c):
        src = yb_hbm.at[pl.ds(pl.multiple_of(off_ref[base + e] + c * RUN_ALIGN, RUN_ALIGN), RUN_ALIGN)]
        dst = buf.at[pl.ds(pl.multiple_of(loff_ref[base + e] + c * RUN_ALIGN, RUN_ALIGN), RUN_ALIGN)]
        return pltpu.make_async_copy(src, dst, sem)

    _for_each_run_chunk(nch_ref, base, lambda e, c: chunk_copy(e, c).start())

    eid = eid_ref[...]
    lane = lax.broadcasted_iota(I32, (tm, N_EXPERTS), 1)
    e1 = jnp.where(lane == eid[:, 0:1], 1.0, 0.0)
    e2 = jnp.where(lane == eid[:, 1:2], 1.0, 0.0)
    before = lax.broadcasted_iota(I32, (tm, tm), 1) < lax.broadcasted_iota(I32, (tm, tm), 0)
    slot = _dg(jnp.where(before, 1.0, 0.0).astype(BF16), (e1 + e2).astype(BF16)) + loffr_ref[0]
    l1 = jnp.sum(slot * e1, axis=1, keepdims=True).astype(I32)
    l2 = jnp.sum(slot * e2, axis=1, keepdims=True).astype(I32)
    cols = lax.broadcasted_iota(I32, (tm, lbuf), 1)
    pick1 = jnp.where(cols == l1, 1.0, 0.0).astype(BF16)
    pick2 = jnp.where(cols == l2, 1.0, 0.0).astype(BF16)

    _for_each_run_chunk(nch_ref, base, lambda e, c: chunk_copy(e, c).wait())
    yhi, ylo = _split2(buf[...])
    gate = gate_ref[...]
    moe = (gate[:, 0:1] * (_dg(pick1, yhi) + _dg(pick1, ylo)) + gate[:, 1:2] * (_dg(pick2, yhi) + _dg(pick2, ylo)))
    h2 = h1_ref[...] + moe
    pg = _sigmoid(_dg(_rms(h2, nple_ref[...]).astype(BF16), wpg_ref[...]))
    h3 = h2 + pg * _dg(p_ref[...].astype(BF16), wpp_ref[...])
    y_out[...] = _rms(h3, nfin_ref[...])


def _final(plan, h1, eid, gate, p2, yb, wd, tm, tile0):
    m, d = h1.shape
    pd = p2.shape[1]
    lbuf = 2 * tm + LOCAL_PAD
    weights = [wd[n] for n in ('norm_ple', 'w_ple_gate', 'w_ple_proj', 'norm_final')]
    imap = lambda f: (lambda i, *_: f(i))
    return pl.pallas_call(
        functools.partial(_final_kernel, tm=tm, lbuf=lbuf, tile0=tile0),
        grid_spec=pltpu.PrefetchScalarGridSpec(
            num_scalar_prefetch=3,
            grid=(m // tm,),
            in_specs=[pl.BlockSpec((tm, d), imap(lambda i: (i, 0))),
                      pl.BlockSpec((tm, ROUTER_LANES), imap(lambda i: (i, 0))),
                      pl.BlockSpec((tm, ROUTER_LANES), imap(lambda i: (i, 0))),
                      pl.BlockSpec((1, 1, N_EXPERTS), imap(lambda i: (tile0 + i, 0, 0))),
                      pl.BlockSpec((tm, pd), imap(lambda i: (i, 0))),
                      pl.BlockSpec(memory_space=pl.ANY)] +
                     [pl.BlockSpec(w.shape, imap(lambda i, n=len(w.shape): (0,) * n)) for w in weights],
            out_specs=pl.BlockSpec((tm, d), imap(lambda i: (i, 0))),
            scratch_shapes=[pltpu.VMEM((lbuf, d), F32), pltpu.SemaphoreType.DMA],
        ),
        out_shape=jax.ShapeDtypeStruct((m, d), F32),
        compiler_params=pltpu.CompilerParams(dimension_semantics=("arbitrary",), vmem_limit_bytes=VMEM_LIMIT),
        name="moe_final",
    )(plan['nch'], plan['off'], plan['loff'], h1, eid, gate, plan['loff_row'], p2, yb, *weights)


def _route_plan(eids, tms, bm):
    experts = jnp.arange(N_EXPERTS, dtype=I32)
    counts = []
    for eid, tm in zip(eids, tms):
        onehot = (eid[:, :2, None] == experts).astype(I32)
        counts.append(onehot.reshape(-1, 2 * tm, N_EXPERTS).sum(axis=1))
    n = jnp.concatenate(counts)
    n_al = (n + RUN_ALIGN - 1) // RUN_ALIGN * RUN_ALIGN
    loff = jnp.cumsum(n_al, axis=1) - n_al
    region = (n_al.sum(axis=0) + bm - 1) // bm * bm
    pend = jnp.cumsum(region)
    off = (pend - region)[None, :] + jnp.cumsum(n_al, axis=0) - n_al
    n_assign = sum(2 * e.shape[0] for e in eids)
    n_blocks = -(-(n_assign + (RUN_ALIGN - 1) * N_EXPERTS * n.shape[0] + N_EXPERTS * (bm - 1)) // bm)
    block_start = jnp.arange(n_blocks, dtype=I32) * bm
    block_expert = jnp.minimum(jnp.sum((pend[None, :] <= block_start[:, None]).astype(I32), axis=1), N_EXPERTS - 1)
    plan = dict(nch=(n_al // RUN_ALIGN).reshape(-1).astype(I32), off=off.reshape(-1).astype(I32),
                loff=loff.reshape(-1).astype(I32), loff_col=loff.astype(F32)[:, :, None],
                loff_row=loff.astype(F32)[:, None, :])
    return plan, block_expert.astype(I32), (pend[-1] // bm).astype(I32).reshape(1), n_blocks


def _layer_front(x, shift0, wkv0, conv0, wd, tm_in, tm_tok, wkv_tile, wkv_chunk, chained):
    bsz, t, d = x.shape
    r, lw, k2, v, ah, bh, g, bonus, yb, shift_new, conv_new = _mix_in(x, shift0, conv0, wd, tm_in)
    y, s_new = _wkv(r, lw, k2, v, ah, bh, jnp.swapaxes(wkv0, -1, -2), wkv_tile, wkv_chunk, chained)
    flat = lambda z: z.reshape(bsz * t, z.shape[-1])
    h1, xn2, eid, gate = _mix_out(flat(x), flat(y), flat(bonus), flat(g), flat(yb), wd, tm_tok)
    return h1, xn2, eid, gate, shift_new.reshape(bsz, d), jnp.swapaxes(s_new, -1, -2), conv_new


def kernel(x_prompt, x_sample, state_shift, state_wkv, cache_conv, p_prompt, p_sample, norm_mix, w_in, mu_rkv, mu_w, mu_a, mu_g, w0, w1, w2, a0, a1, a2, g1, g2, k_k, k_a, r_k, ln_x_w, ln_x_b, dw_w, dw_b, cln_w, cln_b, w_out, norm_ffn, w_router_group, w_router_expert, w_exp_gate, w_exp_up, w_exp_down, norm_ple, w_ple_gate, w_ple_proj, norm_final):
    depth = norm_mix.shape[0]
    assert depth == 1
    d = x_prompt.shape[-1]
    c = w0.shape[-1]
    row = lambda z: z[0].reshape(1, -1).astype(F32)
    lane = jnp.arange(c, dtype=I32) // HEAD_SIZE
    wd = dict(
        norm_mix=row(norm_mix), w_in=w_in[0].astype(BF16), mu_rkv=row(mu_rkv), mu_w=row(mu_w), mu_a=row(mu_a),
        mu_g=row(mu_g), w0=row(w0), w1=w1[0].astype(BF16), w2=w2[0].astype(BF16), a0=row(a0),
        a1=a1[0].astype(BF16), a2=a2[0].astype(BF16), g1=g1[0].astype(BF16), g2=g2[0].astype(BF16),
        k_k=row(k_k), k_a=row(k_a), r_k=row(r_k), ln_x_w=row(ln_x_w), ln_x_b=row(ln_x_b),
        seg=(lane[:, None] == lane[None, :]).astype(BF16),
        dw_w=dw_w[0].astype(F32), dw_b=row(dw_b), cln_w=row(cln_w), cln_b=row(cln_b),
        w_out=w_out[0].astype(BF16), norm_ffn=row(norm_ffn),
        w_router=jnp.concatenate([w_router_expert[0], w_router_group[0],
                                  jnp.zeros((d, ROUTER_LANES - N_EXPERTS - N_EXPERT_GROUPS), F32)], axis=1),
        norm_ple=row(norm_ple), w_ple_gate=w_ple_gate[0].astype(BF16), w_ple_proj=w_ple_proj[0].astype(BF16),
        norm_final=norm_final.reshape(1, -1).astype(F32),
    )
    bp, tp, _ = x_prompt.shape
    bs, ts, _ = x_sample.shape
    mp, ms = bp * tp, bs * ts
    tm_p = min(256, tp)
    tm_s = min(256, ms)

    zeros = lambda *s: jnp.zeros(s, F32)
    h1_p, xn_p, eid_p, gate_p, shift_p, wkv_p, conv_p = _layer_front(
        x_prompt, zeros(bp, 1, d), zeros(bp, c // HEAD_SIZE, HEAD_SIZE, HEAD_SIZE), zeros(bp, CONV_CARRY, c),
        wd, tm_p, tm_p, min(128, tp), min(64, tp), True)
    h1_s, xn_s, eid_s, gate_s, shift_s, wkv_s, conv_s = _layer_front(
        x_sample, state_shift[0][:, None, :], state_wkv[0], cache_conv[0],
        wd, ts, tm_s, min(128, ms), ts, False)

    bm = 256
    tr_p, tr_s = min(512, mp), min(512, ms)
    plan, block_expert, n_used, n_blocks = _route_plan([eid_p, eid_s], [tr_p, tr_s], bm)
    tiles_p = mp // tr_p
    xs = jnp.zeros((n_blocks * bm, d), F32)
    xs = _dispatch(plan, eid_p[:, :2].T, xn_p, xs, tr_p, 0)
    xs = _dispatch(plan, eid_s[:, :2].T, xn_s, xs, tr_s, tiles_p)
    yb = _experts(block_expert, n_used, xs, w_exp_gate[0].astype(BF16), w_exp_up[0].astype(BF16),
                  w_exp_down[0].astype(BF16), bm)
    y_p = _final(plan, h1_p, eid_p, gate_p, p_prompt[0].reshape(mp, -1), yb, wd, tr_p, 0)
    y_s = _final(plan, h1_s, eid_s, gate_s, p_sample[0].reshape(ms, -1), yb, wd, tr_s, tiles_p)
    return (y_p.reshape(x_prompt.shape), y_s.reshape(x_sample.shape), shift_p[None], wkv_p[None], conv_p[None],
            shift_s[None], wkv_s[None], conv_s[None])
```

```python
import functools

import jax
import jax.numpy as jnp
from jax import lax
from jax.experimental import pallas as pl
from jax.experimental.pallas import tpu as pltpu

F32 = jnp.float32
BF16 = jnp.bfloat16
I32 = jnp.int32

HEAD_SIZE = 64
CONV_WIDTH = 31
CONV_CARRY = CONV_WIDTH - 1
SUBLANES = 8
CARRY_PAD = 32
N_EXPERT_GROUPS = 4
EXPERTS_PER_GROUP = 8
N_EXPERTS = N_EXPERT_GROUPS * EXPERTS_PER_GROUP
ROUTER_LANES = 128
RMS_EPS = 1e-6
LN_EPS = 1e-5
GN_EPS = 64e-5
DECAY_SCALE = 0.6065306597126334
INV_BASE = 16
RUN_ALIGN = 8
LOCAL_PAD = N_EXPERTS * RUN_ALIGN
VMEM_LIMIT = 56 * 1024 * 1024

NN = ((1,), (0,))
NT = ((1,), (1,))
TN = ((0,), (0,))


def _dg(a, b, dims=NN):
    return lax.dot_general(a, b, (dims, ((), ())), preferred_element_type=F32)


def _split2(x):
    hi = x.astype(BF16)
    lo = (x - hi.astype(F32)).astype(BF16)
    return hi, lo


def _split3(x):
    hi = x.astype(BF16)
    r1 = x - hi.astype(F32)
    mid = r1.astype(BF16)
    lo = (r1 - mid.astype(F32)).astype(BF16)
    return hi, mid, lo


def _dot3(a, b, dims=NN):
    ah, al = _split2(a)
    bh, bl = _split2(b)
    return _dg(ah, bh, dims) + (_dg(al, bh, dims) + _dg(ah, bl, dims))


def _bdot(a, b, dims=NN):
    return _dg(a.astype(BF16), b.astype(BF16), dims)


def _mask_dot(mask_bf16, x):
    h, m, l = _split3(x)
    return _dg(mask_bf16, h) + (_dg(mask_bf16, m) + _dg(mask_bf16, l))


def _seg_sum(x, seg_bf16):
    h, l = _split2(x)
    return _dg(h, seg_bf16) + _dg(l, seg_bf16)


def _rms(x, g):
    return x * lax.rsqrt(jnp.mean(x * x, axis=-1, keepdims=True) + RMS_EPS) * g


def _sigmoid(x):
    return 0.5 * jnp.tanh(0.5 * x) + 0.5


def _full(shape):
    n = len(shape)
    return pl.BlockSpec(shape, lambda *_: (0,) * n)


def _mix_in_kernel(x_ref, shift_ref, conv_ref, nm_ref, win_ref, murkv_ref, muw_ref, mua_ref, mug_ref,
                   w0_ref, w1_ref, w2_ref, a0_ref, a1_ref, a2_ref, g1_ref, g2_ref,
                   kk_ref, ka_ref, rk_ref, seg_ref, dww_ref, dwb_ref, clnw_ref, clnb_ref,
                   r_out, lw_out, k_out, v_out, a_out, b_out, g_out, bonus_out, yb_out, shift_out, conv_out,
                   xn_last, h_last, up_ext, shifted, *, tm, c):
    i = pl.program_id(1)

    @pl.when(i == 0)
    def _():
        sp = shift_ref[0]
        xn_last[...] = sp
        sp8 = jnp.broadcast_to(sp, (8, sp.shape[1])).astype(BF16)
        h_last[...] = _dg(sp8, win_ref[:, :3 * c])[0:1]
        up_ext[CARRY_PAD - CONV_CARRY:CARRY_PAD, :] = conv_ref[0]

    x = x_ref[0]
    xn = _rms(x, nm_ref[...])
    hin = _dg(xn.astype(BF16), win_ref[...])
    first = lax.broadcasted_iota(I32, (tm, 1), 0) == 0
    xprev = jnp.where(first, xn_last[...], pltpu.roll(xn, 1, 0))
    dx = xprev - xn
    h_rkv = hin[:, :3 * c]
    hprev = jnp.where(first, h_last[...], pltpu.roll(h_rkv, 1, 0))
    rkv = h_rkv + (hprev - h_rkv) * murkv_ref[...]
    r = rkv[:, :c]
    k = rkv[:, c:2 * c]
    v = rkv[:, 2 * c:]
    xw = (xn + dx * muw_ref[...]).astype(BF16)
    xa = (xn + dx * mua_ref[...]).astype(BF16)
    xg = (xn + dx * mug_ref[...]).astype(BF16)
    zw = w0_ref[...] + _dg(jnp.tanh(_dg(xw, w1_ref[...])).astype(BF16), w2_ref[...])
    a = _sigmoid(a0_ref[...] + _dg(_dg(xa, a1_ref[...]).astype(BF16), a2_ref[...]))
    g = _dg(_sigmoid(_dg(xg, g1_ref[...])).astype(BF16), g2_ref[...])
    seg = seg_ref[...]
    kk = k * kk_ref[...]
    kk = kk * jnp.minimum(lax.rsqrt(_seg_sum(kk * kk, seg)), 1e12)
    k2 = k * (1.0 + (a - 1.0) * ka_ref[...])
    r_out[0] = r
    lw_out[0] = -DECAY_SCALE * _sigmoid(zw)
    k_out[0] = k2
    v_out[0] = v
    a_out[0] = -kk
    b_out[0] = kk * a
    g_out[0] = g
    bonus_out[0] = _seg_sum(r * k2 * rk_ref[...], seg) * v

    u = hin[:, 3 * c:4 * c] * _sigmoid(hin[:, 4 * c:])
    up_ext[CARRY_PAD:CARRY_PAD + tm, :] = u
    first_row = CARRY_PAD - CONV_CARRY
    for s in range(SUBLANES):
        span = tm + (CONV_WIDTH - 1 - s) // SUBLANES * SUBLANES
        shifted[s, 0:span, :] = up_ext[pl.ds(first_row + s, span), :]
    z = jnp.zeros_like(u) + dwb_ref[...]
    for j in range(CONV_WIDTH):
        s, m = j % SUBLANES, j // SUBLANES
        z = z + dww_ref[j:j + 1, :] * shifted[s, m * SUBLANES:m * SUBLANES + tm, :]
    mu = jnp.mean(z, axis=-1, keepdims=True)
    zc = z - mu
    var = jnp.mean(zc * zc, axis=-1, keepdims=True)
    zn = zc * lax.rsqrt(var + LN_EPS) * clnw_ref[...] + clnb_ref[...]
    yb_out[0] = zn * _sigmoid(zn)

    tail = up_ext[pl.ds(tm + CARRY_PAD - CONV_CARRY, CONV_CARRY), :]
    up_ext[CARRY_PAD - CONV_CARRY:CARRY_PAD, :] = tail
    conv_out[0] = tail
    xn_last[...] = xn[tm - 1:tm]
    h_last[...] = h_rkv[tm - 1:tm]
    shift_out[0] = xn[tm - 1:tm]


def _mix_in(x, shift0, conv0, wd, tm):
    bsz, t, d = x.shape
    c = wd['w0'].shape[1]
    grid = (bsz, t // tm)
    tok = lambda w: pl.BlockSpec((1, tm, w), lambda b, i: (b, i, 0))
    per_seq = lambda rows, w: pl.BlockSpec((1, rows, w), lambda b, i: (b, 0, 0))
    weights = [wd[n] for n in ('norm_mix', 'w_in', 'mu_rkv', 'mu_w', 'mu_a', 'mu_g', 'w0', 'w1', 'w2', 'a0', 'a1',
                               'a2', 'g1', 'g2', 'k_k', 'k_a', 'r_k', 'seg', 'dw_w', 'dw_b', 'cln_w', 'cln_b')]
    out_tok = jax.ShapeDtypeStruct((bsz, t, c), F32)
    outs = pl.pallas_call(
        functools.partial(_mix_in_kernel, tm=tm, c=c),
        grid=grid,
        in_specs=[tok(d), per_seq(1, d), per_seq(CONV_CARRY, c)] + [_full(w.shape) for w in weights],
        out_specs=[tok(c)] * 9 + [per_seq(1, d), per_seq(CONV_CARRY, c)],
        out_shape=[out_tok] * 9 + [jax.ShapeDtypeStruct((bsz, 1, d), F32),
                                   jax.ShapeDtypeStruct((bsz, CONV_CARRY, c), F32)],
        scratch_shapes=[pltpu.VMEM((1, d), F32), pltpu.VMEM((1, 3 * c), F32),
                        pltpu.VMEM((tm + CARRY_PAD, c), F32),
                        pltpu.VMEM((SUBLANES, tm + CARRY_PAD - SUBLANES, c), F32)],
        compiler_params=pltpu.CompilerParams(dimension_semantics=("arbitrary", "arbitrary"),
                                             vmem_limit_bytes=VMEM_LIMIT),
        name="mix_in",
    )(x, shift0, conv0, *weights)
    return outs


def _tri_inverse(n_strict, row, col, eye, lg_chunk):
    lg_base = INV_BASE.bit_length() - 1
    same = lambda sh: (row >> sh) == (col >> sh)
    lg0 = min(lg_base, lg_chunk)
    p = [jnp.where(same(lg0), n, 0.0) for n in n_strict]
    t = [jnp.where(eye, 1.0, 0.0) + x for x in p]
    for _ in range(lg0 - 1):
        p = [_bdot(x, x) for x in p]
        t = [x + _bdot(x, y) for x, y in zip(t, p)]
    for lg in range(lg0, lg_chunk):
        off_mask = same(lg + 1) & jnp.logical_not(same(lg))
        u = [_bdot(x, jnp.where(off_mask, n, 0.0)) for x, n in zip(t, n_strict)]
        t = [x + _bdot(y, x) for x, y in zip(t, u)]
    return t


def _wkv_kernel(r_ref, lw_ref, k_ref, v_ref, a_ref, b_ref, s0_ref, y_ref, s_out, s_scr, *, tt, chunk, chained):
    ti = pl.program_id(1)
    n_heads = r_ref.shape[2] // HEAD_SIZE
    n_chunks = tt // chunk
    lg_chunk = chunk.bit_length() - 1
    row = lax.broadcasted_iota(I32, (tt, tt), 0)
    col = lax.broadcasted_iota(I32, (tt, tt), 1)
    in_chunk = (row >> lg_chunk) == (col >> lg_chunk)
    eye = row == col
    tri_incl = in_chunk & (col <= row)
    tri_strict = in_chunk & (col < row)
    m_cum = jnp.where(tri_incl, 1.0, 0.0).astype(BF16)
    m_tot = jnp.where(in_chunk, 1.0, 0.0).astype(BF16)
    row_h = lax.broadcasted_iota(I32, (HEAD_SIZE, HEAD_SIZE), 0)
    col_h = lax.broadcasted_iota(I32, (HEAD_SIZE, HEAD_SIZE), 1)
    eye_h = row_h == col_h

    if chained:
        @pl.when(ti == 0)
        def _():
            s_scr[...] = s0_ref[0]

    lw_all = lw_ref[0]
    k_all = k_ref[0]
    b_all = b_ref[0]
    cum = _mask_dot(m_cum, lw_all)
    tot = _mask_dot(m_tot, lw_all)
    e_neg = jnp.exp(-cum)
    e_end = jnp.exp(tot - cum)
    rt_all = r_ref[0] * jnp.exp(cum)
    at_all = a_ref[0] * jnp.exp(cum - lw_all)
    kt_all = k_all * e_neg
    bt_all = b_all * e_neg
    bd_all = b_all * e_end
    kd_all = k_all * e_end
    g_end_all = jnp.exp(tot)

    heads = range(n_heads)
    hsl = [slice(HEAD_SIZE * h, HEAD_SIZE * (h + 1)) for h in heads]
    cut = lambda z: [z[:, s_].astype(BF16) for s_ in hsl]
    v, rt, at, kt, bt, bd, kd = (cut(z) for z in (v_ref[0], rt_all, at_all, kt_all, bt_all, bd_all, kd_all))
    m_ab = [jnp.where(tri_strict, _dg(at[h], bt[h], NT), 0.0) for h in heads]
    m_ak = [jnp.where(tri_strict, _dg(at[h], kt[h], NT), 0.0).astype(BF16) for h in heads]
    m_rb = [jnp.where(tri_incl, _dg(rt[h], bt[h], NT), 0.0).astype(BF16) for h in heads]
    m_rk = [jnp.where(tri_incl, _dg(rt[h], kt[h], NT), 0.0).astype(BF16) for h in heads]
    tinv = [x.astype(BF16) for x in _tri_inverse(m_ab, row, col, eye, lg_chunk)]
    akv = [_dg(m_ak[h], v[h]).astype(BF16) for h in heads]
    w1 = [_dg(tinv[h], at[h]).astype(BF16) for h in heads]
    w2 = [_dg(tinv[h], akv[h]).astype(BF16) for h in heads]
    q = [(rt_all[:, hsl[h]] + _dg(m_rb[h], w1[h])).astype(BF16) for h in heads]
    y0 = [_dg(m_rb[h], w2[h]) + _dg(m_rk[h], v[h]) for h in heads]

    if chained:
        s = [s_scr[h] for h in heads]
    for ci in range(n_chunks):
        cs = slice(ci * chunk, (ci + 1) * chunk)
        if not chained:
            s = [s0_ref[ci, h] for h in heads]
        g_row = g_end_all[ci * chunk:ci * chunk + 1]
        gm = [jnp.where(eye_h, jnp.broadcast_to(g_row[:, hsl[h]], (HEAD_SIZE, HEAD_SIZE)), 0.0)
              + _dg(bd[h][cs], w1[h][cs], TN) for h in heads]
        hm = [_dg(bd[h][cs], w2[h][cs], TN) + _dg(kd[h][cs], v[h][cs], TN) for h in heads]
        for h in heads:
            y_ref[0, cs, hsl[h]] = _bdot(q[h][cs], s[h]) + y0[h][cs]
        s = [_dot3(gm[h], s[h]) + hm[h] for h in heads]
        if not chained:
            for h in heads:
                s_out[ci, h] = s[h]
    if chained:
        for h in heads:
            s_scr[h] = s[h]
            s_out[0, h] = s[h]


def _wkv(r, lw, k, v, a, b, s0t, tt, chunk, chained):
    bsz, t, c = r.shape
    n_heads = c // HEAD_SIZE
    hs = HEAD_SIZE
    if chained:
        grid = (bsz, t // tt)
        tok = pl.BlockSpec((1, tt, c), lambda bi, ti: (bi, ti, 0))
        st = pl.BlockSpec((1, n_heads, hs, hs), lambda bi, ti: (bi, 0, 0, 0))
        args = (r, lw, k, v, a, b)
        y_shape = (bsz, t, c)
    else:
        assert t == chunk and (bsz * t) % tt == 0
        n_seq = tt // chunk
        grid = (1, bsz * t // tt)
        tok = pl.BlockSpec((1, tt, c), lambda bi, ti: (0, ti, 0))
        st = pl.BlockSpec((n_seq, n_heads, hs, hs), lambda bi, ti: (ti, 0, 0, 0))
        args = tuple(z.reshape(1, bsz * t, c) for z in (r, lw, k, v, a, b))
        y_shape = (1, bsz * t, c)
    y, s_new = pl.pallas_call(
        functools.partial(_wkv_kernel, tt=tt, chunk=chunk, chained=chained),
        grid=grid,
        in_specs=[tok] * 6 + [st],
        out_specs=[tok, st],
        out_shape=[jax.ShapeDtypeStruct(y_shape, F32), jax.ShapeDtypeStruct(s0t.shape, F32)],
        scratch_shapes=[pltpu.VMEM((n_heads, hs, hs), F32)],
        compiler_params=pltpu.CompilerParams(dimension_semantics=("arbitrary",) * 2,
                                             vmem_limit_bytes=VMEM_LIMIT),
        name="wkv",
    )(*args, s0t)
    return y.reshape(bsz, t, c), s_new


def _mix_out_kernel(x_ref, y_ref, bonus_ref, g_ref, yb_ref, lnw_ref, lnb_ref, seg_ref, wout_ref, nffn_ref, wr_ref,
                    h1_out, xn_out, eid_out, gate_out, *, c):
    seg = seg_ref[...]
    y = y_ref[...]
    inv_n = 1.0 / HEAD_SIZE
    mu = _seg_sum(y, seg) * inv_n
    yc = y - mu
    var = _seg_sum(yc * yc, seg) * inv_n
    yn = yc * lax.rsqrt(var + GN_EPS) * lnw_ref[...] + lnb_ref[...]
    ya = (yn + bonus_ref[...]) * g_ref[...]
    mix = _dg(ya.astype(BF16), wout_ref[:c, :]) + _dg(yb_ref[...].astype(BF16), wout_ref[c:, :])
    h1 = x_ref[...] + mix
    h1_out[...] = h1
    xn = _rms(h1, nffn_ref[...])
    xn_out[...] = xn.astype(BF16)

    logits = _dot3(xn, wr_ref[...])
    lane = lax.broadcasted_iota(I32, logits.shape, 1)
    neg = jnp.float32(-jnp.inf)
    is_g = (lane >= N_EXPERTS) & (lane < N_EXPERTS + N_EXPERT_GROUPS)
    glog = jnp.where(is_g, logits, neg)
    gmax = jnp.max(glog, axis=-1, keepdims=True)
    gsel = jnp.min(jnp.where(glog == gmax, lane, 4 * ROUTER_LANES), axis=-1, keepdims=True) - N_EXPERTS
    gp = 1.0 / jnp.sum(jnp.where(is_g, jnp.exp(glog - gmax), 0.0), axis=-1, keepdims=True)
    in_grp = (lane >= gsel * EXPERTS_PER_GROUP) & (lane < (gsel + 1) * EXPERTS_PER_GROUP)
    elog = jnp.where(in_grp, logits, neg)
    emax = jnp.max(elog, axis=-1, keepdims=True)
    ex = jnp.where(in_grp, jnp.exp(elog - emax), 0.0)
    eprob = ex / jnp.sum(ex, axis=-1, keepdims=True)
    eprob = jnp.where(in_grp, eprob, -1.0)
    v1 = jnp.max(eprob, axis=-1, keepdims=True)
    i1 = jnp.min(jnp.where(eprob == v1, lane, 4 * ROUTER_LANES), axis=-1, keepdims=True)
    rest = jnp.where(lane == i1, -1.0, eprob)
    v2 = jnp.max(rest, axis=-1, keepdims=True)
    i2 = jnp.min(jnp.where(rest == v2, lane, 4 * ROUTER_LANES), axis=-1, keepdims=True)
    denom = v1 + v2
    eid_out[...] = jnp.where(lane == 0, i1, jnp.where(lane == 1, i2, 0))
    gate_out[...] = jnp.where(lane == 0, gp * v1 / denom, jnp.where(lane == 1, gp * v2 / denom, 0.0))


def _mix_out(x2, y2, bonus2, g2, yb2, wd, tm):
    m, d = x2.shape
    c = y2.shape[1]
    tokd = pl.BlockSpec((tm, d), lambda i: (i, 0))
    tokc = pl.BlockSpec((tm, c), lambda i: (i, 0))
    tokr = pl.BlockSpec((tm, ROUTER_LANES), lambda i: (i, 0))
    weights = [wd[n] for n in ('ln_x_w', 'ln_x_b', 'seg', 'w_out', 'norm_ffn', 'w_router')]
    return pl.pallas_call(
        functools.partial(_mix_out_kernel, c=c),
        grid=(m // tm,),
        in_specs=[tokd, tokc, tokc, tokc, tokc] + [_full(w.shape) for w in weights],
        out_specs=[tokd, tokd, tokr, tokr],
        out_shape=[jax.ShapeDtypeStruct((m, d), F32), jax.ShapeDtypeStruct((m, d), BF16),
                   jax.ShapeDtypeStruct((m, ROUTER_LANES), I32), jax.ShapeDtypeStruct((m, ROUTER_LANES), F32)],
        compiler_params=pltpu.CompilerParams(dimension_semantics=("arbitrary",), vmem_limit_bytes=VMEM_LIMIT),
        name="mix_out",
    )(x2, y2, bonus2, g2, yb2, *weights)


def _for_each_run_chunk(nch_ref, base, fn):
    def per_expert(e, carry):
        def per_chunk(c, carry2):
            fn(e, c)
            return carry2
        return lax.fori_loop(0, nch_ref[base + e], per_chunk, carry)
    lax.fori_loop(0, N_EXPERTS, per_expert, 0)


def _dispatch_kernel(nch_ref, off_ref, loff_ref, eid_ref, loffc_ref, xn_ref, xs_in, xs_out, buf, sem,
                     *, tm, lbuf, tile0):
    del xs_in
    e_rows = eid_ref[...]
    sub = lax.broadcasted_iota(I32, (N_EXPERTS, tm), 0)
    e1 = jnp.where(sub == e_rows[0:1], 1.0, 0.0)
    e2 = jnp.where(sub == e_rows[1:2], 1.0, 0.0)
    before = lax.broadcasted_iota(I32, (tm, tm), 0) < lax.broadcasted_iota(I32, (tm, tm), 1)
    slot = _dg((e1 + e2).astype(BF16), jnp.where(before, 1.0, 0.0).astype(BF16)) + loffc_ref[0]
    l1 = jnp.sum(slot * e1, axis=0, keepdims=True).astype(I32)
    l2 = jnp.sum(slot * e2, axis=0, keepdims=True).astype(I32)
    rows = lax.broadcasted_iota(I32, (lbuf, tm), 0)
    perm = jnp.where((rows == l1) | (rows == l2), 1.0, 0.0).astype(BF16)
    buf[...] = _dg(perm, xn_ref[...])

    base = (tile0 + pl.program_id(0)) * N_EXPERTS

    def chunk_copy(e, c):
        src = buf.at[pl.ds(pl.multiple_of(loff_ref[base + e] + c * RUN_ALIGN, RUN_ALIGN), RUN_ALIGN)]
        dst = xs_out.at[pl.ds(pl.multiple_of(off_ref[base + e] + c * RUN_ALIGN, RUN_ALIGN), RUN_ALIGN)]
        return pltpu.make_async_copy(src, dst, sem)

    _for_each_run_chunk(nch_ref, base, lambda e, c: chunk_copy(e, c).start())
    _for_each_run_chunk(nch_ref, base, lambda e, c: chunk_copy(e, c).wait())


def _dispatch(plan, eid_t, xn2, xs, tm, tile0):
    m, d = xn2.shape
    lbuf = 2 * tm + LOCAL_PAD
    imap = lambda f: (lambda i, *_: f(i))
    return pl.pallas_call(
        functools.partial(_dispatch_kernel, tm=tm, lbuf=lbuf, tile0=tile0),
        grid_spec=pltpu.PrefetchScalarGridSpec(
            num_scalar_prefetch=3,
            grid=(m // tm,),
            in_specs=[pl.BlockSpec((2, tm), imap(lambda i: (0, i))),
                      pl.BlockSpec((1, N_EXPERTS, 1), imap(lambda i: (tile0 + i, 0, 0))),
                      pl.BlockSpec((tm, d), imap(lambda i: (i, 0))),
                      pl.BlockSpec(memory_space=pl.ANY)],
            out_specs=pl.BlockSpec(memory_space=pl.ANY),
            scratch_shapes=[pltpu.VMEM((lbuf, d), F32), pltpu.SemaphoreType.DMA],
        ),
        out_shape=jax.ShapeDtypeStruct(xs.shape, xs.dtype),
        input_output_aliases={6: 0},
        compiler_params=pltpu.CompilerParams(dimension_semantics=("arbitrary",), vmem_limit_bytes=VMEM_LIMIT),
        name="moe_dispatch",
    )(plan['nch'], plan['off'], plan['loff'], eid_t, plan['loff_col'], xn2, xs)


def _experts_kernel(be_ref, nu_ref, xs_ref, wg_ref, wu_ref, wd_ref, yb_ref, wg_b, wu_b, wd_b):
    b = pl.program_id(0)

    @pl.when((b == 0) | (be_ref[b] != be_ref[jnp.maximum(b - 1, 0)]))
    def _():
        wg_b[...] = wg_ref[0].astype(BF16)
        wu_b[...] = wu_ref[0].astype(BF16)
        wd_b[...] = wd_ref[0].astype(BF16)

    @pl.when(b < nu_ref[0])
    def _():
        xb = xs_ref[...].astype(BF16)
        hg = _dg(xb, wg_b[...])
        hu = _dg(xb, wu_b[...])
        act = (hg * _sigmoid(hg) * hu).astype(BF16)
        yb_ref[...] = _dg(act, wd_b[...])

    @pl.when(pl.program_id(0) >= nu_ref[0])
    def _():
        yb_ref[...] = jnp.zeros_like(yb_ref)


def _experts(block_expert, n_used, xs, wg, wu, wdn, bm):
    p, d = xs.shape
    ff = wg.shape[2]
    n_blocks = p // bm
    rows = lambda b, be, nu: (jnp.minimum(b, nu[0] - 1), 0)
    return pl.pallas_call(
        _experts_kernel,
        grid_spec=pltpu.PrefetchScalarGridSpec(
            num_scalar_prefetch=2,
            grid=(n_blocks,),
            in_specs=[pl.BlockSpec((bm, d), rows),
                      pl.BlockSpec((1, d, ff), lambda b, be, nu: (be[b], 0, 0)),
                      pl.BlockSpec((1, d, ff), lambda b, be, nu: (be[b], 0, 0)),
                      pl.BlockSpec((1, ff, d), lambda b, be, nu: (be[b], 0, 0))],
            out_specs=pl.BlockSpec((bm, d), lambda b, be, nu: (b, 0)),
            scratch_shapes=[pltpu.VMEM((d, ff), BF16), pltpu.VMEM((d, ff), BF16), pltpu.VMEM((ff, d), BF16)],
        ),
        out_shape=jax.ShapeDtypeStruct((p, d), F32),
        compiler_params=pltpu.CompilerParams(dimension_semantics=("arbitrary",), vmem_limit_bytes=VMEM_LIMIT),
        name="moe_experts",
    )(block_expert, n_used, xs, wg, wu, wdn)


def _final_kernel(nch_ref, off_ref, loff_ref, h1_ref, eid_ref, gate_ref, loffr_ref, p_ref, yb_hbm,
                  nple_ref, wpg_ref, wpp_ref, nfin_ref, y_out, buf, sem, *, tm, lbuf, tile0):
    @pl.when(pl.program_id(0) == 0)
    def _():
        buf[...] = jnp.zeros_like(buf)

    base = (tile0 + pl.program_id(0)) * N_EXPERTS

    def chunk_copy(e, c):
        src = yb_hbm.at[pl.ds(pl.multiple_of(off_ref[base + e] + c * RUN_ALIGN, RUN_ALIGN), RUN_ALIGN)]
        dst = buf.at[pl.ds(pl.multiple_of(loff_ref[base + e] + c * RUN_ALIGN, RUN_ALIGN), RUN_ALIGN)]
        return pltpu.make_async_copy(src, dst, sem)

    _for_each_run_chunk(nch_ref, base, lambda e, c: chunk_copy(e, c).start())

    eid = eid_ref[...]
    lane = lax.broadcasted_iota(I32, (tm, N_EXPERTS), 1)
    e1 = jnp.where(lane == eid[:, 0:1], 1.0, 0.0)
    e2 = jnp.where(lane == eid[:, 1:2], 1.0, 0.0)
    before = lax.broadcasted_iota(I32, (tm, tm), 1) < lax.broadcasted_iota(I32, (tm, tm), 0)
    slot = _dg(jnp.where(before, 1.0, 0.0).astype(BF16), (e1 + e2).astype(BF16)) + loffr_ref[0]
    l1 = jnp.sum(slot * e1, axis=1, keepdims=True).astype(I32)
    l2 = jnp.sum(slot * e2, axis=1, keepdims=True).astype(I32)
    cols = lax.broadcasted_iota(I32, (tm, lbuf), 1)
    gate = gate_ref[...]
    pick = jnp.where(cols == l1, gate[:, 0:1], jnp.where(cols == l2, gate[:, 1:2], 0.0)).astype(BF16)

    _for_each_run_chunk(nch_ref, base, lambda e, c: chunk_copy(e, c).wait())
    h2 = h1_ref[...] + _dg(pick, buf[...].astype(BF16))
    pg = _sigmoid(_dg(_rms(h2, nple_ref[...]).astype(BF16), wpg_ref[...]))
    h3 = h2 + pg * _dg(p_ref[...].astype(BF16), wpp_ref[...])
    y_out[...] = _rms(h3, nfin_ref[...])


def _final(plan, h1, eid, gate, p2, yb, wd, tm, tile0):
    m, d = h1.shape
    pd = p2.shape[1]
    lbuf = 2 * tm + LOCAL_PAD
    weights = [wd[n] for n in ('norm_ple', 'w_ple_gate', 'w_ple_proj', 'norm_final')]
    imap = lambda f: (lambda i, *_: f(i))
    return pl.pallas_call(
        functools.partial(_final_kernel, tm=tm, lbuf=lbuf, tile0=tile0),
        grid_spec=pltpu.PrefetchScalarGridSpec(
            num_scalar_prefetch=3,
            grid=(m // tm,),
            in_specs=[pl.BlockSpec((tm, d), imap(lambda i: (i, 0))),
                      pl.BlockSpec((tm, ROUTER_LANES), imap(lambda i: (i, 0))),
                      pl.BlockSpec((tm, ROUTER_LANES), imap(lambda i: (i, 0))),
                      pl.BlockSpec((1, 1, N_EXPERTS), imap(lambda i: (tile0 + i, 0, 0))),
                      pl.BlockSpec((tm, pd), imap(lambda i: (i, 0))),
                      pl.BlockSpec(memory_space=pl.ANY)] +
                     [pl.BlockSpec(w.shape, imap(lambda i, n=len(w.shape): (0,) * n)) for w in weights],
            out_specs=pl.BlockSpec((tm, d), imap(lambda i: (i, 0))),
            scratch_shapes=[pltpu.VMEM((lbuf, d), F32), pltpu.SemaphoreType.DMA],
        ),
        out_shape=jax.ShapeDtypeStruct((m, d), F32),
        compiler_params=pltpu.CompilerParams(dimension_semantics=("arbitrary",), vmem_limit_bytes=VMEM_LIMIT),
        name="moe_final",
    )(plan['nch'], plan['off'], plan['loff'], h1, eid, gate, plan['loff_row'], p2, yb, *weights)


def _route_plan(eids, tms, bm):
    experts = jnp.arange(N_EXPERTS, dtype=I32)
    counts = []
    for eid, tm in zip(eids, tms):
        onehot = (eid[:, :2, None] == experts).astype(I32)
        counts.append(onehot.reshape(-1, 2 * tm, N_EXPERTS).sum(axis=1))
    n = jnp.concatenate(counts)
    n_al = (n + RUN_ALIGN - 1) // RUN_ALIGN * RUN_ALIGN
    loff = jnp.cumsum(n_al, axis=1) - n_al
    region = (n_al.sum(axis=0) + bm - 1) // bm * bm
    pend = jnp.cumsum(region)
    off = (pend - region)[None, :] + jnp.cumsum(n_al, axis=0) - n_al
    n_assign = sum(2 * e.shape[0] for e in eids)
    n_blocks = -(-(n_assign + (RUN_ALIGN - 1) * N_EXPERTS * n.shape[0] + N_EXPERTS * (bm - 1)) // bm)
    block_start = jnp.arange(n_blocks, dtype=I32) * bm
    block_expert = jnp.minimum(jnp.sum((pend[None, :] <= block_start[:, None]).astype(I32), axis=1), N_EXPERTS - 1)
    plan = dict(nch=(n_al // RUN_ALIGN).reshape(-1).astype(I32), off=off.reshape(-1).astype(I32),
                loff=loff.reshape(-1).astype(I32), loff_col=loff.astype(F32)[:, :, None],
                loff_row=loff.astype(F32)[:, None, :])
    return plan, block_expert.astype(I32), (pend[-1] // bm).astype(I32).reshape(1), n_blocks


def _layer_front(x, shift0, wkv0, conv0, wd, tm_in, tm_tok, wkv_tile, wkv_chunk, chained):
    bsz, t, d = x.shape
    r, lw, k2, v, ah, bh, g, bonus, yb, shift_new, conv_new = _mix_in(x, shift0, conv0, wd, tm_in)
    y, s_new = _wkv(r, lw, k2, v, ah, bh, jnp.swapaxes(wkv0, -1, -2), wkv_tile, wkv_chunk, chained)
    flat = lambda z: z.reshape(bsz * t, z.shape[-1])
    h1, xn2, eid, gate = _mix_out(flat(x), flat(y), flat(bonus), flat(g), flat(yb), wd, tm_tok)
    return h1, xn2, eid, gate, shift_new.reshape(bsz, d), jnp.swapaxes(s_new, -1, -2), conv_new


def kernel(x_prompt, x_sample, state_shift, state_wkv, cache_conv, p_prompt, p_sample, norm_mix, w_in, mu_rkv, mu_w, mu_a, mu_g, w0, w1, w2, a0, a1, a2, g1, g2, k_k, k_a, r_k, ln_x_w, ln_x_b, dw_w, dw_b, cln_w, cln_b, w_out, norm_ffn, w_router_group, w_router_expert, w_exp_gate, w_exp_up, w_exp_down, norm_ple, w_ple_gate, w_ple_proj, norm_final):
    depth = norm_mix.shape[0]
    assert depth == 1
    d = x_prompt.shape[-1]
    c = w0.shape[-1]
    row = lambda z: z[0].reshape(1, -1).astype(F32)
    lane = jnp.arange(c, dtype=I32) // HEAD_SIZE
    wd = dict(
        norm_mix=row(norm_mix), w_in=w_in[0].astype(BF16), mu_rkv=row(mu_rkv), mu_w=row(mu_w), mu_a=row(mu_a),
        mu_g=row(mu_g), w0=row(w0), w1=w1[0].astype(BF16), w2=w2[0].astype(BF16), a0=row(a0),
        a1=a1[0].astype(BF16), a2=a2[0].astype(BF16), g1=g1[0].astype(BF16), g2=g2[0].astype(BF16),
        k_k=row(k_k), k_a=row(k_a), r_k=row(r_k), ln_x_w=row(ln_x_w), ln_x_b=row(ln_x_b),
        seg=(lane[:, None] == lane[None, :]).astype(BF16),
        dw_w=dw_w[0].astype(F32), dw_b=row(dw_b), cln_w=row(cln_w), cln_b=row(cln_b),
        w_out=w_out[0].astype(BF16), norm_ffn=row(norm_ffn),
        w_router=jnp.concatenate([w_router_expert[0], w_router_group[0],
                                  jnp.zeros((d, ROUTER_LANES - N_EXPERTS - N_EXPERT_GROUPS), F32)], axis=1),
        norm_ple=row(norm_ple), w_ple_gate=w_ple_gate[0].astype(BF16), w_ple_proj=w_ple_proj[0].astype(BF16),
        norm_final=norm_final.reshape(1, -1).astype(F32),
    )
    bp, tp, _ = x_prompt.shape
    bs, ts, _ = x_sample.shape
    mp, ms = bp * tp, bs * ts
    tm_p = min(256, tp)
    tm_s = min(256, ms)

    zeros = lambda *s: jnp.zeros(s, F32)
    h1_p, xn_p, eid_p, gate_p, shift_p, wkv_p, conv_p = _layer_front(
        x_prompt, zeros(bp, 1, d), zeros(bp, c // HEAD_SIZE, HEAD_SIZE, HEAD_SIZE), zeros(bp, CONV_CARRY, c),
        wd, tm_p, tm_p, min(128, tp), min(64, tp), True)
    h1_s, xn_s, eid_s, gate_s, shift_s, wkv_s, conv_s = _layer_front(
        x_sample, state_shift[0][:, None, :], state_wkv[0], cache_conv[0],
        wd, ts, tm_s, min(128, ms), ts, False)

    bm = 256
    tr_p, tr_s = min(512, mp), min(512, ms)
    plan, block_expert, n_used, n_blocks = _route_plan([eid_p, eid_s], [tr_p, tr_s], bm)
    tiles_p = mp // tr_p
    xs = jnp.zeros((n_blocks * bm, d), F32)
    xs = _dispatch(plan, eid_p[:, :2].T, xn_p, xs, tr_p, 0)
    xs = _dispatch(plan, eid_s[:, :2].T, xn_s, xs, tr_s, tiles_p)
    yb = _experts(block_expert, n_used, xs, w_exp_gate[0], w_exp_up[0], w_exp_down[0], bm)
    y_p = _final(plan, h1_p, eid_p, gate_p, p_prompt[0].reshape(mp, -1), yb, wd, tr_p, 0)
    y_s = _final(plan, h1_s, eid_s, gate_s, p_sample[0].reshape(ms, -1), yb, wd, tr_s, tiles_p)
    return (y_p.reshape(x_prompt.shape), y_s.reshape(x_sample.shape), shift_p[None], wkv_p[None], conv_p[None],
            shift_s[None], wkv_s[None], conv_s[None])
```

```python
import functools

import jax
import jax.numpy as jnp
from jax import lax
from jax.experimental import pallas as pl
from jax.experimental.pallas import tpu as pltpu

F32 = jnp.float32
BF16 = jnp.bfloat16
I32 = jnp.int32

HEAD_SIZE = 64
CONV_WIDTH = 31
CONV_CARRY = CONV_WIDTH - 1
SUBLANES = 8
CARRY_PAD = 32
N_EXPERT_GROUPS = 4
EXPERTS_PER_GROUP = 8
N_EXPERTS = N_EXPERT_GROUPS * EXPERTS_PER_GROUP
ROUTER_LANES = 128
RMS_EPS = 1e-6
LN_EPS = 1e-5
GN_EPS = 64e-5
DECAY_SCALE = 0.6065306597126334
INV_BASE = 16
RUN_ALIGN = 8
LOCAL_PAD = N_EXPERTS * RUN_ALIGN
VMEM_LIMIT = 56 * 1024 * 1024

NN = ((1,), (0,))
NT = ((1,), (1,))
TN = ((0,), (0,))


def _dg(a, b, dims=NN):
    return lax.dot_general(a, b, (dims, ((), ())), preferred_element_type=F32)


def _split2(x):
    hi = x.astype(BF16)
    lo = (x - hi.astype(F32)).astype(BF16)
    return hi, lo


def _split3(x):
    hi = x.astype(BF16)
    r1 = x - hi.astype(F32)
    mid = r1.astype(BF16)
    lo = (r1 - mid.astype(F32)).astype(BF16)
    return hi, mid, lo


def _dot3(a, b, dims=NN):
    ah, al = _split2(a)
    bh, bl = _split2(b)
    return _dg(ah, bh, dims) + (_dg(al, bh, dims) + _dg(ah, bl, dims))


def _bdot(a, b, dims=NN):
    return _dg(a.astype(BF16), b.astype(BF16), dims)


def _mask_dot(mask_bf16, x):
    h, m, l = _split3(x)
    return _dg(mask_bf16, h) + (_dg(mask_bf16, m) + _dg(mask_bf16, l))


def _seg_sum(x, seg_bf16):
    h, l = _split2(x)
    return _dg(h, seg_bf16) + _dg(l, seg_bf16)


def _rms(x, g):
    return x * lax.rsqrt(jnp.mean(x * x, axis=-1, keepdims=True) + RMS_EPS) * g


def _sigmoid(x):
    return 0.5 * jnp.tanh(0.5 * x) + 0.5


def _full(shape):
    n = len(shape)
    return pl.BlockSpec(shape, lambda *_: (0,) * n)


def _mix_in_kernel(x_ref, shift_ref, conv_ref, nm_ref, win_ref, murkv_ref, muw_ref, mua_ref, mug_ref,
                   w0_ref, w1_ref, w2_ref, a0_ref, a1_ref, a2_ref, g1_ref, g2_ref,
                   kk_ref, ka_ref, rk_ref, seg_ref, dww_ref, dwb_ref, clnw_ref, clnb_ref,
                   r_out, lw_out, k_out, v_out, a_out, b_out, g_out, bonus_out, yb_out, shift_out, conv_out,
                   xn_last, h_last, up_ext, shifted, *, tm, c):
    i = pl.program_id(1)

    @pl.when(i == 0)
    def _():
        sp = shift_ref[0]
        xn_last[...] = sp
        sp8 = jnp.broadcast_to(sp, (8, sp.shape[1])).astype(BF16)
        h_last[...] = _dg(sp8, win_ref[:, :3 * c])[0:1]
        up_ext[CARRY_PAD - CONV_CARRY:CARRY_PAD, :] = conv_ref[0]

    x = x_ref[0]
    xn = _rms(x, nm_ref[...])
    hin = _dg(xn.astype(BF16), win_ref[...])
    first = lax.broadcasted_iota(I32, (tm, 1), 0) == 0
    xprev = jnp.where(first, xn_last[...], pltpu.roll(xn, 1, 0))
    dx = xprev - xn
    h_rkv = hin[:, :3 * c]
    hprev = jnp.where(first, h_last[...], pltpu.roll(h_rkv, 1, 0))
    rkv = h_rkv + (hprev - h_rkv) * murkv_ref[...]
    r = rkv[:, :c]
    k = rkv[:, c:2 * c]
    v = rkv[:, 2 * c:]
    xw = (xn + dx * muw_ref[...]).astype(BF16)
    xa = (xn + dx * mua_ref[...]).astype(BF16)
    xg = (xn + dx * mug_ref[...]).astype(BF16)
    zw = w0_ref[...] + _dg(jnp.tanh(_dg(xw, w1_ref[...])).astype(BF16), w2_ref[...])
    a = _sigmoid(a0_ref[...] + _dg(_dg(xa, a1_ref[...]).astype(BF16), a2_ref[...]))
    g = _dg(_sigmoid(_dg(xg, g1_ref[...])).astype(BF16), g2_ref[...])
    seg = seg_ref[...]
    kk = k * kk_ref[...]
    kk = kk * jnp.minimum(lax.rsqrt(_seg_sum(kk * kk, seg)), 1e12)
    k2 = k * (1.0 + (a - 1.0) * ka_ref[...])
    r_out[0] = r
    lw_out[0] = -DECAY_SCALE * _sigmoid(zw)
    k_out[0] = k2
    v_out[0] = v
    a_out[0] = -kk
    b_out[0] = kk * a
    g_out[0] = g
    bonus_out[0] = _seg_sum(r * k2 * rk_ref[...], seg) * v

    u = hin[:, 3 * c:4 * c] * _sigmoid(hin[:, 4 * c:])
    up_ext[CARRY_PAD:CARRY_PAD + tm, :] = u
    first_row = CARRY_PAD - CONV_CARRY
    for s in range(SUBLANES):
        span = tm + (CONV_WIDTH - 1 - s) // SUBLANES * SUBLANES
        shifted[s, 0:span, :] = up_ext[pl.ds(first_row + s, span), :]
    z = jnp.zeros_like(u) + dwb_ref[...]
    for j in range(CONV_WIDTH):
        s, m = j % SUBLANES, j // SUBLANES
        z = z + dww_ref[j:j + 1, :] * shifted[s, m * SUBLANES:m * SUBLANES + tm, :]
    mu = jnp.mean(z, axis=-1, keepdims=True)
    zc = z - mu
    var = jnp.mean(zc * zc, axis=-1, keepdims=True)
    zn = zc * lax.rsqrt(var + LN_EPS) * clnw_ref[...] + clnb_ref[...]
    yb_out[0] = zn * _sigmoid(zn)

    tail = up_ext[pl.ds(tm + CARRY_PAD - CONV_CARRY, CONV_CARRY), :]
    up_ext[CARRY_PAD - CONV_CARRY:CARRY_PAD, :] = tail
    conv_out[0] = tail
    xn_last[...] = xn[tm - 1:tm]
    h_last[...] = h_rkv[tm - 1:tm]
    shift_out[0] = xn[tm - 1:tm]


def _mix_in(x, shift0, conv0, wd, tm):
    bsz, t, d = x.shape
    c = wd['w0'].shape[1]
    grid = (bsz, t // tm)
    tok = lambda w: pl.BlockSpec((1, tm, w), lambda b, i: (b, i, 0))
    per_seq = lambda rows, w: pl.BlockSpec((1, rows, w), lambda b, i: (b, 0, 0))
    weights = [wd[n] for n in ('norm_mix', 'w_in', 'mu_rkv', 'mu_w', 'mu_a', 'mu_g', 'w0', 'w1', 'w2', 'a0', 'a1',
                               'a2', 'g1', 'g2', 'k_k', 'k_a', 'r_k', 'seg', 'dw_w', 'dw_b', 'cln_w', 'cln_b')]
    out_tok = jax.ShapeDtypeStruct((bsz, t, c), F32)
    outs = pl.pallas_call(
        functools.partial(_mix_in_kernel, tm=tm, c=c),
        grid=grid,
        in_specs=[tok(d), per_seq(1, d), per_seq(CONV_CARRY, c)] + [_full(w.shape) for w in weights],
        out_specs=[tok(c)] * 9 + [per_seq(1, d), per_seq(CONV_CARRY, c)],
        out_shape=[out_tok] * 9 + [jax.ShapeDtypeStruct((bsz, 1, d), F32),
                                   jax.ShapeDtypeStruct((bsz, CONV_CARRY, c), F32)],
        scratch_shapes=[pltpu.VMEM((1, d), F32), pltpu.VMEM((1, 3 * c), F32),
                        pltpu.VMEM((tm + CARRY_PAD, c), F32),
                        pltpu.VMEM((SUBLANES, tm + CARRY_PAD - SUBLANES, c), F32)],
        compiler_params=pltpu.CompilerParams(dimension_semantics=("arbitrary", "arbitrary"),
                                             vmem_limit_bytes=VMEM_LIMIT),
        name="mix_in",
    )(x, shift0, conv0, *weights)
    return outs


def _tri_inverse(n_strict, row, col, eye, lg_chunk):
    lg_base = INV_BASE.bit_length() - 1
    same = lambda sh: (row >> sh) == (col >> sh)
    lg0 = min(lg_base, lg_chunk)
    p = [jnp.where(same(lg0), n, 0.0) for n in n_strict]
    t = [jnp.where(eye, 1.0, 0.0) + x for x in p]
    for _ in range(lg0 - 1):
        p = [_bdot(x, x) for x in p]
        t = [x + _bdot(x, y) for x, y in zip(t, p)]
    for lg in range(lg0, lg_chunk):
        off_mask = same(lg + 1) & jnp.logical_not(same(lg))
        u = [_bdot(x, jnp.where(off_mask, n, 0.0)) for x, n in zip(t, n_strict)]
        t = [x + _bdot(y, x) for x, y in zip(t, u)]
    return t


def _wkv_kernel(r_ref, lw_ref, k_ref, v_ref, a_ref, b_ref, s0_ref, y_ref, s_out, s_scr, *, tt, chunk, chained):
    ti = pl.program_id(1)
    n_heads = r_ref.shape[2] // HEAD_SIZE
    n_chunks = tt // chunk
    lg_chunk = chunk.bit_length() - 1
    row = lax.broadcasted_iota(I32, (tt, tt), 0)
    col = lax.broadcasted_iota(I32, (tt, tt), 1)
    in_chunk = (row >> lg_chunk) == (col >> lg_chunk)
    eye = row == col
    tri_incl = in_chunk & (col <= row)
    tri_strict = in_chunk & (col < row)
    m_cum = jnp.where(tri_incl, 1.0, 0.0).astype(BF16)
    m_tot = jnp.where(in_chunk, 1.0, 0.0).astype(BF16)
    row_h = lax.broadcasted_iota(I32, (HEAD_SIZE, HEAD_SIZE), 0)
    col_h = lax.broadcasted_iota(I32, (HEAD_SIZE, HEAD_SIZE), 1)
    eye_h = row_h == col_h

    if chained:
        @pl.when(ti == 0)
        def _():
            s_scr[...] = s0_ref[0]

    lw_all = lw_ref[0]
    k_all = k_ref[0]
    b_all = b_ref[0]
    cum = _mask_dot(m_cum, lw_all)
    tot = _mask_dot(m_tot, lw_all)
    e_neg = jnp.exp(-cum)
    e_end = jnp.exp(tot - cum)
    rt_all = r_ref[0] * jnp.exp(cum)
    at_all = a_ref[0] * jnp.exp(cum - lw_all)
    kt_all = k_all * e_neg
    bt_all = b_all * e_neg
    bd_all = b_all * e_end
    kd_all = k_all * e_end
    g_end_all = jnp.exp(tot)

    heads = range(n_heads)
    hsl = [slice(HEAD_SIZE * h, HEAD_SIZE * (h + 1)) for h in heads]
    cut = lambda z: [z[:, s_].astype(BF16) for s_ in hsl]
    v, rt, at, kt, bt, bd, kd = (cut(z) for z in (v_ref[0], rt_all, at_all, kt_all, bt_all, bd_all, kd_all))
    mm = [_dg(jnp.concatenate([at[h], rt[h]], axis=0), jnp.concatenate([bt[h], kt[h]], axis=0), NT) for h in heads]
    m_ab = [jnp.where(tri_strict, mm[h][:tt, :tt], 0.0) for h in heads]
    m_ak = [jnp.where(tri_strict, mm[h][:tt, tt:], 0.0).astype(BF16) for h in heads]
    m_rb = [jnp.where(tri_incl, mm[h][tt:, :tt], 0.0).astype(BF16) for h in heads]
    m_rk = [jnp.where(tri_incl, mm[h][tt:, tt:], 0.0).astype(BF16) for h in heads]
    tinv = [x.astype(BF16) for x in _tri_inverse(m_ab, row, col, eye, lg_chunk)]
    akv = [_dg(m_ak[h], v[h]).astype(BF16) for h in heads]
    w1 = [_dg(tinv[h], at[h]).astype(BF16) for h in heads]
    w2 = [_dg(tinv[h], akv[h]).astype(BF16) for h in heads]
    q = [(rt_all[:, hsl[h]] + _dg(m_rb[h], w1[h])).astype(BF16) for h in heads]
    y0 = [_dg(m_rb[h], w2[h]) + _dg(m_rk[h], v[h]) for h in heads]

    if chained:
        s = [s_scr[h] for h in heads]
    for ci in range(n_chunks):
        cs = slice(ci * chunk, (ci + 1) * chunk)
        if not chained:
            s = [s0_ref[ci, h] for h in heads]
        g_row = g_end_all[ci * chunk:ci * chunk + 1]
        gm = [jnp.where(eye_h, jnp.broadcast_to(g_row[:, hsl[h]], (HEAD_SIZE, HEAD_SIZE)), 0.0)
              + _dg(bd[h][cs], w1[h][cs], TN) for h in heads]
        hm = [_dg(bd[h][cs], w2[h][cs], TN) + _dg(kd[h][cs], v[h][cs], TN) for h in heads]
        for h in heads:
            y_ref[0, cs, hsl[h]] = _bdot(q[h][cs], s[h]) + y0[h][cs]
        s = [_dot3(gm[h], s[h]) + hm[h] for h in heads]
        if not chained:
            for h in heads:
                s_out[ci, h] = s[h]
    if chained:
        for h in heads:
            s_scr[h] = s[h]
            s_out[0, h] = s[h]


def _wkv(r, lw, k, v, a, b, s0t, tt, chunk, chained):
    bsz, t, c = r.shape
    n_heads = c // HEAD_SIZE
    hs = HEAD_SIZE
    if chained:
        grid = (bsz, t // tt)
        tok = pl.BlockSpec((1, tt, c), lambda bi, ti: (bi, ti, 0))
        st = pl.BlockSpec((1, n_heads, hs, hs), lambda bi, ti: (bi, 0, 0, 0))
        args = (r, lw, k, v, a, b)
        y_shape = (bsz, t, c)
    else:
        assert t == chunk and (bsz * t) % tt == 0
        n_seq = tt // chunk
        grid = (1, bsz * t // tt)
        tok = pl.BlockSpec((1, tt, c), lambda bi, ti: (0, ti, 0))
        st = pl.BlockSpec((n_seq, n_heads, hs, hs), lambda bi, ti: (ti, 0, 0, 0))
        args = tuple(z.reshape(1, bsz * t, c) for z in (r, lw, k, v, a, b))
        y_shape = (1, bsz * t, c)
    y, s_new = pl.pallas_call(
        functools.partial(_wkv_kernel, tt=tt, chunk=chunk, chained=chained),
        grid=grid,
        in_specs=[tok] * 6 + [st],
        out_specs=[tok, st],
        out_shape=[jax.ShapeDtypeStruct(y_shape, F32), jax.ShapeDtypeStruct(s0t.shape, F32)],
        scratch_shapes=[pltpu.VMEM((n_heads, hs, hs), F32)],
        compiler_params=pltpu.CompilerParams(dimension_semantics=("arbitrary",) * 2,
                                             vmem_limit_bytes=VMEM_LIMIT),
        name="wkv",
    )(*args, s0t)
    return y.reshape(bsz, t, c), s_new


def _mix_out_kernel(x_ref, y_ref, bonus_ref, g_ref, yb_ref, lnw_ref, lnb_ref, seg_ref, wout_ref, nffn_ref, wr_ref,
                    h1_out, xn_out, eid_out, gate_out, *, c):
    seg = seg_ref[...]
    y = y_ref[...]
    inv_n = 1.0 / HEAD_SIZE
    mu = _seg_sum(y, seg) * inv_n
    yc = y - mu
    var = _seg_sum(yc * yc, seg) * inv_n
    yn = yc * lax.rsqrt(var + GN_EPS) * lnw_ref[...] + lnb_ref[...]
    ya = (yn + bonus_ref[...]) * g_ref[...]
    mix = _dg(ya.astype(BF16), wout_ref[:c, :]) + _dg(yb_ref[...].astype(BF16), wout_ref[c:, :])
    h1 = x_ref[...] + mix
    h1_out[...] = h1
    xn = _rms(h1, nffn_ref[...])
    xn_out[...] = xn.astype(BF16)

    logits = _dot3(xn, wr_ref[...])
    lane = lax.broadcasted_iota(I32, logits.shape, 1)
    neg = jnp.float32(-jnp.inf)
    is_g = (lane >= N_EXPERTS) & (lane < N_EXPERTS + N_EXPERT_GROUPS)
    glog = jnp.where(is_g, logits, neg)
    gmax = jnp.max(glog, axis=-1, keepdims=True)
    gsel = jnp.min(jnp.where(glog == gmax, lane, 4 * ROUTER_LANES), axis=-1, keepdims=True) - N_EXPERTS
    gp = 1.0 / jnp.sum(jnp.where(is_g, jnp.exp(glog - gmax), 0.0), axis=-1, keepdims=True)
    in_grp = (lane >= gsel * EXPERTS_PER_GROUP) & (lane < (gsel + 1) * EXPERTS_PER_GROUP)
    elog = jnp.where(in_grp, logits, neg)
    emax = jnp.max(elog, axis=-1, keepdims=True)
    ex = jnp.where(in_grp, jnp.exp(elog - emax), 0.0)
    eprob = ex / jnp.sum(ex, axis=-1, keepdims=True)
    eprob = jnp.where(in_grp, eprob, -1.0)
    v1 = jnp.max(eprob, axis=-1, keepdims=True)
    i1 = jnp.min(jnp.where(eprob == v1, lane, 4 * ROUTER_LANES), axis=-1, keepdims=True)
    rest = jnp.where(lane == i1, -1.0, eprob)
    v2 = jnp.max(rest, axis=-1, keepdims=True)
    i2 = jnp.min(jnp.where(rest == v2, lane, 4 * ROUTER_LANES), axis=-1, keepdims=True)
    denom = v1 + v2
    eid_out[...] = jnp.where(lane == 0, i1, jnp.where(lane == 1, i2, 0))
    gate_out[...] = jnp.where(lane == 0, gp * v1 / denom, jnp.where(lane == 1, gp * v2 / denom, 0.0))


def _mix_out(x2, y2, bonus2, g2, yb2, wd, tm):
    m, d = x2.shape
    c = y2.shape[1]
    tokd = pl.BlockSpec((tm, d), lambda i: (i, 0))
    tokc = pl.BlockSpec((tm, c), lambda i: (i, 0))
    tokr = pl.BlockSpec((tm, ROUTER_LANES), lambda i: (i, 0))
    weights = [wd[n] for n in ('ln_x_w', 'ln_x_b', 'seg', 'w_out', 'norm_ffn', 'w_router')]
    return pl.pallas_call(
        functools.partial(_mix_out_kernel, c=c),
        grid=(m // tm,),
        in_specs=[tokd, tokc, tokc, tokc, tokc] + [_full(w.shape) for w in weights],
        out_specs=[tokd, tokd, tokr, tokr],
        out_shape=[jax.ShapeDtypeStruct((m, d), F32), jax.ShapeDtypeStruct((m, d), BF16),
                   jax.ShapeDtypeStruct((m, ROUTER_LANES), I32), jax.ShapeDtypeStruct((m, ROUTER_LANES), F32)],
        compiler_params=pltpu.CompilerParams(dimension_semantics=("arbitrary",), vmem_limit_bytes=VMEM_LIMIT),
        name="mix_out",
    )(x2, y2, bonus2, g2, yb2, *weights)


def _for_each_run_chunk(nch_ref, base, fn):
    def per_expert(e, carry):
        def per_chunk(c, carry2):
            fn(e, c)
            return carry2
        return lax.fori_loop(0, nch_ref[base + e], per_chunk, carry)
    lax.fori_loop(0, N_EXPERTS, per_expert, 0)


def _dispatch_kernel(nch_ref, off_ref, loff_ref, tn_ref, toff_ref, nu_ref, *refs, groups, bm, n_blocks):
    n_g = len(groups)
    eid_refs, xn_refs = refs[:n_g], refs[n_g + 1:2 * n_g + 1]
    loffc_ref = refs[n_g]
    xs_out, buf, sem, zbuf, zsem = refs[2 * n_g + 1:]
    i = pl.program_id(0)
    last = pl.num_programs(0) - 1

    def chunk_copy(tile, e, c):
        base = tile * N_EXPERTS
        src = buf.at[tile % 2, pl.ds(pl.multiple_of(loff_ref[base + e] + c * RUN_ALIGN, RUN_ALIGN), RUN_ALIGN)]
        dst = xs_out.at[pl.ds(pl.multiple_of(off_ref[base + e] + c * RUN_ALIGN, RUN_ALIGN), RUN_ALIGN)]
        return pltpu.make_async_copy(src, dst, sem.at[tile % 2])

    def wait_tile(tile):
        _for_each_run_chunk(nch_ref, tile * N_EXPERTS, lambda e, c: chunk_copy(tile, e, c).wait())

    @pl.when(i >= 2)
    def _():
        wait_tile(i - 2)

    def sort_tile(eid_ref, xn_ref, tm):
        e_rows = eid_ref[...]
        sub = lax.broadcasted_iota(I32, (N_EXPERTS, tm), 0)
        e1 = jnp.where(sub == e_rows[0:1], 1.0, 0.0)
        e2 = jnp.where(sub == e_rows[1:2], 1.0, 0.0)
        before = lax.broadcasted_iota(I32, (tm, tm), 0) < lax.broadcasted_iota(I32, (tm, tm), 1)
        slot = _dg((e1 + e2).astype(BF16), jnp.where(before, 1.0, 0.0).astype(BF16)) + loffc_ref[0]
        l1 = jnp.sum(slot * e1, axis=0, keepdims=True).astype(I32)
        l2 = jnp.sum(slot * e2, axis=0, keepdims=True).astype(I32)
        n_rows = 2 * tm + LOCAL_PAD
        rows = lax.broadcasted_iota(I32, (n_rows, tm), 0)
        perm = jnp.where((rows == l1) | (rows == l2), 1.0, 0.0).astype(BF16)
        buf[i % 2, 0:n_rows, :] = _dg(perm, xn_ref[...])

    first = 0
    for g, (tm, n_tiles) in enumerate(groups):
        pl.when((i >= first) & (i < first + n_tiles))(functools.partial(sort_tile, eid_refs[g], xn_refs[g], tm))
        first += n_tiles
    _for_each_run_chunk(nch_ref, i * N_EXPERTS, lambda e, c: chunk_copy(i, e, c).start())

    @pl.when(i == 0)
    def _():
        zbuf[...] = jnp.zeros_like(zbuf)
        half = zbuf.shape[0]

        def zero_copy(off, rows_):
            return pltpu.make_async_copy(zbuf.at[pl.ds(0, rows_)], xs_out.at[pl.ds(off, rows_)], zsem)

        def unused_blocks(fn):
            def body(b, carry):
                fn(pl.multiple_of(b * bm, bm), half)
                fn(pl.multiple_of(b * bm + half, half), half)
                return carry
            lax.fori_loop(nu_ref[0], n_blocks, body, 0)

        _for_each_tail_piece(tn_ref, toff_ref, half, lambda off, r_: zero_copy(off, r_).start())
        unused_blocks(lambda off, r_: zero_copy(off, r_).start())
        _for_each_tail_piece(tn_ref, toff_ref, half, lambda off, r_: zero_copy(off, r_).wait())
        unused_blocks(lambda off, r_: zero_copy(off, r_).wait())

    @pl.when(i == last)
    def _():
        @pl.when(i >= 1)
        def _():
            wait_tile(i - 1)
        wait_tile(i)


def _for_each_tail_piece(tn_ref, toff_ref, max_rows, fn):
    def per_expert(e, carry):
        n_al = tn_ref[e]
        off = toff_ref[e]
        rows = max_rows
        while rows >= RUN_ALIGN:
            bit = rows // RUN_ALIGN
            has = (n_al & bit) != 0

            @pl.when(has)
            def _(off=off, rows=rows):
                fn(pl.multiple_of(off, RUN_ALIGN), rows)
            off = off + jnp.where(has, rows, 0)
            rows //= 2
        return carry
    lax.fori_loop(0, N_EXPERTS, per_expert, 0)


def _dispatch(plan, n_used, eids_t, xns, tms, n_blocks, bm):
    d = xns[0].shape[1]
    groups = tuple((tm, xn.shape[0] // tm) for xn, tm in zip(xns, tms))
    firsts = [sum(n for _, n in groups[:g]) for g in range(len(groups))]
    lbuf = 2 * max(tms) + LOCAL_PAD

    def tile_of(g):
        return lambda i: jnp.clip(i - firsts[g], 0, groups[g][1] - 1)

    imap = lambda f: (lambda i, *_: f(i))
    in_specs = ([pl.BlockSpec((2, tm), imap(lambda i, g=g: (0, tile_of(g)(i)))) for g, tm in enumerate(tms)] +
                [pl.BlockSpec((1, N_EXPERTS, 1), imap(lambda i: (i, 0, 0)))] +
                [pl.BlockSpec((tm, d), imap(lambda i, g=g: (tile_of(g)(i), 0))) for g, tm in enumerate(tms)])
    return pl.pallas_call(
        functools.partial(_dispatch_kernel, groups=groups, bm=bm, n_blocks=n_blocks),
        grid_spec=pltpu.PrefetchScalarGridSpec(
            num_scalar_prefetch=6,
            grid=(sum(n for _, n in groups),),
            in_specs=in_specs,
            out_specs=pl.BlockSpec(memory_space=pl.ANY),
            scratch_shapes=[pltpu.VMEM((2, lbuf, d), F32), pltpu.SemaphoreType.DMA((2,)),
                            pltpu.VMEM((bm // 2, d), F32), pltpu.SemaphoreType.DMA],
        ),
        out_shape=jax.ShapeDtypeStruct((n_blocks * bm, d), F32),
        compiler_params=pltpu.CompilerParams(dimension_semantics=("arbitrary",), vmem_limit_bytes=VMEM_LIMIT),
        name="moe_dispatch",
    )(plan['nch'], plan['off'], plan['loff'], plan['tail_n'], plan['tail_off'], n_used, *eids_t, plan['loff_col'],
      *xns)


def _experts_kernel(be_ref, nu_ref, xs_ref, wg_ref, wu_ref, wd_ref, yb_ref, wg_b, wu_b, wd_b):
    b = pl.program_id(0)

    @pl.when((b == 0) | (be_ref[b] != be_ref[jnp.maximum(b - 1, 0)]))
    def _():
        wg_b[...] = wg_ref[0].astype(BF16)
        wu_b[...] = wu_ref[0].astype(BF16)
        wd_b[...] = wd_ref[0].astype(BF16)

    @pl.when(b < nu_ref[0])
    def _():
        xb = xs_ref[...].astype(BF16)
        hg = _dg(xb, wg_b[...])
        hu = _dg(xb, wu_b[...])
        act = (hg * _sigmoid(hg) * hu).astype(BF16)
        yb_ref[...] = _dg(act, wd_b[...])

    @pl.when(pl.program_id(0) >= nu_ref[0])
    def _():
        yb_ref[...] = jnp.zeros_like(yb_ref)


def _experts(block_expert, n_used, xs, wg, wu, wdn, bm):
    p, d = xs.shape
    ff = wg.shape[2]
    n_blocks = p // bm
    rows = lambda b, be, nu: (jnp.minimum(b, nu[0] - 1), 0)
    return pl.pallas_call(
        _experts_kernel,
        grid_spec=pltpu.PrefetchScalarGridSpec(
            num_scalar_prefetch=2,
            grid=(n_blocks,),
            in_specs=[pl.BlockSpec((bm, d), rows),
                      pl.BlockSpec((1, d, ff), lambda b, be, nu: (be[b], 0, 0)),
                      pl.BlockSpec((1, d, ff), lambda b, be, nu: (be[b], 0, 0)),
                      pl.BlockSpec((1, ff, d), lambda b, be, nu: (be[b], 0, 0))],
            out_specs=pl.BlockSpec((bm, d), lambda b, be, nu: (b, 0)),
            scratch_shapes=[pltpu.VMEM((d, ff), BF16), pltpu.VMEM((d, ff), BF16), pltpu.VMEM((ff, d), BF16)],
        ),
        out_shape=jax.ShapeDtypeStruct((p, d), F32),
        compiler_params=pltpu.CompilerParams(dimension_semantics=("arbitrary",), vmem_limit_bytes=VMEM_LIMIT),
        name="moe_experts",
    )(block_expert, n_used, xs, wg, wu, wdn)


def _final_kernel(nch_ref, off_ref, loff_ref, h1_ref, eid_ref, gate_ref, loffr_ref, p_ref, yb_hbm,
                  nple_ref, wpg_ref, wpp_ref, nfin_ref, y_out, buf, sem, *, tm, lbuf, tile0):
    i = pl.program_id(0)

    def chunk_copy(tile, e, c):
        base = (tile0 + tile) * N_EXPERTS
        src = yb_hbm.at[pl.ds(pl.multiple_of(off_ref[base + e] + c * RUN_ALIGN, RUN_ALIGN), RUN_ALIGN)]
        dst = buf.at[tile % 2, pl.ds(pl.multiple_of(loff_ref[base + e] + c * RUN_ALIGN, RUN_ALIGN), RUN_ALIGN)]
        return pltpu.make_async_copy(src, dst, sem.at[tile % 2])

    def fetch(tile):
        _for_each_run_chunk(nch_ref, (tile0 + tile) * N_EXPERTS, lambda e, c: chunk_copy(tile, e, c).start())

    @pl.when(i == 0)
    def _():
        buf[...] = jnp.zeros_like(buf)
        fetch(i)

    @pl.when(i + 1 < pl.num_programs(0))
    def _():
        fetch(i + 1)

    eid = eid_ref[...]
    lane = lax.broadcasted_iota(I32, (tm, N_EXPERTS), 1)
    e1 = jnp.where(lane == eid[:, 0:1], 1.0, 0.0)
    e2 = jnp.where(lane == eid[:, 1:2], 1.0, 0.0)
    before = lax.broadcasted_iota(I32, (tm, tm), 1) < lax.broadcasted_iota(I32, (tm, tm), 0)
    slot = _dg(jnp.where(before, 1.0, 0.0).astype(BF16), (e1 + e2).astype(BF16)) + loffr_ref[0]
    l1 = jnp.sum(slot * e1, axis=1, keepdims=True).astype(I32)
    l2 = jnp.sum(slot * e2, axis=1, keepdims=True).astype(I32)
    cols = lax.broadcasted_iota(I32, (tm, lbuf), 1)
    gate = gate_ref[...]
    pick = jnp.where(cols == l1, gate[:, 0:1], jnp.where(cols == l2, gate[:, 1:2], 0.0)).astype(BF16)

    _for_each_run_chunk(nch_ref, (tile0 + i) * N_EXPERTS, lambda e, c: chunk_copy(i, e, c).wait())
    h2 = h1_ref[...] + _dg(pick, buf[i % 2].astype(BF16))
    pg = _sigmoid(_dg(_rms(h2, nple_ref[...]).astype(BF16), wpg_ref[...]))
    h3 = h2 + pg * _dg(p_ref[...].astype(BF16), wpp_ref[...])
    y_out[...] = _rms(h3, nfin_ref[...])


def _final(plan, h1, eid, gate, p2, yb, wd, tm, tile0):
    m, d = h1.shape
    pd = p2.shape[1]
    lbuf = 2 * tm + LOCAL_PAD
    weights = [wd[n] for n in ('norm_ple', 'w_ple_gate', 'w_ple_proj', 'norm_final')]
    imap = lambda f: (lambda i, *_: f(i))
    return pl.pallas_call(
        functools.partial(_final_kernel, tm=tm, lbuf=lbuf, tile0=tile0),
        grid_spec=pltpu.PrefetchScalarGridSpec(
            num_scalar_prefetch=3,
            grid=(m // tm,),
            in_specs=[pl.BlockSpec((tm, d), imap(lambda i: (i, 0))),
                      pl.BlockSpec((tm, ROUTER_LANES), imap(lambda i: (i, 0))),
                      pl.BlockSpec((tm, ROUTER_LANES), imap(lambda i: (i, 0))),
                      pl.BlockSpec((1, 1, N_EXPERTS), imap(lambda i: (tile0 + i, 0, 0))),
                      pl.BlockSpec((tm, pd), imap(lambda i: (i, 0))),
                      pl.BlockSpec(memory_space=pl.ANY)] +
                     [pl.BlockSpec(w.shape, imap(lambda i, n=len(w.shape): (0,) * n)) for w in weights],
            out_specs=pl.BlockSpec((tm, d), imap(lambda i: (i, 0))),
            scratch_shapes=[pltpu.VMEM((2, lbuf, d), F32), pltpu.SemaphoreType.DMA((2,))],
        ),
        out_shape=jax.ShapeDtypeStruct((m, d), F32),
        compiler_params=pltpu.CompilerParams(dimension_semantics=("arbitrary",), vmem_limit_bytes=VMEM_LIMIT),
        name="moe_final",
    )(plan['nch'], plan['off'], plan['loff'], h1, eid, gate, plan['loff_row'], p2, yb, *weights)


def _route_plan(eids, tms, bm):
    experts = jnp.arange(N_EXPERTS, dtype=I32)
    counts = []
    for eid, tm in zip(eids, tms):
        onehot = (eid[:, :2, None] == experts).astype(I32)
        counts.append(onehot.reshape(-1, 2 * tm, N_EXPERTS).sum(axis=1))
    n = jnp.concatenate(counts)
    n_al = (n + RUN_ALIGN - 1) // RUN_ALIGN * RUN_ALIGN
    loff = jnp.cumsum(n_al, axis=1) - n_al
    used = n_al.sum(axis=0)
    region = (used + bm - 1) // bm * bm
    pend = jnp.cumsum(region)
    off = (pend - region)[None, :] + jnp.cumsum(n_al, axis=0) - n_al
    n_assign = sum(2 * e.shape[0] for e in eids)
    n_blocks = -(-(n_assign + (RUN_ALIGN - 1) * N_EXPERTS * n.shape[0] + N_EXPERTS * (bm - 1)) // bm)
    block_start = jnp.arange(n_blocks, dtype=I32) * bm
    block_expert = jnp.minimum(jnp.sum((pend[None, :] <= block_start[:, None]).astype(I32), axis=1), N_EXPERTS - 1)
    plan = dict(nch=(n_al // RUN_ALIGN).reshape(-1).astype(I32), off=off.reshape(-1).astype(I32),
                loff=loff.reshape(-1).astype(I32), tail_n=((region - used) // RUN_ALIGN).astype(I32),
                tail_off=(pend - region + used).astype(I32), loff_col=loff.astype(F32)[:, :, None],
                loff_row=loff.astype(F32)[:, None, :])
    return plan, block_expert.astype(I32), (pend[-1] // bm).astype(I32).reshape(1), n_blocks


def _layer_front(x, shift0, wkv0, conv0, wd, tm_in, tm_tok, wkv_tile, wkv_chunk, chained):
    bsz, t, d = x.shape
    r, lw, k2, v, ah, bh, g, bonus, yb, shift_new, conv_new = _mix_in(x, shift0, conv0, wd, tm_in)
    y, s_new = _wkv(r, lw, k2, v, ah, bh, jnp.swapaxes(wkv0, -1, -2), wkv_tile, wkv_chunk, chained)
    flat = lambda z: z.reshape(bsz * t, z.shape[-1])
    h1, xn2, eid, gate = _mix_out(flat(x), flat(y), flat(bonus), flat(g), flat(yb), wd, tm_tok)
    return h1, xn2, eid, gate, shift_new.reshape(bsz, d), jnp.swapaxes(s_new, -1, -2), conv_new


def kernel(x_prompt, x_sample, state_shift, state_wkv, cache_conv, p_prompt, p_sample, norm_mix, w_in, mu_rkv, mu_w, mu_a, mu_g, w0, w1, w2, a0, a1, a2, g1, g2, k_k, k_a, r_k, ln_x_w, ln_x_b, dw_w, dw_b, cln_w, cln_b, w_out, norm_ffn, w_router_group, w_router_expert, w_exp_gate, w_exp_up, w_exp_down, norm_ple, w_ple_gate, w_ple_proj, norm_final):
    depth = norm_mix.shape[0]
    assert depth == 1
    d = x_prompt.shape[-1]
    c = w0.shape[-1]
    row = lambda z: z[0].reshape(1, -1).astype(F32)
    lane = jnp.arange(c, dtype=I32) // HEAD_SIZE
    wd = dict(
        norm_mix=row(norm_mix), w_in=w_in[0].astype(BF16), mu_rkv=row(mu_rkv), mu_w=row(mu_w), mu_a=row(mu_a),
        mu_g=row(mu_g), w0=row(w0), w1=w1[0].astype(BF16), w2=w2[0].astype(BF16), a0=row(a0),
        a1=a1[0].astype(BF16), a2=a2[0].astype(BF16), g1=g1[0].astype(BF16), g2=g2[0].astype(BF16),
        k_k=row(k_k), k_a=row(k_a), r_k=row(r_k), ln_x_w=row(ln_x_w), ln_x_b=row(ln_x_b),
        seg=(lane[:, None] == lane[None, :]).astype(BF16),
        dw_w=dw_w[0].astype(F32), dw_b=row(dw_b), cln_w=row(cln_w), cln_b=row(cln_b),
        w_out=w_out[0].astype(BF16), norm_ffn=row(norm_ffn),
        w_router=jnp.concatenate([w_router_expert[0], w_router_group[0],
                                  jnp.zeros((d, ROUTER_LANES - N_EXPERTS - N_EXPERT_GROUPS), F32)], axis=1),
        norm_ple=row(norm_ple), w_ple_gate=w_ple_gate[0].astype(BF16), w_ple_proj=w_ple_proj[0].astype(BF16),
        norm_final=norm_final.reshape(1, -1).astype(F32),
    )
    bp, tp, _ = x_prompt.shape
    bs, ts, _ = x_sample.shape
    mp, ms = bp * tp, bs * ts
    tm_p = min(256, tp)
    tm_s = min(256, ms)

    zeros = lambda *s: jnp.zeros(s, F32)
    h1_p, xn_p, eid_p, gate_p, shift_p, wkv_p, conv_p = _layer_front(
        x_prompt, zeros(bp, 1, d), zeros(bp, c // HEAD_SIZE, HEAD_SIZE, HEAD_SIZE), zeros(bp, CONV_CARRY, c),
        wd, tm_p, tm_p, min(128, tp), min(64, tp), True)
    h1_s, xn_s, eid_s, gate_s, shift_s, wkv_s, conv_s = _layer_front(
        x_sample, state_shift[0][:, None, :], state_wkv[0], cache_conv[0],
        wd, ts, tm_s, min(128, ms), ts, False)

    bm = 512
    tr_p, tr_s = min(512, mp), min(512, ms)
    plan, block_expert, n_used, n_blocks = _route_plan([eid_p, eid_s], [tr_p, tr_s], bm)
    tiles_p = mp // tr_p
    xs = _dispatch(plan, n_used, [eid_p[:, :2].T, eid_s[:, :2].T], [xn_p, xn_s], [tr_p, tr_s], n_blocks, bm)
    yb = _experts(block_expert, n_used, xs, w_exp_gate[0], w_exp_up[0], w_exp_down[0], bm)
    y_p = _final(plan, h1_p, eid_p, gate_p, p_prompt[0].reshape(mp, -1), yb, wd, tr_p, 0)
    y_s = _final(plan, h1_s, eid_s, gate_s, p_sample[0].reshape(ms, -1), yb, wd, tr_s, tiles_p)
    return (y_p.reshape(x_prompt.shape), y_s.reshape(x_sample.shape), shift_p[None], wkv_p[None], conv_p[None],
            shift_s[None], wkv_s[None], conv_s[None])
```

```python
import functools

import jax
import jax.numpy as jnp
from jax import lax
from jax.experimental import pallas as pl
from jax.experimental.pallas import tpu as pltpu

F32 = jnp.float32
BF16 = jnp.bfloat16
I32 = jnp.int32

HEAD_SIZE = 64
CONV_WIDTH = 31
CONV_CARRY = CONV_WIDTH - 1
SUBLANES = 8
LANES = 128
CARRY_PAD = 32
N_EXPERT_GROUPS = 4
EXPERTS_PER_GROUP = 8
N_EXPERTS = N_EXPERT_GROUPS * EXPERTS_PER_GROUP
ROUTER_LANES = 128
RMS_EPS = 1e-6
LN_EPS = 1e-5
GN_EPS = 64e-5
DECAY_SCALE = 0.6065306597126334
INV_BASE = 16
RUN_ALIGN = 8
LOCAL_PAD = N_EXPERTS * RUN_ALIGN
VMEM_LIMIT = 56 * 1024 * 1024

NN = ((1,), (0,))
NT = ((1,), (1,))
TN = ((0,), (0,))


def _dg(a, b, dims=NN):
    return lax.dot_general(a, b, (dims, ((), ())), preferred_element_type=F32)


def _split2(x):
    hi = x.astype(BF16)
    lo = (x - hi.astype(F32)).astype(BF16)
    return hi, lo


def _split3(x):
    hi = x.astype(BF16)
    r1 = x - hi.astype(F32)
    mid = r1.astype(BF16)
    lo = (r1 - mid.astype(F32)).astype(BF16)
    return hi, mid, lo


def _dot3(a, b, dims=NN):
    ah, al = _split2(a)
    bh, bl = _split2(b)
    return _dg(ah, bh, dims) + (_dg(al, bh, dims) + _dg(ah, bl, dims))


def _bdot(a, b, dims=NN):
    return _dg(a.astype(BF16), b.astype(BF16), dims)


def _mask_dot(mask_bf16, x):
    h, m, l = _split3(x)
    return _dg(mask_bf16, h) + (_dg(mask_bf16, m) + _dg(mask_bf16, l))


def _seg_sum(x, seg_bf16):
    h, l = _split2(x)
    w = seg_bf16.shape[0]
    return jnp.concatenate([_dg(h[:, j:j + w], seg_bf16) + _dg(l[:, j:j + w], seg_bf16)
                            for j in range(0, x.shape[1], w)], axis=1)


def _rms(x, g):
    return x * lax.rsqrt(jnp.mean(x * x, axis=-1, keepdims=True) + RMS_EPS) * g


def _sigmoid(x):
    return 0.5 * jnp.tanh(0.5 * x) + 0.5


def _full(shape):
    n = len(shape)
    return pl.BlockSpec(shape, lambda *_: (0,) * n)


def _mix_in_kernel(x_ref, shift_ref, conv_ref, nm_ref, win_ref, murkv_ref, muw_ref, mua_ref, mug_ref,
                   w0_ref, w1_ref, w2_ref, a0_ref, a1_ref, a2_ref, g1_ref, g2_ref,
                   kk_ref, ka_ref, rk_ref, seg_ref, dww_ref, dwb_ref, clnw_ref, clnb_ref,
                   r_out, lw_out, k_out, v_out, a_out, b_out, g_out, bonus_out, yb_out, shift_out, conv_out,
                   xn_last, h_last, up_ext, shifted, *, tm, c):
    i = pl.program_id(1)

    @pl.when(i == 0)
    def _():
        sp = shift_ref[0]
        xn_last[...] = sp
        sp8 = jnp.broadcast_to(sp, (8, sp.shape[1])).astype(BF16)
        h_last[...] = _dg(sp8, win_ref[:, :3 * c])[0:1]
        up_ext[CARRY_PAD - CONV_CARRY:CARRY_PAD, :] = conv_ref[0]

    x = x_ref[0]
    xn = _rms(x, nm_ref[...])
    hin = _dg(xn.astype(BF16), win_ref[...])
    first = lax.broadcasted_iota(I32, (tm, 1), 0) == 0
    xprev = jnp.where(first, xn_last[...], pltpu.roll(xn, 1, 0))
    dx = xprev - xn
    h_rkv = hin[:, :3 * c]
    hprev = jnp.where(first, h_last[...], pltpu.roll(h_rkv, 1, 0))
    rkv = h_rkv + (hprev - h_rkv) * murkv_ref[...]
    r = rkv[:, :c]
    k = rkv[:, c:2 * c]
    v = rkv[:, 2 * c:]
    xw = (xn + dx * muw_ref[...]).astype(BF16)
    xa = (xn + dx * mua_ref[...]).astype(BF16)
    xg = (xn + dx * mug_ref[...]).astype(BF16)
    zw = w0_ref[...] + _dg(jnp.tanh(_dg(xw, w1_ref[...])).astype(BF16), w2_ref[...])
    a = _sigmoid(a0_ref[...] + _dg(_dg(xa, a1_ref[...]).astype(BF16), a2_ref[...]))
    g = _dg(_sigmoid(_dg(xg, g1_ref[...])).astype(BF16), g2_ref[...])
    seg = seg_ref[...]
    kk = k * kk_ref[...]
    kk = kk * jnp.minimum(lax.rsqrt(_seg_sum(kk * kk, seg)), 1e12)
    k2 = k * (1.0 + (a - 1.0) * ka_ref[...])
    r_out[0] = r
    lw_out[0] = -DECAY_SCALE * _sigmoid(zw)
    k_out[0] = k2
    v_out[0] = v
    a_out[0] = -kk
    b_out[0] = kk * a
    g_out[0] = g
    bonus_out[0] = _seg_sum(r * k2 * rk_ref[...], seg) * v

    u = hin[:, 3 * c:4 * c] * _sigmoid(hin[:, 4 * c:])
    up_ext[CARRY_PAD:CARRY_PAD + tm, :] = u
    first_row = CARRY_PAD - CONV_CARRY
    for s in range(SUBLANES):
        span = tm + (CONV_WIDTH - 1 - s) // SUBLANES * SUBLANES
        shifted[s, 0:span, :] = up_ext[pl.ds(first_row + s, span), :]
    z = jnp.zeros_like(u) + dwb_ref[...]
    for j in range(CONV_WIDTH):
        s, m = j % SUBLANES, j // SUBLANES
        z = z + dww_ref[j:j + 1, :] * shifted[s, m * SUBLANES:m * SUBLANES + tm, :]
    mu = jnp.mean(z, axis=-1, keepdims=True)
    zc = z - mu
    var = jnp.mean(zc * zc, axis=-1, keepdims=True)
    zn = zc * lax.rsqrt(var + LN_EPS) * clnw_ref[...] + clnb_ref[...]
    yb_out[0] = zn * _sigmoid(zn)

    tail = up_ext[pl.ds(tm + CARRY_PAD - CONV_CARRY, CONV_CARRY), :]
    up_ext[CARRY_PAD - CONV_CARRY:CARRY_PAD, :] = tail
    conv_out[0] = tail
    xn_last[...] = xn[tm - 1:tm]
    h_last[...] = h_rkv[tm - 1:tm]
    shift_out[0] = xn[tm - 1:tm]


def _mix_in(x, shift0, conv0, wd, tm):
    bsz, t, d = x.shape
    c = wd['w0'].shape[1]
    grid = (bsz, t // tm)
    tok = lambda w: pl.BlockSpec((1, tm, w), lambda b, i: (b, i, 0))
    per_seq = lambda rows, w: pl.BlockSpec((1, rows, w), lambda b, i: (b, 0, 0))
    weights = [wd[n] for n in ('norm_mix', 'w_in', 'mu_rkv', 'mu_w', 'mu_a', 'mu_g', 'w0', 'w1', 'w2', 'a0', 'a1',
                               'a2', 'g1', 'g2', 'k_k', 'k_a', 'r_k', 'seg', 'dw_w', 'dw_b', 'cln_w', 'cln_b')]
    out_tok = jax.ShapeDtypeStruct((bsz, t, c), F32)
    outs = pl.pallas_call(
        functools.partial(_mix_in_kernel, tm=tm, c=c),
        grid=grid,
        in_specs=[tok(d), per_seq(1, d), per_seq(CONV_CARRY, c)] + [_full(w.shape) for w in weights],
        out_specs=[tok(c)] * 9 + [per_seq(1, d), per_seq(CONV_CARRY, c)],
        out_shape=[out_tok] * 9 + [jax.ShapeDtypeStruct((bsz, 1, d), F32),
                                   jax.ShapeDtypeStruct((bsz, CONV_CARRY, c), F32)],
        scratch_shapes=[pltpu.VMEM((1, d), F32), pltpu.VMEM((1, 3 * c), F32),
                        pltpu.VMEM((tm + CARRY_PAD, c), F32),
                        pltpu.VMEM((SUBLANES, tm + CARRY_PAD - SUBLANES, c), F32)],
        compiler_params=pltpu.CompilerParams(dimension_semantics=("arbitrary", "arbitrary"),
                                             vmem_limit_bytes=VMEM_LIMIT),
        name="mix_in",
    )(x, shift0, conv0, *weights)
    return outs


def _tri_inverse(n_strict, row, col, eye, lg_chunk):
    lg_base = INV_BASE.bit_length() - 1
    same = lambda sh: (row >> sh) == (col >> sh)
    lg0 = min(lg_base, lg_chunk)
    p = [jnp.where(same(lg0), n, 0.0) for n in n_strict]
    t = [jnp.where(eye, 1.0, 0.0) + x for x in p]
    for _ in range(lg0 - 1):
        p = [_bdot(x, x) for x in p]
        t = [x + _bdot(x, y) for x, y in zip(t, p)]
    for lg in range(lg0, lg_chunk):
        off_mask = same(lg + 1) & jnp.logical_not(same(lg))
        u = [_bdot(x, jnp.where(off_mask, n, 0.0)) for x, n in zip(t, n_strict)]
        t = [x + _bdot(y, x) for x, y in zip(t, u)]
    return t


def _wkv_kernel(r_ref, lw_ref, k_ref, v_ref, a_ref, b_ref, s0_ref, y_ref, s_out, s_scr, *, tt, chunk, chained):
    ti = pl.program_id(1)
    n_heads = r_ref.shape[2] // HEAD_SIZE
    n_chunks = tt // chunk
    lg_chunk = chunk.bit_length() - 1
    row = lax.broadcasted_iota(I32, (tt, tt), 0)
    col = lax.broadcasted_iota(I32, (tt, tt), 1)
    in_chunk = (row >> lg_chunk) == (col >> lg_chunk)
    eye = row == col
    tri_incl = in_chunk & (col <= row)
    tri_strict = in_chunk & (col < row)
    m_cum = jnp.where(tri_incl, 1.0, 0.0).astype(BF16)
    m_tot = jnp.where(in_chunk, 1.0, 0.0).astype(BF16)
    row_h = lax.broadcasted_iota(I32, (HEAD_SIZE, HEAD_SIZE), 0)
    col_h = lax.broadcasted_iota(I32, (HEAD_SIZE, HEAD_SIZE), 1)
    eye_h = row_h == col_h

    if chained:
        @pl.when(ti == 0)
        def _():
            s_scr[...] = s0_ref[0]

    lw_all = lw_ref[0]
    k_all = k_ref[0]
    b_all = b_ref[0]
    cum = _mask_dot(m_cum, lw_all)
    tot = _mask_dot(m_tot, lw_all)
    e_neg = jnp.exp(-cum)
    e_end = jnp.exp(tot - cum)
    rt_all = r_ref[0] * jnp.exp(cum)
    at_all = a_ref[0] * jnp.exp(cum - lw_all)
    kt_all = k_all * e_neg
    bt_all = b_all * e_neg
    bd_all = b_all * e_end
    kd_all = k_all * e_end
    g_end_all = jnp.exp(tot)

    heads = range(n_heads)
    hsl = [slice(HEAD_SIZE * h, HEAD_SIZE * (h + 1)) for h in heads]
    cut = lambda z: [z[:, s_].astype(BF16) for s_ in hsl]
    v, rt, at, kt, bt, bd, kd = (cut(z) for z in (v_ref[0], rt_all, at_all, kt_all, bt_all, bd_all, kd_all))
    mm = [_dg(jnp.concatenate([at[h], rt[h]], axis=0), jnp.concatenate([bt[h], kt[h]], axis=0), NT) for h in heads]
    m_ab = [jnp.where(tri_strict, mm[h][:tt, :tt], 0.0) for h in heads]
    m_ak = [jnp.where(tri_strict, mm[h][:tt, tt:], 0.0).astype(BF16) for h in heads]
    m_rb = [jnp.where(tri_incl, mm[h][tt:, :tt], 0.0).astype(BF16) for h in heads]
    m_rk = [jnp.where(tri_incl, mm[h][tt:, tt:], 0.0).astype(BF16) for h in heads]
    tinv = [x.astype(BF16) for x in _tri_inverse(m_ab, row, col, eye, lg_chunk)]
    akv = [_dg(m_ak[h], v[h]).astype(BF16) for h in heads]
    w1 = [_dg(tinv[h], at[h]).astype(BF16) for h in heads]
    w2 = [_dg(tinv[h], akv[h]).astype(BF16) for h in heads]
    q = [(rt_all[:, hsl[h]] + _dg(m_rb[h], w1[h])).astype(BF16) for h in heads]
    y0 = [_dg(m_rb[h], w2[h]) + _dg(m_rk[h], v[h]) for h in heads]

    if chained:
        s = [s_scr[h] for h in heads]
    for ci in range(n_chunks):
        cs = slice(ci * chunk, (ci + 1) * chunk)
        if not chained:
            s = [s0_ref[ci, h] for h in heads]
        g_row = g_end_all[ci * chunk:ci * chunk + 1]
        gm = [jnp.where(eye_h, jnp.broadcast_to(g_row[:, hsl[h]], (HEAD_SIZE, HEAD_SIZE)), 0.0)
              + _dg(bd[h][cs], w1[h][cs], TN) for h in heads]
        hm = [_dg(bd[h][cs], w2[h][cs], TN) + _dg(kd[h][cs], v[h][cs], TN) for h in heads]
        for h in heads:
            y_ref[0, cs, hsl[h]] = _bdot(q[h][cs], s[h]) + y0[h][cs]
        s = [_dot3(gm[h], s[h]) + hm[h] for h in heads]
        if not chained:
            for h in heads:
                s_out[ci, h] = s[h]
    if chained:
        for h in heads:
            s_scr[h] = s[h]
            s_out[0, h] = s[h]


def _wkv(r, lw, k, v, a, b, s0t, tt, chunk, chained):
    bsz, t, c = r.shape
    n_heads = c // HEAD_SIZE
    hs = HEAD_SIZE
    if chained:
        grid = (bsz, t // tt)
        tok = pl.BlockSpec((1, tt, c), lambda bi, ti: (bi, ti, 0))
        st = pl.BlockSpec((1, n_heads, hs, hs), lambda bi, ti: (bi, 0, 0, 0))
        args = (r, lw, k, v, a, b)
        y_shape = (bsz, t, c)
    else:
        assert t == chunk and (bsz * t) % tt == 0
        n_seq = tt // chunk
        grid = (1, bsz * t // tt)
        tok = pl.BlockSpec((1, tt, c), lambda bi, ti: (0, ti, 0))
        st = pl.BlockSpec((n_seq, n_heads, hs, hs), lambda bi, ti: (ti, 0, 0, 0))
        args = tuple(z.reshape(1, bsz * t, c) for z in (r, lw, k, v, a, b))
        y_shape = (1, bsz * t, c)
    y, s_new = pl.pallas_call(
        functools.partial(_wkv_kernel, tt=tt, chunk=chunk, chained=chained),
        grid=grid,
        in_specs=[tok] * 6 + [st],
        out_specs=[tok, st],
        out_shape=[jax.ShapeDtypeStruct(y_shape, F32), jax.ShapeDtypeStruct(s0t.shape, F32)],
        scratch_shapes=[pltpu.VMEM((n_heads, hs, hs), F32)],
        compiler_params=pltpu.CompilerParams(dimension_semantics=("arbitrary",) * 2,
                                             vmem_limit_bytes=VMEM_LIMIT),
        name="wkv",
    )(*args, s0t)
    return y.reshape(bsz, t, c), s_new


def _interleave(make_stream, n_parts):
    for _ in zip(*[make_stream(part) for part in range(n_parts)]):
        pass


def _mix_out_kernel(x_ref, y_ref, bonus_ref, g_ref, yb_ref, lnw_ref, lnb_ref, seg_ref, wout_ref, nffn_ref,
                    wr_ref, h1_out, xn_out, eid_out, gate_out, *, c, n_parts):
    rows_per = x_ref.shape[0] // n_parts

    def stream(part):
        rs = pl.ds(part * rows_per, rows_per)
        seg = seg_ref[...]
        y = y_ref[rs, :]
        inv_n = 1.0 / HEAD_SIZE
        mu = _seg_sum(y, seg) * inv_n
        yield
        yc = y - mu
        var = _seg_sum(yc * yc, seg) * inv_n
        yield
        yn = yc * lax.rsqrt(var + GN_EPS) * lnw_ref[...] + lnb_ref[...]
        ya = (yn + bonus_ref[rs, :]) * g_ref[rs, :]
        mix = _dg(ya.astype(BF16), wout_ref[:c, :]) + _dg(yb_ref[rs, :].astype(BF16), wout_ref[c:, :])
        yield
        h1 = x_ref[rs, :] + mix
        h1_out[rs, :] = h1
        xn = _rms(h1, nffn_ref[...])
        xn_out[rs, :] = xn.astype(BF16)
        xh, xl = _split2(xn)
        hi_lo = _dg(xh, wr_ref[...])
        logits = hi_lo[:, :ROUTER_LANES] + (hi_lo[:, ROUTER_LANES:] + _dg(xl, wr_ref[:, :ROUTER_LANES]))
        yield
        lane = lax.broadcasted_iota(I32, logits.shape, 1)
        neg = jnp.float32(-jnp.inf)
        is_g = (lane >= N_EXPERTS) & (lane < N_EXPERTS + N_EXPERT_GROUPS)
        glog = jnp.where(is_g, logits, neg)
        gmax = jnp.max(glog, axis=-1, keepdims=True)
        gsel = jnp.min(jnp.where(glog == gmax, lane, 4 * ROUTER_LANES), axis=-1, keepdims=True) - N_EXPERTS
        gp = 1.0 / jnp.sum(jnp.where(is_g, jnp.exp(glog - gmax), 0.0), axis=-1, keepdims=True)
        in_grp = (lane >= gsel * EXPERTS_PER_GROUP) & (lane < (gsel + 1) * EXPERTS_PER_GROUP)
        elog = jnp.where(in_grp, logits, neg)
        emax = jnp.max(elog, axis=-1, keepdims=True)
        ex = jnp.where(in_grp, jnp.exp(elog - emax), 0.0)
        eprob = ex / jnp.sum(ex, axis=-1, keepdims=True)
        eprob = jnp.where(in_grp, eprob, -1.0)
        yield
        v1 = jnp.max(eprob, axis=-1, keepdims=True)
        i1 = jnp.min(jnp.where(eprob == v1, lane, 4 * ROUTER_LANES), axis=-1, keepdims=True)
        rest = jnp.where(lane == i1, -1.0, eprob)
        v2 = jnp.max(rest, axis=-1, keepdims=True)
        i2 = jnp.min(jnp.where(rest == v2, lane, 4 * ROUTER_LANES), axis=-1, keepdims=True)
        denom = v1 + v2
        eid_out[rs, :] = jnp.where(lane == 0, i1, jnp.where(lane == 1, i2, 0))
        gate_out[rs, :] = jnp.where(lane == 0, gp * v1 / denom, jnp.where(lane == 1, gp * v2 / denom, 0.0))
        yield

    _interleave(stream, n_parts)


def _mix_out(x2, y2, bonus2, g2, yb2, wd, tm, n_parts):
    m, d = x2.shape
    c = y2.shape[1]
    tokd = pl.BlockSpec((tm, d), lambda i: (i, 0))
    tokc = pl.BlockSpec((tm, c), lambda i: (i, 0))
    tokr = pl.BlockSpec((tm, ROUTER_LANES), lambda i: (i, 0))
    weights = [wd[n] for n in ('ln_x_w', 'ln_x_b', 'seg', 'w_out', 'norm_ffn', 'w_router')]
    return pl.pallas_call(
        functools.partial(_mix_out_kernel, c=c, n_parts=n_parts),
        grid=(m // tm,),
        in_specs=[tokd, tokc, tokc, tokc, tokc] + [_full(w.shape) for w in weights],
        out_specs=[tokd, tokd, tokr, tokr],
        out_shape=[jax.ShapeDtypeStruct((m, d), F32), jax.ShapeDtypeStruct((m, d), BF16),
                   jax.ShapeDtypeStruct((m, ROUTER_LANES), I32), jax.ShapeDtypeStruct((m, ROUTER_LANES), F32)],
        compiler_params=pltpu.CompilerParams(dimension_semantics=("arbitrary",), vmem_limit_bytes=VMEM_LIMIT),
        name="mix_out",
    )(x2, y2, bonus2, g2, yb2, *weights)


def _pow2_pieces(count, max_rows, fn):
    off = 0
    rows = max_rows
    while rows >= RUN_ALIGN:
        has = (count & (rows // RUN_ALIGN)) != 0

        @pl.when(has)
        def _(off=off, rows=rows):
            fn(off, rows)
        off = off + jnp.where(has, rows, 0)
        rows //= 2


def _for_each_expert(fn):
    def body(e, carry):
        fn(e)
        return carry
    lax.fori_loop(0, N_EXPERTS, body, 0)


def _pow2_floor(n):
    return 1 << (n.bit_length() - 1)


def _dispatch_kernel(nch_ref, tot_ref, off_ref, loff_ref, tn_ref, toff_ref, nu_ref, *refs, groups, bm, n_blocks):
    n_g = len(groups)
    eid_refs, xn_refs = refs[:n_g], refs[n_g + 1:2 * n_g + 1]
    loffc_ref = refs[n_g]
    xs_out, buf, sem, zbuf, zsem = refs[2 * n_g + 1:]
    i = pl.program_id(0)
    last = pl.num_programs(0) - 1
    max_run = max(tm for tm, _ in groups)

    def start_tile(tile):
        def per_expert(e):
            src0 = loff_ref[tile * N_EXPERTS + e]
            dst0 = off_ref[tile * N_EXPERTS + e]

            def piece(o, rows):
                src = buf.at[tile % 2, pl.ds(pl.multiple_of(src0 + o, RUN_ALIGN), rows)]
                dst = xs_out.at[pl.ds(pl.multiple_of(dst0 + o, RUN_ALIGN), rows)]
                pltpu.make_async_copy(src, dst, sem.at[tile % 2]).start()
            _pow2_pieces(nch_ref[tile * N_EXPERTS + e], max_run, piece)
        _for_each_expert(per_expert)

    def wait_tile(tile):
        def piece(o, rows):
            pltpu.make_async_copy(buf.at[tile % 2, pl.ds(0, rows)], xs_out.at[pl.ds(0, rows)], sem.at[tile % 2]).wait()
        _pow2_pieces(tot_ref[tile], _pow2_floor(buf.shape[1]), piece)

    @pl.when(i >= 2)
    def _():
        wait_tile(i - 2)

    def sort_tile(eid_ref, xn_ref, tm):
        e_rows = eid_ref[...]
        sub = lax.broadcasted_iota(I32, (N_EXPERTS, tm), 0)
        e1 = jnp.where(sub == e_rows[0:1], 1.0, 0.0)
        e2 = jnp.where(sub == e_rows[1:2], 1.0, 0.0)
        before = lax.broadcasted_iota(I32, (tm, tm), 0) < lax.broadcasted_iota(I32, (tm, tm), 1)
        slot = _dg((e1 + e2).astype(BF16), jnp.where(before, 1.0, 0.0).astype(BF16)) + loffc_ref[0]
        l1 = jnp.sum(slot * e1, axis=0, keepdims=True).astype(I32)
        l2 = jnp.sum(slot * e2, axis=0, keepdims=True).astype(I32)
        n_rows = 2 * tm + LOCAL_PAD
        rows = lax.broadcasted_iota(I32, (n_rows, tm), 0)
        perm = jnp.where((rows == l1) | (rows == l2), 1.0, 0.0).astype(BF16)
        buf[i % 2, 0:n_rows, :] = _dg(perm, xn_ref[...])

    first = 0
    for g, (tm, n_tiles) in enumerate(groups):
        pl.when((i >= first) & (i < first + n_tiles))(functools.partial(sort_tile, eid_refs[g], xn_refs[g], tm))
        first += n_tiles
    start_tile(i)

    @pl.when(i == 0)
    def _():
        zbuf[...] = jnp.zeros_like(zbuf)
        half = zbuf.shape[0]

        def zero_copy(off, rows):
            return pltpu.make_async_copy(zbuf.at[pl.ds(0, rows)], xs_out.at[pl.ds(off, rows)], zsem)

        def tails(fn):
            _for_each_expert(lambda e: _pow2_pieces(
                tn_ref[e], half, lambda o, rows: fn(pl.multiple_of(toff_ref[e] + o, RUN_ALIGN), rows)))

        def unused_blocks(fn):
            def body(b, carry):
                fn(pl.multiple_of(b * bm, bm), half)
                fn(pl.multiple_of(b * bm + half, half), half)
                return carry
            lax.fori_loop(nu_ref[0], n_blocks, body, 0)

        tails(lambda off, rows: zero_copy(off, rows).start())
        unused_blocks(lambda off, rows: zero_copy(off, rows).start())
        tails(lambda off, rows: zero_copy(off, rows).wait())
        unused_blocks(lambda off, rows: zero_copy(off, rows).wait())

    @pl.when(i == last)
    def _():
        @pl.when(i >= 1)
        def _():
            wait_tile(i - 1)
        wait_tile(i)


def _dispatch(plan, n_used, eids_t, xns, tms, n_blocks, bm):
    d = xns[0].shape[1]
    groups = tuple((tm, xn.shape[0] // tm) for xn, tm in zip(xns, tms))
    firsts = [sum(n for _, n in groups[:g]) for g in range(len(groups))]
    lbuf = 2 * max(tms) + LOCAL_PAD

    def tile_of(g):
        return lambda i: jnp.clip(i - firsts[g], 0, groups[g][1] - 1)

    imap = lambda f: (lambda i, *_: f(i))
    in_specs = ([pl.BlockSpec((2, tm), imap(lambda i, g=g: (0, tile_of(g)(i)))) for g, tm in enumerate(tms)] +
                [pl.BlockSpec((1, N_EXPERTS, 1), imap(lambda i: (i, 0, 0)))] +
                [pl.BlockSpec((tm, d), imap(lambda i, g=g: (tile_of(g)(i), 0))) for g, tm in enumerate(tms)])
    return pl.pallas_call(
        functools.partial(_dispatch_kernel, groups=groups, bm=bm, n_blocks=n_blocks),
        grid_spec=pltpu.PrefetchScalarGridSpec(
            num_scalar_prefetch=7,
            grid=(sum(n for _, n in groups),),
            in_specs=in_specs,
            out_specs=pl.BlockSpec(memory_space=pl.ANY),
            scratch_shapes=[pltpu.VMEM((2, lbuf, d), F32), pltpu.SemaphoreType.DMA((2,)),
                            pltpu.VMEM((bm // 2, d), F32), pltpu.SemaphoreType.DMA],
        ),
        out_shape=jax.ShapeDtypeStruct((n_blocks * bm, d), F32),
        compiler_params=pltpu.CompilerParams(dimension_semantics=("arbitrary",), vmem_limit_bytes=VMEM_LIMIT),
        name="moe_dispatch",
    )(plan['nch'], plan['tot'], plan['off'], plan['loff'], plan['tail_n'], plan['tail_off'], n_used, *eids_t,
      plan['loff_col'], *xns)


def _experts_kernel(be_ref, nu_ref, xs_ref, wg_ref, wu_ref, wd_ref, yb_ref, wg_b, wu_b, wd_b):
    b = pl.program_id(0)

    @pl.when((b == 0) | (be_ref[b] != be_ref[jnp.maximum(b - 1, 0)]))
    def _():
        wg_b[...] = wg_ref[0].astype(BF16)
        wu_b[...] = wu_ref[0].astype(BF16)
        wd_b[...] = wd_ref[0].astype(BF16)

    @pl.when(b < nu_ref[0])
    def _():
        xb = xs_ref[...].astype(BF16)
        hg = _dg(xb, wg_b[...])
        hu = _dg(xb, wu_b[...])
        act = (hg * _sigmoid(hg) * hu).astype(BF16)
        yb_ref[...] = _dg(act, wd_b[...])

    @pl.when(pl.program_id(0) >= nu_ref[0])
    def _():
        yb_ref[...] = jnp.zeros_like(yb_ref)


def _experts(block_expert, n_used, xs, wg, wu, wdn, bm):
    p, d = xs.shape
    ff = wg.shape[2]
    n_blocks = p // bm
    rows = lambda b, be, nu: (jnp.minimum(b, nu[0] - 1), 0)
    return pl.pallas_call(
        _experts_kernel,
        grid_spec=pltpu.PrefetchScalarGridSpec(
            num_scalar_prefetch=2,
            grid=(n_blocks,),
            in_specs=[pl.BlockSpec((bm, d), rows),
                      pl.BlockSpec((1, d, ff), lambda b, be, nu: (be[b], 0, 0)),
                      pl.BlockSpec((1, d, ff), lambda b, be, nu: (be[b], 0, 0)),
                      pl.BlockSpec((1, ff, d), lambda b, be, nu: (be[b], 0, 0))],
            out_specs=pl.BlockSpec((bm, d), lambda b, be, nu: (b, 0)),
            scratch_shapes=[pltpu.VMEM((d, ff), BF16), pltpu.VMEM((d, ff), BF16), pltpu.VMEM((ff, d), BF16)],
        ),
        out_shape=jax.ShapeDtypeStruct((p, d), F32),
        compiler_params=pltpu.CompilerParams(dimension_semantics=("arbitrary",), vmem_limit_bytes=VMEM_LIMIT),
        name="moe_experts",
    )(block_expert, n_used, xs, wg, wu, wdn)


def _final_kernel(nch_ref, tot_ref, off_ref, loff_ref, h1_ref, eid_ref, gate_ref, loffr_ref, p_ref, yb_hbm,
                  nple_ref, wpg_ref, wpp_ref, nfin_ref, y_out, buf, sem, *, tm, lbuf, tile0, n_parts):
    i = pl.program_id(0)

    def fetch(tile):
        base = (tile0 + tile) * N_EXPERTS

        def per_expert(e):
            src0 = off_ref[base + e]
            dst0 = loff_ref[base + e]

            def piece(o, rows):
                src = yb_hbm.at[pl.ds(pl.multiple_of(src0 + o, RUN_ALIGN), rows)]
                dst = buf.at[tile % 2, pl.ds(pl.multiple_of(dst0 + o, RUN_ALIGN), rows)]
                pltpu.make_async_copy(src, dst, sem.at[tile % 2]).start()
            _pow2_pieces(nch_ref[base + e], tm, piece)
        _for_each_expert(per_expert)

    def wait_fetch(tile):
        def piece(o, rows):
            pltpu.make_async_copy(yb_hbm.at[pl.ds(0, rows)], buf.at[tile % 2, pl.ds(0, rows)], sem.at[tile % 2]).wait()
        _pow2_pieces(tot_ref[tile0 + tile], _pow2_floor(lbuf), piece)

    @pl.when(i == 0)
    def _():
        buf[...] = jnp.zeros_like(buf)
        fetch(i)

    @pl.when(i + 1 < pl.num_programs(0))
    def _():
        fetch(i + 1)

    eid = eid_ref[...]
    lane = lax.broadcasted_iota(I32, (tm, N_EXPERTS), 1)
    e12 = (jnp.where(lane == eid[:, 0:1], 1.0, 0.0) + jnp.where(lane == eid[:, 1:2], 1.0, 0.0)).astype(BF16)
    rows_per = tm // n_parts
    picks = []
    for part in range(n_parts):
        rs = pl.ds(part * rows_per, rows_per)
        lane_p = lax.broadcasted_iota(I32, (rows_per, N_EXPERTS), 1)
        eid_p = eid_ref[rs, :]
        e1 = jnp.where(lane_p == eid_p[:, 0:1], 1.0, 0.0)
        e2 = jnp.where(lane_p == eid_p[:, 1:2], 1.0, 0.0)
        before = (lax.broadcasted_iota(I32, (rows_per, tm), 1)
                  < lax.broadcasted_iota(I32, (rows_per, tm), 0) + part * rows_per)
        slot = _dg(jnp.where(before, 1.0, 0.0).astype(BF16), e12) + loffr_ref[0]
        l1 = jnp.sum(slot * e1, axis=1, keepdims=True).astype(I32)
        l2 = jnp.sum(slot * e2, axis=1, keepdims=True).astype(I32)
        cols = lax.broadcasted_iota(I32, (rows_per, lbuf), 1)
        gate = gate_ref[rs, :]
        picks.append(jnp.where(cols == l1, gate[:, 0:1], jnp.where(cols == l2, gate[:, 1:2], 0.0)).astype(BF16))

    wait_fetch(i)
    sorted_rows = buf[i % 2].astype(BF16)

    def stream(part):
        rs = pl.ds(part * rows_per, rows_per)
        h2 = h1_ref[rs, :] + _dg(picks[part], sorted_rows)
        yield
        gate_in = _rms(h2, nple_ref[...]).astype(BF16)
        pg = _sigmoid(_dg(gate_in, wpg_ref[...]))
        yield
        h3 = h2 + pg * _dg(p_ref[rs, :].astype(BF16), wpp_ref[...])
        y_out[rs, :] = _rms(h3, nfin_ref[...])
        yield

    _interleave(stream, n_parts)


def _final(plan, h1, eid, gate, p2, yb, wd, tm, tile0):
    m, d = h1.shape
    pd = p2.shape[1]
    lbuf = 2 * tm + LOCAL_PAD
    weights = [wd[n] for n in ('norm_ple', 'w_ple_gate', 'w_ple_proj', 'norm_final')]
    imap = lambda f: (lambda i, *_: f(i))
    return pl.pallas_call(
        functools.partial(_final_kernel, tm=tm, lbuf=lbuf, tile0=tile0, n_parts=2),
        grid_spec=pltpu.PrefetchScalarGridSpec(
            num_scalar_prefetch=4,
            grid=(m // tm,),
            in_specs=[pl.BlockSpec((tm, d), imap(lambda i: (i, 0))),
                      pl.BlockSpec((tm, ROUTER_LANES), imap(lambda i: (i, 0))),
                      pl.BlockSpec((tm, ROUTER_LANES), imap(lambda i: (i, 0))),
                      pl.BlockSpec((1, 1, N_EXPERTS), imap(lambda i: (tile0 + i, 0, 0))),
                      pl.BlockSpec((tm, pd), imap(lambda i: (i, 0))),
                      pl.BlockSpec(memory_space=pl.ANY)] +
                     [pl.BlockSpec(w.shape, imap(lambda i, n=len(w.shape): (0,) * n)) for w in weights],
            out_specs=pl.BlockSpec((tm, d), imap(lambda i: (i, 0))),
            scratch_shapes=[pltpu.VMEM((2, lbuf, d), F32), pltpu.SemaphoreType.DMA((2,))],
        ),
        out_shape=jax.ShapeDtypeStruct((m, d), F32),
        compiler_params=pltpu.CompilerParams(dimension_semantics=("arbitrary",), vmem_limit_bytes=VMEM_LIMIT),
        name="moe_final",
    )(plan['nch'], plan['tot'], plan['off'], plan['loff'], h1, eid, gate, plan['loff_row'], p2, yb, *weights)


def _route_plan(eids, tms, bm):
    experts = jnp.arange(N_EXPERTS, dtype=I32)
    counts = []
    for eid, tm in zip(eids, tms):
        onehot = (eid[:, :2, None] == experts).astype(I32)
        counts.append(onehot.reshape(-1, 2 * tm, N_EXPERTS).sum(axis=1))
    n = jnp.concatenate(counts)
    n_al = (n + RUN_ALIGN - 1) // RUN_ALIGN * RUN_ALIGN
    loff = jnp.cumsum(n_al, axis=1) - n_al
    used = n_al.sum(axis=0)
    region = (used + bm - 1) // bm * bm
    pend = jnp.cumsum(region)
    off = (pend - region)[None, :] + jnp.cumsum(n_al, axis=0) - n_al
    n_assign = sum(2 * e.shape[0] for e in eids)
    n_blocks = -(-(n_assign + (RUN_ALIGN - 1) * N_EXPERTS * n.shape[0] + N_EXPERTS * (bm - 1)) // bm)
    block_start = jnp.arange(n_blocks, dtype=I32) * bm
    block_expert = jnp.minimum(jnp.sum((pend[None, :] <= block_start[:, None]).astype(I32), axis=1), N_EXPERTS - 1)
    plan = dict(nch=(n_al // RUN_ALIGN).reshape(-1).astype(I32), off=off.reshape(-1).astype(I32),
                tot=(n_al.sum(axis=1) // RUN_ALIGN).astype(I32),
                loff=loff.reshape(-1).astype(I32), tail_n=((region - used) // RUN_ALIGN).astype(I32),
                tail_off=(pend - region + used).astype(I32), loff_col=loff.astype(F32)[:, :, None],
                loff_row=loff.astype(F32)[:, None, :])
    return plan, block_expert.astype(I32), (pend[-1] // bm).astype(I32).reshape(1), n_blocks


def _layer_front(x, shift0, wkv0, conv0, wd, tm_in, tm_tok, wkv_tile, wkv_chunk, chained):
    bsz, t, d = x.shape
    r, lw, k2, v, ah, bh, g, bonus, yb, shift_new, conv_new = _mix_in(x, shift0, conv0, wd, tm_in)
    y, s_new = _wkv(r, lw, k2, v, ah, bh, jnp.swapaxes(wkv0, -1, -2), wkv_tile, wkv_chunk, chained)
    flat = lambda z: z.reshape(bsz * t, z.shape[-1])
    h1, xn2, eid, gate = _mix_out(flat(x), flat(y), flat(bonus), flat(g), flat(yb), wd, tm_tok, 2)
    return h1, xn2, eid, gate, shift_new.reshape(bsz, d), jnp.swapaxes(s_new, -1, -2), conv_new


def kernel(x_prompt, x_sample, state_shift, state_wkv, cache_conv, p_prompt, p_sample, norm_mix, w_in, mu_rkv, mu_w, mu_a, mu_g, w0, w1, w2, a0, a1, a2, g1, g2, k_k, k_a, r_k, ln_x_w, ln_x_b, dw_w, dw_b, cln_w, cln_b, w_out, norm_ffn, w_router_group, w_router_expert, w_exp_gate, w_exp_up, w_exp_down, norm_ple, w_ple_gate, w_ple_proj, norm_final):
    depth = norm_mix.shape[0]
    assert depth == 1
    d = x_prompt.shape[-1]
    c = w0.shape[-1]
    row = lambda z: z[0].reshape(1, -1).astype(F32)
    lane = jnp.arange(LANES, dtype=I32) // HEAD_SIZE
    w_router = jnp.concatenate([w_router_expert[0], w_router_group[0],
                                jnp.zeros((d, ROUTER_LANES - N_EXPERTS - N_EXPERT_GROUPS), F32)], axis=1)
    wd = dict(
        norm_mix=row(norm_mix), w_in=w_in[0].astype(BF16), mu_rkv=row(mu_rkv), mu_w=row(mu_w), mu_a=row(mu_a),
        mu_g=row(mu_g), w0=row(w0), w1=w1[0].astype(BF16), w2=w2[0].astype(BF16), a0=row(a0),
        a1=a1[0].astype(BF16), a2=a2[0].astype(BF16), g1=g1[0].astype(BF16), g2=g2[0].astype(BF16),
        k_k=row(k_k), k_a=row(k_a), r_k=row(r_k), ln_x_w=row(ln_x_w), ln_x_b=row(ln_x_b),
        seg=(lane[:, None] == lane[None, :]).astype(BF16),
        dw_w=dw_w[0].astype(F32), dw_b=row(dw_b), cln_w=row(cln_w), cln_b=row(cln_b),
        w_out=w_out[0].astype(BF16), norm_ffn=row(norm_ffn),
        w_router=jnp.concatenate(_split2(w_router), axis=1),
        norm_ple=row(norm_ple), w_ple_gate=w_ple_gate[0].astype(BF16), w_ple_proj=w_ple_proj[0].astype(BF16),
        norm_final=norm_final.reshape(1, -1).astype(F32),
    )
    bp, tp, _ = x_prompt.shape
    bs, ts, _ = x_sample.shape
    mp, ms = bp * tp, bs * ts
    tm_p = min(256, tp)
    tm_s = min(256, ms)

    zeros = lambda *s: jnp.zeros(s, F32)
    h1_p, xn_p, eid_p, gate_p, shift_p, wkv_p, conv_p = _layer_front(
        x_prompt, zeros(bp, 1, d), zeros(bp, c // HEAD_SIZE, HEAD_SIZE, HEAD_SIZE), zeros(bp, CONV_CARRY, c),
        wd, tm_p, min(512, mp), min(128, tp), min(64, tp), True)
    h1_s, xn_s, eid_s, gate_s, shift_s, wkv_s, conv_s = _layer_front(
        x_sample, state_shift[0][:, None, :], state_wkv[0], cache_conv[0],
        wd, ts, tm_s, min(128, ms), ts, False)

    bm = 512
    tr_p, tr_s = min(512, mp), min(512, ms)
    plan, block_expert, n_used, n_blocks = _route_plan([eid_p, eid_s], [tr_p, tr_s], bm)
    tiles_p = mp // tr_p
    xs = _dispatch(plan, n_used, [eid_p[:, :2].T, eid_s[:, :2].T], [xn_p, xn_s], [tr_p, tr_s], n_blocks, bm)
    yb = _experts(block_expert, n_used, xs, w_exp_gate[0], w_exp_up[0], w_exp_down[0], bm)
    y_p = _final(plan, h1_p, eid_p, gate_p, p_prompt[0].reshape(mp, -1), yb, wd, tr_p, 0)
    y_s = _final(plan, h1_s, eid_s, gate_s, p_sample[0].reshape(ms, -1), yb, wd, tr_s, tiles_p)
    return (y_p.reshape(x_prompt.shape), y_s.reshape(x_sample.shape), shift_p[None], wkv_p[None], conv_p[None],
            shift_s[None], wkv_s[None], conv_s[None])
```

```python
import functools

import jax
import jax.numpy as jnp
from jax import lax
from jax.experimental import pallas as pl
from jax.experimental.pallas import tpu as pltpu

F32 = jnp.float32
BF16 = jnp.bfloat16
I32 = jnp.int32

HEAD_SIZE = 64
CONV_WIDTH = 31
CONV_CARRY = CONV_WIDTH - 1
SUBLANES = 8
LANES = 128
CARRY_PAD = 32
N_EXPERT_GROUPS = 4
EXPERTS_PER_GROUP = 8
N_EXPERTS = N_EXPERT_GROUPS * EXPERTS_PER_GROUP
ROUTER_LANES = 128
RMS_EPS = 1e-6
LN_EPS = 1e-5
GN_EPS = 64e-5
DECAY_SCALE = 0.6065306597126334
INV_BASE = 16
RUN_ALIGN = 8
LOCAL_PAD = N_EXPERTS * RUN_ALIGN
VMEM_LIMIT = 56 * 1024 * 1024

NN = ((1,), (0,))
NT = ((1,), (1,))
TN = ((0,), (0,))


def _dg(a, b, dims=NN):
    return lax.dot_general(a, b, (dims, ((), ())), preferred_element_type=F32)


def _split2(x):
    hi = x.astype(BF16)
    lo = (x - hi.astype(F32)).astype(BF16)
    return hi, lo


def _split3(x):
    hi = x.astype(BF16)
    r1 = x - hi.astype(F32)
    mid = r1.astype(BF16)
    lo = (r1 - mid.astype(F32)).astype(BF16)
    return hi, mid, lo


def _dot3(a, b, dims=NN):
    ah, al = _split2(a)
    bh, bl = _split2(b)
    return _dg(ah, bh, dims) + (_dg(al, bh, dims) + _dg(ah, bl, dims))


def _bdot(a, b, dims=NN):
    return _dg(a.astype(BF16), b.astype(BF16), dims)


def _mask_dot(mask_bf16, x):
    h, m, l = _split3(x)
    return _dg(mask_bf16, h) + (_dg(mask_bf16, m) + _dg(mask_bf16, l))


def _seg_sum(x, seg_bf16):
    h, l = _split2(x)
    w = seg_bf16.shape[0]
    return jnp.concatenate([_dg(h[:, j:j + w], seg_bf16) + _dg(l[:, j:j + w], seg_bf16)
                            for j in range(0, x.shape[1], w)], axis=1)


def _rms(x, g):
    return x * lax.rsqrt(jnp.mean(x * x, axis=-1, keepdims=True) + RMS_EPS) * g


def _sigmoid(x):
    return 0.5 * jnp.tanh(0.5 * x) + 0.5


def _full(shape):
    n = len(shape)
    return pl.BlockSpec(shape, lambda *_: (0,) * n)


def _mix_in_kernel(x_ref, shift_ref, conv_ref, nm_ref, win_ref, murkv_ref, muw_ref, mua_ref, mug_ref,
                   w0_ref, w1_ref, w2_ref, a0_ref, a1_ref, a2_ref, g1_ref, g2_ref,
                   kk_ref, ka_ref, rk_ref, seg_ref, dww_ref, dwb_ref, clnw_ref, clnb_ref,
                   r_out, lw_out, k_out, v_out, a_out, b_out, g_out, bonus_out, yb_out, shift_out, conv_out,
                   xn_last, h_last, up_ext, shifted, *, tm, c):
    i = pl.program_id(1)

    @pl.when(i == 0)
    def _():
        sp = shift_ref[0]
        xn_last[...] = sp
        sp8 = jnp.broadcast_to(sp, (8, sp.shape[1])).astype(BF16)
        h_last[...] = _dg(sp8, win_ref[:, :3 * c])[0:1]
        up_ext[CARRY_PAD - CONV_CARRY:CARRY_PAD, :] = conv_ref[0]

    x = x_ref[0]
    xn = _rms(x, nm_ref[...])
    hin = _dg(xn.astype(BF16), win_ref[...])
    first = lax.broadcasted_iota(I32, (tm, 1), 0) == 0
    xprev = jnp.where(first, xn_last[...], pltpu.roll(xn, 1, 0))
    dx = xprev - xn
    h_rkv = hin[:, :3 * c]
    hprev = jnp.where(first, h_last[...], pltpu.roll(h_rkv, 1, 0))
    rkv = h_rkv + (hprev - h_rkv) * murkv_ref[...]
    r = rkv[:, :c]
    k = rkv[:, c:2 * c]
    v = rkv[:, 2 * c:]
    xw = (xn + dx * muw_ref[...]).astype(BF16)
    xa = (xn + dx * mua_ref[...]).astype(BF16)
    xg = (xn + dx * mug_ref[...]).astype(BF16)
    zw = w0_ref[...] + _dg(jnp.tanh(_dg(xw, w1_ref[...])).astype(BF16), w2_ref[...])
    a = _sigmoid(a0_ref[...] + _dg(_dg(xa, a1_ref[...]).astype(BF16), a2_ref[...]))
    g = _dg(_sigmoid(_dg(xg, g1_ref[...])).astype(BF16), g2_ref[...])
    seg = seg_ref[...]
    kk = k * kk_ref[...]
    kk = kk * jnp.minimum(lax.rsqrt(_seg_sum(kk * kk, seg)), 1e12)
    k2 = k * (1.0 + (a - 1.0) * ka_ref[...])
    r_out[0] = r
    lw_out[0] = -DECAY_SCALE * _sigmoid(zw)
    k_out[0] = k2
    v_out[0] = v
    a_out[0] = -kk
    b_out[0] = kk * a
    g_out[0] = g
    bonus_out[0] = _seg_sum(r * k2 * rk_ref[...], seg) * v

    u = hin[:, 3 * c:4 * c] * _sigmoid(hin[:, 4 * c:])
    up_ext[CARRY_PAD:CARRY_PAD + tm, :] = u
    first_row = CARRY_PAD - CONV_CARRY
    for s in range(SUBLANES):
        span = tm + (CONV_WIDTH - 1 - s) // SUBLANES * SUBLANES
        shifted[s, 0:span, :] = up_ext[pl.ds(first_row + s, span), :]
    z = jnp.zeros_like(u) + dwb_ref[...]
    for j in range(CONV_WIDTH):
        s, m = j % SUBLANES, j // SUBLANES
        z = z + dww_ref[j:j + 1, :] * shifted[s, m * SUBLANES:m * SUBLANES + tm, :]
    mu = jnp.mean(z, axis=-1, keepdims=True)
    zc = z - mu
    var = jnp.mean(zc * zc, axis=-1, keepdims=True)
    zn = zc * lax.rsqrt(var + LN_EPS) * clnw_ref[...] + clnb_ref[...]
    yb_out[0] = zn * _sigmoid(zn)

    tail = up_ext[pl.ds(tm + CARRY_PAD - CONV_CARRY, CONV_CARRY), :]
    up_ext[CARRY_PAD - CONV_CARRY:CARRY_PAD, :] = tail
    conv_out[0] = tail
    xn_last[...] = xn[tm - 1:tm]
    h_last[...] = h_rkv[tm - 1:tm]
    shift_out[0] = xn[tm - 1:tm]


def _mix_in(x, shift0, conv0, wd, tm):
    bsz, t, d = x.shape
    c = wd['w0'].shape[1]
    grid = (bsz, t // tm)
    tok = lambda w: pl.BlockSpec((1, tm, w), lambda b, i: (b, i, 0))
    per_seq = lambda rows, w: pl.BlockSpec((1, rows, w), lambda b, i: (b, 0, 0))
    weights = [wd[n] for n in ('norm_mix', 'w_in', 'mu_rkv', 'mu_w', 'mu_a', 'mu_g', 'w0', 'w1', 'w2', 'a0', 'a1',
                               'a2', 'g1', 'g2', 'k_k', 'k_a', 'r_k', 'seg', 'dw_w', 'dw_b', 'cln_w', 'cln_b')]
    out_tok = jax.ShapeDtypeStruct((bsz, t, c), F32)
    outs = pl.pallas_call(
        functools.partial(_mix_in_kernel, tm=tm, c=c),
        grid=grid,
        in_specs=[tok(d), per_seq(1, d), per_seq(CONV_CARRY, c)] + [_full(w.shape) for w in weights],
        out_specs=[tok(c)] * 9 + [per_seq(1, d), per_seq(CONV_CARRY, c)],
        out_shape=[out_tok] * 9 + [jax.ShapeDtypeStruct((bsz, 1, d), F32),
                                   jax.ShapeDtypeStruct((bsz, CONV_CARRY, c), F32)],
        scratch_shapes=[pltpu.VMEM((1, d), F32), pltpu.VMEM((1, 3 * c), F32),
                        pltpu.VMEM((tm + CARRY_PAD, c), F32),
                        pltpu.VMEM((SUBLANES, tm + CARRY_PAD - SUBLANES, c), F32)],
        compiler_params=pltpu.CompilerParams(dimension_semantics=("arbitrary", "arbitrary"),
                                             vmem_limit_bytes=VMEM_LIMIT),
        name="mix_in",
    )(x, shift0, conv0, *weights)
    return outs


def _tri_inverse(n_strict, row, col, lg_chunk):
    lg_base = INV_BASE.bit_length() - 1
    same = lambda sh: (row >> sh) == (col >> sh)
    lg0 = min(lg_base, lg_chunk)
    n = row.shape[0]

    def expand(c, lg):
        return jnp.where(same(lg), jnp.concatenate([c] * (n >> lg), axis=0), 0.0).astype(BF16)

    def fold(x, lg):
        b = 1 << lg
        return functools.reduce(lambda u, v: u + v, [x[i:i + b] for i in range(0, n, b)])

    b0 = 1 << lg0
    row_c = lax.broadcasted_iota(I32, (b0, n), 0)
    col_c = lax.broadcasted_iota(I32, (b0, n), 1)
    p_full = [jnp.where(same(lg0), x, 0.0) for x in n_strict]
    p = [fold(x, lg0) for x in p_full]
    t = [jnp.where((col_c & (b0 - 1)) == row_c, 1.0, 0.0) + x for x in p]
    p_full = [x.astype(BF16) for x in p_full]
    for _ in range(lg0 - 1):
        p = [_dg(x.astype(BF16), y) for x, y in zip(p, p_full)]
        p_full = [expand(x, lg0) for x in p]
        t = [x + _dg(x.astype(BF16), y) for x, y in zip(t, p_full)]
    for lg in range(lg0, lg_chunk):
        off_mask = same(lg + 1) & jnp.logical_not(same(lg))
        t_full = [expand(x, lg) for x in t]
        u = [_dg(x.astype(BF16), jnp.where(off_mask, m, 0.0).astype(BF16)) for x, m in zip(t, n_strict)]
        add = [_dg(x.astype(BF16), y) for x, y in zip(u, t_full)]
        even = ((lax.broadcasted_iota(I32, (1 << lg, n), 1) >> lg) & 1) == 0
        t = [jnp.concatenate([jnp.where(even, x, 0.0), jnp.where(even, a, x)], axis=0) for x, a in zip(t, add)]
    return [expand(x, lg_chunk) for x in t]


def _wkv_kernel(r_ref, lw_ref, k_ref, v_ref, a_ref, b_ref, s0_ref, y_ref, s_out, s_scr,
                *, tt, n_sub, chunk, chained):
    ti = pl.program_id(1)
    n_heads = r_ref.shape[2] // HEAD_SIZE
    n_chunks = tt // chunk
    lg_chunk = chunk.bit_length() - 1
    row = lax.broadcasted_iota(I32, (tt, tt), 0)
    col = lax.broadcasted_iota(I32, (tt, tt), 1)
    in_chunk = (row >> lg_chunk) == (col >> lg_chunk)
    tri_incl = in_chunk & (col <= row)
    tri_strict = in_chunk & (col < row)
    m_cum = jnp.where(tri_incl, 1.0, 0.0).astype(BF16)
    m_tot = jnp.where(in_chunk, 1.0, 0.0).astype(BF16)
    row_h = lax.broadcasted_iota(I32, (HEAD_SIZE, HEAD_SIZE), 0)
    col_h = lax.broadcasted_iota(I32, (HEAD_SIZE, HEAD_SIZE), 1)
    eye_h = row_h == col_h

    if chained:
        @pl.when(ti == 0)
        def _():
            s_scr[...] = s0_ref[0]

    rt_all, g_end_all, cols = [], [], {name: [] for name in ('v', 'rt', 'at', 'kt', 'bt', 'bd', 'kd')}
    hsl = [slice(HEAD_SIZE * h, HEAD_SIZE * (h + 1)) for h in range(n_heads)]
    for sub in range(n_sub):
        rows = pl.ds(sub * tt, tt)
        lw_all = lw_ref[0, rows, :]
        k_all = k_ref[0, rows, :]
        b_all = b_ref[0, rows, :]
        cum = _mask_dot(m_cum, lw_all)
        tot = _mask_dot(m_tot, lw_all)
        e_neg = jnp.exp(-cum)
        e_end = jnp.exp(tot - cum)
        full = dict(v=v_ref[0, rows, :], rt=r_ref[0, rows, :] * jnp.exp(cum),
                    at=a_ref[0, rows, :] * jnp.exp(cum - lw_all), kt=k_all * e_neg, bt=b_all * e_neg,
                    bd=b_all * e_end, kd=k_all * e_end)
        rt_all.append(full['rt'])
        g_end_all.append(jnp.exp(tot))
        for name, z in full.items():
            cols[name] += [z[:, s_].astype(BF16) for s_ in hsl]
    v, rt, at, kt, bt, bd, kd = (cols[name] for name in ('v', 'rt', 'at', 'kt', 'bt', 'bd', 'kd'))

    units = range(n_sub * n_heads)
    mm = [_dg(jnp.concatenate([at[u], rt[u]], axis=0), jnp.concatenate([bt[u], kt[u]], axis=0), NT) for u in units]
    m_ab = [jnp.where(tri_strict, mm[u][:tt, :tt], 0.0) for u in units]
    m_ak = [jnp.where(tri_strict, mm[u][:tt, tt:], 0.0).astype(BF16) for u in units]
    m_rb = [jnp.where(tri_incl, mm[u][tt:, :tt], 0.0).astype(BF16) for u in units]
    m_rk = [jnp.where(tri_incl, mm[u][tt:, tt:], 0.0).astype(BF16) for u in units]
    tinv = _tri_inverse(m_ab, row, col, lg_chunk)
    akv = [_dg(m_ak[u], v[u]).astype(BF16) for u in units]
    w1 = [_dg(tinv[u], at[u]).astype(BF16) for u in units]
    w2 = [_dg(tinv[u], akv[u]).astype(BF16) for u in units]
    q = [(rt_all[u // n_heads][:, hsl[u % n_heads]] + _dg(m_rb[u], w1[u])).astype(BF16) for u in units]
    y0 = [_dg(m_rb[u], w2[u]) + _dg(m_rk[u], v[u]) for u in units]

    heads = range(n_heads)
    if chained:
        s = [s_scr[h] for h in heads]
    for sub in range(n_sub):
        for ci in range(n_chunks):
            cs = slice(ci * chunk, (ci + 1) * chunk)
            seq = sub * n_chunks + ci
            if not chained:
                s = [s0_ref[seq, h] for h in heads]
            g_row = g_end_all[sub][ci * chunk:ci * chunk + 1]
            un = [sub * n_heads + h for h in heads]
            gm = [jnp.where(eye_h, jnp.broadcast_to(g_row[:, hsl[h]], (HEAD_SIZE, HEAD_SIZE)), 0.0)
                  + _dg(bd[un[h]][cs], w1[un[h]][cs], TN) for h in heads]
            hm = [_dg(bd[un[h]][cs], w2[un[h]][cs], TN) + _dg(kd[un[h]][cs], v[un[h]][cs], TN) for h in heads]
            for h in heads:
                y_ref[0, pl.ds(sub * tt + ci * chunk, chunk), hsl[h]] = _bdot(q[un[h]][cs], s[h]) + y0[un[h]][cs]
            s = [_dot3(gm[h], s[h]) + hm[h] for h in heads]
            if not chained:
                for h in heads:
                    s_out[seq, h] = s[h]
    if chained:
        for h in heads:
            s_scr[h] = s[h]
            s_out[0, h] = s[h]


def _wkv(r, lw, k, v, a, b, s0t, tt, n_sub, chunk, chained):
    bsz, t, c = r.shape
    n_heads = c // HEAD_SIZE
    hs = HEAD_SIZE
    step = tt * n_sub
    if chained:
        grid = (bsz, t // step)
        tok = pl.BlockSpec((1, step, c), lambda bi, ti: (bi, ti, 0))
        st = pl.BlockSpec((1, n_heads, hs, hs), lambda bi, ti: (bi, 0, 0, 0))
        args = (r, lw, k, v, a, b)
        y_shape = (bsz, t, c)
    else:
        assert t == chunk and (bsz * t) % step == 0
        n_seq = step // chunk
        grid = (1, bsz * t // step)
        tok = pl.BlockSpec((1, step, c), lambda bi, ti: (0, ti, 0))
        st = pl.BlockSpec((n_seq, n_heads, hs, hs), lambda bi, ti: (ti, 0, 0, 0))
        args = tuple(z.reshape(1, bsz * t, c) for z in (r, lw, k, v, a, b))
        y_shape = (1, bsz * t, c)
    y, s_new = pl.pallas_call(
        functools.partial(_wkv_kernel, tt=tt, n_sub=n_sub, chunk=chunk, chained=chained),
        grid=grid,
        in_specs=[tok] * 6 + [st],
        out_specs=[tok, st],
        out_shape=[jax.ShapeDtypeStruct(y_shape, F32), jax.ShapeDtypeStruct(s0t.shape, F32)],
        scratch_shapes=[pltpu.VMEM((n_heads, hs, hs), F32)],
        compiler_params=pltpu.CompilerParams(dimension_semantics=("arbitrary",) * 2,
                                             vmem_limit_bytes=VMEM_LIMIT),
        name="wkv",
    )(*args, s0t)
    return y.reshape(bsz, t, c), s_new


def _interleave(make_stream, n_parts):
    for _ in zip(*[make_stream(part) for part in range(n_parts)]):
        pass


def _mix_out_kernel(x_ref, y_ref, bonus_ref, g_ref, yb_ref, lnw_ref, lnb_ref, seg_ref, wout_ref, nffn_ref,
                    wr_ref, h1_out, xn_out, eid_out, gate_out, *, c, n_parts):
    rows_per = x_ref.shape[0] // n_parts

    def stream(part):
        rs = pl.ds(part * rows_per, rows_per)
        seg = seg_ref[...]
        y = y_ref[rs, :]
        inv_n = 1.0 / HEAD_SIZE
        mu = _seg_sum(y, seg) * inv_n
        yield
        yc = y - mu
        var = _seg_sum(yc * yc, seg) * inv_n
        yield
        yn = yc * lax.rsqrt(var + GN_EPS) * lnw_ref[...] + lnb_ref[...]
        ya = (yn + bonus_ref[rs, :]) * g_ref[rs, :]
        mix = _dg(ya.astype(BF16), wout_ref[:c, :]) + _dg(yb_ref[rs, :].astype(BF16), wout_ref[c:, :])
        yield
        h1 = x_ref[rs, :] + mix
        h1_out[rs, :] = h1
        xn = _rms(h1, nffn_ref[...])
        xn_out[rs, :] = xn.astype(BF16)
        xh, xl = _split2(xn)
        hi_lo = _dg(xh, wr_ref[...])
        logits = hi_lo[:, :ROUTER_LANES] + (hi_lo[:, ROUTER_LANES:] + _dg(xl, wr_ref[:, :ROUTER_LANES]))
        yield
        lane = lax.broadcasted_iota(I32, logits.shape, 1)
        neg = jnp.float32(-jnp.inf)
        is_g = (lane >= N_EXPERTS) & (lane < N_EXPERTS + N_EXPERT_GROUPS)
        glog = jnp.where(is_g, logits, neg)
        gmax = jnp.max(glog, axis=-1, keepdims=True)
        gsel = jnp.min(jnp.where(glog == gmax, lane, 4 * ROUTER_LANES), axis=-1, keepdims=True) - N_EXPERTS
        gp = 1.0 / jnp.sum(jnp.where(is_g, jnp.exp(glog - gmax), 0.0), axis=-1, keepdims=True)
        in_grp = (lane >= gsel * EXPERTS_PER_GROUP) & (lane < (gsel + 1) * EXPERTS_PER_GROUP)
        elog = jnp.where(in_grp, logits, neg)
        emax = jnp.max(elog, axis=-1, keepdims=True)
        ex = jnp.where(in_grp, jnp.exp(elog - emax), 0.0)
        eprob = ex / jnp.sum(ex, axis=-1, keepdims=True)
        eprob = jnp.where(in_grp, eprob, -1.0)
        yield
        v1 = jnp.max(eprob, axis=-1, keepdims=True)
        i1 = jnp.min(jnp.where(eprob == v1, lane, 4 * ROUTER_LANES), axis=-1, keepdims=True)
        rest = jnp.where(lane == i1, -1.0, eprob)
        v2 = jnp.max(rest, axis=-1, keepdims=True)
        i2 = jnp.min(jnp.where(rest == v2, lane, 4 * ROUTER_LANES), axis=-1, keepdims=True)
        denom = v1 + v2
        eid_out[rs, :] = jnp.where(lane == 0, i1, jnp.where(lane == 1, i2, 0))
        gate_out[rs, :] = jnp.where(lane == 0, gp * v1 / denom, jnp.where(lane == 1, gp * v2 / denom, 0.0))
        yield

    _interleave(stream, n_parts)


def _mix_out(x2, y2, bonus2, g2, yb2, wd, tm, n_parts):
    m, d = x2.shape
    c = y2.shape[1]
    tokd = pl.BlockSpec((tm, d), lambda i: (i, 0))
    tokc = pl.BlockSpec((tm, c), lambda i: (i, 0))
    tokr = pl.BlockSpec((tm, ROUTER_LANES), lambda i: (i, 0))
    weights = [wd[n] for n in ('ln_x_w', 'ln_x_b', 'seg', 'w_out', 'norm_ffn', 'w_router')]
    return pl.pallas_call(
        functools.partial(_mix_out_kernel, c=c, n_parts=n_parts),
        grid=(m // tm,),
        in_specs=[tokd, tokc, tokc, tokc, tokc] + [_full(w.shape) for w in weights],
        out_specs=[tokd, tokd, tokr, tokr],
        out_shape=[jax.ShapeDtypeStruct((m, d), F32), jax.ShapeDtypeStruct((m, d), BF16),
                   jax.ShapeDtypeStruct((m, ROUTER_LANES), I32), jax.ShapeDtypeStruct((m, ROUTER_LANES), F32)],
        compiler_params=pltpu.CompilerParams(dimension_semantics=("arbitrary",), vmem_limit_bytes=VMEM_LIMIT),
        name="mix_out",
    )(x2, y2, bonus2, g2, yb2, *weights)


def _pow2_pieces(count, max_rows, fn):
    off = 0
    rows = max_rows
    while rows >= RUN_ALIGN:
        has = (count & (rows // RUN_ALIGN)) != 0

        @pl.when(has)
        def _(off=off, rows=rows):
            fn(off, rows)
        off = off + jnp.where(has, rows, 0)
        rows //= 2


def _for_each_expert(fn):
    def body(e, carry):
        fn(e)
        return carry
    lax.fori_loop(0, N_EXPERTS, body, 0)


def _pow2_floor(n):
    return 1 << (n.bit_length() - 1)


def _dispatch_kernel(nch_ref, tot_ref, off_ref, loff_ref, tn_ref, toff_ref, nu_ref, *refs, groups, bm, n_blocks):
    n_g = len(groups)
    eid_refs, xn_refs = refs[:n_g], refs[n_g + 1:2 * n_g + 1]
    loffc_ref = refs[n_g]
    xs_out, buf, sem, zbuf, zsem = refs[2 * n_g + 1:]
    i = pl.program_id(0)
    last = pl.num_programs(0) - 1
    max_run = max(tm for tm, _ in groups)

    def start_tile(tile):
        def per_expert(e):
            src0 = loff_ref[tile * N_EXPERTS + e]
            dst0 = off_ref[tile * N_EXPERTS + e]

            def piece(o, rows):
                src = buf.at[tile % 2, pl.ds(pl.multiple_of(src0 + o, RUN_ALIGN), rows)]
                dst = xs_out.at[pl.ds(pl.multiple_of(dst0 + o, RUN_ALIGN), rows)]
                pltpu.make_async_copy(src, dst, sem.at[tile % 2]).start()
            _pow2_pieces(nch_ref[tile * N_EXPERTS + e], max_run, piece)
        _for_each_expert(per_expert)

    def wait_tile(tile):
        def piece(o, rows):
            pltpu.make_async_copy(buf.at[tile % 2, pl.ds(0, rows)], xs_out.at[pl.ds(0, rows)], sem.at[tile % 2]).wait()
        _pow2_pieces(tot_ref[tile], _pow2_floor(buf.shape[1]), piece)

    @pl.when(i >= 2)
    def _():
        wait_tile(i - 2)

    def sort_tile(eid_ref, xn_ref, tm):
        e_rows = eid_ref[...]
        sub = lax.broadcasted_iota(I32, (N_EXPERTS, tm), 0)
        e1 = jnp.where(sub == e_rows[0:1], 1.0, 0.0)
        e2 = jnp.where(sub == e_rows[1:2], 1.0, 0.0)
        before = lax.broadcasted_iota(I32, (tm, tm), 0) < lax.broadcasted_iota(I32, (tm, tm), 1)
        slot = _dg((e1 + e2).astype(BF16), jnp.where(before, 1.0, 0.0).astype(BF16)) + loffc_ref[0]
        l1 = jnp.sum(slot * e1, axis=0, keepdims=True).astype(I32)
        l2 = jnp.sum(slot * e2, axis=0, keepdims=True).astype(I32)
        n_rows = 2 * tm + LOCAL_PAD
        rows = lax.broadcasted_iota(I32, (n_rows, tm), 0)
        perm = jnp.where((rows == l1) | (rows == l2), 1.0, 0.0).astype(BF16)
        buf[i % 2, 0:n_rows, :] = _dg(perm, xn_ref[...])

    first = 0
    for g, (tm, n_tiles) in enumerate(groups):
        pl.when((i >= first) & (i < first + n_tiles))(functools.partial(sort_tile, eid_refs[g], xn_refs[g], tm))
        first += n_tiles
    start_tile(i)

    @pl.when(i == 0)
    def _():
        zbuf[...] = jnp.zeros_like(zbuf)
        half = zbuf.shape[0]

        def zero_copy(off, rows):
            return pltpu.make_async_copy(zbuf.at[pl.ds(0, rows)], xs_out.at[pl.ds(off, rows)], zsem)

        def tails(fn):
            _for_each_expert(lambda e: _pow2_pieces(
                tn_ref[e], half, lambda o, rows: fn(pl.multiple_of(toff_ref[e] + o, RUN_ALIGN), rows)))

        def unused_blocks(fn):
            def body(b, carry):
                fn(pl.multiple_of(b * bm, bm), half)
                fn(pl.multiple_of(b * bm + half, half), half)
                return carry
            lax.fori_loop(nu_ref[0], n_blocks, body, 0)

        tails(lambda off, rows: zero_copy(off, rows).start())
        unused_blocks(lambda off, rows: zero_copy(off, rows).start())
        tails(lambda off, rows: zero_copy(off, rows).wait())
        unused_blocks(lambda off, rows: zero_copy(off, rows).wait())

    @pl.when(i == last)
    def _():
        @pl.when(i >= 1)
        def _():
            wait_tile(i - 1)
        wait_tile(i)


def _dispatch(plan, n_used, eids_t, xns, tms, n_blocks, bm):
    d = xns[0].shape[1]
    groups = tuple((tm, xn.shape[0] // tm) for xn, tm in zip(xns, tms))
    firsts = [sum(n for _, n in groups[:g]) for g in range(len(groups))]
    lbuf = 2 * max(tms) + LOCAL_PAD

    def tile_of(g):
        return lambda i: jnp.clip(i - firsts[g], 0, groups[g][1] - 1)

    imap = lambda f: (lambda i, *_: f(i))
    in_specs = ([pl.BlockSpec((2, tm), imap(lambda i, g=g: (0, tile_of(g)(i)))) for g, tm in enumerate(tms)] +
                [pl.BlockSpec((1, N_EXPERTS, 1), imap(lambda i: (i, 0, 0)))] +
                [pl.BlockSpec((tm, d), imap(lambda i, g=g: (tile_of(g)(i), 0))) for g, tm in enumerate(tms)])
    return pl.pallas_call(
        functools.partial(_dispatch_kernel, groups=groups, bm=bm, n_blocks=n_blocks),
        grid_spec=pltpu.PrefetchScalarGridSpec(
            num_scalar_prefetch=7,
            grid=(sum(n for _, n in groups),),
            in_specs=in_specs,
            out_specs=pl.BlockSpec(memory_space=pl.ANY),
            scratch_shapes=[pltpu.VMEM((2, lbuf, d), F32), pltpu.SemaphoreType.DMA((2,)),
                            pltpu.VMEM((bm // 2, d), F32), pltpu.SemaphoreType.DMA],
        ),
        out_shape=jax.ShapeDtypeStruct((n_blocks * bm, d), F32),
        compiler_params=pltpu.CompilerParams(dimension_semantics=("arbitrary",), vmem_limit_bytes=VMEM_LIMIT),
        name="moe_dispatch",
    )(plan['nch'], plan['tot'], plan['off'], plan['loff'], plan['tail_n'], plan['tail_off'], n_used, *eids_t,
      plan['loff_col'], *xns)


def _experts_kernel(be_ref, nu_ref, xs_ref, wg_ref, wu_ref, wd_ref, yb_ref, wg_b, wu_b, wd_b):
    b = pl.program_id(0)

    @pl.when((b == 0) | (be_ref[b] != be_ref[jnp.maximum(b - 1, 0)]))
    def _():
        wg_b[...] = wg_ref[0].astype(BF16)
        wu_b[...] = wu_ref[0].astype(BF16)
        wd_b[...] = wd_ref[0].astype(BF16)

    @pl.when(b < nu_ref[0])
    def _():
        xb = xs_ref[...].astype(BF16)
        hg = _dg(xb, wg_b[...])
        hu = _dg(xb, wu_b[...])
        act = (hg * _sigmoid(hg) * hu).astype(BF16)
        yb_ref[...] = _dg(act, wd_b[...])

    @pl.when(pl.program_id(0) >= nu_ref[0])
    def _():
        yb_ref[...] = jnp.zeros_like(yb_ref)


def _experts(block_expert, n_used, xs, wg, wu, wdn, bm):
    p, d = xs.shape
    ff = wg.shape[2]
    n_blocks = p // bm
    rows = lambda b, be, nu: (jnp.minimum(b, nu[0] - 1), 0)
    return pl.pallas_call(
        _experts_kernel,
        grid_spec=pltpu.PrefetchScalarGridSpec(
            num_scalar_prefetch=2,
            grid=(n_blocks,),
            in_specs=[pl.BlockSpec((bm, d), rows),
                      pl.BlockSpec((1, d, ff), lambda b, be, nu: (be[b], 0, 0)),
                      pl.BlockSpec((1, d, ff), lambda b, be, nu: (be[b], 0, 0)),
                      pl.BlockSpec((1, ff, d), lambda b, be, nu: (be[b], 0, 0))],
            out_specs=pl.BlockSpec((bm, d), lambda b, be, nu: (b, 0)),
            scratch_shapes=[pltpu.VMEM((d, ff), BF16), pltpu.VMEM((d, ff), BF16), pltpu.VMEM((ff, d), BF16)],
        ),
        out_shape=jax.ShapeDtypeStruct((p, d), F32),
        compiler_params=pltpu.CompilerParams(dimension_semantics=("arbitrary",), vmem_limit_bytes=VMEM_LIMIT),
        name="moe_experts",
    )(block_expert, n_used, xs, wg, wu, wdn)


def _final_kernel(nch_ref, tot_ref, off_ref, loff_ref, h1_ref, eid_ref, gate_ref, loffr_ref, p_ref, yb_hbm,
                  nple_ref, wpg_ref, wpp_ref, nfin_ref, y_out, buf, sem, *, tm, lbuf, tile0, n_parts):
    i = pl.program_id(0)

    def fetch(tile):
        base = (tile0 + tile) * N_EXPERTS

        def per_expert(e):
            src0 = off_ref[base + e]
            dst0 = loff_ref[base + e]

            def piece(o, rows):
                src = yb_hbm.at[pl.ds(pl.multiple_of(src0 + o, RUN_ALIGN), rows)]
                dst = buf.at[tile % 2, pl.ds(pl.multiple_of(dst0 + o, RUN_ALIGN), rows)]
                pltpu.make_async_copy(src, dst, sem.at[tile % 2]).start()
            _pow2_pieces(nch_ref[base + e], tm, piece)
        _for_each_expert(per_expert)

    def wait_fetch(tile):
        def piece(o, rows):
            pltpu.make_async_copy(yb_hbm.at[pl.ds(0, rows)], buf.at[tile % 2, pl.ds(0, rows)], sem.at[tile % 2]).wait()
        _pow2_pieces(tot_ref[tile0 + tile], _pow2_floor(lbuf), piece)

    @pl.when(i == 0)
    def _():
        buf[...] = jnp.zeros_like(buf)
        fetch(i)

    @pl.when(i + 1 < pl.num_programs(0))
    def _():
        fetch(i + 1)

    eid = eid_ref[...]
    lane = lax.broadcasted_iota(I32, (tm, N_EXPERTS), 1)
    e12 = (jnp.where(lane == eid[:, 0:1], 1.0, 0.0) + jnp.where(lane == eid[:, 1:2], 1.0, 0.0)).astype(BF16)
    rows_per = tm // n_parts
    picks = []
    for part in range(n_parts):
        rs = pl.ds(part * rows_per, rows_per)
        lane_p = lax.broadcasted_iota(I32, (rows_per, N_EXPERTS), 1)
        eid_p = eid_ref[rs, :]
        e1 = jnp.where(lane_p == eid_p[:, 0:1], 1.0, 0.0)
        e2 = jnp.where(lane_p == eid_p[:, 1:2], 1.0, 0.0)
        before = (lax.broadcasted_iota(I32, (rows_per, tm), 1)
                  < lax.broadcasted_iota(I32, (rows_per, tm), 0) + part * rows_per)
        slot = _dg(jnp.where(before, 1.0, 0.0).astype(BF16), e12) + loffr_ref[0]
        l1 = jnp.sum(slot * e1, axis=1, keepdims=True).astype(I32)
        l2 = jnp.sum(slot * e2, axis=1, keepdims=True).astype(I32)
        cols = lax.broadcasted_iota(I32, (rows_per, lbuf), 1)
        gate = gate_ref[rs, :]
        picks.append(jnp.where(cols == l1, gate[:, 0:1], jnp.where(cols == l2, gate[:, 1:2], 0.0)).astype(BF16))

    wait_fetch(i)
    sorted_rows = buf[i % 2].astype(BF16)

    def stream(part):
        rs = pl.ds(part * rows_per, rows_per)
        h2 = h1_ref[rs, :] + _dg(picks[part], sorted_rows)
        yield
        gate_in = _rms(h2, nple_ref[...]).astype(BF16)
        pg = _sigmoid(_dg(gate_in, wpg_ref[...]))
        yield
        h3 = h2 + pg * _dg(p_ref[rs, :].astype(BF16), wpp_ref[...])
        y_out[rs, :] = _rms(h3, nfin_ref[...])
        yield

    _interleave(stream, n_parts)


def _final(plan, h1, eid, gate, p2, yb, wd, tm, tile0):
    m, d = h1.shape
    pd = p2.shape[1]
    lbuf = 2 * tm + LOCAL_PAD
    weights = [wd[n] for n in ('norm_ple', 'w_ple_gate', 'w_ple_proj', 'norm_final')]
    imap = lambda f: (lambda i, *_: f(i))
    return pl.pallas_call(
        functools.partial(_final_kernel, tm=tm, lbuf=lbuf, tile0=tile0, n_parts=2),
        grid_spec=pltpu.PrefetchScalarGridSpec(
            num_scalar_prefetch=4,
            grid=(m // tm,),
            in_specs=[pl.BlockSpec((tm, d), imap(lambda i: (i, 0))),
                      pl.BlockSpec((tm, ROUTER_LANES), imap(lambda i: (i, 0))),
                      pl.BlockSpec((tm, ROUTER_LANES), imap(lambda i: (i, 0))),
                      pl.BlockSpec((1, 1, N_EXPERTS), imap(lambda i: (tile0 + i, 0, 0))),
                      pl.BlockSpec((tm, pd), imap(lambda i: (i, 0))),
                      pl.BlockSpec(memory_space=pl.ANY)] +
                     [pl.BlockSpec(w.shape, imap(lambda i, n=len(w.shape): (0,) * n)) for w in weights],
            out_specs=pl.BlockSpec((tm, d), imap(lambda i: (i, 0))),
            scratch_shapes=[pltpu.VMEM((2, lbuf, d), F32), pltpu.SemaphoreType.DMA((2,))],
        ),
        out_shape=jax.ShapeDtypeStruct((m, d), F32),
        compiler_params=pltpu.CompilerParams(dimension_semantics=("arbitrary",), vmem_limit_bytes=VMEM_LIMIT),
        name="moe_final",
    )(plan['nch'], plan['tot'], plan['off'], plan['loff'], h1, eid, gate, plan['loff_row'], p2, yb, *weights)


def _route_plan(eids, tms, bm):
    experts = jnp.arange(N_EXPERTS, dtype=I32)
    counts = []
    for eid, tm in zip(eids, tms):
        onehot = (eid[:, :2, None] == experts).astype(I32)
        counts.append(onehot.reshape(-1, 2 * tm, N_EXPERTS).sum(axis=1))
    n = jnp.concatenate(counts)
    n_al = (n + RUN_ALIGN - 1) // RUN_ALIGN * RUN_ALIGN
    loff = jnp.cumsum(n_al, axis=1) - n_al
    used = n_al.sum(axis=0)
    region = (used + bm - 1) // bm * bm
    pend = jnp.cumsum(region)
    off = (pend - region)[None, :] + jnp.cumsum(n_al, axis=0) - n_al
    n_assign = sum(2 * e.shape[0] for e in eids)
    n_blocks = -(-(n_assign + (RUN_ALIGN - 1) * N_EXPERTS * n.shape[0] + N_EXPERTS * (bm - 1)) // bm)
    block_start = jnp.arange(n_blocks, dtype=I32) * bm
    block_expert = jnp.minimum(jnp.sum((pend[None, :] <= block_start[:, None]).astype(I32), axis=1), N_EXPERTS - 1)
    plan = dict(nch=(n_al // RUN_ALIGN).reshape(-1).astype(I32), off=off.reshape(-1).astype(I32),
                tot=(n_al.sum(axis=1) // RUN_ALIGN).astype(I32),
                loff=loff.reshape(-1).astype(I32), tail_n=((region - used) // RUN_ALIGN).astype(I32),
                tail_off=(pend - region + used).astype(I32), loff_col=loff.astype(F32)[:, :, None],
                loff_row=loff.astype(F32)[:, None, :])
    return plan, block_expert.astype(I32), (pend[-1] // bm).astype(I32).reshape(1), n_blocks


def _layer_front(x, shift0, wkv0, conv0, wd, tm_in, tm_tok, wkv_tile, wkv_sub, wkv_chunk, chained):
    bsz, t, d = x.shape
    r, lw, k2, v, ah, bh, g, bonus, yb, shift_new, conv_new = _mix_in(x, shift0, conv0, wd, tm_in)
    y, s_new = _wkv(r, lw, k2, v, ah, bh, jnp.swapaxes(wkv0, -1, -2), wkv_tile, wkv_sub, wkv_chunk, chained)
    flat = lambda z: z.reshape(bsz * t, z.shape[-1])
    h1, xn2, eid, gate = _mix_out(flat(x), flat(y), flat(bonus), flat(g), flat(yb), wd, tm_tok, 2)
    return h1, xn2, eid, gate, shift_new.reshape(bsz, d), jnp.swapaxes(s_new, -1, -2), conv_new


def kernel(x_prompt, x_sample, state_shift, state_wkv, cache_conv, p_prompt, p_sample, norm_mix, w_in, mu_rkv, mu_w, mu_a, mu_g, w0, w1, w2, a0, a1, a2, g1, g2, k_k, k_a, r_k, ln_x_w, ln_x_b, dw_w, dw_b, cln_w, cln_b, w_out, norm_ffn, w_router_group, w_router_expert, w_exp_gate, w_exp_up, w_exp_down, norm_ple, w_ple_gate, w_ple_proj, norm_final):
    depth = norm_mix.shape[0]
    assert depth == 1
    d = x_prompt.shape[-1]
    c = w0.shape[-1]
    row = lambda z: z[0].reshape(1, -1).astype(F32)
    lane = jnp.arange(LANES, dtype=I32) // HEAD_SIZE
    w_router = jnp.concatenate([w_router_expert[0], w_router_group[0],
                                jnp.zeros((d, ROUTER_LANES - N_EXPERTS - N_EXPERT_GROUPS), F32)], axis=1)
    wd = dict(
        norm_mix=row(norm_mix), w_in=w_in[0].astype(BF16), mu_rkv=row(mu_rkv), mu_w=row(mu_w), mu_a=row(mu_a),
        mu_g=row(mu_g), w0=row(w0), w1=w1[0].astype(BF16), w2=w2[0].astype(BF16), a0=row(a0),
        a1=a1[0].astype(BF16), a2=a2[0].astype(BF16), g1=g1[0].astype(BF16), g2=g2[0].astype(BF16),
        k_k=row(k_k), k_a=row(k_a), r_k=row(r_k), ln_x_w=row(ln_x_w), ln_x_b=row(ln_x_b),
        seg=(lane[:, None] == lane[None, :]).astype(BF16),
        dw_w=dw_w[0].astype(F32), dw_b=row(dw_b), cln_w=row(cln_w), cln_b=row(cln_b),
        w_out=w_out[0].astype(BF16), norm_ffn=row(norm_ffn),
        w_router=jnp.concatenate(_split2(w_router), axis=1),
        norm_ple=row(norm_ple), w_ple_gate=w_ple_gate[0].astype(BF16), w_ple_proj=w_ple_proj[0].astype(BF16),
        norm_final=norm_final.reshape(1, -1).astype(F32),
    )
    bp, tp, _ = x_prompt.shape
    bs, ts, _ = x_sample.shape
    mp, ms = bp * tp, bs * ts
    tm_p = min(256, tp)
    tm_s = min(256, ms)

    zeros = lambda *s: jnp.zeros(s, F32)
    h1_p, xn_p, eid_p, gate_p, shift_p, wkv_p, conv_p = _layer_front(
        x_prompt, zeros(bp, 1, d), zeros(bp, c // HEAD_SIZE, HEAD_SIZE, HEAD_SIZE), zeros(bp, CONV_CARRY, c),
        wd, tm_p, min(512, mp), min(128, tp), 2 if tp >= 256 else 1, min(64, tp), True)
    h1_s, xn_s, eid_s, gate_s, shift_s, wkv_s, conv_s = _layer_front(
        x_sample, state_shift[0][:, None, :], state_wkv[0], cache_conv[0],
        wd, ts, tm_s, min(128, ms), 2 if ms >= 256 else 1, ts, False)

    bm = 512
    tr_p, tr_s = min(512, mp), min(512, ms)
    plan, block_expert, n_used, n_blocks = _route_plan([eid_p, eid_s], [tr_p, tr_s], bm)
    tiles_p = mp // tr_p
    xs = _dispatch(plan, n_used, [eid_p[:, :2].T, eid_s[:, :2].T], [xn_p, xn_s], [tr_p, tr_s], n_blocks, bm)
    yb = _experts(block_expert, n_used, xs, w_exp_gate[0], w_exp_up[0], w_exp_down[0], bm)
    y_p = _final(plan, h1_p, eid_p, gate_p, p_prompt[0].reshape(mp, -1), yb, wd, tr_p, 0)
    y_s = _final(plan, h1_s, eid_s, gate_s, p_sample[0].reshape(ms, -1), yb, wd, tr_s, tiles_p)
    return (y_p.reshape(x_prompt.shape), y_s.reshape(x_sample.shape), shift_p[None], wkv_p[None], conv_p[None],
            shift_s[None], wkv_s[None], conv_s[None])
```

```python
import functools

import jax
import jax.numpy as jnp
from jax import lax
from jax.experimental import pallas as pl
from jax.experimental.pallas import tpu as pltpu

F32 = jnp.float32
BF16 = jnp.bfloat16
I32 = jnp.int32

HEAD_SIZE = 64
CONV_WIDTH = 31
CONV_CARRY = CONV_WIDTH - 1
SUBLANES = 8
LANES = 128
CARRY_PAD = 32
N_EXPERT_GROUPS = 4
EXPERTS_PER_GROUP = 8
N_EXPERTS = N_EXPERT_GROUPS * EXPERTS_PER_GROUP
ROUTER_LANES = 128
RMS_EPS = 1e-6
LN_EPS = 1e-5
GN_EPS = 64e-5
DECAY_SCALE = 0.6065306597126334
INV_BASE = 16
RUN_ALIGN = 16
LOCAL_PAD = N_EXPERTS * RUN_ALIGN
VMEM_LIMIT = 56 * 1024 * 1024

NN = ((1,), (0,))
NT = ((1,), (1,))
TN = ((0,), (0,))


def _dg(a, b, dims=NN):
    return lax.dot_general(a, b, (dims, ((), ())), preferred_element_type=F32)


def _split2(x):
    hi = x.astype(BF16)
    lo = (x - hi.astype(F32)).astype(BF16)
    return hi, lo


def _split3(x):
    hi = x.astype(BF16)
    r1 = x - hi.astype(F32)
    mid = r1.astype(BF16)
    lo = (r1 - mid.astype(F32)).astype(BF16)
    return hi, mid, lo


def _dot3(a, b, dims=NN):
    ah, al = _split2(a)
    bh, bl = _split2(b)
    return _dg(ah, bh, dims) + (_dg(al, bh, dims) + _dg(ah, bl, dims))


def _bdot(a, b, dims=NN):
    return _dg(a.astype(BF16), b.astype(BF16), dims)


def _mask_dot(mask_bf16, x):
    h, m, l = _split3(x)
    return _dg(mask_bf16, h) + (_dg(mask_bf16, m) + _dg(mask_bf16, l))


def _seg_sum(x, seg_bf16):
    h, l = _split2(x)
    w = seg_bf16.shape[0]
    return jnp.concatenate([_dg(h[:, j:j + w], seg_bf16) + _dg(l[:, j:j + w], seg_bf16)
                            for j in range(0, x.shape[1], w)], axis=1)


def _rms(x, g):
    return x * lax.rsqrt(jnp.mean(x * x, axis=-1, keepdims=True) + RMS_EPS) * g


def _sigmoid(x):
    return 0.5 * jnp.tanh(0.5 * x) + 0.5


def _full(shape):
    n = len(shape)
    return pl.BlockSpec(shape, lambda *_: (0,) * n, pipeline_mode=pl.Buffered(1))


def _mix_in_kernel(x_ref, shift_ref, conv_ref, nm_ref, win_ref, murkv_ref, w0_ref, a0_ref, wl1_ref, wl2_ref, g2_ref,
                   kk_ref, ka_ref, rk_ref, seg_ref, dww_ref, dwb_ref, clnw_ref, clnb_ref,
                   r_out, lw_out, k_out, v_out, a_out, b_out, g_out, bonus_out, yb_out, shift_out, conv_out,
                   xn_last, h_last, up_ext, shifted, *, tm, c, n_parts, d_w, d_a):
    i = pl.program_id(1)

    @pl.when(i == 0)
    def _():
        sp = shift_ref[0]
        xn_last[...] = sp
        sp8 = jnp.broadcast_to(sp, (8, sp.shape[1])).astype(BF16)
        h_last[...] = _dg(sp8, win_ref[:, :3 * c])[0:1]
        up_ext[CARRY_PAD - CONV_CARRY:CARRY_PAD, :] = conv_ref[0]

    rp = tm // n_parts
    last_rows = {}

    def stream(part):
        rs = pl.ds(part * rp, rp)
        xn = _rms(x_ref[0, rs, :], nm_ref[...])
        hin = _dg(xn.astype(BF16), win_ref[...])
        h_rkv = hin[:, :3 * c]
        last_rows[part] = (xn[rp - 1:rp], h_rkv[rp - 1:rp])
        yield
        xn_prev, h_prev = (xn_last[...], h_last[...]) if part == 0 else last_rows[part - 1]
        first = lax.broadcasted_iota(I32, (rp, 1), 0) == 0
        dx = jnp.where(first, xn_prev, pltpu.roll(xn, 1, 0)) - xn
        hprev = jnp.where(first, h_prev, pltpu.roll(h_rkv, 1, 0))
        rkv = h_rkv + (hprev - h_rkv) * murkv_ref[...]
        r = rkv[:, :c]
        k = rkv[:, c:2 * c]
        v = rkv[:, 2 * c:]
        l1 = _dg(jnp.concatenate([xn.astype(BF16), dx.astype(BF16)], axis=1), wl1_ref[...])
        lane = lax.broadcasted_iota(I32, l1.shape, 1)
        act = jnp.where(lane < d_w, jnp.tanh(l1), jnp.where(lane < d_w + d_a, l1, _sigmoid(l1))).astype(BF16)
        yield
        za = _dg(act[:, :d_w + d_a], wl2_ref[...])
        zw = w0_ref[...] + za[:, :c]
        a = _sigmoid(a0_ref[...] + za[:, c:])
        g_out[0, rs, :] = _dg(act[:, d_w + d_a:], g2_ref[...])
        seg = seg_ref[...]
        kk = k * kk_ref[...]
        kk = kk * jnp.minimum(lax.rsqrt(_seg_sum(kk * kk, seg)), 1e12)
        k2 = k * (1.0 + (a - 1.0) * ka_ref[...])
        r_out[0, rs, :] = r
        lw_out[0, rs, :] = -DECAY_SCALE * _sigmoid(zw)
        k_out[0, rs, :] = k2
        v_out[0, rs, :] = v
        a_out[0, rs, :] = -kk
        b_out[0, rs, :] = kk * a
        yield
        bonus_out[0, rs, :] = _seg_sum(r * k2 * rk_ref[...], seg) * v
        u = hin[:, 3 * c:4 * c] * _sigmoid(hin[:, 4 * c:])
        up_ext[pl.ds(CARRY_PAD + part * rp, rp), :] = u
        yield
        first_row = CARRY_PAD - CONV_CARRY + part * rp
        for s in range(SUBLANES):
            span = rp + (CONV_WIDTH - 1 - s) // SUBLANES * SUBLANES
            shifted[part, s, 0:span, :] = up_ext[pl.ds(first_row + s, span), :]
        z = jnp.zeros_like(u) + dwb_ref[...]
        for j in range(CONV_WIDTH):
            s, m = j % SUBLANES, j // SUBLANES
            z = z + dww_ref[j:j + 1, :] * shifted[part, s, m * SUBLANES:m * SUBLANES + rp, :]
        mu = jnp.mean(z, axis=-1, keepdims=True)
        zc = z - mu
        var = jnp.mean(zc * zc, axis=-1, keepdims=True)
        zn = zc * lax.rsqrt(var + LN_EPS) * clnw_ref[...] + clnb_ref[...]
        yb_out[0, rs, :] = zn * _sigmoid(zn)
        yield

    _interleave(stream, n_parts)

    tail = up_ext[pl.ds(tm + CARRY_PAD - CONV_CARRY, CONV_CARRY), :]
    up_ext[CARRY_PAD - CONV_CARRY:CARRY_PAD, :] = tail
    conv_out[0] = tail
    xn_end, h_end = last_rows[n_parts - 1]
    xn_last[...] = xn_end
    h_last[...] = h_end
    shift_out[0] = xn_end


def _mix_in(x, shift0, conv0, wd, tm, n_parts):
    bsz, t, d = x.shape
    c = wd['w0'].shape[1]
    grid = (bsz, t // tm)
    tok = lambda w: pl.BlockSpec((1, tm, w), lambda b, i: (b, i, 0))
    per_seq = lambda rows, w: pl.BlockSpec((1, rows, w), lambda b, i: (b, 0, 0))
    weights = [wd[n] for n in ('norm_mix', 'w_in', 'mu_rkv', 'w0', 'a0', 'w_lora1', 'w_lora2', 'g2',
                               'k_k', 'k_a', 'r_k', 'seg', 'dw_w', 'dw_b', 'cln_w', 'cln_b')]
    d_w, d_a = wd['lora_dims']
    out_tok = jax.ShapeDtypeStruct((bsz, t, c), F32)
    outs = pl.pallas_call(
        functools.partial(_mix_in_kernel, tm=tm, c=c, n_parts=n_parts, d_w=d_w, d_a=d_a),
        grid=grid,
        in_specs=[tok(d), per_seq(1, d), per_seq(CONV_CARRY, c)] + [_full(w.shape) for w in weights],
        out_specs=[tok(c)] * 9 + [per_seq(1, d), per_seq(CONV_CARRY, c)],
        out_shape=[out_tok] * 9 + [jax.ShapeDtypeStruct((bsz, 1, d), F32),
                                   jax.ShapeDtypeStruct((bsz, CONV_CARRY, c), F32)],
        scratch_shapes=[pltpu.VMEM((1, d), F32), pltpu.VMEM((1, 3 * c), F32),
                        pltpu.VMEM((tm + CARRY_PAD, c), F32),
                        pltpu.VMEM((n_parts, SUBLANES, tm // n_parts + CARRY_PAD - SUBLANES, c), F32)],
        compiler_params=pltpu.CompilerParams(dimension_semantics=("arbitrary", "arbitrary"),
                                             vmem_limit_bytes=VMEM_LIMIT),
        name="mix_in",
    )(x, shift0, conv0, *weights)
    return outs


def _tri_inverse(n_strict, row, col, lg_chunk):
    lg_base = INV_BASE.bit_length() - 1
    same = lambda sh: (row >> sh) == (col >> sh)
    lg0 = min(lg_base, lg_chunk)
    n = row.shape[0]

    def expand(c, lg):
        return jnp.where(same(lg), jnp.concatenate([c] * (n >> lg), axis=0), 0.0).astype(BF16)

    def fold(x, lg):
        b = 1 << lg
        return functools.reduce(lambda u, v: u + v, [x[i:i + b] for i in range(0, n, b)])

    b0 = 1 << lg0
    row_c = lax.broadcasted_iota(I32, (b0, n), 0)
    col_c = lax.broadcasted_iota(I32, (b0, n), 1)
    p_full = [jnp.where(same(lg0), x, 0.0) for x in n_strict]
    p = [fold(x, lg0) for x in p_full]
    t = [jnp.where((col_c & (b0 - 1)) == row_c, 1.0, 0.0) + x for x in p]
    p_full = [x.astype(BF16) for x in p_full]
    for _ in range(lg0 - 1):
        p = [_dg(x.astype(BF16), y) for x, y in zip(p, p_full)]
        p_full = [expand(x, lg0) for x in p]
        t = [x + _dg(x.astype(BF16), y) for x, y in zip(t, p_full)]
    for lg in range(lg0, lg_chunk):
        off_mask = same(lg + 1) & jnp.logical_not(same(lg))
        t_full = [expand(x, lg) for x in t]
        u = [_dg(x.astype(BF16), jnp.where(off_mask, m, 0.0).astype(BF16)) for x, m in zip(t, n_strict)]
        add = [_dg(x.astype(BF16), y) for x, y in zip(u, t_full)]
        even = ((lax.broadcasted_iota(I32, (1 << lg, n), 1) >> lg) & 1) == 0
        t = [jnp.concatenate([jnp.where(even, x, 0.0), jnp.where(even, a, x)], axis=0) for x, a in zip(t, add)]
    return [expand(x, lg_chunk) for x in t]


def _wkv_kernel(r_ref, lw_ref, k_ref, v_ref, a_ref, b_ref, s0_ref, y_ref, s_out, s_scr,
                *, tt, n_sub, chunk, chained):
    ti = pl.program_id(1)
    n_heads = r_ref.shape[2] // HEAD_SIZE
    n_chunks = tt // chunk
    lg_chunk = chunk.bit_length() - 1
    row = lax.broadcasted_iota(I32, (tt, tt), 0)
    col = lax.broadcasted_iota(I32, (tt, tt), 1)
    in_chunk = (row >> lg_chunk) == (col >> lg_chunk)
    tri_incl = in_chunk & (col <= row)
    tri_strict = in_chunk & (col < row)
    m_cum = jnp.where(tri_incl, 1.0, 0.0).astype(BF16)
    m_tot = jnp.where(in_chunk, 1.0, 0.0).astype(BF16)
    row_h = lax.broadcasted_iota(I32, (HEAD_SIZE, HEAD_SIZE), 0)
    col_h = lax.broadcasted_iota(I32, (HEAD_SIZE, HEAD_SIZE), 1)
    eye_h = row_h == col_h

    if chained:
        @pl.when(ti == 0)
        def _():
            s_scr[...] = s0_ref[0]

    rt_all, g_end_all, cols = [], [], {name: [] for name in ('v', 'rt', 'at', 'kt', 'bt', 'bd', 'kd')}
    hsl = [slice(HEAD_SIZE * h, HEAD_SIZE * (h + 1)) for h in range(n_heads)]
    for sub in range(n_sub):
        rows = pl.ds(sub * tt, tt)
        lw_all = lw_ref[0, rows, :]
        k_all = k_ref[0, rows, :]
        b_all = b_ref[0, rows, :]
        cum = _mask_dot(m_cum, lw_all)
        tot = _mask_dot(m_tot, lw_all)
        e_neg = jnp.exp(-cum)
        e_end = jnp.exp(tot - cum)
        full = dict(v=v_ref[0, rows, :], rt=r_ref[0, rows, :] * jnp.exp(cum),
                    at=a_ref[0, rows, :] * jnp.exp(cum - lw_all), kt=k_all * e_neg, bt=b_all * e_neg,
                    bd=b_all * e_end, kd=k_all * e_end)
        rt_all.append(full['rt'])
        g_end_all.append(jnp.exp(tot))
        for name, z in full.items():
            cols[name] += [z[:, s_].astype(BF16) for s_ in hsl]
    v, rt, at, kt, bt, bd, kd = (cols[name] for name in ('v', 'rt', 'at', 'kt', 'bt', 'bd', 'kd'))

    units = range(n_sub * n_heads)
    mm = [_dg(jnp.concatenate([at[u], rt[u]], axis=0), jnp.concatenate([bt[u], kt[u]], axis=0), NT) for u in units]
    m_ab = [jnp.where(tri_strict, mm[u][:tt, :tt], 0.0) for u in units]
    m_ak = [jnp.where(tri_strict, mm[u][:tt, tt:], 0.0).astype(BF16) for u in units]
    m_rb = [jnp.where(tri_incl, mm[u][tt:, :tt], 0.0).astype(BF16) for u in units]
    m_rk = [jnp.where(tri_incl, mm[u][tt:, tt:], 0.0).astype(BF16) for u in units]
    tinv = _tri_inverse(m_ab, row, col, lg_chunk)
    akv = [_dg(m_ak[u], v[u]).astype(BF16) for u in units]
    w1 = [_dg(tinv[u], at[u]).astype(BF16) for u in units]
    w2 = [_dg(tinv[u], akv[u]).astype(BF16) for u in units]
    q = [(rt_all[u // n_heads][:, hsl[u % n_heads]] + _dg(m_rb[u], w1[u])).astype(BF16) for u in units]
    y0 = [_dg(m_rb[u], w2[u]) + _dg(m_rk[u], v[u]) for u in units]

    heads = range(n_heads)
    if chained:
        s = [s_scr[h] for h in heads]
    for sub in range(n_sub):
        for ci in range(n_chunks):
            cs = slice(ci * chunk, (ci + 1) * chunk)
            seq = sub * n_chunks + ci
            if not chained:
                s = [s0_ref[seq, h] for h in heads]
            g_row = g_end_all[sub][ci * chunk:ci * chunk + 1]
            un = [sub * n_heads + h for h in heads]
            gm = [jnp.where(eye_h, jnp.broadcast_to(g_row[:, hsl[h]], (HEAD_SIZE, HEAD_SIZE)), 0.0)
                  + _dg(bd[un[h]][cs], w1[un[h]][cs], TN) for h in heads]
            hm = [_dg(bd[un[h]][cs], w2[un[h]][cs], TN) + _dg(kd[un[h]][cs], v[un[h]][cs], TN) for h in heads]
            for h in heads:
                y_ref[0, pl.ds(sub * tt + ci * chunk, chunk), hsl[h]] = _bdot(q[un[h]][cs], s[h]) + y0[un[h]][cs]
            s = [_dot3(gm[h], s[h]) + hm[h] for h in heads]
            if not chained:
                for h in heads:
                    s_out[seq, h] = s[h]
    if chained:
        for h in heads:
            s_scr[h] = s[h]
            s_out[0, h] = s[h]


def _wkv(r, lw, k, v, a, b, s0t, tt, n_sub, chunk, chained):
    bsz, t, c = r.shape
    n_heads = c // HEAD_SIZE
    hs = HEAD_SIZE
    step = tt * n_sub
    if chained:
        grid = (bsz, t // step)
        tok = pl.BlockSpec((1, step, c), lambda bi, ti: (bi, ti, 0))
        st = pl.BlockSpec((1, n_heads, hs, hs), lambda bi, ti: (bi, 0, 0, 0))
        args = (r, lw, k, v, a, b)
        y_shape = (bsz, t, c)
    else:
        assert t == chunk and (bsz * t) % step == 0
        n_seq = step // chunk
        grid = (1, bsz * t // step)
        tok = pl.BlockSpec((1, step, c), lambda bi, ti: (0, ti, 0))
        st = pl.BlockSpec((n_seq, n_heads, hs, hs), lambda bi, ti: (ti, 0, 0, 0))
        args = tuple(z.reshape(1, bsz * t, c) for z in (r, lw, k, v, a, b))
        y_shape = (1, bsz * t, c)
    y, s_new = pl.pallas_call(
        functools.partial(_wkv_kernel, tt=tt, n_sub=n_sub, chunk=chunk, chained=chained),
        grid=grid,
        in_specs=[tok] * 6 + [st],
        out_specs=[tok, st],
        out_shape=[jax.ShapeDtypeStruct(y_shape, F32), jax.ShapeDtypeStruct(s0t.shape, F32)],
        scratch_shapes=[pltpu.VMEM((n_heads, hs, hs), F32)],
        compiler_params=pltpu.CompilerParams(dimension_semantics=("arbitrary",) * 2,
                                             vmem_limit_bytes=VMEM_LIMIT),
        name="wkv",
    )(*args, s0t)
    return y.reshape(bsz, t, c), s_new


def _interleave(make_stream, n_parts):
    for _ in zip(*[make_stream(part) for part in range(n_parts)]):
        pass


def _mix_out_kernel(x_ref, y_ref, bonus_ref, g_ref, yb_ref, lnw_ref, lnb_ref, seg_ref, wout_ref, nffn_ref,
                    wr_ref, h1_out, xn_out, eid_out, gate_out, *, c, n_parts):
    rows_per = x_ref.shape[0] // n_parts

    def stream(part):
        rs = pl.ds(part * rows_per, rows_per)
        seg = seg_ref[...]
        y = y_ref[rs, :]
        inv_n = 1.0 / HEAD_SIZE
        mu = _seg_sum(y, seg) * inv_n
        yield
        yc = y - mu
        var = _seg_sum(yc * yc, seg) * inv_n
        yield
        yn = yc * lax.rsqrt(var + GN_EPS) * lnw_ref[...] + lnb_ref[...]
        ya = (yn + bonus_ref[rs, :]) * g_ref[rs, :]
        mix = _dg(ya.astype(BF16), wout_ref[:c, :]) + _dg(yb_ref[rs, :].astype(BF16), wout_ref[c:, :])
        yield
        h1 = x_ref[rs, :] + mix
        h1_out[rs, :] = h1
        xn = _rms(h1, nffn_ref[...])
        xn_out[rs, :] = xn.astype(BF16)
        xh, xl = _split2(xn)
        hi_lo = _dg(xh, wr_ref[...])
        logits = hi_lo[:, :ROUTER_LANES] + (hi_lo[:, ROUTER_LANES:] + _dg(xl, wr_ref[:, :ROUTER_LANES]))
        yield
        lane = lax.broadcasted_iota(I32, logits.shape, 1)
        neg = jnp.float32(-jnp.inf)
        is_g = (lane >= N_EXPERTS) & (lane < N_EXPERTS + N_EXPERT_GROUPS)
        glog = jnp.where(is_g, logits, neg)
        gmax = jnp.max(glog, axis=-1, keepdims=True)
        gsel = jnp.min(jnp.where(glog == gmax, lane, 4 * ROUTER_LANES), axis=-1, keepdims=True) - N_EXPERTS
        gp = 1.0 / jnp.sum(jnp.where(is_g, jnp.exp(glog - gmax), 0.0), axis=-1, keepdims=True)
        in_grp = (lane >= gsel * EXPERTS_PER_GROUP) & (lane < (gsel + 1) * EXPERTS_PER_GROUP)
        elog = jnp.where(in_grp, logits, neg)
        emax = jnp.max(elog, axis=-1, keepdims=True)
        ex = jnp.where(in_grp, jnp.exp(elog - emax), 0.0)
        eprob = ex / jnp.sum(ex, axis=-1, keepdims=True)
        eprob = jnp.where(in_grp, eprob, -1.0)
        yield
        v1 = jnp.max(eprob, axis=-1, keepdims=True)
        i1 = jnp.min(jnp.where(eprob == v1, lane, 4 * ROUTER_LANES), axis=-1, keepdims=True)
        rest = jnp.where(lane == i1, -1.0, eprob)
        v2 = jnp.max(rest, axis=-1, keepdims=True)
        i2 = jnp.min(jnp.where(rest == v2, lane, 4 * ROUTER_LANES), axis=-1, keepdims=True)
        denom = v1 + v2
        eid_out[rs, :] = jnp.where(lane == 0, i1, jnp.where(lane == 1, i2, 0))
        gate_out[rs, :] = jnp.where(lane == 0, gp * v1 / denom, jnp.where(lane == 1, gp * v2 / denom, 0.0))
        yield

    _interleave(stream, n_parts)


def _mix_out(x2, y2, bonus2, g2, yb2, wd, tm, n_parts):
    m, d = x2.shape
    c = y2.shape[1]
    tokd = pl.BlockSpec((tm, d), lambda i: (i, 0))
    tokc = pl.BlockSpec((tm, c), lambda i: (i, 0))
    tokr = pl.BlockSpec((tm, ROUTER_LANES), lambda i: (i, 0))
    weights = [wd[n] for n in ('ln_x_w', 'ln_x_b', 'seg', 'w_out', 'norm_ffn', 'w_router')]
    return pl.pallas_call(
        functools.partial(_mix_out_kernel, c=c, n_parts=n_parts),
        grid=(m // tm,),
        in_specs=[tokd, tokc, tokc, tokc, tokc] + [_full(w.shape) for w in weights],
        out_specs=[tokd, tokd, tokr, tokr],
        out_shape=[jax.ShapeDtypeStruct((m, d), F32), jax.ShapeDtypeStruct((m, d), BF16),
                   jax.ShapeDtypeStruct((m, ROUTER_LANES), I32), jax.ShapeDtypeStruct((m, ROUTER_LANES), F32)],
        compiler_params=pltpu.CompilerParams(dimension_semantics=("arbitrary",), vmem_limit_bytes=VMEM_LIMIT),
        name="mix_out",
    )(x2, y2, bonus2, g2, yb2, *weights)


def _pow2_pieces(count, max_rows, fn):
    off = 0
    rows = max_rows
    while rows >= RUN_ALIGN:
        has = (count & (rows // RUN_ALIGN)) != 0

        @pl.when(has)
        def _(off=off, rows=rows):
            fn(off, rows)
        off = off + jnp.where(has, rows, 0)
        rows //= 2


def _for_each_expert(fn):
    def body(e, carry):
        fn(e)
        return carry
    lax.fori_loop(0, N_EXPERTS, body, 0)


def _pow2_floor(n):
    return 1 << (n.bit_length() - 1)


def _dispatch_kernel(nch_ref, tot_ref, off_ref, loff_ref, tn_ref, toff_ref, nu_ref, *refs, groups, bm, n_blocks):
    n_g = len(groups)
    eid_refs, xn_refs = refs[:n_g], refs[n_g + 1:2 * n_g + 1]
    loffc_ref = refs[n_g]
    xs_out, buf, sem, zbuf, zsem = refs[2 * n_g + 1:]
    i = pl.program_id(0)
    last = pl.num_programs(0) - 1
    max_run = max(tm for tm, _ in groups)

    def start_tile(tile):
        def per_expert(e):
            src0 = loff_ref[tile * N_EXPERTS + e]
            dst0 = off_ref[tile * N_EXPERTS + e]

            def piece(o, rows):
                src = buf.at[tile % 2, pl.ds(pl.multiple_of(src0 + o, RUN_ALIGN), rows)]
                dst = xs_out.at[pl.ds(pl.multiple_of(dst0 + o, RUN_ALIGN), rows)]
                pltpu.make_async_copy(src, dst, sem.at[tile % 2]).start()
            _pow2_pieces(nch_ref[tile * N_EXPERTS + e], max_run, piece)
        _for_each_expert(per_expert)

    def wait_tile(tile):
        def piece(o, rows):
            pltpu.make_async_copy(buf.at[tile % 2, pl.ds(0, rows)], xs_out.at[pl.ds(0, rows)], sem.at[tile % 2]).wait()
        _pow2_pieces(tot_ref[tile], _pow2_floor(buf.shape[1]), piece)

    @pl.when(i >= 2)
    def _():
        wait_tile(i - 2)

    def sort_tile(eid_ref, xn_ref, tm):
        e_rows = eid_ref[...]
        sub = lax.broadcasted_iota(I32, (N_EXPERTS, tm), 0)
        e1 = jnp.where(sub == e_rows[0:1], 1.0, 0.0)
        e2 = jnp.where(sub == e_rows[1:2], 1.0, 0.0)
        before = lax.broadcasted_iota(I32, (tm, tm), 0) < lax.broadcasted_iota(I32, (tm, tm), 1)
        slot = _dg((e1 + e2).astype(BF16), jnp.where(before, 1.0, 0.0).astype(BF16)) + loffc_ref[0]
        l1 = jnp.sum(slot * e1, axis=0, keepdims=True).astype(I32)
        l2 = jnp.sum(slot * e2, axis=0, keepdims=True).astype(I32)
        n_rows = 2 * tm + LOCAL_PAD
        rows = lax.broadcasted_iota(I32, (n_rows, tm), 0)
        perm = jnp.where((rows == l1) | (rows == l2), 1.0, 0.0).astype(BF16)
        buf[i % 2, 0:n_rows, :] = _dg(perm, xn_ref[...]).astype(BF16)

    first = 0
    for g, (tm, n_tiles) in enumerate(groups):
        pl.when((i >= first) & (i < first + n_tiles))(functools.partial(sort_tile, eid_refs[g], xn_refs[g], tm))
        first += n_tiles
    start_tile(i)

    @pl.when(i == 0)
    def _():
        zbuf[...] = jnp.zeros_like(zbuf)
        half = zbuf.shape[0]

        def zero_copy(off, rows):
            return pltpu.make_async_copy(zbuf.at[pl.ds(0, rows)], xs_out.at[pl.ds(off, rows)], zsem)

        def tails(fn):
            _for_each_expert(lambda e: _pow2_pieces(
                tn_ref[e], half, lambda o, rows: fn(pl.multiple_of(toff_ref[e] + o, RUN_ALIGN), rows)))

        def unused_blocks(fn):
            def body(b, carry):
                fn(pl.multiple_of(b * bm, bm), half)
                fn(pl.multiple_of(b * bm + half, half), half)
                return carry
            lax.fori_loop(nu_ref[0], n_blocks, body, 0)

        tails(lambda off, rows: zero_copy(off, rows).start())
        unused_blocks(lambda off, rows: zero_copy(off, rows).start())
        tails(lambda off, rows: zero_copy(off, rows).wait())
        unused_blocks(lambda off, rows: zero_copy(off, rows).wait())

    @pl.when(i == last)
    def _():
        @pl.when(i >= 1)
        def _():
            wait_tile(i - 1)
        wait_tile(i)


def _dispatch(plan, n_used, eids_t, xns, tms, n_blocks, bm):
    d = xns[0].shape[1]
    groups = tuple((tm, xn.shape[0] // tm) for xn, tm in zip(xns, tms))
    firsts = [sum(n for _, n in groups[:g]) for g in range(len(groups))]
    lbuf = 2 * max(tms) + LOCAL_PAD

    def tile_of(g):
        return lambda i: jnp.clip(i - firsts[g], 0, groups[g][1] - 1)

    imap = lambda f: (lambda i, *_: f(i))
    in_specs = ([pl.BlockSpec((2, tm), imap(lambda i, g=g: (0, tile_of(g)(i)))) for g, tm in enumerate(tms)] +
                [pl.BlockSpec((1, N_EXPERTS, 1), imap(lambda i: (i, 0, 0)))] +
                [pl.BlockSpec((tm, d), imap(lambda i, g=g: (tile_of(g)(i), 0))) for g, tm in enumerate(tms)])
    return pl.pallas_call(
        functools.partial(_dispatch_kernel, groups=groups, bm=bm, n_blocks=n_blocks),
        grid_spec=pltpu.PrefetchScalarGridSpec(
            num_scalar_prefetch=7,
            grid=(sum(n for _, n in groups),),
            in_specs=in_specs,
            out_specs=pl.BlockSpec(memory_space=pl.ANY),
            scratch_shapes=[pltpu.VMEM((2, lbuf, d), BF16), pltpu.SemaphoreType.DMA((2,)),
                            pltpu.VMEM((bm // 2, d), BF16), pltpu.SemaphoreType.DMA],
        ),
        out_shape=jax.ShapeDtypeStruct((n_blocks * bm, d), BF16),
        compiler_params=pltpu.CompilerParams(dimension_semantics=("arbitrary",), vmem_limit_bytes=VMEM_LIMIT),
        name="moe_dispatch",
    )(plan['nch'], plan['tot'], plan['off'], plan['loff'], plan['tail_n'], plan['tail_off'], n_used, *eids_t,
      plan['loff_col'], *xns)


def _experts_kernel(be_ref, nu_ref, xs_ref, wg_ref, wu_ref, wd_ref, yb_ref, wg_b, wu_b, wd_b):
    b = pl.program_id(0)

    @pl.when((b == 0) | (be_ref[b] != be_ref[jnp.maximum(b - 1, 0)]))
    def _():
        wg_b[...] = wg_ref[0].astype(BF16)
        wu_b[...] = wu_ref[0].astype(BF16)
        wd_b[...] = wd_ref[0].astype(BF16)

    @pl.when(b < nu_ref[0])
    def _():
        xb = xs_ref[...]
        hg = _dg(xb, wg_b[...])
        hu = _dg(xb, wu_b[...])
        act = (hg * _sigmoid(hg) * hu).astype(BF16)
        yb_ref[...] = _dg(act, wd_b[...]).astype(BF16)

    @pl.when(pl.program_id(0) >= nu_ref[0])
    def _():
        yb_ref[...] = jnp.zeros_like(yb_ref)


def _experts(block_expert, n_used, xs, wg, wu, wdn, bm):
    p, d = xs.shape
    ff = wg.shape[2]
    n_blocks = p // bm
    rows = lambda b, be, nu: (jnp.minimum(b, nu[0] - 1), 0)
    return pl.pallas_call(
        _experts_kernel,
        grid_spec=pltpu.PrefetchScalarGridSpec(
            num_scalar_prefetch=2,
            grid=(n_blocks,),
            in_specs=[pl.BlockSpec((bm, d), rows),
                      pl.BlockSpec((1, d, ff), lambda b, be, nu: (be[b], 0, 0)),
                      pl.BlockSpec((1, d, ff), lambda b, be, nu: (be[b], 0, 0)),
                      pl.BlockSpec((1, ff, d), lambda b, be, nu: (be[b], 0, 0))],
            out_specs=pl.BlockSpec((bm, d), lambda b, be, nu: (b, 0)),
            scratch_shapes=[pltpu.VMEM((d, ff), BF16), pltpu.VMEM((d, ff), BF16), pltpu.VMEM((ff, d), BF16)],
        ),
        out_shape=jax.ShapeDtypeStruct((p, d), BF16),
        compiler_params=pltpu.CompilerParams(dimension_semantics=("arbitrary",), vmem_limit_bytes=VMEM_LIMIT),
        name="moe_experts",
    )(block_expert, n_used, xs, wg, wu, wdn)


def _final_kernel(nch_ref, tot_ref, off_ref, loff_ref, h1_ref, eid_ref, gate_ref, loffr_ref, p_ref, yb_hbm,
                  nple_ref, wpg_ref, wpp_ref, nfin_ref, y_out, buf, sem, *, tm, lbuf, tile0, n_parts):
    i = pl.program_id(0)

    def fetch(tile):
        base = (tile0 + tile) * N_EXPERTS

        def per_expert(e):
            src0 = off_ref[base + e]
            dst0 = loff_ref[base + e]

            def piece(o, rows):
                src = yb_hbm.at[pl.ds(pl.multiple_of(src0 + o, RUN_ALIGN), rows)]
                dst = buf.at[tile % 2, pl.ds(pl.multiple_of(dst0 + o, RUN_ALIGN), rows)]
                pltpu.make_async_copy(src, dst, sem.at[tile % 2]).start()
            _pow2_pieces(nch_ref[base + e], tm, piece)
        _for_each_expert(per_expert)

    def wait_fetch(tile):
        def piece(o, rows):
            pltpu.make_async_copy(yb_hbm.at[pl.ds(0, rows)], buf.at[tile % 2, pl.ds(0, rows)], sem.at[tile % 2]).wait()
        _pow2_pieces(tot_ref[tile0 + tile], _pow2_floor(lbuf), piece)

    @pl.when(i == 0)
    def _():
        buf[...] = jnp.zeros_like(buf)
        fetch(i)

    @pl.when(i + 1 < pl.num_programs(0))
    def _():
        fetch(i + 1)

    eid = eid_ref[...]
    lane = lax.broadcasted_iota(I32, (tm, N_EXPERTS), 1)
    e12 = (jnp.where(lane == eid[:, 0:1], 1.0, 0.0) + jnp.where(lane == eid[:, 1:2], 1.0, 0.0)).astype(BF16)
    rows_per = tm // n_parts
    picks = []
    for part in range(n_parts):
        rs = pl.ds(part * rows_per, rows_per)
        lane_p = lax.broadcasted_iota(I32, (rows_per, N_EXPERTS), 1)
        eid_p = eid_ref[rs, :]
        e1 = jnp.where(lane_p == eid_p[:, 0:1], 1.0, 0.0)
        e2 = jnp.where(lane_p == eid_p[:, 1:2], 1.0, 0.0)
        before = (lax.broadcasted_iota(I32, (rows_per, tm), 1)
                  < lax.broadcasted_iota(I32, (rows_per, tm), 0) + part * rows_per)
        slot = _dg(jnp.where(before, 1.0, 0.0).astype(BF16), e12) + loffr_ref[0]
        l1 = jnp.sum(slot * e1, axis=1, keepdims=True).astype(I32)
        l2 = jnp.sum(slot * e2, axis=1, keepdims=True).astype(I32)
        cols = lax.broadcasted_iota(I32, (rows_per, lbuf), 1)
        gate = gate_ref[rs, :]
        picks.append(jnp.where(cols == l1, gate[:, 0:1], jnp.where(cols == l2, gate[:, 1:2], 0.0)).astype(BF16))

    wait_fetch(i)
    sorted_rows = buf[i % 2]

    def stream(part):
        rs = pl.ds(part * rows_per, rows_per)
        h2 = h1_ref[rs, :] + _dg(picks[part], sorted_rows)
        yield
        gate_in = _rms(h2, nple_ref[...]).astype(BF16)
        pg = _sigmoid(_dg(gate_in, wpg_ref[...]))
        yield
        h3 = h2 + pg * _dg(p_ref[rs, :].astype(BF16), wpp_ref[...])
        y_out[rs, :] = _rms(h3, nfin_ref[...])
        yield

    _interleave(stream, n_parts)


def _final(plan, h1, eid, gate, p2, yb, wd, tm, tile0):
    m, d = h1.shape
    pd = p2.shape[1]
    lbuf = 2 * tm + LOCAL_PAD
    weights = [wd[n] for n in ('norm_ple', 'w_ple_gate', 'w_ple_proj', 'norm_final')]
    imap = lambda f: (lambda i, *_: f(i))
    return pl.pallas_call(
        functools.partial(_final_kernel, tm=tm, lbuf=lbuf, tile0=tile0, n_parts=2),
        grid_spec=pltpu.PrefetchScalarGridSpec(
            num_scalar_prefetch=4,
            grid=(m // tm,),
            in_specs=[pl.BlockSpec((tm, d), imap(lambda i: (i, 0))),
                      pl.BlockSpec((tm, ROUTER_LANES), imap(lambda i: (i, 0))),
                      pl.BlockSpec((tm, ROUTER_LANES), imap(lambda i: (i, 0))),
                      pl.BlockSpec((1, 1, N_EXPERTS), imap(lambda i: (tile0 + i, 0, 0))),
                      pl.BlockSpec((tm, pd), imap(lambda i: (i, 0))),
                      pl.BlockSpec(memory_space=pl.ANY)] +
                     [pl.BlockSpec(w.shape, imap(lambda i, n=len(w.shape): (0,) * n)) for w in weights],
            out_specs=pl.BlockSpec((tm, d), imap(lambda i: (i, 0))),
            scratch_shapes=[pltpu.VMEM((2, lbuf, d), BF16), pltpu.SemaphoreType.DMA((2,))],
        ),
        out_shape=jax.ShapeDtypeStruct((m, d), F32),
        compiler_params=pltpu.CompilerParams(dimension_semantics=("arbitrary",), vmem_limit_bytes=VMEM_LIMIT),
        name="moe_final",
    )(plan['nch'], plan['tot'], plan['off'], plan['loff'], h1, eid, gate, plan['loff_row'], p2, yb, *weights)


def _route_plan(eids, tms, bm):
    experts = jnp.arange(N_EXPERTS, dtype=I32)
    counts = []
    for eid, tm in zip(eids, tms):
        onehot = (eid[:, :2, None] == experts).astype(I32)
        counts.append(onehot.reshape(-1, 2 * tm, N_EXPERTS).sum(axis=1))
    n = jnp.concatenate(counts)
    n_al = (n + RUN_ALIGN - 1) // RUN_ALIGN * RUN_ALIGN
    loff = jnp.cumsum(n_al, axis=1) - n_al
    used = n_al.sum(axis=0)
    region = (used + bm - 1) // bm * bm
    pend = jnp.cumsum(region)
    off = (pend - region)[None, :] + jnp.cumsum(n_al, axis=0) - n_al
    n_assign = sum(2 * e.shape[0] for e in eids)
    n_blocks = -(-(n_assign + (RUN_ALIGN - 1) * N_EXPERTS * n.shape[0] + N_EXPERTS * (bm - 1)) // bm)
    block_start = jnp.arange(n_blocks, dtype=I32) * bm
    block_expert = jnp.minimum(jnp.sum((pend[None, :] <= block_start[:, None]).astype(I32), axis=1), N_EXPERTS - 1)
    plan = dict(nch=(n_al // RUN_ALIGN).reshape(-1).astype(I32), off=off.reshape(-1).astype(I32),
                tot=(n_al.sum(axis=1) // RUN_ALIGN).astype(I32),
                loff=loff.reshape(-1).astype(I32), tail_n=((region - used) // RUN_ALIGN).astype(I32),
                tail_off=(pend - region + used).astype(I32), loff_col=loff.astype(F32)[:, :, None],
                loff_row=loff.astype(F32)[:, None, :])
    return plan, block_expert.astype(I32), (pend[-1] // bm).astype(I32).reshape(1), n_blocks


def _layer_front(x, shift0, wkv0, conv0, wd, tm_in, tm_tok, wkv_tile, wkv_sub, wkv_chunk, chained):
    bsz, t, d = x.shape
    r, lw, k2, v, ah, bh, g, bonus, yb, shift_new, conv_new = _mix_in(x, shift0, conv0, wd, tm_in, 2 if tm_in >= 512 else 1)
    y, s_new = _wkv(r, lw, k2, v, ah, bh, jnp.swapaxes(wkv0, -1, -2), wkv_tile, wkv_sub, wkv_chunk, chained)
    flat = lambda z: z.reshape(bsz * t, z.shape[-1])
    h1, xn2, eid, gate = _mix_out(flat(x), flat(y), flat(bonus), flat(g), flat(yb), wd, tm_tok, 2)
    return h1, xn2, eid, gate, shift_new.reshape(bsz, d), jnp.swapaxes(s_new, -1, -2), conv_new


def kernel(x_prompt, x_sample, state_shift, state_wkv, cache_conv, p_prompt, p_sample, norm_mix, w_in, mu_rkv, mu_w, mu_a, mu_g, w0, w1, w2, a0, a1, a2, g1, g2, k_k, k_a, r_k, ln_x_w, ln_x_b, dw_w, dw_b, cln_w, cln_b, w_out, norm_ffn, w_router_group, w_router_expert, w_exp_gate, w_exp_up, w_exp_down, norm_ple, w_ple_gate, w_ple_proj, norm_final):
    depth = norm_mix.shape[0]
    assert depth == 1
    d = x_prompt.shape[-1]
    c = w0.shape[-1]
    row = lambda z: z[0].reshape(1, -1).astype(F32)
    lane = jnp.arange(LANES, dtype=I32) // HEAD_SIZE
    w_router = jnp.concatenate([w_router_expert[0], w_router_group[0],
                                jnp.zeros((d, ROUTER_LANES - N_EXPERTS - N_EXPERT_GROUPS), F32)], axis=1)
    first = jnp.concatenate([w1[0], a1[0], g1[0]], axis=1)
    mixed = jnp.concatenate([mu_w[0][:, None] * w1[0], mu_a[0][:, None] * a1[0], mu_g[0][:, None] * g1[0]], axis=1)
    d_w, d_a = w1.shape[2], a1.shape[2]
    second = jnp.concatenate([jnp.concatenate([w2[0], jnp.zeros((d_w, c), F32)], axis=1),
                              jnp.concatenate([jnp.zeros((d_a, c), F32), a2[0]], axis=1)], axis=0)
    wd = dict(
        norm_mix=row(norm_mix), w_in=w_in[0].astype(BF16), mu_rkv=row(mu_rkv), w0=row(w0), a0=row(a0),
        w_lora1=jnp.concatenate([first, mixed], axis=0).astype(BF16), w_lora2=second.astype(BF16),
        g2=g2[0].astype(BF16), lora_dims=(d_w, d_a),
        k_k=row(k_k), k_a=row(k_a), r_k=row(r_k), ln_x_w=row(ln_x_w), ln_x_b=row(ln_x_b),
        seg=(lane[:, None] == lane[None, :]).astype(BF16),
        dw_w=dw_w[0].astype(F32), dw_b=row(dw_b), cln_w=row(cln_w), cln_b=row(cln_b),
        w_out=w_out[0].astype(BF16), norm_ffn=row(norm_ffn),
        w_router=jnp.concatenate(_split2(w_router), axis=1),
        norm_ple=row(norm_ple), w_ple_gate=w_ple_gate[0].astype(BF16), w_ple_proj=w_ple_proj[0].astype(BF16),
        norm_final=norm_final.reshape(1, -1).astype(F32),
    )
    bp, tp, _ = x_prompt.shape
    bs, ts, _ = x_sample.shape
    mp, ms = bp * tp, bs * ts
    tm_p = min(512, tp)
    tm_s = min(256, ms)

    zeros = lambda *s: jnp.zeros(s, F32)
    h1_p, xn_p, eid_p, gate_p, shift_p, wkv_p, conv_p = _layer_front(
        x_prompt, zeros(bp, 1, d), zeros(bp, c // HEAD_SIZE, HEAD_SIZE, HEAD_SIZE), zeros(bp, CONV_CARRY, c),
        wd, tm_p, min(512, mp), min(128, tp), 2 if tp >= 256 else 1, min(64, tp), True)
    h1_s, xn_s, eid_s, gate_s, shift_s, wkv_s, conv_s = _layer_front(
        x_sample, state_shift[0][:, None, :], state_wkv[0], cache_conv[0],
        wd, ts, tm_s, min(128, ms), 2 if ms >= 256 else 1, ts, False)

    bm = 512
    tr_p, tr_s = min(512, mp), min(512, ms)
    plan, block_expert, n_used, n_blocks = _route_plan([eid_p, eid_s], [tr_p, tr_s], bm)
    tiles_p = mp // tr_p
    xs = _dispatch(plan, n_used, [eid_p[:, :2].T, eid_s[:, :2].T], [xn_p, xn_s], [tr_p, tr_s], n_blocks, bm)
    yb = _experts(block_expert, n_used, xs, w_exp_gate[0], w_exp_up[0], w_exp_down[0], bm)
    y_p = _final(plan, h1_p, eid_p, gate_p, p_prompt[0].reshape(mp, -1), yb, wd, tr_p, 0)
    y_s = _final(plan, h1_s, eid_s, gate_s, p_sample[0].reshape(ms, -1), yb, wd, tr_s, tiles_p)
    return (y_p.reshape(x_prompt.shape), y_s.reshape(x_sample.shape), shift_p[None], wkv_p[None], conv_p[None],
            shift_s[None], wkv_s[None], conv_s[None])
```

```python
import functools

import jax
import jax.numpy as jnp
from jax import lax
from jax.experimental import pallas as pl
from jax.experimental.pallas import tpu as pltpu

F32 = jnp.float32
BF16 = jnp.bfloat16
I32 = jnp.int32

HEAD_SIZE = 64
CONV_WIDTH = 31
CONV_CARRY = CONV_WIDTH - 1
SUBLANES = 8
LANES = 128
CARRY_PAD = 32
N_EXPERT_GROUPS = 4
EXPERTS_PER_GROUP = 8
N_EXPERTS = N_EXPERT_GROUPS * EXPERTS_PER_GROUP
ROUTER_LANES = 128
RMS_EPS = 1e-6
LN_EPS = 1e-5
GN_EPS = 64e-5
DECAY_SCALE = 0.6065306597126334
INV_BASE = 16
RUN_ALIGN = 16
LOCAL_PAD = N_EXPERTS * RUN_ALIGN
VMEM_LIMIT = 56 * 1024 * 1024

NN = ((1,), (0,))
NT = ((1,), (1,))
TN = ((0,), (0,))


def _dg(a, b, dims=NN):
    return lax.dot_general(a, b, (dims, ((), ())), preferred_element_type=F32)


def _split2(x):
    hi = x.astype(BF16)
    lo = (x - hi.astype(F32)).astype(BF16)
    return hi, lo


def _split3(x):
    hi = x.astype(BF16)
    r1 = x - hi.astype(F32)
    mid = r1.astype(BF16)
    lo = (r1 - mid.astype(F32)).astype(BF16)
    return hi, mid, lo


def _bdot(a, b, dims=NN):
    return _dg(a.astype(BF16), b.astype(BF16), dims)


def _mask_dot(mask_bf16, x):
    h, m, l = _split3(x)
    return _dg(mask_bf16, h) + (_dg(mask_bf16, m) + _dg(mask_bf16, l))


def _seg_sum(x, seg_bf16):
    h, l = _split2(x)
    w = seg_bf16.shape[0]
    return jnp.concatenate([_dg(h[:, j:j + w], seg_bf16) + _dg(l[:, j:j + w], seg_bf16)
                            for j in range(0, x.shape[1], w)], axis=1)


def _rms(x, g):
    return x * lax.rsqrt(jnp.mean(x * x, axis=-1, keepdims=True) + RMS_EPS) * g


def _sigmoid(x):
    return 0.5 * jnp.tanh(0.5 * x) + 0.5


def _full(shape):
    n = len(shape)
    return pl.BlockSpec(shape, lambda *_: (0,) * n, pipeline_mode=pl.Buffered(1))


def _mix_in_kernel(x_ref, shift_ref, conv_ref, nm_ref, win_ref, murkv_ref, w0_ref, a0_ref, wl1_ref, wl2_ref, g2_ref,
                   kk_ref, ka_ref, rk_ref, seg_ref, dww_ref, dwb_ref, clnw_ref, clnb_ref,
                   r_out, lw_out, k_out, v_out, a_out, b_out, g_out, bonus_out, yb_out, shift_out, conv_out,
                   xn_last, h_last, up_ext, shifted, *, tm, c, n_parts, d_w, d_a):
    i = pl.program_id(1)

    @pl.when(i == 0)
    def _():
        sp = shift_ref[0]
        xn_last[...] = sp
        sp8 = jnp.broadcast_to(sp, (8, sp.shape[1])).astype(BF16)
        h_last[...] = _dg(sp8, win_ref[:, :3 * c])[0:1]
        up_ext[CARRY_PAD - CONV_CARRY:CARRY_PAD, :] = conv_ref[0]

    rp = tm // n_parts
    last_rows = {}

    def stream(part):
        rs = pl.ds(part * rp, rp)
        xn = _rms(x_ref[0, rs, :], nm_ref[...])
        hin = _dg(xn.astype(BF16), win_ref[...])
        h_rkv = hin[:, :3 * c]
        last_rows[part] = (xn[rp - 1:rp], h_rkv[rp - 1:rp])
        yield
        xn_prev, h_prev = (xn_last[...], h_last[...]) if part == 0 else last_rows[part - 1]
        first = lax.broadcasted_iota(I32, (rp, 1), 0) == 0
        dx = jnp.where(first, xn_prev, pltpu.roll(xn, 1, 0)) - xn
        hprev = jnp.where(first, h_prev, pltpu.roll(h_rkv, 1, 0))
        rkv = h_rkv + (hprev - h_rkv) * murkv_ref[...]
        r = rkv[:, :c]
        k = rkv[:, c:2 * c]
        v = rkv[:, 2 * c:]
        l1 = _dg(jnp.concatenate([xn.astype(BF16), dx.astype(BF16)], axis=1), wl1_ref[...])
        lane = lax.broadcasted_iota(I32, l1.shape, 1)
        act = jnp.where(lane < d_w, jnp.tanh(l1), jnp.where(lane < d_w + d_a, l1, _sigmoid(l1))).astype(BF16)
        yield
        za = _dg(act[:, :d_w + d_a], wl2_ref[...])
        zw = w0_ref[...] + za[:, :c]
        a = _sigmoid(a0_ref[...] + za[:, c:])
        g_out[0, rs, :] = _dg(act[:, d_w + d_a:], g2_ref[...])
        seg = seg_ref[...]
        kk = k * kk_ref[...]
        kk = kk * jnp.minimum(lax.rsqrt(_seg_sum(kk * kk, seg)), 1e12)
        k2 = k * (1.0 + (a - 1.0) * ka_ref[...])
        r_out[0, rs, :] = r
        lw_out[0, rs, :] = -DECAY_SCALE * _sigmoid(zw)
        k_out[0, rs, :] = k2
        v_out[0, rs, :] = v
        a_out[0, rs, :] = -kk
        b_out[0, rs, :] = kk * a
        yield
        bonus_out[0, rs, :] = _seg_sum(r * k2 * rk_ref[...], seg) * v
        u = hin[:, 3 * c:4 * c] * _sigmoid(hin[:, 4 * c:])
        up_ext[pl.ds(CARRY_PAD + part * rp, rp), :] = u
        yield
        first_row = CARRY_PAD - CONV_CARRY + part * rp
        for s in range(SUBLANES):
            span = rp + (CONV_WIDTH - 1 - s) // SUBLANES * SUBLANES
            shifted[part, s, 0:span, :] = up_ext[pl.ds(first_row + s, span), :]
        z = jnp.zeros_like(u) + dwb_ref[...]
        for j in range(CONV_WIDTH):
            s, m = j % SUBLANES, j // SUBLANES
            z = z + dww_ref[j:j + 1, :] * shifted[part, s, m * SUBLANES:m * SUBLANES + rp, :]
        mu = jnp.mean(z, axis=-1, keepdims=True)
        zc = z - mu
        var = jnp.mean(zc * zc, axis=-1, keepdims=True)
        zn = zc * lax.rsqrt(var + LN_EPS) * clnw_ref[...] + clnb_ref[...]
        yb_out[0, rs, :] = zn * _sigmoid(zn)
        yield

    _interleave(stream, n_parts)

    tail = up_ext[pl.ds(tm + CARRY_PAD - CONV_CARRY, CONV_CARRY), :]
    up_ext[CARRY_PAD - CONV_CARRY:CARRY_PAD, :] = tail
    conv_out[0] = tail
    xn_end, h_end = last_rows[n_parts - 1]
    xn_last[...] = xn_end
    h_last[...] = h_end
    shift_out[0] = xn_end


def _mix_in(x, shift0, conv0, wd, tm, n_parts):
    bsz, t, d = x.shape
    c = wd['w0'].shape[1]
    grid = (bsz, t // tm)
    tok = lambda w: pl.BlockSpec((1, tm, w), lambda b, i: (b, i, 0))
    per_seq = lambda rows, w: pl.BlockSpec((1, rows, w), lambda b, i: (b, 0, 0))
    weights = [wd[n] for n in ('norm_mix', 'w_in', 'mu_rkv', 'w0', 'a0', 'w_lora1', 'w_lora2', 'g2',
                               'k_k', 'k_a', 'r_k', 'seg', 'dw_w', 'dw_b', 'cln_w', 'cln_b')]
    d_w, d_a = wd['lora_dims']
    out_tok = jax.ShapeDtypeStruct((bsz, t, c), F32)
    outs = pl.pallas_call(
        functools.partial(_mix_in_kernel, tm=tm, c=c, n_parts=n_parts, d_w=d_w, d_a=d_a),
        grid=grid,
        in_specs=[tok(d), per_seq(1, d), per_seq(CONV_CARRY, c)] + [_full(w.shape) for w in weights],
        out_specs=[tok(c)] * 9 + [per_seq(1, d), per_seq(CONV_CARRY, c)],
        out_shape=[out_tok] * 9 + [jax.ShapeDtypeStruct((bsz, 1, d), F32),
                                   jax.ShapeDtypeStruct((bsz, CONV_CARRY, c), F32)],
        scratch_shapes=[pltpu.VMEM((1, d), F32), pltpu.VMEM((1, 3 * c), F32),
                        pltpu.VMEM((tm + CARRY_PAD, c), F32),
                        pltpu.VMEM((n_parts, SUBLANES, tm // n_parts + CARRY_PAD - SUBLANES, c), F32)],
        compiler_params=pltpu.CompilerParams(dimension_semantics=("arbitrary", "arbitrary"),
                                             vmem_limit_bytes=VMEM_LIMIT),
        name="mix_in",
    )(x, shift0, conv0, *weights)
    return outs


def _tri_inverse(n_strict, row, col, lg_chunk):
    lg_base = INV_BASE.bit_length() - 1
    same = lambda sh: (row >> sh) == (col >> sh)
    lg0 = min(lg_base, lg_chunk)
    n = row.shape[0]

    def expand(c, lg):
        return jnp.where(same(lg), jnp.concatenate([c] * (n >> lg), axis=0), 0.0).astype(BF16)

    def fold(x, lg):
        b = 1 << lg
        return functools.reduce(lambda u, v: u + v, [x[i:i + b] for i in range(0, n, b)])

    b0 = 1 << lg0
    row_c = lax.broadcasted_iota(I32, (b0, n), 0)
    col_c = lax.broadcasted_iota(I32, (b0, n), 1)
    p_full = [jnp.where(same(lg0), x, 0.0) for x in n_strict]
    p = [fold(x, lg0) for x in p_full]
    t = [jnp.where((col_c & (b0 - 1)) == row_c, 1.0, 0.0) + x for x in p]
    p_full = [x.astype(BF16) for x in p_full]
    for _ in range(lg0 - 1):
        p = [_dg(x.astype(BF16), y) for x, y in zip(p, p_full)]
        p_full = [expand(x, lg0) for x in p]
        t = [x + _dg(x.astype(BF16), y) for x, y in zip(t, p_full)]
    for lg in range(lg0, lg_chunk):
        off_mask = same(lg + 1) & jnp.logical_not(same(lg))
        t_full = [expand(x, lg) for x in t]
        u = [_dg(x.astype(BF16), jnp.where(off_mask, m, 0.0).astype(BF16)) for x, m in zip(t, n_strict)]
        add = [_dg(x.astype(BF16), y) for x, y in zip(u, t_full)]
        even = ((lax.broadcasted_iota(I32, (1 << lg, n), 1) >> lg) & 1) == 0
        t = [jnp.concatenate([jnp.where(even, x, 0.0), jnp.where(even, a, x)], axis=0) for x, a in zip(t, add)]
    return [expand(x, lg_chunk) for x in t]


def _wkv_kernel(r_ref, lw_ref, k_ref, v_ref, a_ref, b_ref, s0_ref, y_ref, s_out, s_scr,
                *, tt, n_sub, chunk, chained):
    ti = pl.program_id(1)
    n_heads = r_ref.shape[2] // HEAD_SIZE
    n_chunks = tt // chunk
    lg_chunk = chunk.bit_length() - 1
    row = lax.broadcasted_iota(I32, (tt, tt), 0)
    col = lax.broadcasted_iota(I32, (tt, tt), 1)
    in_chunk = (row >> lg_chunk) == (col >> lg_chunk)
    tri_incl = in_chunk & (col <= row)
    tri_strict = in_chunk & (col < row)
    m_cum = jnp.where(tri_incl, 1.0, 0.0).astype(BF16)
    m_tot = jnp.where(in_chunk, 1.0, 0.0).astype(BF16)
    row_h = lax.broadcasted_iota(I32, (HEAD_SIZE, HEAD_SIZE), 0)
    col_h = lax.broadcasted_iota(I32, (HEAD_SIZE, HEAD_SIZE), 1)
    eye_h = row_h == col_h

    if chained:
        @pl.when(ti == 0)
        def _():
            s_scr[...] = s0_ref[0]

    rt_all, g_end_all, cols = [], [], {name: [] for name in ('v', 'rt', 'at', 'kt', 'bt', 'bd', 'kd')}
    hsl = [slice(HEAD_SIZE * h, HEAD_SIZE * (h + 1)) for h in range(n_heads)]
    for sub in range(n_sub):
        rows = pl.ds(sub * tt, tt)
        lw_all = lw_ref[0, rows, :]
        k_all = k_ref[0, rows, :]
        b_all = b_ref[0, rows, :]
        cum = _mask_dot(m_cum, lw_all)
        tot = _mask_dot(m_tot, lw_all)
        e_neg = jnp.exp(-cum)
        e_end = jnp.exp(tot - cum)
        full = dict(v=v_ref[0, rows, :], rt=r_ref[0, rows, :] * jnp.exp(cum),
                    at=a_ref[0, rows, :] * jnp.exp(cum - lw_all), kt=k_all * e_neg, bt=b_all * e_neg,
                    bd=b_all * e_end, kd=k_all * e_end)
        rt_all.append(full['rt'])
        g_end_all.append(jnp.exp(tot))
        for name, z in full.items():
            cols[name] += [z[:, s_].astype(BF16) for s_ in hsl]
    v, rt, at, kt, bt, bd, kd = (cols[name] for name in ('v', 'rt', 'at', 'kt', 'bt', 'bd', 'kd'))

    units = range(n_sub * n_heads)
    mm = [_dg(jnp.concatenate([at[u], rt[u]], axis=0), jnp.concatenate([bt[u], kt[u]], axis=0), NT) for u in units]
    m_ab = [jnp.where(tri_strict, mm[u][:tt, :tt], 0.0) for u in units]
    m_ak = [jnp.where(tri_strict, mm[u][:tt, tt:], 0.0).astype(BF16) for u in units]
    m_rb = [jnp.where(tri_incl, mm[u][tt:, :tt], 0.0).astype(BF16) for u in units]
    m_rk = [jnp.where(tri_incl, mm[u][tt:, tt:], 0.0).astype(BF16) for u in units]
    tinv = _tri_inverse(m_ab, row, col, lg_chunk)
    akv = [_dg(m_ak[u], v[u]).astype(BF16) for u in units]
    w1 = [_dg(tinv[u], at[u]).astype(BF16) for u in units]
    w2 = [_dg(tinv[u], akv[u]).astype(BF16) for u in units]
    q = [(rt_all[u // n_heads][:, hsl[u % n_heads]] + _dg(m_rb[u], w1[u])).astype(BF16) for u in units]
    y0 = [_dg(m_rb[u], w2[u]) + _dg(m_rk[u], v[u]) for u in units]

    heads = range(n_heads)
    if chained:
        s = [s_scr[h] for h in heads]
    for sub in range(n_sub):
        for ci in range(n_chunks):
            cs = slice(ci * chunk, (ci + 1) * chunk)
            seq = sub * n_chunks + ci
            if not chained:
                s = [s0_ref[seq, h] for h in heads]
            g_row = g_end_all[sub][ci * chunk:ci * chunk + 1]
            un = [sub * n_heads + h for h in heads]
            gm = [jnp.where(eye_h, jnp.broadcast_to(g_row[:, hsl[h]], (HEAD_SIZE, HEAD_SIZE)), 0.0)
                  + _dg(bd[un[h]][cs], w1[un[h]][cs], TN) for h in heads]
            hm = [_dg(bd[un[h]][cs], w2[un[h]][cs], TN) + _dg(kd[un[h]][cs], v[un[h]][cs], TN) for h in heads]
            for h in heads:
                y_ref[0, pl.ds(sub * tt + ci * chunk, chunk), hsl[h]] = _bdot(q[un[h]][cs], s[h]) + y0[un[h]][cs]
            s = [_bdot(gm[h], s[h]) + hm[h] for h in heads]
            if not chained:
                for h in heads:
                    s_out[seq, h] = s[h]
    if chained:
        for h in heads:
            s_scr[h] = s[h]
            s_out[0, h] = s[h]


def _wkv(r, lw, k, v, a, b, s0t, tt, n_sub, chunk, chained):
    bsz, t, c = r.shape
    n_heads = c // HEAD_SIZE
    hs = HEAD_SIZE
    step = tt * n_sub
    if chained:
        grid = (bsz, t // step)
        tok = pl.BlockSpec((1, step, c), lambda bi, ti: (bi, ti, 0))
        st = pl.BlockSpec((1, n_heads, hs, hs), lambda bi, ti: (bi, 0, 0, 0))
        args = (r, lw, k, v, a, b)
        y_shape = (bsz, t, c)
    else:
        assert t == chunk and (bsz * t) % step == 0
        n_seq = step // chunk
        grid = (1, bsz * t // step)
        tok = pl.BlockSpec((1, step, c), lambda bi, ti: (0, ti, 0))
        st = pl.BlockSpec((n_seq, n_heads, hs, hs), lambda bi, ti: (ti, 0, 0, 0))
        args = tuple(z.reshape(1, bsz * t, c) for z in (r, lw, k, v, a, b))
        y_shape = (1, bsz * t, c)
    y, s_new = pl.pallas_call(
        functools.partial(_wkv_kernel, tt=tt, n_sub=n_sub, chunk=chunk, chained=chained),
        grid=grid,
        in_specs=[tok] * 6 + [st],
        out_specs=[tok, st],
        out_shape=[jax.ShapeDtypeStruct(y_shape, F32), jax.ShapeDtypeStruct(s0t.shape, F32)],
        scratch_shapes=[pltpu.VMEM((n_heads, hs, hs), F32)],
        compiler_params=pltpu.CompilerParams(dimension_semantics=("arbitrary",) * 2,
                                             vmem_limit_bytes=VMEM_LIMIT),
        name="wkv",
    )(*args, s0t)
    return y.reshape(bsz, t, c), s_new


def _interleave(make_stream, n_parts):
    for _ in zip(*[make_stream(part) for part in range(n_parts)]):
        pass


def _mix_out_kernel(x_ref, y_ref, bonus_ref, g_ref, yb_ref, lnw_ref, lnb_ref, seg_ref, wout_ref, nffn_ref,
                    wr_ref, h1_out, xn_out, eid_out, gate_out, *, c, n_parts):
    rows_per = x_ref.shape[0] // n_parts

    def stream(part):
        rs = pl.ds(part * rows_per, rows_per)
        seg = seg_ref[...]
        y = y_ref[rs, :]
        inv_n = 1.0 / HEAD_SIZE
        mu = _seg_sum(y, seg) * inv_n
        yield
        yc = y - mu
        var = _seg_sum(yc * yc, seg) * inv_n
        yield
        yn = yc * lax.rsqrt(var + GN_EPS) * lnw_ref[...] + lnb_ref[...]
        ya = (yn + bonus_ref[rs, :]) * g_ref[rs, :]
        mix = _dg(ya.astype(BF16), wout_ref[:c, :]) + _dg(yb_ref[rs, :].astype(BF16), wout_ref[c:, :])
        yield
        h1 = x_ref[rs, :] + mix
        h1_out[rs, :] = h1
        xn = _rms(h1, nffn_ref[...])
        xn_out[rs, :] = xn.astype(BF16)
        xh, xl = _split2(xn)
        hi_lo = _dg(xh, wr_ref[...])
        logits = hi_lo[:, :ROUTER_LANES] + (hi_lo[:, ROUTER_LANES:] + _dg(xl, wr_ref[:, :ROUTER_LANES]))
        yield
        lane = lax.broadcasted_iota(I32, logits.shape, 1)
        neg = jnp.float32(-jnp.inf)
        is_g = (lane >= N_EXPERTS) & (lane < N_EXPERTS + N_EXPERT_GROUPS)
        glog = jnp.where(is_g, logits, neg)
        gmax = jnp.max(glog, axis=-1, keepdims=True)
        gsel = jnp.min(jnp.where(glog == gmax, lane, 4 * ROUTER_LANES), axis=-1, keepdims=True) - N_EXPERTS
        gp = 1.0 / jnp.sum(jnp.where(is_g, jnp.exp(glog - gmax), 0.0), axis=-1, keepdims=True)
        in_grp = (lane >= gsel * EXPERTS_PER_GROUP) & (lane < (gsel + 1) * EXPERTS_PER_GROUP)
        elog = jnp.where(in_grp, logits, neg)
        emax = jnp.max(elog, axis=-1, keepdims=True)
        ex = jnp.where(in_grp, jnp.exp(elog - emax), 0.0)
        eprob = ex / jnp.sum(ex, axis=-1, keepdims=True)
        eprob = jnp.where(in_grp, eprob, -1.0)
        yield
        v1 = jnp.max(eprob, axis=-1, keepdims=True)
        i1 = jnp.min(jnp.where(eprob == v1, lane, 4 * ROUTER_LANES), axis=-1, keepdims=True)
        rest = jnp.where(lane == i1, -1.0, eprob)
        v2 = jnp.max(rest, axis=-1, keepdims=True)
        i2 = jnp.min(jnp.where(rest == v2, lane, 4 * ROUTER_LANES), axis=-1, keepdims=True)
        denom = v1 + v2
        eid_out[rs, :] = jnp.where(lane == 0, i1, jnp.where(lane == 1, i2, 0))
        gate_out[rs, :] = jnp.where(lane == 0, gp * v1 / denom, jnp.where(lane == 1, gp * v2 / denom, 0.0))
        yield

    _interleave(stream, n_parts)


def _mix_out(x2, y2, bonus2, g2, yb2, wd, tm, n_parts):
    m, d = x2.shape
    c = y2.shape[1]
    tokd = pl.BlockSpec((tm, d), lambda i: (i, 0))
    tokc = pl.BlockSpec((tm, c), lambda i: (i, 0))
    tokr = pl.BlockSpec((tm, ROUTER_LANES), lambda i: (i, 0))
    weights = [wd[n] for n in ('ln_x_w', 'ln_x_b', 'seg', 'w_out', 'norm_ffn', 'w_router')]
    return pl.pallas_call(
        functools.partial(_mix_out_kernel, c=c, n_parts=n_parts),
        grid=(m // tm,),
        in_specs=[tokd, tokc, tokc, tokc, tokc] + [_full(w.shape) for w in weights],
        out_specs=[tokd, tokd, tokr, tokr],
        out_shape=[jax.ShapeDtypeStruct((m, d), F32), jax.ShapeDtypeStruct((m, d), BF16),
                   jax.ShapeDtypeStruct((m, ROUTER_LANES), I32), jax.ShapeDtypeStruct((m, ROUTER_LANES), F32)],
        compiler_params=pltpu.CompilerParams(dimension_semantics=("arbitrary",), vmem_limit_bytes=VMEM_LIMIT),
        name="mix_out",
    )(x2, y2, bonus2, g2, yb2, *weights)


def _pow2_pieces(count, max_rows, fn):
    off = 0
    rows = max_rows
    while rows >= RUN_ALIGN:
        has = (count & (rows // RUN_ALIGN)) != 0

        @pl.when(has)
        def _(off=off, rows=rows):
            fn(off, rows)
        off = off + jnp.where(has, rows, 0)
        rows //= 2


def _for_each_expert(fn):
    def body(e, carry):
        fn(e)
        return carry
    lax.fori_loop(0, N_EXPERTS, body, 0)


def _pow2_floor(n):
    return 1 << (n.bit_length() - 1)


def _dispatch_kernel(nch_ref, tot_ref, off_ref, loff_ref, tn_ref, toff_ref, nu_ref, *refs, groups, bm, n_blocks):
    n_g = len(groups)
    eid_refs, xn_refs = refs[:n_g], refs[n_g + 1:2 * n_g + 1]
    loffc_ref = refs[n_g]
    xs_out, buf, sem, zbuf, zsem = refs[2 * n_g + 1:]
    i = pl.program_id(0)
    last = pl.num_programs(0) - 1
    max_run = max(tm for tm, _ in groups)

    def start_tile(tile):
        def per_expert(e):
            src0 = loff_ref[tile * N_EXPERTS + e]
            dst0 = off_ref[tile * N_EXPERTS + e]

            def piece(o, rows):
                src = buf.at[tile % 2, pl.ds(pl.multiple_of(src0 + o, RUN_ALIGN), rows)]
                dst = xs_out.at[pl.ds(pl.multiple_of(dst0 + o, RUN_ALIGN), rows)]
                pltpu.make_async_copy(src, dst, sem.at[tile % 2]).start()
            _pow2_pieces(nch_ref[tile * N_EXPERTS + e], max_run, piece)
        _for_each_expert(per_expert)

    def wait_tile(tile):
        def piece(o, rows):
            pltpu.make_async_copy(buf.at[tile % 2, pl.ds(0, rows)], xs_out.at[pl.ds(0, rows)], sem.at[tile % 2]).wait()
        _pow2_pieces(tot_ref[tile], _pow2_floor(buf.shape[1]), piece)

    @pl.when(i >= 2)
    def _():
        wait_tile(i - 2)

    def sort_tile(eid_ref, xn_ref, tm):
        e_rows = eid_ref[...]
        sub = lax.broadcasted_iota(I32, (N_EXPERTS, tm), 0)
        e1 = jnp.where(sub == e_rows[0:1], 1.0, 0.0)
        e2 = jnp.where(sub == e_rows[1:2], 1.0, 0.0)
        before = lax.broadcasted_iota(I32, (tm, tm), 0) < lax.broadcasted_iota(I32, (tm, tm), 1)
        slot = _dg((e1 + e2).astype(BF16), jnp.where(before, 1.0, 0.0).astype(BF16)) + loffc_ref[0]
        l1 = jnp.sum(slot * e1, axis=0, keepdims=True).astype(I32)
        l2 = jnp.sum(slot * e2, axis=0, keepdims=True).astype(I32)
        n_rows = 2 * tm + LOCAL_PAD
        rows = lax.broadcasted_iota(I32, (n_rows, tm), 0)
        perm = jnp.where((rows == l1) | (rows == l2), 1.0, 0.0).astype(BF16)
        buf[i % 2, 0:n_rows, :] = _dg(perm, xn_ref[...]).astype(BF16)

    first = 0
    for g, (tm, n_tiles) in enumerate(groups):
        pl.when((i >= first) & (i < first + n_tiles))(functools.partial(sort_tile, eid_refs[g], xn_refs[g], tm))
        first += n_tiles
    start_tile(i)

    @pl.when(i == 0)
    def _():
        zbuf[...] = jnp.zeros_like(zbuf)
        half = zbuf.shape[0]

        def zero_copy(off, rows):
            return pltpu.make_async_copy(zbuf.at[pl.ds(0, rows)], xs_out.at[pl.ds(off, rows)], zsem)

        def tails(fn):
            _for_each_expert(lambda e: _pow2_pieces(
                tn_ref[e], half, lambda o, rows: fn(pl.multiple_of(toff_ref[e] + o, RUN_ALIGN), rows)))

        def unused_blocks(fn):
            def body(b, carry):
                fn(pl.multiple_of(b * bm, bm), half)
                fn(pl.multiple_of(b * bm + half, half), half)
                return carry
            lax.fori_loop(nu_ref[0], n_blocks, body, 0)

        tails(lambda off, rows: zero_copy(off, rows).start())
        unused_blocks(lambda off, rows: zero_copy(off, rows).start())
        tails(lambda off, rows: zero_copy(off, rows).wait())
        unused_blocks(lambda off, rows: zero_copy(off, rows).wait())

    @pl.when(i == last)
    def _():
        @pl.when(i >= 1)
        def _():
            wait_tile(i - 1)
        wait_tile(i)


def _dispatch(plan, n_used, eids_t, xns, tms, n_blocks, bm):
    d = xns[0].shape[1]
    groups = tuple((tm, xn.shape[0] // tm) for xn, tm in zip(xns, tms))
    firsts = [sum(n for _, n in groups[:g]) for g in range(len(groups))]
    lbuf = 2 * max(tms) + LOCAL_PAD

    def tile_of(g):
        return lambda i: jnp.clip(i - firsts[g], 0, groups[g][1] - 1)

    imap = lambda f: (lambda i, *_: f(i))
    in_specs = ([pl.BlockSpec((2, tm), imap(lambda i, g=g: (0, tile_of(g)(i)))) for g, tm in enumerate(tms)] +
                [pl.BlockSpec((1, N_EXPERTS, 1), imap(lambda i: (i, 0, 0)))] +
                [pl.BlockSpec((tm, d), imap(lambda i, g=g: (tile_of(g)(i), 0))) for g, tm in enumerate(tms)])
    return pl.pallas_call(
        functools.partial(_dispatch_kernel, groups=groups, bm=bm, n_blocks=n_blocks),
        grid_spec=pltpu.PrefetchScalarGridSpec(
            num_scalar_prefetch=7,
            grid=(sum(n for _, n in groups),),
            in_specs=in_specs,
            out_specs=pl.BlockSpec(memory_space=pl.ANY),
            scratch_shapes=[pltpu.VMEM((2, lbuf, d), BF16), pltpu.SemaphoreType.DMA((2,)),
                            pltpu.VMEM((bm // 2, d), BF16), pltpu.SemaphoreType.DMA],
        ),
        out_shape=jax.ShapeDtypeStruct((n_blocks * bm, d), BF16),
        compiler_params=pltpu.CompilerParams(dimension_semantics=("arbitrary",), vmem_limit_bytes=VMEM_LIMIT),
        name="moe_dispatch",
    )(plan['nch'], plan['tot'], plan['off'], plan['loff'], plan['tail_n'], plan['tail_off'], n_used, *eids_t,
      plan['loff_col'], *xns)


def _experts_kernel(be_ref, nu_ref, xs_ref, wg_ref, wu_ref, wd_ref, yb_ref, wg_b, wu_b, wd_b):
    b = pl.program_id(0)

    @pl.when((b == 0) | (be_ref[b] != be_ref[jnp.maximum(b - 1, 0)]))
    def _():
        wg_b[...] = wg_ref[0].astype(BF16)
        wu_b[...] = wu_ref[0].astype(BF16)
        wd_b[...] = wd_ref[0].astype(BF16)

    @pl.when(b < nu_ref[0])
    def _():
        xb = xs_ref[...]
        hg = _dg(xb, wg_b[...])
        hu = _dg(xb, wu_b[...])
        act = (hg * _sigmoid(hg) * hu).astype(BF16)
        yb_ref[...] = _dg(act, wd_b[...]).astype(BF16)

    @pl.when(pl.program_id(0) >= nu_ref[0])
    def _():
        yb_ref[...] = jnp.zeros_like(yb_ref)


def _experts(block_expert, n_used, xs, wg, wu, wdn, bm):
    p, d = xs.shape
    ff = wg.shape[2]
    n_blocks = p // bm
    rows = lambda b, be, nu: (jnp.minimum(b, nu[0] - 1), 0)
    return pl.pallas_call(
        _experts_kernel,
        grid_spec=pltpu.PrefetchScalarGridSpec(
            num_scalar_prefetch=2,
            grid=(n_blocks,),
            in_specs=[pl.BlockSpec((bm, d), rows),
                      pl.BlockSpec((1, d, ff), lambda b, be, nu: (be[b], 0, 0)),
                      pl.BlockSpec((1, d, ff), lambda b, be, nu: (be[b], 0, 0)),
                      pl.BlockSpec((1, ff, d), lambda b, be, nu: (be[b], 0, 0))],
            out_specs=pl.BlockSpec((bm, d), lambda b, be, nu: (b, 0)),
            scratch_shapes=[pltpu.VMEM((d, ff), BF16), pltpu.VMEM((d, ff), BF16), pltpu.VMEM((ff, d), BF16)],
        ),
        out_shape=jax.ShapeDtypeStruct((p, d), BF16),
        compiler_params=pltpu.CompilerParams(dimension_semantics=("arbitrary",), vmem_limit_bytes=VMEM_LIMIT),
        name="moe_experts",
    )(block_expert, n_used, xs, wg, wu, wdn)


def _final_kernel(nch_ref, tot_ref, off_ref, loff_ref, h1_ref, eid_ref, gate_ref, loffr_ref, p_ref, yb_hbm,
                  nple_ref, wpg_ref, wpp_ref, nfin_ref, y_out, buf, sem, *, tm, lbuf, tile0, n_parts):
    i = pl.program_id(0)

    def fetch(tile):
        base = (tile0 + tile) * N_EXPERTS

        def per_expert(e):
            src0 = off_ref[base + e]
            dst0 = loff_ref[base + e]

            def piece(o, rows):
                src = yb_hbm.at[pl.ds(pl.multiple_of(src0 + o, RUN_ALIGN), rows)]
                dst = buf.at[tile % 2, pl.ds(pl.multiple_of(dst0 + o, RUN_ALIGN), rows)]
                pltpu.make_async_copy(src, dst, sem.at[tile % 2]).start()
            _pow2_pieces(nch_ref[base + e], tm, piece)
        _for_each_expert(per_expert)

    def wait_fetch(tile):
        def piece(o, rows):
            pltpu.make_async_copy(yb_hbm.at[pl.ds(0, rows)], buf.at[tile % 2, pl.ds(0, rows)], sem.at[tile % 2]).wait()
        _pow2_pieces(tot_ref[tile0 + tile], _pow2_floor(lbuf), piece)

    @pl.when(i == 0)
    def _():
        buf[...] = jnp.zeros_like(buf)
        fetch(i)

    @pl.when(i + 1 < pl.num_programs(0))
    def _():
        fetch(i + 1)

    eid = eid_ref[...]
    lane = lax.broadcasted_iota(I32, (tm, N_EXPERTS), 1)
    e12 = (jnp.where(lane == eid[:, 0:1], 1.0, 0.0) + jnp.where(lane == eid[:, 1:2], 1.0, 0.0)).astype(BF16)
    rows_per = tm // n_parts
    picks = []
    for part in range(n_parts):
        rs = pl.ds(part * rows_per, rows_per)
        lane_p = lax.broadcasted_iota(I32, (rows_per, N_EXPERTS), 1)
        eid_p = eid_ref[rs, :]
        e1 = jnp.where(lane_p == eid_p[:, 0:1], 1.0, 0.0)
        e2 = jnp.where(lane_p == eid_p[:, 1:2], 1.0, 0.0)
        before = (lax.broadcasted_iota(I32, (rows_per, tm), 1)
                  < lax.broadcasted_iota(I32, (rows_per, tm), 0) + part * rows_per)
        slot = _dg(jnp.where(before, 1.0, 0.0).astype(BF16), e12) + loffr_ref[0]
        l1 = jnp.sum(slot * e1, axis=1, keepdims=True).astype(I32)
        l2 = jnp.sum(slot * e2, axis=1, keepdims=True).astype(I32)
        cols = lax.broadcasted_iota(I32, (rows_per, lbuf), 1)
        gate = gate_ref[rs, :]
        picks.append(jnp.where(cols == l1, gate[:, 0:1], jnp.where(cols == l2, gate[:, 1:2], 0.0)).astype(BF16))

    wait_fetch(i)
    sorted_rows = buf[i % 2]

    def stream(part):
        rs = pl.ds(part * rows_per, rows_per)
        h2 = h1_ref[rs, :] + _dg(picks[part], sorted_rows)
        yield
        gate_in = _rms(h2, nple_ref[...]).astype(BF16)
        pg = _sigmoid(_dg(gate_in, wpg_ref[...]))
        yield
        h3 = h2 + pg * _dg(p_ref[rs, :].astype(BF16), wpp_ref[...])
        y_out[rs, :] = _rms(h3, nfin_ref[...])
        yield

    _interleave(stream, n_parts)


def _final(plan, h1, eid, gate, p2, yb, wd, tm, tile0):
    m, d = h1.shape
    pd = p2.shape[1]
    lbuf = 2 * tm + LOCAL_PAD
    weights = [wd[n] for n in ('norm_ple', 'w_ple_gate', 'w_ple_proj', 'norm_final')]
    imap = lambda f: (lambda i, *_: f(i))
    return pl.pallas_call(
        functools.partial(_final_kernel, tm=tm, lbuf=lbuf, tile0=tile0, n_parts=2),
        grid_spec=pltpu.PrefetchScalarGridSpec(
            num_scalar_prefetch=4,
            grid=(m // tm,),
            in_specs=[pl.BlockSpec((tm, d), imap(lambda i: (i, 0))),
                      pl.BlockSpec((tm, ROUTER_LANES), imap(lambda i: (i, 0))),
                      pl.BlockSpec((tm, ROUTER_LANES), imap(lambda i: (i, 0))),
                      pl.BlockSpec((1, 1, N_EXPERTS), imap(lambda i: (tile0 + i, 0, 0))),
                      pl.BlockSpec((tm, pd), imap(lambda i: (i, 0))),
                      pl.BlockSpec(memory_space=pl.ANY)] +
                     [pl.BlockSpec(w.shape, imap(lambda i, n=len(w.shape): (0,) * n)) for w in weights],
            out_specs=pl.BlockSpec((tm, d), imap(lambda i: (i, 0))),
            scratch_shapes=[pltpu.VMEM((2, lbuf, d), BF16), pltpu.SemaphoreType.DMA((2,))],
        ),
        out_shape=jax.ShapeDtypeStruct((m, d), F32),
        compiler_params=pltpu.CompilerParams(dimension_semantics=("arbitrary",), vmem_limit_bytes=VMEM_LIMIT),
        name="moe_final",
    )(plan['nch'], plan['tot'], plan['off'], plan['loff'], h1, eid, gate, plan['loff_row'], p2, yb, *weights)


def _route_plan(eids, tms, bm):
    experts = jnp.arange(N_EXPERTS, dtype=I32)
    counts = []
    for eid, tm in zip(eids, tms):
        onehot = (eid[:, :2, None] == experts).astype(I32)
        counts.append(onehot.reshape(-1, 2 * tm, N_EXPERTS).sum(axis=1))
    n = jnp.concatenate(counts)
    n_al = (n + RUN_ALIGN - 1) // RUN_ALIGN * RUN_ALIGN
    loff = jnp.cumsum(n_al, axis=1) - n_al
    used = n_al.sum(axis=0)
    region = (used + bm - 1) // bm * bm
    pend = jnp.cumsum(region)
    off = (pend - region)[None, :] + jnp.cumsum(n_al, axis=0) - n_al
    n_assign = sum(2 * e.shape[0] for e in eids)
    n_blocks = -(-(n_assign + (RUN_ALIGN - 1) * N_EXPERTS * n.shape[0] + N_EXPERTS * (bm - 1)) // bm)
    block_start = jnp.arange(n_blocks, dtype=I32) * bm
    block_expert = jnp.minimum(jnp.sum((pend[None, :] <= block_start[:, None]).astype(I32), axis=1), N_EXPERTS - 1)
    plan = dict(nch=(n_al // RUN_ALIGN).reshape(-1).astype(I32), off=off.reshape(-1).astype(I32),
                tot=(n_al.sum(axis=1) // RUN_ALIGN).astype(I32),
                loff=loff.reshape(-1).astype(I32), tail_n=((region - used) // RUN_ALIGN).astype(I32),
                tail_off=(pend - region + used).astype(I32), loff_col=loff.astype(F32)[:, :, None],
                loff_row=loff.astype(F32)[:, None, :])
    return plan, block_expert.astype(I32), (pend[-1] // bm).astype(I32).reshape(1), n_blocks


def _layer_front(x, shift0, wkv0, conv0, wd, tm_in, tm_tok, wkv_tile, wkv_sub, wkv_chunk, chained):
    bsz, t, d = x.shape
    r, lw, k2, v, ah, bh, g, bonus, yb, shift_new, conv_new = _mix_in(x, shift0, conv0, wd, tm_in, 4 if tm_in >= 512 else 1)
    y, s_new = _wkv(r, lw, k2, v, ah, bh, jnp.swapaxes(wkv0, -1, -2), wkv_tile, wkv_sub, wkv_chunk, chained)
    flat = lambda z: z.reshape(bsz * t, z.shape[-1])
    h1, xn2, eid, gate = _mix_out(flat(x), flat(y), flat(bonus), flat(g), flat(yb), wd, tm_tok, 2)
    return h1, xn2, eid, gate, shift_new.reshape(bsz, d), jnp.swapaxes(s_new, -1, -2), conv_new


def kernel(x_prompt, x_sample, state_shift, state_wkv, cache_conv, p_prompt, p_sample, norm_mix, w_in, mu_rkv, mu_w, mu_a, mu_g, w0, w1, w2, a0, a1, a2, g1, g2, k_k, k_a, r_k, ln_x_w, ln_x_b, dw_w, dw_b, cln_w, cln_b, w_out, norm_ffn, w_router_group, w_router_expert, w_exp_gate, w_exp_up, w_exp_down, norm_ple, w_ple_gate, w_ple_proj, norm_final):
    depth = norm_mix.shape[0]
    assert depth == 1
    d = x_prompt.shape[-1]
    c = w0.shape[-1]
    row = lambda z: z[0].reshape(1, -1).astype(F32)
    lane = jnp.arange(LANES, dtype=I32) // HEAD_SIZE
    w_router = jnp.concatenate([w_router_expert[0], w_router_group[0],
                                jnp.zeros((d, ROUTER_LANES - N_EXPERTS - N_EXPERT_GROUPS), F32)], axis=1)
    first = jnp.concatenate([w1[0], a1[0], g1[0]], axis=1)
    mixed = jnp.concatenate([mu_w[0][:, None] * w1[0], mu_a[0][:, None] * a1[0], mu_g[0][:, None] * g1[0]], axis=1)
    d_w, d_a = w1.shape[2], a1.shape[2]
    second = jnp.concatenate([jnp.concatenate([w2[0], jnp.zeros((d_w, c), F32)], axis=1),
                              jnp.concatenate([jnp.zeros((d_a, c), F32), a2[0]], axis=1)], axis=0)
    wd = dict(
        norm_mix=row(norm_mix), w_in=w_in[0].astype(BF16), mu_rkv=row(mu_rkv), w0=row(w0), a0=row(a0),
        w_lora1=jnp.concatenate([first, mixed], axis=0).astype(BF16), w_lora2=second.astype(BF16),
        g2=g2[0].astype(BF16), lora_dims=(d_w, d_a),
        k_k=row(k_k), k_a=row(k_a), r_k=row(r_k), ln_x_w=row(ln_x_w), ln_x_b=row(ln_x_b),
        seg=(lane[:, None] == lane[None, :]).astype(BF16),
        dw_w=dw_w[0].astype(F32), dw_b=row(dw_b), cln_w=row(cln_w), cln_b=row(cln_b),
        w_out=w_out[0].astype(BF16), norm_ffn=row(norm_ffn),
        w_router=jnp.concatenate(_split2(w_router), axis=1),
        norm_ple=row(norm_ple), w_ple_gate=w_ple_gate[0].astype(BF16), w_ple_proj=w_ple_proj[0].astype(BF16),
        norm_final=norm_final.reshape(1, -1).astype(F32),
    )
    bp, tp, _ = x_prompt.shape
    bs, ts, _ = x_sample.shape
    mp, ms = bp * tp, bs * ts
    tm_p = min(512, tp)
    tm_s = min(256, ms)

    zeros = lambda *s: jnp.zeros(s, F32)
    h1_p, xn_p, eid_p, gate_p, shift_p, wkv_p, conv_p = _layer_front(
        x_prompt, zeros(bp, 1, d), zeros(bp, c // HEAD_SIZE, HEAD_SIZE, HEAD_SIZE), zeros(bp, CONV_CARRY, c),
        wd, tm_p, min(512, mp), min(128, tp), 2 if tp >= 256 else 1, min(64, tp), True)
    h1_s, xn_s, eid_s, gate_s, shift_s, wkv_s, conv_s = _layer_front(
        x_sample, state_shift[0][:, None, :], state_wkv[0], cache_conv[0],
        wd, ts, tm_s, min(128, ms), 2 if ms >= 256 else 1, ts, False)

    bm = 512
    tr_p, tr_s = min(512, mp), min(512, ms)
    plan, block_expert, n_used, n_blocks = _route_plan([eid_p, eid_s], [tr_p, tr_s], bm)
    tiles_p = mp // tr_p
    xs = _dispatch(plan, n_used, [eid_p[:, :2].T, eid_s[:, :2].T], [xn_p, xn_s], [tr_p, tr_s], n_blocks, bm)
    yb = _experts(block_expert, n_used, xs, w_exp_gate[0], w_exp_up[0], w_exp_down[0], bm)
    y_p = _final(plan, h1_p, eid_p, gate_p, p_prompt[0].reshape(mp, -1), yb, wd, tr_p, 0)
    y_s = _final(plan, h1_s, eid_s, gate_s, p_sample[0].reshape(ms, -1), yb, wd, tr_s, tiles_p)
    return (y_p.reshape(x_prompt.shape), y_s.reshape(x_sample.shape), shift_p[None], wkv_p[None], conv_p[None],
            shift_s[None], wkv_s[None], conv_s[None])
```

```python
import functools

import jax
import jax.numpy as jnp
from jax import lax
from jax.experimental import pallas as pl
from jax.experimental.pallas import tpu as pltpu

F32 = jnp.float32
BF16 = jnp.bfloat16
I32 = jnp.int32

HEAD_SIZE = 64
CONV_WIDTH = 31
CONV_CARRY = CONV_WIDTH - 1
SUBLANES = 8
LANES = 128
CARRY_PAD = 32
N_EXPERT_GROUPS = 4
EXPERTS_PER_GROUP = 8
N_EXPERTS = N_EXPERT_GROUPS * EXPERTS_PER_GROUP
ROUTER_LANES = 128
RMS_EPS = 1e-6
LN_EPS = 1e-5
GN_EPS = 64e-5
DECAY_SCALE = 0.6065306597126334
INV_BASE = 16
RUN_ALIGN = 16
LOCAL_PAD = N_EXPERTS * RUN_ALIGN
SORT_CHUNK = 256
VMEM_LIMIT = 56 * 1024 * 1024

NN = ((1,), (0,))
NT = ((1,), (1,))
TN = ((0,), (0,))


def _dg(a, b, dims=NN):
    return lax.dot_general(a, b, (dims, ((), ())), preferred_element_type=F32)


def _split2(x):
    hi = x.astype(BF16)
    lo = (x - hi.astype(F32)).astype(BF16)
    return hi, lo


def _split3(x):
    hi = x.astype(BF16)
    r1 = x - hi.astype(F32)
    mid = r1.astype(BF16)
    lo = (r1 - mid.astype(F32)).astype(BF16)
    return hi, mid, lo


def _bdot(a, b, dims=NN):
    return _dg(a.astype(BF16), b.astype(BF16), dims)


def _mask_dot(mask_bf16, x):
    h, m, l = _split3(x)
    return _dg(mask_bf16, h) + (_dg(mask_bf16, m) + _dg(mask_bf16, l))


def _seg_sum(x, seg_bf16):
    h, l = _split2(x)
    w = seg_bf16.shape[0]
    return jnp.concatenate([_dg(h[:, j:j + w], seg_bf16) + _dg(l[:, j:j + w], seg_bf16)
                            for j in range(0, x.shape[1], w)], axis=1)


def _rms(x, g):
    return x * lax.rsqrt(jnp.mean(x * x, axis=-1, keepdims=True) + RMS_EPS) * g


def _sigmoid(x):
    return 0.5 * jnp.tanh(0.5 * x) + 0.5


def _full(shape):
    n = len(shape)
    return pl.BlockSpec(shape, lambda *_: (0,) * n, pipeline_mode=pl.Buffered(1))


def _mix_in_kernel(x_ref, shift_ref, conv_ref, nm_ref, win_ref, murkv_ref, w0_ref, a0_ref, wl1_ref, wl2_ref, g2_ref,
                   kk_ref, ka_ref, rk_ref, seg_ref, dww_ref, dwb_ref, clnw_ref, clnb_ref,
                   r_out, lw_out, k_out, v_out, a_out, b_out, g_out, bonus_out, yb_out, shift_out, conv_out,
                   xn_last, h_last, up_ext, shifted, *, tm, c, n_parts, d_w, d_a):
    i = pl.program_id(1)

    @pl.when(i == 0)
    def _():
        sp = shift_ref[0]
        xn_last[...] = sp
        sp8 = jnp.broadcast_to(sp, (8, sp.shape[1])).astype(BF16)
        h_last[...] = _dg(sp8, win_ref[:, :3 * c])[0:1]
        up_ext[CARRY_PAD - CONV_CARRY:CARRY_PAD, :] = conv_ref[0]

    rp = tm // n_parts
    last_rows = {}

    def stream(part):
        rs = pl.ds(part * rp, rp)
        xn = _rms(x_ref[0, rs, :], nm_ref[...])
        hin = _dg(xn.astype(BF16), win_ref[...])
        h_rkv = hin[:, :3 * c]
        last_rows[part] = (xn[rp - 1:rp], h_rkv[rp - 1:rp])
        yield
        xn_prev, h_prev = (xn_last[...], h_last[...]) if part == 0 else last_rows[part - 1]
        first = lax.broadcasted_iota(I32, (rp, 1), 0) == 0
        dx = jnp.where(first, xn_prev, pltpu.roll(xn, 1, 0)) - xn
        hprev = jnp.where(first, h_prev, pltpu.roll(h_rkv, 1, 0))
        rkv = h_rkv + (hprev - h_rkv) * murkv_ref[...]
        r = rkv[:, :c]
        k = rkv[:, c:2 * c]
        v = rkv[:, 2 * c:]
        l1 = _dg(jnp.concatenate([xn.astype(BF16), dx.astype(BF16)], axis=1), wl1_ref[...])
        lane = lax.broadcasted_iota(I32, l1.shape, 1)
        act = jnp.where(lane < d_w, jnp.tanh(l1), jnp.where(lane < d_w + d_a, l1, _sigmoid(l1))).astype(BF16)
        yield
        za = _dg(act[:, :d_w + d_a], wl2_ref[...])
        zw = w0_ref[...] + za[:, :c]
        a = _sigmoid(a0_ref[...] + za[:, c:])
        g_out[0, rs, :] = _dg(act[:, d_w + d_a:], g2_ref[...])
        seg = seg_ref[...]
        kk = k * kk_ref[...]
        kk = kk * jnp.minimum(lax.rsqrt(_seg_sum(kk * kk, seg)), 1e12)
        k2 = k * (1.0 + (a - 1.0) * ka_ref[...])
        r_out[0, rs, :] = r
        lw_out[0, rs, :] = -DECAY_SCALE * _sigmoid(zw)
        k_out[0, rs, :] = k2
        v_out[0, rs, :] = v
        a_out[0, rs, :] = -kk
        b_out[0, rs, :] = kk * a
        yield
        bonus_out[0, rs, :] = _seg_sum(r * k2 * rk_ref[...], seg) * v
        u = hin[:, 3 * c:4 * c] * _sigmoid(hin[:, 4 * c:])
        up_ext[pl.ds(CARRY_PAD + part * rp, rp), :] = u
        yield
        first_row = CARRY_PAD - CONV_CARRY + part * rp
        for s in range(SUBLANES):
            span = rp + (CONV_WIDTH - 1 - s) // SUBLANES * SUBLANES
            shifted[part, s, 0:span, :] = up_ext[pl.ds(first_row + s, span), :]
        z = jnp.zeros_like(u) + dwb_ref[...]
        for j in range(CONV_WIDTH):
            s, m = j % SUBLANES, j // SUBLANES
            z = z + dww_ref[j:j + 1, :] * shifted[part, s, m * SUBLANES:m * SUBLANES + rp, :]
        mu = jnp.mean(z, axis=-1, keepdims=True)
        zc = z - mu
        var = jnp.mean(zc * zc, axis=-1, keepdims=True)
        zn = zc * lax.rsqrt(var + LN_EPS) * clnw_ref[...] + clnb_ref[...]
        yb_out[0, rs, :] = zn * _sigmoid(zn)
        yield

    _interleave(stream, n_parts)

    tail = up_ext[pl.ds(tm + CARRY_PAD - CONV_CARRY, CONV_CARRY), :]
    up_ext[CARRY_PAD - CONV_CARRY:CARRY_PAD, :] = tail
    conv_out[0] = tail
    xn_end, h_end = last_rows[n_parts - 1]
    xn_last[...] = xn_end
    h_last[...] = h_end
    shift_out[0] = xn_end


def _mix_in(x, shift0, conv0, wd, tm, n_parts):
    bsz, t, d = x.shape
    c = wd['w0'].shape[1]
    grid = (bsz, t // tm)
    tok = lambda w: pl.BlockSpec((1, tm, w), lambda b, i: (b, i, 0))
    per_seq = lambda rows, w: pl.BlockSpec((1, rows, w), lambda b, i: (b, 0, 0))
    weights = [wd[n] for n in ('norm_mix', 'w_in', 'mu_rkv', 'w0', 'a0', 'w_lora1', 'w_lora2', 'g2',
                               'k_k', 'k_a', 'r_k', 'seg', 'dw_w', 'dw_b', 'cln_w', 'cln_b')]
    d_w, d_a = wd['lora_dims']
    out_tok = jax.ShapeDtypeStruct((bsz, t, c), F32)
    outs = pl.pallas_call(
        functools.partial(_mix_in_kernel, tm=tm, c=c, n_parts=n_parts, d_w=d_w, d_a=d_a),
        grid=grid,
        in_specs=[tok(d), per_seq(1, d), per_seq(CONV_CARRY, c)] + [_full(w.shape) for w in weights],
        out_specs=[tok(c)] * 9 + [per_seq(1, d), per_seq(CONV_CARRY, c)],
        out_shape=[out_tok] * 9 + [jax.ShapeDtypeStruct((bsz, 1, d), F32),
                                   jax.ShapeDtypeStruct((bsz, CONV_CARRY, c), F32)],
        scratch_shapes=[pltpu.VMEM((1, d), F32), pltpu.VMEM((1, 3 * c), F32),
                        pltpu.VMEM((tm + CARRY_PAD, c), F32),
                        pltpu.VMEM((n_parts, SUBLANES, tm // n_parts + CARRY_PAD - SUBLANES, c), F32)],
        compiler_params=pltpu.CompilerParams(dimension_semantics=("arbitrary", "arbitrary"),
                                             vmem_limit_bytes=VMEM_LIMIT),
        name="mix_in",
    )(x, shift0, conv0, *weights)
    return outs


def _tri_inverse(n_strict, row, col, lg_chunk):
    lg_base = INV_BASE.bit_length() - 1
    same = lambda sh: (row >> sh) == (col >> sh)
    lg0 = min(lg_base, lg_chunk)
    n = row.shape[0]

    def expand(c, lg):
        return jnp.where(same(lg), jnp.concatenate([c] * (n >> lg), axis=0), 0.0).astype(BF16)

    def fold(x, lg):
        b = 1 << lg
        return functools.reduce(lambda u, v: u + v, [x[i:i + b] for i in range(0, n, b)])

    b0 = 1 << lg0
    row_c = lax.broadcasted_iota(I32, (b0, n), 0)
    col_c = lax.broadcasted_iota(I32, (b0, n), 1)
    p_full = [jnp.where(same(lg0), x, 0.0) for x in n_strict]
    p = [fold(x, lg0) for x in p_full]
    t = [jnp.where((col_c & (b0 - 1)) == row_c, 1.0, 0.0) + x for x in p]
    p_full = [x.astype(BF16) for x in p_full]
    for _ in range(lg0 - 1):
        p = [_dg(x.astype(BF16), y) for x, y in zip(p, p_full)]
        p_full = [expand(x, lg0) for x in p]
        t = [x + _dg(x.astype(BF16), y) for x, y in zip(t, p_full)]
    for lg in range(lg0, lg_chunk):
        off_mask = same(lg + 1) & jnp.logical_not(same(lg))
        t_full = [expand(x, lg) for x in t]
        u = [_dg(x.astype(BF16), jnp.where(off_mask, m, 0.0).astype(BF16)) for x, m in zip(t, n_strict)]
        add = [_dg(x.astype(BF16), y) for x, y in zip(u, t_full)]
        even = ((lax.broadcasted_iota(I32, (1 << lg, n), 1) >> lg) & 1) == 0
        t = [jnp.concatenate([jnp.where(even, x, 0.0), jnp.where(even, a, x)], axis=0) for x, a in zip(t, add)]
    return [expand(x, lg_chunk) for x in t]


def _wkv_kernel(r_ref, lw_ref, k_ref, v_ref, a_ref, b_ref, s0_ref, y_ref, s_out, s_scr,
                *, tt, n_sub, chunk, chained):
    ti = pl.program_id(1)
    n_heads = r_ref.shape[2] // HEAD_SIZE
    n_chunks = tt // chunk
    lg_chunk = chunk.bit_length() - 1
    row = lax.broadcasted_iota(I32, (tt, tt), 0)
    col = lax.broadcasted_iota(I32, (tt, tt), 1)
    in_chunk = (row >> lg_chunk) == (col >> lg_chunk)
    tri_incl = in_chunk & (col <= row)
    tri_strict = in_chunk & (col < row)
    m_cum = jnp.where(tri_incl, 1.0, 0.0).astype(BF16)
    m_tot = jnp.where(in_chunk, 1.0, 0.0).astype(BF16)
    row_h = lax.broadcasted_iota(I32, (HEAD_SIZE, HEAD_SIZE), 0)
    col_h = lax.broadcasted_iota(I32, (HEAD_SIZE, HEAD_SIZE), 1)
    eye_h = row_h == col_h

    if chained:
        @pl.when(ti == 0)
        def _():
            s_scr[...] = s0_ref[0]

    rt_all, g_end_all, cols = [], [], {name: [] for name in ('v', 'rt', 'at', 'kt', 'bt', 'bd', 'kd')}
    hsl = [slice(HEAD_SIZE * h, HEAD_SIZE * (h + 1)) for h in range(n_heads)]
    for sub in range(n_sub):
        rows = pl.ds(sub * tt, tt)
        lw_all = lw_ref[0, rows, :]
        k_all = k_ref[0, rows, :]
        b_all = b_ref[0, rows, :]
        cum = _mask_dot(m_cum, lw_all)
        tot = _mask_dot(m_tot, lw_all)
        e_neg = jnp.exp(-cum)
        e_end = jnp.exp(tot - cum)
        full = dict(v=v_ref[0, rows, :], rt=r_ref[0, rows, :] * jnp.exp(cum),
                    at=a_ref[0, rows, :] * jnp.exp(cum - lw_all), kt=k_all * e_neg, bt=b_all * e_neg,
                    bd=b_all * e_end, kd=k_all * e_end)
        rt_all.append(full['rt'])
        g_end_all.append(jnp.exp(tot))
        for name, z in full.items():
            cols[name] += [z[:, s_].astype(BF16) for s_ in hsl]
    v, rt, at, kt, bt, bd, kd = (cols[name] for name in ('v', 'rt', 'at', 'kt', 'bt', 'bd', 'kd'))

    units = range(n_sub * n_heads)
    mm = [_dg(jnp.concatenate([at[u], rt[u]], axis=0), jnp.concatenate([bt[u], kt[u]], axis=0), NT) for u in units]
    m_ab = [jnp.where(tri_strict, mm[u][:tt, :tt], 0.0) for u in units]
    m_ak = [jnp.where(tri_strict, mm[u][:tt, tt:], 0.0).astype(BF16) for u in units]
    m_rb = [jnp.where(tri_incl, mm[u][tt:, :tt], 0.0).astype(BF16) for u in units]
    m_rk = [jnp.where(tri_incl, mm[u][tt:, tt:], 0.0).astype(BF16) for u in units]
    tinv = _tri_inverse(m_ab, row, col, lg_chunk)
    akv = [_dg(m_ak[u], v[u]).astype(BF16) for u in units]
    w1 = [_dg(tinv[u], at[u]).astype(BF16) for u in units]
    w2 = [_dg(tinv[u], akv[u]).astype(BF16) for u in units]
    q = [(rt_all[u // n_heads][:, hsl[u % n_heads]] + _dg(m_rb[u], w1[u])).astype(BF16) for u in units]
    y0 = [_dg(m_rb[u], w2[u]) + _dg(m_rk[u], v[u]) for u in units]

    heads = range(n_heads)
    if chained:
        s = [s_scr[h] for h in heads]
    for sub in range(n_sub):
        for ci in range(n_chunks):
            cs = slice(ci * chunk, (ci + 1) * chunk)
            seq = sub * n_chunks + ci
            if not chained:
                s = [s0_ref[seq, h] for h in heads]
            g_row = g_end_all[sub][ci * chunk:ci * chunk + 1]
            un = [sub * n_heads + h for h in heads]
            gm = [jnp.where(eye_h, jnp.broadcast_to(g_row[:, hsl[h]], (HEAD_SIZE, HEAD_SIZE)), 0.0)
                  + _dg(bd[un[h]][cs], w1[un[h]][cs], TN) for h in heads]
            hm = [_dg(bd[un[h]][cs], w2[un[h]][cs], TN) + _dg(kd[un[h]][cs], v[un[h]][cs], TN) for h in heads]
            for h in heads:
                y_ref[0, pl.ds(sub * tt + ci * chunk, chunk), hsl[h]] = _bdot(q[un[h]][cs], s[h]) + y0[un[h]][cs]
            s = [_bdot(gm[h], s[h]) + hm[h] for h in heads]
            if not chained:
                for h in heads:
                    s_out[seq, h] = s[h]
    if chained:
        for h in heads:
            s_scr[h] = s[h]
            s_out[0, h] = s[h]


def _wkv(r, lw, k, v, a, b, s0t, tt, n_sub, chunk, chained):
    bsz, t, c = r.shape
    n_heads = c // HEAD_SIZE
    hs = HEAD_SIZE
    step = tt * n_sub
    if chained:
        grid = (bsz, t // step)
        tok = pl.BlockSpec((1, step, c), lambda bi, ti: (bi, ti, 0))
        st = pl.BlockSpec((1, n_heads, hs, hs), lambda bi, ti: (bi, 0, 0, 0))
        args = (r, lw, k, v, a, b)
        y_shape = (bsz, t, c)
    else:
        assert t == chunk and (bsz * t) % step == 0
        n_seq = step // chunk
        grid = (1, bsz * t // step)
        tok = pl.BlockSpec((1, step, c), lambda bi, ti: (0, ti, 0))
        st = pl.BlockSpec((n_seq, n_heads, hs, hs), lambda bi, ti: (ti, 0, 0, 0))
        args = tuple(z.reshape(1, bsz * t, c) for z in (r, lw, k, v, a, b))
        y_shape = (1, bsz * t, c)
    y, s_new = pl.pallas_call(
        functools.partial(_wkv_kernel, tt=tt, n_sub=n_sub, chunk=chunk, chained=chained),
        grid=grid,
        in_specs=[tok] * 6 + [st],
        out_specs=[tok, st],
        out_shape=[jax.ShapeDtypeStruct(y_shape, F32), jax.ShapeDtypeStruct(s0t.shape, F32)],
        scratch_shapes=[pltpu.VMEM((n_heads, hs, hs), F32)],
        compiler_params=pltpu.CompilerParams(dimension_semantics=("arbitrary",) * 2,
                                             vmem_limit_bytes=VMEM_LIMIT),
        name="wkv",
    )(*args, s0t)
    return y.reshape(bsz, t, c), s_new


def _interleave(make_stream, n_parts):
    for _ in zip(*[make_stream(part) for part in range(n_parts)]):
        pass


def _mix_out_kernel(x_ref, y_ref, bonus_ref, g_ref, yb_ref, lnw_ref, lnb_ref, seg_ref, wout_ref, nffn_ref,
                    wr_ref, h1_out, xn_out, eid_out, gate_out, *, c, n_parts):
    rows_per = x_ref.shape[0] // n_parts

    def stream(part):
        rs = pl.ds(part * rows_per, rows_per)
        seg = seg_ref[...]
        y = y_ref[rs, :]
        inv_n = 1.0 / HEAD_SIZE
        mu = _seg_sum(y, seg) * inv_n
        yield
        yc = y - mu
        var = _seg_sum(yc * yc, seg) * inv_n
        yield
        yn = yc * lax.rsqrt(var + GN_EPS) * lnw_ref[...] + lnb_ref[...]
        ya = (yn + bonus_ref[rs, :]) * g_ref[rs, :]
        mix = _dg(ya.astype(BF16), wout_ref[:c, :]) + _dg(yb_ref[rs, :].astype(BF16), wout_ref[c:, :])
        yield
        h1 = x_ref[rs, :] + mix
        h1_out[rs, :] = h1
        xn = _rms(h1, nffn_ref[...])
        xn_out[rs, :] = xn.astype(BF16)
        xh, xl = _split2(xn)
        hi_lo = _dg(xh, wr_ref[...])
        logits = hi_lo[:, :ROUTER_LANES] + (hi_lo[:, ROUTER_LANES:] + _dg(xl, wr_ref[:, :ROUTER_LANES]))
        yield
        lane = lax.broadcasted_iota(I32, logits.shape, 1)
        neg = jnp.float32(-jnp.inf)
        is_g = (lane >= N_EXPERTS) & (lane < N_EXPERTS + N_EXPERT_GROUPS)
        glog = jnp.where(is_g, logits, neg)
        gmax = jnp.max(glog, axis=-1, keepdims=True)
        gsel = jnp.min(jnp.where(glog == gmax, lane, 4 * ROUTER_LANES), axis=-1, keepdims=True) - N_EXPERTS
        gp = 1.0 / jnp.sum(jnp.where(is_g, jnp.exp(glog - gmax), 0.0), axis=-1, keepdims=True)
        in_grp = (lane >= gsel * EXPERTS_PER_GROUP) & (lane < (gsel + 1) * EXPERTS_PER_GROUP)
        elog = jnp.where(in_grp, logits, neg)
        emax = jnp.max(elog, axis=-1, keepdims=True)
        ex = jnp.where(in_grp, jnp.exp(elog - emax), 0.0)
        eprob = ex / jnp.sum(ex, axis=-1, keepdims=True)
        eprob = jnp.where(in_grp, eprob, -1.0)
        yield
        v1 = jnp.max(eprob, axis=-1, keepdims=True)
        i1 = jnp.min(jnp.where(eprob == v1, lane, 4 * ROUTER_LANES), axis=-1, keepdims=True)
        rest = jnp.where(lane == i1, -1.0, eprob)
        v2 = jnp.max(rest, axis=-1, keepdims=True)
        i2 = jnp.min(jnp.where(rest == v2, lane, 4 * ROUTER_LANES), axis=-1, keepdims=True)
        denom = v1 + v2
        eid_out[rs, :] = jnp.where(lane == 0, i1, jnp.where(lane == 1, i2, 0))
        gate_out[rs, :] = jnp.where(lane == 0, gp * v1 / denom, jnp.where(lane == 1, gp * v2 / denom, 0.0))
        yield

    _interleave(stream, n_parts)


def _mix_out(x2, y2, bonus2, g2, yb2, wd, tm, n_parts):
    m, d = x2.shape
    c = y2.shape[1]
    tokd = pl.BlockSpec((tm, d), lambda i: (i, 0))
    tokc = pl.BlockSpec((tm, c), lambda i: (i, 0))
    tokr = pl.BlockSpec((tm, ROUTER_LANES), lambda i: (i, 0))
    weights = [wd[n] for n in ('ln_x_w', 'ln_x_b', 'seg', 'w_out', 'norm_ffn', 'w_router')]
    return pl.pallas_call(
        functools.partial(_mix_out_kernel, c=c, n_parts=n_parts),
        grid=(m // tm,),
        in_specs=[tokd, tokc, tokc, tokc, tokc] + [_full(w.shape) for w in weights],
        out_specs=[tokd, tokd, tokr, tokr],
        out_shape=[jax.ShapeDtypeStruct((m, d), F32), jax.ShapeDtypeStruct((m, d), BF16),
                   jax.ShapeDtypeStruct((m, ROUTER_LANES), I32), jax.ShapeDtypeStruct((m, ROUTER_LANES), F32)],
        compiler_params=pltpu.CompilerParams(dimension_semantics=("arbitrary",), vmem_limit_bytes=VMEM_LIMIT),
        name="mix_out",
    )(x2, y2, bonus2, g2, yb2, *weights)


def _pow2_pieces(count, max_rows, fn):
    off = 0
    rows = max_rows
    while rows >= RUN_ALIGN:
        has = (count & (rows // RUN_ALIGN)) != 0

        @pl.when(has)
        def _(off=off, rows=rows):
            fn(off, rows)
        off = off + jnp.where(has, rows, 0)
        rows //= 2


def _for_each_expert(fn):
    def body(e, carry):
        fn(e)
        return carry
    lax.fori_loop(0, N_EXPERTS, body, 0)


def _pow2_floor(n):
    return 1 << (n.bit_length() - 1)


def _dispatch_kernel(nch_ref, tot_ref, off_ref, loff_ref, tn_ref, toff_ref, nu_ref, *refs, groups, bm, n_blocks):
    n_g = len(groups)
    eid_refs, xn_refs = refs[:n_g], refs[n_g + 1:2 * n_g + 1]
    loffc_ref = refs[n_g]
    xs_out, buf, sem, zbuf, zsem = refs[2 * n_g + 1:]
    i = pl.program_id(0)
    last = pl.num_programs(0) - 1
    max_run = max(tm for tm, _ in groups)

    def start_tile(tile):
        def per_expert(e):
            src0 = loff_ref[tile * N_EXPERTS + e]
            dst0 = off_ref[tile * N_EXPERTS + e]

            def piece(o, rows):
                src = buf.at[tile % 2, pl.ds(pl.multiple_of(src0 + o, RUN_ALIGN), rows)]
                dst = xs_out.at[pl.ds(pl.multiple_of(dst0 + o, RUN_ALIGN), rows)]
                pltpu.make_async_copy(src, dst, sem.at[tile % 2]).start()
            _pow2_pieces(nch_ref[tile * N_EXPERTS + e], max_run, piece)
        _for_each_expert(per_expert)

    def wait_tile(tile):
        def piece(o, rows):
            pltpu.make_async_copy(buf.at[tile % 2, pl.ds(0, rows)], xs_out.at[pl.ds(0, rows)], sem.at[tile % 2]).wait()
        _pow2_pieces(tot_ref[tile], _pow2_floor(buf.shape[1]), piece)

    @pl.when(i >= 2)
    def _():
        wait_tile(i - 2)

    def sort_tile(eid_ref, xn_ref, tm):
        e_rows = eid_ref[...]
        sub = lax.broadcasted_iota(I32, (N_EXPERTS, tm), 0)
        e1 = jnp.where(sub == e_rows[0:1], 1.0, 0.0)
        e2 = jnp.where(sub == e_rows[1:2], 1.0, 0.0)
        before = lax.broadcasted_iota(I32, (tm, tm), 0) < lax.broadcasted_iota(I32, (tm, tm), 1)
        slot = _dg((e1 + e2).astype(BF16), jnp.where(before, 1.0, 0.0).astype(BF16)) + loffc_ref[0]
        l1 = jnp.sum(slot * e1, axis=0, keepdims=True).astype(I32)
        l2 = jnp.sum(slot * e2, axis=0, keepdims=True).astype(I32)
        used_rows = tot_ref[i] * RUN_ALIGN

        def sort_chunk(ci, carry):
            r0 = pl.multiple_of(ci * SORT_CHUNK, SORT_CHUNK)
            rows = lax.broadcasted_iota(I32, (SORT_CHUNK, tm), 0) + r0
            perm = jnp.where((rows == l1) | (rows == l2), 1.0, 0.0).astype(BF16)
            buf[i % 2, pl.ds(r0, SORT_CHUNK), :] = _dg(perm, xn_ref[...]).astype(BF16)
            return carry
        lax.fori_loop(0, (used_rows + SORT_CHUNK - 1) // SORT_CHUNK, sort_chunk, 0)

    first = 0
    for g, (tm, n_tiles) in enumerate(groups):
        pl.when((i >= first) & (i < first + n_tiles))(functools.partial(sort_tile, eid_refs[g], xn_refs[g], tm))
        first += n_tiles
    start_tile(i)

    @pl.when(i == 0)
    def _():
        zbuf[...] = jnp.zeros_like(zbuf)
        half = zbuf.shape[0]

        def zero_copy(off, rows):
            return pltpu.make_async_copy(zbuf.at[pl.ds(0, rows)], xs_out.at[pl.ds(off, rows)], zsem)

        def tails(fn):
            _for_each_expert(lambda e: _pow2_pieces(
                tn_ref[e], half, lambda o, rows: fn(pl.multiple_of(toff_ref[e] + o, RUN_ALIGN), rows)))

        def unused_blocks(fn):
            def body(b, carry):
                fn(pl.multiple_of(b * bm, bm), half)
                fn(pl.multiple_of(b * bm + half, half), half)
                return carry
            lax.fori_loop(nu_ref[0], n_blocks, body, 0)

        tails(lambda off, rows: zero_copy(off, rows).start())
        unused_blocks(lambda off, rows: zero_copy(off, rows).start())
        tails(lambda off, rows: zero_copy(off, rows).wait())
        unused_blocks(lambda off, rows: zero_copy(off, rows).wait())

    @pl.when(i == last)
    def _():
        @pl.when(i >= 1)
        def _():
            wait_tile(i - 1)
        wait_tile(i)


def _dispatch(plan, n_used, eids_t, xns, tms, n_blocks, bm):
    d = xns[0].shape[1]
    groups = tuple((tm, xn.shape[0] // tm) for xn, tm in zip(xns, tms))
    firsts = [sum(n for _, n in groups[:g]) for g in range(len(groups))]
    lbuf = 2 * max(tms) + LOCAL_PAD

    def tile_of(g):
        return lambda i: jnp.clip(i - firsts[g], 0, groups[g][1] - 1)

    imap = lambda f: (lambda i, *_: f(i))
    in_specs = ([pl.BlockSpec((2, tm), imap(lambda i, g=g: (0, tile_of(g)(i)))) for g, tm in enumerate(tms)] +
                [pl.BlockSpec((1, N_EXPERTS, 1), imap(lambda i: (i, 0, 0)))] +
                [pl.BlockSpec((tm, d), imap(lambda i, g=g: (tile_of(g)(i), 0))) for g, tm in enumerate(tms)])
    return pl.pallas_call(
        functools.partial(_dispatch_kernel, groups=groups, bm=bm, n_blocks=n_blocks),
        grid_spec=pltpu.PrefetchScalarGridSpec(
            num_scalar_prefetch=7,
            grid=(sum(n for _, n in groups),),
            in_specs=in_specs,
            out_specs=pl.BlockSpec(memory_space=pl.ANY),
            scratch_shapes=[pltpu.VMEM((2, lbuf, d), BF16), pltpu.SemaphoreType.DMA((2,)),
                            pltpu.VMEM((bm // 2, d), BF16), pltpu.SemaphoreType.DMA],
        ),
        out_shape=jax.ShapeDtypeStruct((n_blocks * bm, d), BF16),
        compiler_params=pltpu.CompilerParams(dimension_semantics=("arbitrary",), vmem_limit_bytes=VMEM_LIMIT),
        name="moe_dispatch",
    )(plan['nch'], plan['tot'], plan['off'], plan['loff'], plan['tail_n'], plan['tail_off'], n_used, *eids_t,
      plan['loff_col'], *xns)


def _experts_kernel(be_ref, slot_ref, nxt_ref, nu_ref, xs_ref, wg_hbm, wu_hbm, wd_hbm, yb_ref,
                    wg_f, wu_f, wd_f, wg_b, wu_b, wd_b, sem):
    b = pl.program_id(0)

    def weight_copies(e, slot):
        pairs = ((wg_hbm, wg_f), (wu_hbm, wu_f), (wd_hbm, wd_f))
        return [pltpu.make_async_copy(src.at[e], dst.at[slot], sem.at[slot, j]) for j, (src, dst) in enumerate(pairs)]

    @pl.when(b == 0)
    def _():
        for cp in weight_copies(be_ref[0], slot_ref[0]):
            cp.start()

    @pl.when((b < nu_ref[0]) & ((b == 0) | (be_ref[b] != be_ref[jnp.maximum(b - 1, 0)])))
    def _():
        slot = slot_ref[b]
        for cp in weight_copies(be_ref[b], slot):
            cp.wait()

        @pl.when(nxt_ref[b] >= 0)
        def _():
            for cp in weight_copies(nxt_ref[b], 1 - slot):
                cp.start()
        wg_b[...] = wg_f[slot].astype(BF16)
        wu_b[...] = wu_f[slot].astype(BF16)
        wd_b[...] = wd_f[slot].astype(BF16)

    @pl.when(b < nu_ref[0])
    def _():
        xb = xs_ref[...]
        hg = _dg(xb, wg_b[...])
        hu = _dg(xb, wu_b[...])
        act = (hg * _sigmoid(hg) * hu).astype(BF16)
        yb_ref[...] = _dg(act, wd_b[...]).astype(BF16)

    @pl.when(pl.program_id(0) >= nu_ref[0])
    def _():
        yb_ref[...] = jnp.zeros_like(yb_ref)


def _experts(sched, n_used, xs, wg, wu, wdn, bm):
    p, d = xs.shape
    ff = wg.shape[2]
    n_blocks = p // bm
    return pl.pallas_call(
        _experts_kernel,
        grid_spec=pltpu.PrefetchScalarGridSpec(
            num_scalar_prefetch=4,
            grid=(n_blocks,),
            in_specs=[pl.BlockSpec((bm, d), lambda b, be, sl, nx, nu: (jnp.minimum(b, nu[0] - 1), 0)),
                      pl.BlockSpec(memory_space=pl.ANY), pl.BlockSpec(memory_space=pl.ANY),
                      pl.BlockSpec(memory_space=pl.ANY)],
            out_specs=pl.BlockSpec((bm, d), lambda b, *_: (b, 0)),
            scratch_shapes=[pltpu.VMEM((2, d, ff), F32), pltpu.VMEM((2, d, ff), F32), pltpu.VMEM((2, ff, d), F32),
                            pltpu.VMEM((d, ff), BF16), pltpu.VMEM((d, ff), BF16), pltpu.VMEM((ff, d), BF16),
                            pltpu.SemaphoreType.DMA((2, 3))],
        ),
        out_shape=jax.ShapeDtypeStruct((p, d), BF16),
        compiler_params=pltpu.CompilerParams(dimension_semantics=("arbitrary",), vmem_limit_bytes=VMEM_LIMIT),
        name="moe_experts",
    )(sched['expert'], sched['slot'], sched['next'], n_used, xs, wg, wu, wdn)


def _final_kernel(nch_ref, tot_ref, off_ref, loff_ref, h1_ref, eid_ref, gate_ref, loffr_ref, p_ref, yb_hbm,
                  nple_ref, wpg_ref, wpp_ref, nfin_ref, y_out, buf, sem, *, tm, lbuf, tile0, n_parts):
    i = pl.program_id(0)

    def fetch(tile):
        base = (tile0 + tile) * N_EXPERTS

        def per_expert(e):
            src0 = off_ref[base + e]
            dst0 = loff_ref[base + e]

            def piece(o, rows):
                src = yb_hbm.at[pl.ds(pl.multiple_of(src0 + o, RUN_ALIGN), rows)]
                dst = buf.at[tile % 2, pl.ds(pl.multiple_of(dst0 + o, RUN_ALIGN), rows)]
                pltpu.make_async_copy(src, dst, sem.at[tile % 2]).start()
            _pow2_pieces(nch_ref[base + e], tm, piece)
        _for_each_expert(per_expert)

    def wait_fetch(tile):
        def piece(o, rows):
            pltpu.make_async_copy(yb_hbm.at[pl.ds(0, rows)], buf.at[tile % 2, pl.ds(0, rows)], sem.at[tile % 2]).wait()
        _pow2_pieces(tot_ref[tile0 + tile], _pow2_floor(lbuf), piece)

    @pl.when(i == 0)
    def _():
        buf[...] = jnp.zeros_like(buf)
        fetch(i)

    @pl.when(i + 1 < pl.num_programs(0))
    def _():
        fetch(i + 1)

    eid = eid_ref[...]
    lane = lax.broadcasted_iota(I32, (tm, N_EXPERTS), 1)
    e12 = (jnp.where(lane == eid[:, 0:1], 1.0, 0.0) + jnp.where(lane == eid[:, 1:2], 1.0, 0.0)).astype(BF16)
    rows_per = tm // n_parts
    picks = []
    for part in range(n_parts):
        rs = pl.ds(part * rows_per, rows_per)
        lane_p = lax.broadcasted_iota(I32, (rows_per, N_EXPERTS), 1)
        eid_p = eid_ref[rs, :]
        e1 = jnp.where(lane_p == eid_p[:, 0:1], 1.0, 0.0)
        e2 = jnp.where(lane_p == eid_p[:, 1:2], 1.0, 0.0)
        before = (lax.broadcasted_iota(I32, (rows_per, tm), 1)
                  < lax.broadcasted_iota(I32, (rows_per, tm), 0) + part * rows_per)
        slot = _dg(jnp.where(before, 1.0, 0.0).astype(BF16), e12) + loffr_ref[0]
        l1 = jnp.sum(slot * e1, axis=1, keepdims=True).astype(I32)
        l2 = jnp.sum(slot * e2, axis=1, keepdims=True).astype(I32)
        cols = lax.broadcasted_iota(I32, (rows_per, lbuf), 1)
        gate = gate_ref[rs, :]
        picks.append(jnp.where(cols == l1, gate[:, 0:1], jnp.where(cols == l2, gate[:, 1:2], 0.0)).astype(BF16))

    wait_fetch(i)
    sorted_rows = buf[i % 2]

    def stream(part):
        rs = pl.ds(part * rows_per, rows_per)
        h2 = h1_ref[rs, :] + _dg(picks[part], sorted_rows)
        yield
        gate_in = _rms(h2, nple_ref[...]).astype(BF16)
        pg = _sigmoid(_dg(gate_in, wpg_ref[...]))
        yield
        h3 = h2 + pg * _dg(p_ref[rs, :].astype(BF16), wpp_ref[...])
        y_out[rs, :] = _rms(h3, nfin_ref[...])
        yield

    _interleave(stream, n_parts)


def _final(plan, h1, eid, gate, p2, yb, wd, tm, tile0):
    m, d = h1.shape
    pd = p2.shape[1]
    lbuf = 2 * tm + LOCAL_PAD
    weights = [wd[n] for n in ('norm_ple', 'w_ple_gate', 'w_ple_proj', 'norm_final')]
    imap = lambda f: (lambda i, *_: f(i))
    return pl.pallas_call(
        functools.partial(_final_kernel, tm=tm, lbuf=lbuf, tile0=tile0, n_parts=2),
        grid_spec=pltpu.PrefetchScalarGridSpec(
            num_scalar_prefetch=4,
            grid=(m // tm,),
            in_specs=[pl.BlockSpec((tm, d), imap(lambda i: (i, 0))),
                      pl.BlockSpec((tm, ROUTER_LANES), imap(lambda i: (i, 0))),
                      pl.BlockSpec((tm, ROUTER_LANES), imap(lambda i: (i, 0))),
                      pl.BlockSpec((1, 1, N_EXPERTS), imap(lambda i: (tile0 + i, 0, 0))),
                      pl.BlockSpec((tm, pd), imap(lambda i: (i, 0))),
                      pl.BlockSpec(memory_space=pl.ANY)] +
                     [pl.BlockSpec(w.shape, imap(lambda i, n=len(w.shape): (0,) * n)) for w in weights],
            out_specs=pl.BlockSpec((tm, d), imap(lambda i: (i, 0))),
            scratch_shapes=[pltpu.VMEM((2, lbuf, d), BF16), pltpu.SemaphoreType.DMA((2,))],
        ),
        out_shape=jax.ShapeDtypeStruct((m, d), F32),
        compiler_params=pltpu.CompilerParams(dimension_semantics=("arbitrary",), vmem_limit_bytes=VMEM_LIMIT),
        name="moe_final",
    )(plan['nch'], plan['tot'], plan['off'], plan['loff'], h1, eid, gate, plan['loff_row'], p2, yb, *weights)


def _route_plan(eids, tms, bm):
    experts = jnp.arange(N_EXPERTS, dtype=I32)
    counts = []
    for eid, tm in zip(eids, tms):
        onehot = (eid[:, :2, None] == experts).astype(I32)
        counts.append(onehot.reshape(-1, 2 * tm, N_EXPERTS).sum(axis=1))
    n = jnp.concatenate(counts)
    n_al = (n + RUN_ALIGN - 1) // RUN_ALIGN * RUN_ALIGN
    loff = jnp.cumsum(n_al, axis=1) - n_al
    used = n_al.sum(axis=0)
    region = (used + bm - 1) // bm * bm
    pend = jnp.cumsum(region)
    off = (pend - region)[None, :] + jnp.cumsum(n_al, axis=0) - n_al
    n_assign = sum(2 * e.shape[0] for e in eids)
    n_blocks = -(-(n_assign + (RUN_ALIGN - 1) * N_EXPERTS * n.shape[0] + N_EXPERTS * (bm - 1)) // bm)
    block_start = jnp.arange(n_blocks, dtype=I32) * bm
    block_expert = jnp.minimum(jnp.sum((pend[None, :] <= block_start[:, None]).astype(I32), axis=1), N_EXPERTS - 1)
    plan = dict(nch=(n_al // RUN_ALIGN).reshape(-1).astype(I32), off=off.reshape(-1).astype(I32),
                tot=(n_al.sum(axis=1) // RUN_ALIGN).astype(I32),
                loff=loff.reshape(-1).astype(I32), tail_n=((region - used) // RUN_ALIGN).astype(I32),
                tail_off=(pend - region + used).astype(I32), loff_col=loff.astype(F32)[:, :, None],
                loff_row=loff.astype(F32)[:, None, :])
    n_used = pend[-1] // bm
    prev = jnp.concatenate([jnp.full((1,), -1, I32), block_expert[:-1].astype(I32)])
    slot = (jnp.cumsum((block_expert != prev).astype(I32)) - 1) % 2
    after = pend[block_expert] // bm
    nxt = jnp.where(after < n_used, block_expert[jnp.minimum(after, n_blocks - 1)], -1)
    sched = dict(expert=block_expert.astype(I32), slot=slot.astype(I32), next=nxt.astype(I32))
    return plan, sched, n_used.astype(I32).reshape(1), n_blocks


def _layer_front(x, shift0, wkv0, conv0, wd, tm_in, tm_tok, wkv_tile, wkv_sub, wkv_chunk, chained):
    bsz, t, d = x.shape
    r, lw, k2, v, ah, bh, g, bonus, yb, shift_new, conv_new = _mix_in(x, shift0, conv0, wd, tm_in, 4 if tm_in >= 512 else 1)
    y, s_new = _wkv(r, lw, k2, v, ah, bh, jnp.swapaxes(wkv0, -1, -2), wkv_tile, wkv_sub, wkv_chunk, chained)
    flat = lambda z: z.reshape(bsz * t, z.shape[-1])
    h1, xn2, eid, gate = _mix_out(flat(x), flat(y), flat(bonus), flat(g), flat(yb), wd, tm_tok, 2)
    return h1, xn2, eid, gate, shift_new.reshape(bsz, d), jnp.swapaxes(s_new, -1, -2), conv_new


def kernel(x_prompt, x_sample, state_shift, state_wkv, cache_conv, p_prompt, p_sample, norm_mix, w_in, mu_rkv, mu_w, mu_a, mu_g, w0, w1, w2, a0, a1, a2, g1, g2, k_k, k_a, r_k, ln_x_w, ln_x_b, dw_w, dw_b, cln_w, cln_b, w_out, norm_ffn, w_router_group, w_router_expert, w_exp_gate, w_exp_up, w_exp_down, norm_ple, w_ple_gate, w_ple_proj, norm_final):
    depth = norm_mix.shape[0]
    assert depth == 1
    d = x_prompt.shape[-1]
    c = w0.shape[-1]
    row = lambda z: z[0].reshape(1, -1).astype(F32)
    lane = jnp.arange(LANES, dtype=I32) // HEAD_SIZE
    w_router = jnp.concatenate([w_router_expert[0], w_router_group[0],
                                jnp.zeros((d, ROUTER_LANES - N_EXPERTS - N_EXPERT_GROUPS), F32)], axis=1)
    first = jnp.concatenate([w1[0], a1[0], g1[0]], axis=1)
    mixed = jnp.concatenate([mu_w[0][:, None] * w1[0], mu_a[0][:, None] * a1[0], mu_g[0][:, None] * g1[0]], axis=1)
    d_w, d_a = w1.shape[2], a1.shape[2]
    second = jnp.concatenate([jnp.concatenate([w2[0], jnp.zeros((d_w, c), F32)], axis=1),
                              jnp.concatenate([jnp.zeros((d_a, c), F32), a2[0]], axis=1)], axis=0)
    wd = dict(
        norm_mix=row(norm_mix), w_in=w_in[0].astype(BF16), mu_rkv=row(mu_rkv), w0=row(w0), a0=row(a0),
        w_lora1=jnp.concatenate([first, mixed], axis=0).astype(BF16), w_lora2=second.astype(BF16),
        g2=g2[0].astype(BF16), lora_dims=(d_w, d_a),
        k_k=row(k_k), k_a=row(k_a), r_k=row(r_k), ln_x_w=row(ln_x_w), ln_x_b=row(ln_x_b),
        seg=(lane[:, None] == lane[None, :]).astype(BF16),
        dw_w=dw_w[0].astype(F32), dw_b=row(dw_b), cln_w=row(cln_w), cln_b=row(cln_b),
        w_out=w_out[0].astype(BF16), norm_ffn=row(norm_ffn),
        w_router=jnp.concatenate(_split2(w_router), axis=1),
        norm_ple=row(norm_ple), w_ple_gate=w_ple_gate[0].astype(BF16), w_ple_proj=w_ple_proj[0].astype(BF16),
        norm_final=norm_final.reshape(1, -1).astype(F32),
    )
    bp, tp, _ = x_prompt.shape
    bs, ts, _ = x_sample.shape
    mp, ms = bp * tp, bs * ts
    tm_p = min(512, tp)
    tm_s = min(256, ms)

    zeros = lambda *s: jnp.zeros(s, F32)
    h1_p, xn_p, eid_p, gate_p, shift_p, wkv_p, conv_p = _layer_front(
        x_prompt, zeros(bp, 1, d), zeros(bp, c // HEAD_SIZE, HEAD_SIZE, HEAD_SIZE), zeros(bp, CONV_CARRY, c),
        wd, tm_p, min(512, mp), min(128, tp), 2 if tp >= 256 else 1, min(64, tp), True)
    h1_s, xn_s, eid_s, gate_s, shift_s, wkv_s, conv_s = _layer_front(
        x_sample, state_shift[0][:, None, :], state_wkv[0], cache_conv[0],
        wd, ts, tm_s, min(128, ms), 2 if ms >= 256 else 1, ts, False)

    bm = 512
    tr_p, tr_s = min(512, mp), min(512, ms)
    plan, sched, n_used, n_blocks = _route_plan([eid_p, eid_s], [tr_p, tr_s], bm)
    tiles_p = mp // tr_p
    xs = _dispatch(plan, n_used, [eid_p[:, :2].T, eid_s[:, :2].T], [xn_p, xn_s], [tr_p, tr_s], n_blocks, bm)
    yb = _experts(sched, n_used, xs, w_exp_gate[0], w_exp_up[0], w_exp_down[0], bm)
    y_p = _final(plan, h1_p, eid_p, gate_p, p_prompt[0].reshape(mp, -1), yb, wd, tr_p, 0)
    y_s = _final(plan, h1_s, eid_s, gate_s, p_sample[0].reshape(ms, -1), yb, wd, tr_s, tiles_p)
    return (y_p.reshape(x_prompt.shape), y_s.reshape(x_sample.shape), shift_p[None], wkv_p[None], conv_p[None],
            shift_s[None], wkv_s[None], conv_s[None])
```

```python
import functools

import jax
import jax.numpy as jnp
from jax import lax
from jax.experimental import pallas as pl
from jax.experimental.pallas import tpu as pltpu

F32 = jnp.float32
BF16 = jnp.bfloat16
I32 = jnp.int32

HEAD_SIZE = 64
CONV_WIDTH = 31
CONV_CARRY = CONV_WIDTH - 1
SUBLANES = 8
LANES = 128
CARRY_PAD = 32
N_EXPERT_GROUPS = 4
EXPERTS_PER_GROUP = 8
N_EXPERTS = N_EXPERT_GROUPS * EXPERTS_PER_GROUP
ROUTER_LANES = 128
RMS_EPS = 1e-6
LN_EPS = 1e-5
GN_EPS = 64e-5
DECAY_SCALE = 0.6065306597126334
INV_BASE = 16
RUN_ALIGN = 16
LOCAL_PAD = N_EXPERTS * RUN_ALIGN
VMEM_LIMIT = 56 * 1024 * 1024

NN = ((1,), (0,))
NT = ((1,), (1,))
TN = ((0,), (0,))


def _dg(a, b, dims=NN):
    return lax.dot_general(a, b, (dims, ((), ())), preferred_element_type=F32)


def _split2(x):
    hi = x.astype(BF16)
    lo = (x - hi.astype(F32)).astype(BF16)
    return hi, lo


def _split3(x):
    hi = x.astype(BF16)
    r1 = x - hi.astype(F32)
    mid = r1.astype(BF16)
    lo = (r1 - mid.astype(F32)).astype(BF16)
    return hi, mid, lo


def _bdot(a, b, dims=NN):
    return _dg(a.astype(BF16), b.astype(BF16), dims)


def _mask_dot(mask_bf16, x):
    h, m, l = _split3(x)
    return _dg(mask_bf16, h) + (_dg(mask_bf16, m) + _dg(mask_bf16, l))


def _seg_sum(x, seg_bf16):
    h, l = _split2(x)
    w = seg_bf16.shape[0]
    return jnp.concatenate([_dg(h[:, j:j + w], seg_bf16) + _dg(l[:, j:j + w], seg_bf16)
                            for j in range(0, x.shape[1], w)], axis=1)


def _rms(x, g):
    return x * lax.rsqrt(jnp.mean(x * x, axis=-1, keepdims=True) + RMS_EPS) * g


def _sigmoid(x):
    return 0.5 * jnp.tanh(0.5 * x) + 0.5


def _full(shape):
    n = len(shape)
    return pl.BlockSpec(shape, lambda *_: (0,) * n, pipeline_mode=pl.Buffered(1))


def _mix_in_kernel(x_ref, shift_ref, conv_ref, nm_ref, win_ref, murkv_ref, w0_ref, a0_ref, wl1_ref, wl2_ref, g2_ref,
                   kk_ref, ka_ref, rk_ref, seg_ref, dww_ref, dwb_ref, clnw_ref, clnb_ref,
                   r_out, lw_out, k_out, v_out, a_out, b_out, g_out, bonus_out, yb_out, shift_out, conv_out,
                   xn_last, h_last, up_ext, shifted, *, tm, c, n_parts, d_w, d_a):
    i = pl.program_id(1)

    @pl.when(i == 0)
    def _():
        sp = shift_ref[0]
        xn_last[...] = sp
        sp8 = jnp.broadcast_to(sp, (8, sp.shape[1])).astype(BF16)
        h_last[...] = _dg(sp8, win_ref[:, :3 * c])[0:1]
        up_ext[CARRY_PAD - CONV_CARRY:CARRY_PAD, :] = conv_ref[0]

    rp = tm // n_parts
    last_rows = {}

    def stream(part):
        rs = pl.ds(part * rp, rp)
        xn = _rms(x_ref[0, rs, :], nm_ref[...])
        hin = _dg(xn.astype(BF16), win_ref[...])
        h_rkv = hin[:, :3 * c]
        last_rows[part] = (xn[rp - 1:rp], h_rkv[rp - 1:rp])
        yield
        xn_prev, h_prev = (xn_last[...], h_last[...]) if part == 0 else last_rows[part - 1]
        first = lax.broadcasted_iota(I32, (rp, 1), 0) == 0
        dx = jnp.where(first, xn_prev, pltpu.roll(xn, 1, 0)) - xn
        hprev = jnp.where(first, h_prev, pltpu.roll(h_rkv, 1, 0))
        rkv = h_rkv + (hprev - h_rkv) * murkv_ref[...]
        r = rkv[:, :c]
        k = rkv[:, c:2 * c]
        v = rkv[:, 2 * c:]
        l1 = _dg(jnp.concatenate([xn.astype(BF16), dx.astype(BF16)], axis=1), wl1_ref[...])
        lane = lax.broadcasted_iota(I32, l1.shape, 1)
        act = jnp.where(lane < d_w, jnp.tanh(l1), jnp.where(lane < d_w + d_a, l1, _sigmoid(l1))).astype(BF16)
        yield
        za = _dg(act[:, :d_w + d_a], wl2_ref[...])
        a = 0.5 * jnp.tanh(a0_ref[...] + za[:, c:]) + 0.5
        g_out[0, rs, :] = _dg(act[:, d_w + d_a:], g2_ref[...])
        seg = seg_ref[...]
        kk = k * kk_ref[...]
        kk = kk * jnp.minimum(lax.rsqrt(_seg_sum(kk * kk, seg)), 1e12)
        k2 = k * (1.0 + (a - 1.0) * ka_ref[...])
        r_out[0, rs, :] = r
        lw_out[0, rs, :] = (-0.5 * DECAY_SCALE) * jnp.tanh(w0_ref[...] + za[:, :c]) - 0.5 * DECAY_SCALE
        k_out[0, rs, :] = k2
        v_out[0, rs, :] = v
        a_out[0, rs, :] = -kk
        b_out[0, rs, :] = kk * a
        yield
        bonus_out[0, rs, :] = _seg_sum(r * k2 * rk_ref[...], seg) * v
        u = hin[:, 3 * c:4 * c] * (jnp.tanh(hin[:, 4 * c:]) + 1.0)
        up_ext[pl.ds(CARRY_PAD + part * rp, rp), :] = u
        yield
        first_row = CARRY_PAD - CONV_CARRY + part * rp
        for s in range(SUBLANES):
            span = rp + (CONV_WIDTH - 1 - s) // SUBLANES * SUBLANES
            shifted[part, s, 0:span, :] = up_ext[pl.ds(first_row + s, span), :]
        z = jnp.zeros_like(u) + dwb_ref[...]
        for j in range(CONV_WIDTH):
            s, m = j % SUBLANES, j // SUBLANES
            z = z + dww_ref[j:j + 1, :] * shifted[part, s, m * SUBLANES:m * SUBLANES + rp, :]
        mu = jnp.mean(z, axis=-1, keepdims=True)
        zc = z - mu
        var = jnp.mean(zc * zc, axis=-1, keepdims=True)
        zh = zc * lax.rsqrt(var + LN_EPS) * clnw_ref[...] + clnb_ref[...]
        yb_out[0, rs, :] = zh * (jnp.tanh(zh) + 1.0)
        yield

    _interleave(stream, n_parts)

    tail = up_ext[pl.ds(tm + CARRY_PAD - CONV_CARRY, CONV_CARRY), :]
    up_ext[CARRY_PAD - CONV_CARRY:CARRY_PAD, :] = tail
    conv_out[0] = tail
    xn_end, h_end = last_rows[n_parts - 1]
    xn_last[...] = xn_end
    h_last[...] = h_end
    shift_out[0] = xn_end


def _mix_in(x, shift0, conv0, wd, tm, n_parts):
    bsz, t, d = x.shape
    c = wd['w0'].shape[1]
    grid = (bsz, t // tm)
    tok = lambda w: pl.BlockSpec((1, tm, w), lambda b, i: (b, i, 0))
    per_seq = lambda rows, w: pl.BlockSpec((1, rows, w), lambda b, i: (b, 0, 0))
    weights = [wd[n] for n in ('norm_mix', 'w_in', 'mu_rkv', 'w0', 'a0', 'w_lora1', 'w_lora2', 'g2',
                               'k_k', 'k_a', 'r_k', 'seg', 'dw_w', 'dw_b', 'cln_w', 'cln_b')]
    d_w, d_a = wd['lora_dims']
    out_tok = jax.ShapeDtypeStruct((bsz, t, c), F32)
    outs = pl.pallas_call(
        functools.partial(_mix_in_kernel, tm=tm, c=c, n_parts=n_parts, d_w=d_w, d_a=d_a),
        grid=grid,
        in_specs=[tok(d), per_seq(1, d), per_seq(CONV_CARRY, c)] + [_full(w.shape) for w in weights],
        out_specs=[tok(c)] * 9 + [per_seq(1, d), per_seq(CONV_CARRY, c)],
        out_shape=[out_tok] * 9 + [jax.ShapeDtypeStruct((bsz, 1, d), F32),
                                   jax.ShapeDtypeStruct((bsz, CONV_CARRY, c), F32)],
        scratch_shapes=[pltpu.VMEM((1, d), F32), pltpu.VMEM((1, 3 * c), F32),
                        pltpu.VMEM((tm + CARRY_PAD, c), F32),
                        pltpu.VMEM((n_parts, SUBLANES, tm // n_parts + CARRY_PAD - SUBLANES, c), F32)],
        compiler_params=pltpu.CompilerParams(dimension_semantics=("arbitrary", "arbitrary"),
                                             vmem_limit_bytes=VMEM_LIMIT),
        name="mix_in",
    )(x, shift0, conv0, *weights)
    return outs


def _tri_inverse(n_strict, row, col, lg_chunk):
    lg_base = INV_BASE.bit_length() - 1
    same = lambda sh: (row >> sh) == (col >> sh)
    lg0 = min(lg_base, lg_chunk)
    n = row.shape[0]

    def expand(c, lg):
        return jnp.where(same(lg), jnp.concatenate([c] * (n >> lg), axis=0), 0.0).astype(BF16)

    def fold(x, lg):
        b = 1 << lg
        return functools.reduce(lambda u, v: u + v, [x[i:i + b] for i in range(0, n, b)])

    b0 = 1 << lg0
    row_c = lax.broadcasted_iota(I32, (b0, n), 0)
    col_c = lax.broadcasted_iota(I32, (b0, n), 1)
    p_full = [jnp.where(same(lg0), x, 0.0) for x in n_strict]
    p = [fold(x, lg0) for x in p_full]
    t = [jnp.where((col_c & (b0 - 1)) == row_c, 1.0, 0.0) + x for x in p]
    p_full = [x.astype(BF16) for x in p_full]
    for _ in range(lg0 - 1):
        p = [_dg(x.astype(BF16), y) for x, y in zip(p, p_full)]
        p_full = [expand(x, lg0) for x in p]
        t = [x + _dg(x.astype(BF16), y) for x, y in zip(t, p_full)]
    for lg in range(lg0, lg_chunk):
        off_mask = same(lg + 1) & jnp.logical_not(same(lg))
        t_full = [expand(x, lg) for x in t]
        u = [_dg(x.astype(BF16), jnp.where(off_mask, m, 0.0).astype(BF16)) for x, m in zip(t, n_strict)]
        add = [_dg(x.astype(BF16), y) for x, y in zip(u, t_full)]
        even = ((lax.broadcasted_iota(I32, (1 << lg, n), 1) >> lg) & 1) == 0
        t = [jnp.concatenate([jnp.where(even, x, 0.0), jnp.where(even, a, x)], axis=0) for x, a in zip(t, add)]
    return [expand(x, lg_chunk) for x in t]


def _wkv_kernel(r_ref, lw_ref, k_ref, v_ref, a_ref, b_ref, s0_ref, y_ref, s_out, s_scr,
                *, tt, n_sub, chunk, chained):
    ti = pl.program_id(1)
    n_heads = r_ref.shape[2] // HEAD_SIZE
    n_chunks = tt // chunk
    lg_chunk = chunk.bit_length() - 1
    row = lax.broadcasted_iota(I32, (tt, tt), 0)
    col = lax.broadcasted_iota(I32, (tt, tt), 1)
    in_chunk = (row >> lg_chunk) == (col >> lg_chunk)
    tri_incl = in_chunk & (col <= row)
    tri_strict = in_chunk & (col < row)
    m_cum = jnp.where(tri_incl, 1.0, 0.0).astype(BF16)
    m_tot = jnp.where(in_chunk, 1.0, 0.0).astype(BF16)
    row_h = lax.broadcasted_iota(I32, (HEAD_SIZE, HEAD_SIZE), 0)
    col_h = lax.broadcasted_iota(I32, (HEAD_SIZE, HEAD_SIZE), 1)
    eye_h = row_h == col_h

    if chained:
        @pl.when(ti == 0)
        def _():
            s_scr[...] = s0_ref[0]

    rt_all, g_end_all, cols = [], [], {name: [] for name in ('v', 'rt', 'at', 'kt', 'bt', 'bd', 'kd')}
    hsl = [slice(HEAD_SIZE * h, HEAD_SIZE * (h + 1)) for h in range(n_heads)]
    for sub in range(n_sub):
        rows = pl.ds(sub * tt, tt)
        lw_all = lw_ref[0, rows, :]
        k_all = k_ref[0, rows, :]
        b_all = b_ref[0, rows, :]
        cum = _mask_dot(m_cum, lw_all)
        tot = _mask_dot(m_tot, lw_all)
        e_neg = jnp.exp(-cum)
        e_end = jnp.exp(tot - cum)
        full = dict(v=v_ref[0, rows, :], rt=r_ref[0, rows, :] * jnp.exp(cum),
                    at=a_ref[0, rows, :] * jnp.exp(cum - lw_all), kt=k_all * e_neg, bt=b_all * e_neg,
                    bd=b_all * e_end, kd=k_all * e_end)
        rt_all.append(full['rt'])
        g_end_all.append(jnp.exp(tot))
        for name, z in full.items():
            cols[name] += [z[:, s_].astype(BF16) for s_ in hsl]
    v, rt, at, kt, bt, bd, kd = (cols[name] for name in ('v', 'rt', 'at', 'kt', 'bt', 'bd', 'kd'))

    units = range(n_sub * n_heads)
    mm = [_dg(jnp.concatenate([at[u], rt[u]], axis=0), jnp.concatenate([bt[u], kt[u]], axis=0), NT) for u in units]
    m_ab = [jnp.where(tri_strict, mm[u][:tt, :tt], 0.0) for u in units]
    m_ak = [jnp.where(tri_strict, mm[u][:tt, tt:], 0.0).astype(BF16) for u in units]
    m_rb = [jnp.where(tri_incl, mm[u][tt:, :tt], 0.0).astype(BF16) for u in units]
    m_rk = [jnp.where(tri_incl, mm[u][tt:, tt:], 0.0).astype(BF16) for u in units]
    tinv = _tri_inverse(m_ab, row, col, lg_chunk)
    akv = [_dg(m_ak[u], v[u]).astype(BF16) for u in units]
    w1 = [_dg(tinv[u], at[u]).astype(BF16) for u in units]
    w2 = [_dg(tinv[u], akv[u]).astype(BF16) for u in units]
    q = [(rt_all[u // n_heads][:, hsl[u % n_heads]] + _dg(m_rb[u], w1[u])).astype(BF16) for u in units]
    y0 = [_dg(m_rb[u], w2[u]) + _dg(m_rk[u], v[u]) for u in units]

    heads = range(n_heads)
    if chained:
        s = [s_scr[h] for h in heads]
    for sub in range(n_sub):
        for ci in range(n_chunks):
            cs = slice(ci * chunk, (ci + 1) * chunk)
            seq = sub * n_chunks + ci
            if not chained:
                s = [s0_ref[seq, h] for h in heads]
            g_row = g_end_all[sub][ci * chunk:ci * chunk + 1]
            un = [sub * n_heads + h for h in heads]
            gm = [jnp.where(eye_h, jnp.broadcast_to(g_row[:, hsl[h]], (HEAD_SIZE, HEAD_SIZE)), 0.0)
                  + _dg(bd[un[h]][cs], w1[un[h]][cs], TN) for h in heads]
            hm = [_dg(bd[un[h]][cs], w2[un[h]][cs], TN) + _dg(kd[un[h]][cs], v[un[h]][cs], TN) for h in heads]
            for h in heads:
                y_ref[0, pl.ds(sub * tt + ci * chunk, chunk), hsl[h]] = _bdot(q[un[h]][cs], s[h]) + y0[un[h]][cs]
            s = [_bdot(gm[h], s[h]) + hm[h] for h in heads]
            if not chained:
                for h in heads:
                    s_out[seq, h] = s[h]
    if chained:
        for h in heads:
            s_scr[h] = s[h]
            s_out[0, h] = s[h]


def _wkv(r, lw, k, v, a, b, s0t, tt, n_sub, chunk, chained):
    bsz, t, c = r.shape
    n_heads = c // HEAD_SIZE
    hs = HEAD_SIZE
    step = tt * n_sub
    if chained:
        grid = (bsz, t // step)
        tok = pl.BlockSpec((1, step, c), lambda bi, ti: (bi, ti, 0))
        st = pl.BlockSpec((1, n_heads, hs, hs), lambda bi, ti: (bi, 0, 0, 0))
        args = (r, lw, k, v, a, b)
        y_shape = (bsz, t, c)
    else:
        assert t == chunk and (bsz * t) % step == 0
        n_seq = step // chunk
        grid = (1, bsz * t // step)
        tok = pl.BlockSpec((1, step, c), lambda bi, ti: (0, ti, 0))
        st = pl.BlockSpec((n_seq, n_heads, hs, hs), lambda bi, ti: (ti, 0, 0, 0))
        args = tuple(z.reshape(1, bsz * t, c) for z in (r, lw, k, v, a, b))
        y_shape = (1, bsz * t, c)
    y, s_new = pl.pallas_call(
        functools.partial(_wkv_kernel, tt=tt, n_sub=n_sub, chunk=chunk, chained=chained),
        grid=grid,
        in_specs=[tok] * 6 + [st],
        out_specs=[tok, st],
        out_shape=[jax.ShapeDtypeStruct(y_shape, F32), jax.ShapeDtypeStruct(s0t.shape, F32)],
        scratch_shapes=[pltpu.VMEM((n_heads, hs, hs), F32)],
        compiler_params=pltpu.CompilerParams(dimension_semantics=("arbitrary",) * 2,
                                             vmem_limit_bytes=VMEM_LIMIT),
        name="wkv",
    )(*args, s0t)
    return y.reshape(bsz, t, c), s_new


def _interleave(make_stream, n_parts):
    for _ in zip(*[make_stream(part) for part in range(n_parts)]):
        pass


def _mix_out_kernel(x_ref, y_ref, bonus_ref, g_ref, yb_ref, lnw_ref, lnb_ref, seg_ref, wout_ref, nffn_ref,
                    wr_ref, h1_out, xn_out, eid_out, eidt_out, gate_out, *, c, n_parts):
    rows_per = x_ref.shape[0] // n_parts

    def stream(part):
        rs = pl.ds(part * rows_per, rows_per)
        seg = seg_ref[...]
        y = y_ref[rs, :]
        inv_n = 1.0 / HEAD_SIZE
        mu = _seg_sum(y, seg) * inv_n
        yield
        yc = y - mu
        var = _seg_sum(yc * yc, seg) * inv_n
        yield
        yn = yc * lax.rsqrt(var + GN_EPS) * lnw_ref[...] + lnb_ref[...]
        ya = (yn + bonus_ref[rs, :]) * g_ref[rs, :]
        mix = _dg(ya.astype(BF16), wout_ref[:c, :]) + _dg(yb_ref[rs, :].astype(BF16), wout_ref[c:, :])
        yield
        h1 = x_ref[rs, :] + mix
        h1_out[rs, :] = h1
        xn = _rms(h1, nffn_ref[...])
        xn_out[rs, :] = xn.astype(BF16)
        xh, xl = _split2(xn)
        hi_lo = _dg(xh, wr_ref[...])
        logits = hi_lo[:, :ROUTER_LANES] + (hi_lo[:, ROUTER_LANES:] + _dg(xl, wr_ref[:, :ROUTER_LANES]))
        yield
        lane = lax.broadcasted_iota(I32, logits.shape, 1)
        neg = jnp.float32(-jnp.inf)
        is_g = (lane >= N_EXPERTS) & (lane < N_EXPERTS + N_EXPERT_GROUPS)
        glog = jnp.where(is_g, logits, neg)
        gmax = jnp.max(glog, axis=-1, keepdims=True)
        gsel = jnp.min(jnp.where(glog == gmax, lane, 4 * ROUTER_LANES), axis=-1, keepdims=True) - N_EXPERTS
        gp = 1.0 / jnp.sum(jnp.where(is_g, jnp.exp(glog - gmax), 0.0), axis=-1, keepdims=True)
        in_grp = (lane >= gsel * EXPERTS_PER_GROUP) & (lane < (gsel + 1) * EXPERTS_PER_GROUP)
        elog = jnp.where(in_grp, logits, neg)
        emax = jnp.max(elog, axis=-1, keepdims=True)
        ex = jnp.where(in_grp, jnp.exp(elog - emax), 0.0)
        eprob = ex / jnp.sum(ex, axis=-1, keepdims=True)
        eprob = jnp.where(in_grp, eprob, -1.0)
        yield
        v1 = jnp.max(eprob, axis=-1, keepdims=True)
        i1 = jnp.min(jnp.where(eprob == v1, lane, 4 * ROUTER_LANES), axis=-1, keepdims=True)
        rest = jnp.where(lane == i1, -1.0, eprob)
        v2 = jnp.max(rest, axis=-1, keepdims=True)
        i2 = jnp.min(jnp.where(rest == v2, lane, 4 * ROUTER_LANES), axis=-1, keepdims=True)
        denom = v1 + v2
        eid = jnp.where(lane == 0, i1, jnp.where(lane == 1, i2, 0))
        eid_out[rs, :] = eid
        eidt_out[:, rs] = jnp.transpose(eid)[:SUBLANES]
        gate_out[rs, :] =jnp.where(lane == 0, gp * v1 / denom, jnp.where(lane == 1, gp * v2 / denom, 0.0))
        yield

    _interleave(stream, n_parts)


def _mix_out(x2, y2, bonus2, g2, yb2, wd, tm, n_parts):
    m, d = x2.shape
    c = y2.shape[1]
    tokd = pl.BlockSpec((tm, d), lambda i: (i, 0))
    tokc = pl.BlockSpec((tm, c), lambda i: (i, 0))
    tokr = pl.BlockSpec((tm, ROUTER_LANES), lambda i: (i, 0))
    weights = [wd[n] for n in ('ln_x_w', 'ln_x_b', 'seg', 'w_out', 'norm_ffn', 'w_router')]
    return pl.pallas_call(
        functools.partial(_mix_out_kernel, c=c, n_parts=n_parts),
        grid=(m // tm,),
        in_specs=[tokd, tokc, tokc, tokc, tokc] + [_full(w.shape) for w in weights],
        out_specs=[tokd, tokd, tokr, pl.BlockSpec((SUBLANES, tm), lambda i: (0, i)), tokr],
        out_shape=[jax.ShapeDtypeStruct((m, d), F32), jax.ShapeDtypeStruct((m, d), BF16),
                   jax.ShapeDtypeStruct((m, ROUTER_LANES), I32), jax.ShapeDtypeStruct((SUBLANES, m), I32),
                   jax.ShapeDtypeStruct((m, ROUTER_LANES), F32)],
        compiler_params=pltpu.CompilerParams(dimension_semantics=("arbitrary",), vmem_limit_bytes=VMEM_LIMIT),
        name="mix_out",
    )(x2, y2, bonus2, g2, yb2, *weights)


def _pow2_pieces(count, max_rows, fn):
    off = 0
    rows = max_rows
    while rows >= RUN_ALIGN:
        has = (count & (rows // RUN_ALIGN)) != 0

        @pl.when(has)
        def _(off=off, rows=rows):
            fn(off, rows)
        off = off + jnp.where(has, rows, 0)
        rows //= 2


def _for_each_expert(fn):
    def body(e, carry):
        fn(e)
        return carry
    lax.fori_loop(0, N_EXPERTS, body, 0)


def _pow2_floor(n):
    return 1 << (n.bit_length() - 1)


def _dispatch_kernel(nch_ref, tot_ref, off_ref, loff_ref, tn_ref, toff_ref, nu_ref, *refs, groups, bm, n_blocks):
    n_g = len(groups)
    eid_refs, xn_refs = refs[:n_g], refs[n_g + 1:2 * n_g + 1]
    loffc_ref = refs[n_g]
    xs_out, buf, sem, zbuf, zsem = refs[2 * n_g + 1:]
    i = pl.program_id(0)
    last = pl.num_programs(0) - 1
    max_run = max(tm for tm, _ in groups)

    def start_tile(tile):
        def per_expert(e):
            src0 = loff_ref[tile * N_EXPERTS + e]
            dst0 = off_ref[tile * N_EXPERTS + e]

            def piece(o, rows):
                src = buf.at[tile % 2, pl.ds(pl.multiple_of(src0 + o, RUN_ALIGN), rows)]
                dst = xs_out.at[pl.ds(pl.multiple_of(dst0 + o, RUN_ALIGN), rows)]
                pltpu.make_async_copy(src, dst, sem.at[tile % 2]).start()
            _pow2_pieces(nch_ref[tile * N_EXPERTS + e], max_run, piece)
        _for_each_expert(per_expert)

    def wait_tile(tile):
        def piece(o, rows):
            pltpu.make_async_copy(buf.at[tile % 2, pl.ds(0, rows)], xs_out.at[pl.ds(0, rows)], sem.at[tile % 2]).wait()
        _pow2_pieces(tot_ref[tile], _pow2_floor(buf.shape[1]), piece)

    @pl.when(i >= 2)
    def _():
        wait_tile(i - 2)

    def sort_tile(eid_ref, xn_ref, tm):
        e_rows = eid_ref[...]
        sub = lax.broadcasted_iota(I32, (N_EXPERTS, tm), 0)
        e1 = jnp.where(sub == e_rows[0:1], 1.0, 0.0)
        e2 = jnp.where(sub == e_rows[1:2], 1.0, 0.0)
        before = lax.broadcasted_iota(I32, (tm, tm), 0) < lax.broadcasted_iota(I32, (tm, tm), 1)
        slot = _dg((e1 + e2).astype(BF16), jnp.where(before, 1.0, 0.0).astype(BF16)) + loffc_ref[0]
        l1 = jnp.sum(slot * e1, axis=0, keepdims=True).astype(I32)
        l2 = jnp.sum(slot * e2, axis=0, keepdims=True).astype(I32)
        n_rows = 2 * tm + LOCAL_PAD
        rows = lax.broadcasted_iota(I32, (n_rows, tm), 0)
        perm = jnp.where((rows == l1) | (rows == l2), 1.0, 0.0).astype(BF16)
        buf[i % 2, 0:n_rows, :] = _dg(perm, xn_ref[...]).astype(BF16)

    first = 0
    for g, (tm, n_tiles) in enumerate(groups):
        pl.when((i >= first) & (i < first + n_tiles))(functools.partial(sort_tile, eid_refs[g], xn_refs[g], tm))
        first += n_tiles
    start_tile(i)

    @pl.when(i == 0)
    def _():
        zbuf[...] = jnp.zeros_like(zbuf)
        half = zbuf.shape[0]

        def zero_copy(off, rows):
            return pltpu.make_async_copy(zbuf.at[pl.ds(0, rows)], xs_out.at[pl.ds(off, rows)], zsem)

        def tails(fn):
            _for_each_expert(lambda e: _pow2_pieces(
                tn_ref[e], half, lambda o, rows: fn(pl.multiple_of(toff_ref[e] + o, RUN_ALIGN), rows)))

        def unused_blocks(fn):
            def body(b, carry):
                fn(pl.multiple_of(b * bm, bm), half)
                fn(pl.multiple_of(b * bm + half, half), half)
                return carry
            lax.fori_loop(nu_ref[0], n_blocks, body, 0)

        tails(lambda off, rows: zero_copy(off, rows).start())
        unused_blocks(lambda off, rows: zero_copy(off, rows).start())
        tails(lambda off, rows: zero_copy(off, rows).wait())
        unused_blocks(lambda off, rows: zero_copy(off, rows).wait())

    @pl.when(i == last)
    def _():
        @pl.when(i >= 1)
        def _():
            wait_tile(i - 1)
        wait_tile(i)


def _dispatch(plan, n_used, eids_t, xns, tms, n_blocks, bm):
    d = xns[0].shape[1]
    groups = tuple((tm, xn.shape[0] // tm) for xn, tm in zip(xns, tms))
    firsts = [sum(n for _, n in groups[:g]) for g in range(len(groups))]
    lbuf = 2 * max(tms) + LOCAL_PAD

    def tile_of(g):
        return lambda i: jnp.clip(i - firsts[g], 0, groups[g][1] - 1)

    imap = lambda f: (lambda i, *_: f(i))
    in_specs = ([pl.BlockSpec((SUBLANES, tm), imap(lambda i, g=g: (0, tile_of(g)(i)))) for g, tm in enumerate(tms)] +
                [pl.BlockSpec((1, N_EXPERTS, 1), imap(lambda i: (i, 0, 0)))] +
                [pl.BlockSpec((tm, d), imap(lambda i, g=g: (tile_of(g)(i), 0))) for g, tm in enumerate(tms)])
    return pl.pallas_call(
        functools.partial(_dispatch_kernel, groups=groups, bm=bm, n_blocks=n_blocks),
        grid_spec=pltpu.PrefetchScalarGridSpec(
            num_scalar_prefetch=7,
            grid=(sum(n for _, n in groups),),
            in_specs=in_specs,
            out_specs=pl.BlockSpec(memory_space=pl.ANY),
            scratch_shapes=[pltpu.VMEM((2, lbuf, d), BF16), pltpu.SemaphoreType.DMA((2,)),
                            pltpu.VMEM((bm // 2, d), BF16), pltpu.SemaphoreType.DMA],
        ),
        out_shape=jax.ShapeDtypeStruct((n_blocks * bm, d), BF16),
        compiler_params=pltpu.CompilerParams(dimension_semantics=("arbitrary",), vmem_limit_bytes=VMEM_LIMIT),
        name="moe_dispatch",
    )(plan['nch'], plan['tot'], plan['off'], plan['loff'], plan['tail_n'], plan['tail_off'], n_used, *eids_t,
      plan['loff_col'], *xns)


def _experts_kernel(be_ref, slot_ref, nxt_ref, nu_ref, xs_ref, wg_hbm, wu_hbm, wd_hbm, yb_ref,
                    wg_f, wu_f, wd_f, wg_b, wu_b, wd_b, sem):
    b = pl.program_id(0)

    def weight_copies(e, slot):
        pairs = ((wg_hbm, wg_f), (wu_hbm, wu_f), (wd_hbm, wd_f))
        return [pltpu.make_async_copy(src.at[e], dst.at[slot], sem.at[slot, j]) for j, (src, dst) in enumerate(pairs)]

    @pl.when(b == 0)
    def _():
        for cp in weight_copies(be_ref[0], slot_ref[0]):
            cp.start()

    @pl.when((b < nu_ref[0]) & ((b == 0) | (be_ref[b] != be_ref[jnp.maximum(b - 1, 0)])))
    def _():
        slot = slot_ref[b]
        for cp in weight_copies(be_ref[b], slot):
            cp.wait()

        @pl.when(nxt_ref[b] >= 0)
        def _():
            for cp in weight_copies(nxt_ref[b], 1 - slot):
                cp.start()
        wg_b[...] = wg_f[slot].astype(BF16)
        wu_b[...] = wu_f[slot].astype(BF16)
        wd_b[...] = wd_f[slot].astype(BF16)

    @pl.when(b < nu_ref[0])
    def _():
        xb = xs_ref[...]
        hg = _dg(xb, wg_b[...])
        hu = _dg(xb, wu_b[...])
        act = (hg * _sigmoid(hg) * hu).astype(BF16)
        yb_ref[...] = _dg(act, wd_b[...]).astype(BF16)

    @pl.when(pl.program_id(0) >= nu_ref[0])
    def _():
        yb_ref[...] = jnp.zeros_like(yb_ref)


def _experts(sched, n_used, xs, wg, wu, wdn, bm):
    p, d = xs.shape
    ff = wg.shape[2]
    n_blocks = p // bm
    return pl.pallas_call(
        _experts_kernel,
        grid_spec=pltpu.PrefetchScalarGridSpec(
            num_scalar_prefetch=4,
            grid=(n_blocks,),
            in_specs=[pl.BlockSpec((bm, d), lambda b, be, sl, nx, nu: (jnp.minimum(b, nu[0] - 1), 0)),
                      pl.BlockSpec(memory_space=pl.ANY), pl.BlockSpec(memory_space=pl.ANY),
                      pl.BlockSpec(memory_space=pl.ANY)],
            out_specs=pl.BlockSpec((bm, d), lambda b, *_: (b, 0)),
            scratch_shapes=[pltpu.VMEM((2, d, ff), F32), pltpu.VMEM((2, d, ff), F32), pltpu.VMEM((2, ff, d), F32),
                            pltpu.VMEM((d, ff), BF16), pltpu.VMEM((d, ff), BF16), pltpu.VMEM((ff, d), BF16),
                            pltpu.SemaphoreType.DMA((2, 3))],
        ),
        out_shape=jax.ShapeDtypeStruct((p, d), BF16),
        compiler_params=pltpu.CompilerParams(dimension_semantics=("arbitrary",), vmem_limit_bytes=VMEM_LIMIT),
        name="moe_experts",
    )(sched['expert'], sched['slot'], sched['next'], n_used, xs, wg, wu, wdn)


def _final_kernel(nch_ref, tot_ref, off_ref, loff_ref, h1_ref, eid_ref, gate_ref, loffr_ref, p_ref, yb_hbm,
                  nple_ref, wpg_ref, wpp_ref, nfin_ref, y_out, buf, sem, *, tm, lbuf, tile0, n_parts):
    i = pl.program_id(0)

    def fetch(tile):
        base = (tile0 + tile) * N_EXPERTS

        def per_expert(e):
            src0 = off_ref[base + e]
            dst0 = loff_ref[base + e]

            def piece(o, rows):
                src = yb_hbm.at[pl.ds(pl.multiple_of(src0 + o, RUN_ALIGN), rows)]
                dst = buf.at[tile % 2, pl.ds(pl.multiple_of(dst0 + o, RUN_ALIGN), rows)]
                pltpu.make_async_copy(src, dst, sem.at[tile % 2]).start()
            _pow2_pieces(nch_ref[base + e], tm, piece)
        _for_each_expert(per_expert)

    def wait_fetch(tile):
        def piece(o, rows):
            pltpu.make_async_copy(yb_hbm.at[pl.ds(0, rows)], buf.at[tile % 2, pl.ds(0, rows)], sem.at[tile % 2]).wait()
        _pow2_pieces(tot_ref[tile0 + tile], _pow2_floor(lbuf), piece)

    @pl.when(i == 0)
    def _():
        buf[...] = jnp.zeros_like(buf)
        fetch(i)

    @pl.when(i + 1 < pl.num_programs(0))
    def _():
        fetch(i + 1)

    eid = eid_ref[...]
    lane = lax.broadcasted_iota(I32, (tm, N_EXPERTS), 1)
    e12 = (jnp.where(lane == eid[:, 0:1], 1.0, 0.0) + jnp.where(lane == eid[:, 1:2], 1.0, 0.0)).astype(BF16)
    rows_per = tm // n_parts
    picks = []
    for part in range(n_parts):
        rs = pl.ds(part * rows_per, rows_per)
        lane_p = lax.broadcasted_iota(I32, (rows_per, N_EXPERTS), 1)
        eid_p = eid_ref[rs, :]
        e1 = jnp.where(lane_p == eid_p[:, 0:1], 1.0, 0.0)
        e2 = jnp.where(lane_p == eid_p[:, 1:2], 1.0, 0.0)
        before = (lax.broadcasted_iota(I32, (rows_per, tm), 1)
                  < lax.broadcasted_iota(I32, (rows_per, tm), 0) + part * rows_per)
        slot = _dg(jnp.where(before, 1.0, 0.0).astype(BF16), e12) + loffr_ref[0]
        l1 = jnp.sum(slot * e1, axis=1, keepdims=True).astype(I32)
        l2 = jnp.sum(slot * e2, axis=1, keepdims=True).astype(I32)
        cols = lax.broadcasted_iota(I32, (rows_per, lbuf), 1)
        gate = gate_ref[rs, :]
        picks.append(jnp.where(cols == l1, gate[:, 0:1], jnp.where(cols == l2, gate[:, 1:2], 0.0)).astype(BF16))

    wait_fetch(i)
    sorted_rows = buf[i % 2]

    def stream(part):
        rs = pl.ds(part * rows_per, rows_per)
        h2 = h1_ref[rs, :] + _dg(picks[part], sorted_rows)
        yield
        gate_in = _rms(h2, nple_ref[...]).astype(BF16)
        pg = _sigmoid(_dg(gate_in, wpg_ref[...]))
        yield
        h3 = h2 + pg * _dg(p_ref[rs, :].astype(BF16), wpp_ref[...])
        y_out[rs, :] = _rms(h3, nfin_ref[...])
        yield

    _interleave(stream, n_parts)


def _final(plan, h1, eid, gate, p2, yb, wd, tm, tile0):
    m, d = h1.shape
    pd = p2.shape[1]
    lbuf = 2 * tm + LOCAL_PAD
    weights = [wd[n] for n in ('norm_ple', 'w_ple_gate', 'w_ple_proj', 'norm_final')]
    imap = lambda f: (lambda i, *_: f(i))
    return pl.pallas_call(
        functools.partial(_final_kernel, tm=tm, lbuf=lbuf, tile0=tile0, n_parts=2),
        grid_spec=pltpu.PrefetchScalarGridSpec(
            num_scalar_prefetch=4,
            grid=(m // tm,),
            in_specs=[pl.BlockSpec((tm, d), imap(lambda i: (i, 0))),
                      pl.BlockSpec((tm, ROUTER_LANES), imap(lambda i: (i, 0))),
                      pl.BlockSpec((tm, ROUTER_LANES), imap(lambda i: (i, 0))),
                      pl.BlockSpec((1, 1, N_EXPERTS), imap(lambda i: (tile0 + i, 0, 0))),
                      pl.BlockSpec((tm, pd), imap(lambda i: (i, 0))),
                      pl.BlockSpec(memory_space=pl.ANY)] +
                     [pl.BlockSpec(w.shape, imap(lambda i, n=len(w.shape): (0,) * n)) for w in weights],
            out_specs=pl.BlockSpec((tm, d), imap(lambda i: (i, 0))),
            scratch_shapes=[pltpu.VMEM((2, lbuf, d), BF16), pltpu.SemaphoreType.DMA((2,))],
        ),
        out_shape=jax.ShapeDtypeStruct((m, d), F32),
        compiler_params=pltpu.CompilerParams(dimension_semantics=("arbitrary",), vmem_limit_bytes=VMEM_LIMIT),
        name="moe_final",
    )(plan['nch'], plan['tot'], plan['off'], plan['loff'], h1, eid, gate, plan['loff_row'], p2, yb, *weights)


def _route_plan(eids, tms, bm):
    experts = jnp.arange(N_EXPERTS, dtype=I32)
    counts = []
    for eid, tm in zip(eids, tms):
        onehot = (eid[:2, :, None] == experts).astype(I32)
        counts.append(onehot.reshape(2, -1, tm, N_EXPERTS).sum(axis=(0, 2)))
    n = jnp.concatenate(counts)
    n_al = (n + RUN_ALIGN - 1) // RUN_ALIGN * RUN_ALIGN
    loff = jnp.cumsum(n_al, axis=1) - n_al
    used = n_al.sum(axis=0)
    region = (used + bm - 1) // bm * bm
    pend = jnp.cumsum(region)
    off = (pend - region)[None, :] + jnp.cumsum(n_al, axis=0) - n_al
    n_assign = sum(2 * e.shape[1] for e in eids)
    n_blocks = -(-(n_assign + (RUN_ALIGN - 1) * N_EXPERTS * n.shape[0] + N_EXPERTS * (bm - 1)) // bm)
    block_start = jnp.arange(n_blocks, dtype=I32) * bm
    block_expert = jnp.minimum(jnp.sum((pend[None, :] <= block_start[:, None]).astype(I32), axis=1), N_EXPERTS - 1)
    plan = dict(nch=(n_al // RUN_ALIGN).reshape(-1).astype(I32), off=off.reshape(-1).astype(I32),
                tot=(n_al.sum(axis=1) // RUN_ALIGN).astype(I32),
                loff=loff.reshape(-1).astype(I32), tail_n=((region - used) // RUN_ALIGN).astype(I32),
                tail_off=(pend - region + used).astype(I32), loff_col=loff.astype(F32)[:, :, None],
                loff_row=loff.astype(F32)[:, None, :])
    n_used = pend[-1] // bm
    prev = jnp.concatenate([jnp.full((1,), -1, I32), block_expert[:-1].astype(I32)])
    slot = (jnp.cumsum((block_expert != prev).astype(I32)) - 1) % 2
    after = pend[block_expert] // bm
    nxt = jnp.where(after < n_used, block_expert[jnp.minimum(after, n_blocks - 1)], -1)
    sched = dict(expert=block_expert.astype(I32), slot=slot.astype(I32), next=nxt.astype(I32))
    return plan, sched, n_used.astype(I32).reshape(1), n_blocks


def _layer_front(x, shift0, wkv0, conv0, wd, tm_in, tm_tok, wkv_tile, wkv_sub, wkv_chunk, chained):
    bsz, t, d = x.shape
    r, lw, k2, v, ah, bh, g, bonus, yb, shift_new, conv_new = _mix_in(x, shift0, conv0, wd, tm_in, 4 if tm_in >= 512 else 1)
    y, s_new = _wkv(r, lw, k2, v, ah, bh, jnp.swapaxes(wkv0, -1, -2), wkv_tile, wkv_sub, wkv_chunk, chained)
    flat = lambda z: z.reshape(bsz * t, z.shape[-1])
    h1, xn2, eid, eid_t, gate = _mix_out(flat(x), flat(y), flat(bonus), flat(g), flat(yb), wd, tm_tok, 2)
    return h1, xn2, (eid, eid_t), gate, shift_new.reshape(bsz, d), jnp.swapaxes(s_new, -1, -2), conv_new


def kernel(x_prompt, x_sample, state_shift, state_wkv, cache_conv, p_prompt, p_sample, norm_mix, w_in, mu_rkv, mu_w, mu_a, mu_g, w0, w1, w2, a0, a1, a2, g1, g2, k_k, k_a, r_k, ln_x_w, ln_x_b, dw_w, dw_b, cln_w, cln_b, w_out, norm_ffn, w_router_group, w_router_expert, w_exp_gate, w_exp_up, w_exp_down, norm_ple, w_ple_gate, w_ple_proj, norm_final):
    depth = norm_mix.shape[0]
    assert depth == 1
    d = x_prompt.shape[-1]
    c = w0.shape[-1]
    row = lambda z: z[0].reshape(1, -1).astype(F32)
    lane = jnp.arange(LANES, dtype=I32) // HEAD_SIZE
    w_router = jnp.concatenate([w_router_expert[0], w_router_group[0],
                                jnp.zeros((d, ROUTER_LANES - N_EXPERTS - N_EXPERT_GROUPS), F32)], axis=1)
    first = jnp.concatenate([w1[0], a1[0], g1[0]], axis=1)
    mixed = jnp.concatenate([mu_w[0][:, None] * w1[0], mu_a[0][:, None] * a1[0], mu_g[0][:, None] * g1[0]], axis=1)
    d_w, d_a = w1.shape[2], a1.shape[2]
    second = jnp.concatenate([jnp.concatenate([w2[0], jnp.zeros((d_w, c), F32)], axis=1),
                              jnp.concatenate([jnp.zeros((d_a, c), F32), a2[0]], axis=1)], axis=0)
    glu_half = jnp.concatenate([jnp.ones((3 * c,), F32), jnp.full((w_in.shape[2] - 3 * c,), 0.5, F32)])
    wd = dict(
        norm_mix=row(norm_mix), w_in=(w_in[0] * glu_half).astype(BF16), mu_rkv=row(mu_rkv),
        w0=0.5 * row(w0), a0=0.5 * row(a0),
        w_lora1=jnp.concatenate([first, mixed], axis=0).astype(BF16), w_lora2=(0.5 * second).astype(BF16),
        g2=g2[0].astype(BF16), lora_dims=(d_w, d_a),
        k_k=row(k_k), k_a=row(k_a), r_k=row(r_k), ln_x_w=row(ln_x_w), ln_x_b=row(ln_x_b),
        seg=(lane[:, None] == lane[None, :]).astype(BF16),
        dw_w=dw_w[0].astype(F32), dw_b=row(dw_b), cln_w=0.5 * row(cln_w), cln_b=0.5 * row(cln_b),
        w_out=w_out[0].astype(BF16), norm_ffn=row(norm_ffn),
        w_router=jnp.concatenate(_split2(w_router), axis=1),
        norm_ple=row(norm_ple), w_ple_gate=w_ple_gate[0].astype(BF16), w_ple_proj=w_ple_proj[0].astype(BF16),
        norm_final=norm_final.reshape(1, -1).astype(F32),
    )
    bp, tp, _ = x_prompt.shape
    bs, ts, _ = x_sample.shape
    mp, ms = bp * tp, bs * ts
    tm_p = min(512, tp)
    tm_s = min(256, ms)

    zeros = lambda *s: jnp.zeros(s, F32)
    h1_p, xn_p, eid_p, gate_p, shift_p, wkv_p, conv_p = _layer_front(
        x_prompt, zeros(bp, 1, d), zeros(bp, c // HEAD_SIZE, HEAD_SIZE, HEAD_SIZE), zeros(bp, CONV_CARRY, c),
        wd, tm_p, min(512, mp), min(128, tp), 2 if tp >= 256 else 1, min(64, tp), True)
    h1_s, xn_s, eid_s, gate_s, shift_s, wkv_s, conv_s = _layer_front(
        x_sample, state_shift[0][:, None, :], state_wkv[0], cache_conv[0],
        wd, ts, tm_s, min(128, ms), 2 if ms >= 256 else 1, ts, False)

    bm = 512
    tr_p, tr_s = min(512, mp), min(512, ms)
    plan, sched, n_used, n_blocks = _route_plan([eid_p[1], eid_s[1]], [tr_p, tr_s], bm)
    tiles_p = mp // tr_p
    xs = _dispatch(plan, n_used, [eid_p[1], eid_s[1]], [xn_p, xn_s], [tr_p, tr_s], n_blocks, bm)
    yb = _experts(sched, n_used, xs, w_exp_gate[0], w_exp_up[0], w_exp_down[0], bm)
    y_p = _final(plan, h1_p, eid_p[0], gate_p, p_prompt[0].reshape(mp, -1), yb, wd, tr_p, 0)
    y_s = _final(plan, h1_s, eid_s[0], gate_s, p_sample[0].reshape(ms, -1), yb, wd, tr_s, tiles_p)
    return (y_p.reshape(x_prompt.shape), y_s.reshape(x_sample.shape), shift_p[None], wkv_p[None], conv_p[None],
            shift_s[None], wkv_s[None], conv_s[None])
```

```python
import functools

import jax
import jax.numpy as jnp
from jax import lax
from jax.experimental import pallas as pl
from jax.experimental.pallas import tpu as pltpu

F32 = jnp.float32
BF16 = jnp.bfloat16
I32 = jnp.int32

HEAD_SIZE = 64
CONV_WIDTH = 31
CONV_CARRY = CONV_WIDTH - 1
SUBLANES = 8
LANES = 128
CARRY_PAD = 32
N_EXPERT_GROUPS = 4
EXPERTS_PER_GROUP = 8
N_EXPERTS = N_EXPERT_GROUPS * EXPERTS_PER_GROUP
ROUTER_LANES = 128
RMS_EPS = 1e-6
LN_EPS = 1e-5
GN_EPS = 64e-5
DECAY_SCALE = 0.6065306597126334
INV_BASE = 16
RUN_ALIGN = 16
LOCAL_PAD = N_EXPERTS * RUN_ALIGN
VMEM_LIMIT = 56 * 1024 * 1024

NN = ((1,), (0,))
NT = ((1,), (1,))
TN = ((0,), (0,))


def _dg(a, b, dims=NN):
    return lax.dot_general(a, b, (dims, ((), ())), preferred_element_type=F32)


def _split2(x):
    hi = x.astype(BF16)
    lo = (x - hi.astype(F32)).astype(BF16)
    return hi, lo


def _split3(x):
    hi = x.astype(BF16)
    r1 = x - hi.astype(F32)
    mid = r1.astype(BF16)
    lo = (r1 - mid.astype(F32)).astype(BF16)
    return hi, mid, lo


def _bdot(a, b, dims=NN):
    return _dg(a.astype(BF16), b.astype(BF16), dims)


def _mask_dot(mask_bf16, x):
    h, m, l = _split3(x)
    return _dg(mask_bf16, h) + (_dg(mask_bf16, m) + _dg(mask_bf16, l))


def _seg_sum(x, seg_bf16):
    h, l = _split2(x)
    w = seg_bf16.shape[0]
    return jnp.concatenate([_dg(h[:, j:j + w], seg_bf16) + _dg(l[:, j:j + w], seg_bf16)
                            for j in range(0, x.shape[1], w)], axis=1)


def _rms(x, g):
    return x * lax.rsqrt(jnp.mean(x * x, axis=-1, keepdims=True) + RMS_EPS) * g


def _sigmoid(x):
    return 0.5 * jnp.tanh(0.5 * x) + 0.5


def _full(shape):
    n = len(shape)
    return pl.BlockSpec(shape, lambda *_: (0,) * n, pipeline_mode=pl.Buffered(1))


def _mix_in_kernel(x_ref, shift_ref, conv_ref, nm_ref, win_ref, murkv_ref, w0_ref, a0_ref, wl1_ref, wl2_ref, g2_ref,
                   kk_ref, ka_ref, rk_ref, seg_ref, dww_ref, dwb_ref, clnw_ref, clnb_ref,
                   r_out, lw_out, k_out, v_out, a_out, b_out, g_out, bonus_out, yb_out, shift_out, conv_out,
                   xn_last, h_last, up_ext, shifted, *, tm, c, n_parts, d_w, d_a, seg_len):
    i = pl.program_id(1)
    n_seq = tm // seg_len if seg_len else 1
    stride = CARRY_PAD + seg_len

    if seg_len:
        for b in range(n_seq):
            up_ext[b * stride:b * stride + CARRY_PAD - CONV_CARRY, :] = jnp.zeros((CARRY_PAD - CONV_CARRY, c), F32)
            up_ext[b * stride + CARRY_PAD - CONV_CARRY:b * stride + CARRY_PAD, :] = conv_ref[b]
    else:
        @pl.when(i == 0)
        def _():
            sp = shift_ref[0]
            xn_last[...] = sp
            sp8 = jnp.broadcast_to(sp, (8, sp.shape[1])).astype(BF16)
            h_last[...] = _dg(sp8, win_ref[:, :3 * c])[0:1]
            up_ext[CARRY_PAD - CONV_CARRY:CARRY_PAD, :] = conv_ref[0]

    rp = tm // n_parts
    last_rows = {}

    def stream(part):
        rs = pl.ds(part * rp, rp)
        xn = _rms(x_ref[0, rs, :], nm_ref[...])
        hin = _dg(xn.astype(BF16), win_ref[...])
        h_rkv = hin[:, :3 * c]
        last_rows[part] = (xn[rp - 1:rp], h_rkv[rp - 1:rp])
        last_rows['xn'] = xn
        yield
        if seg_len:
            xn_prev = shift_ref[0]
            h_prev = _dg(xn_prev.astype(BF16), win_ref[:, :3 * c])
            first = (lax.broadcasted_iota(I32, (rp, 1), 0) & (seg_len - 1)) == 0
        else:
            xn_prev, h_prev = (xn_last[...], h_last[...]) if part == 0 else last_rows[part - 1]
            first = lax.broadcasted_iota(I32, (rp, 1), 0) == 0
        dx = jnp.where(first, xn_prev, pltpu.roll(xn, 1, 0)) - xn
        hprev = jnp.where(first, h_prev, pltpu.roll(h_rkv, 1, 0))
        rkv = h_rkv + (hprev - h_rkv) * murkv_ref[...]
        r = rkv[:, :c]
        k = rkv[:, c:2 * c]
        v = rkv[:, 2 * c:]
        l1 = _dg(jnp.concatenate([xn.astype(BF16), dx.astype(BF16)], axis=1), wl1_ref[...])
        lane = lax.broadcasted_iota(I32, l1.shape, 1)
        act = jnp.where(lane < d_w, jnp.tanh(l1), jnp.where(lane < d_w + d_a, l1, _sigmoid(l1))).astype(BF16)
        yield
        za = _dg(act[:, :d_w + d_a], wl2_ref[...])
        a = 0.5 * jnp.tanh(a0_ref[...] + za[:, c:]) + 0.5
        g_out[0, rs, :] = _dg(act[:, d_w + d_a:], g2_ref[...])
        seg = seg_ref[...]
        kk = k * kk_ref[...]
        kk = kk * jnp.minimum(lax.rsqrt(_seg_sum(kk * kk, seg)), 1e12)
        k2 = k * (1.0 + (a - 1.0) * ka_ref[...])
        r_out[0, rs, :] = r
        lw_out[0, rs, :] = (-0.5 * DECAY_SCALE) * jnp.tanh(w0_ref[...] + za[:, :c]) - 0.5 * DECAY_SCALE
        k_out[0, rs, :] = k2
        v_out[0, rs, :] = v
        a_out[0, rs, :] = -kk
        b_out[0, rs, :] = kk * a
        yield
        bonus_out[0, rs, :] = _seg_sum(r * k2 * rk_ref[...], seg) * v
        u = hin[:, 3 * c:4 * c] * (jnp.tanh(hin[:, 4 * c:]) + 1.0)
        if seg_len:
            for b in range(n_seq):
                up_ext[b * stride + CARRY_PAD:(b + 1) * stride, :] = u[b * seg_len:(b + 1) * seg_len]
        else:
            up_ext[pl.ds(CARRY_PAD + part * rp, rp), :] = u
        yield
        first_row = CARRY_PAD - CONV_CARRY + part * rp
        n_out = n_seq * stride - CARRY_PAD if seg_len else rp
        for s in range(SUBLANES):
            span = n_out + (CONV_WIDTH - 1 - s) // SUBLANES * SUBLANES
            shifted[part, s, 0:span, :] = up_ext[pl.ds(first_row + s, span), :]
        z = jnp.zeros((n_out, c), F32) + dwb_ref[...]
        for j in range(CONV_WIDTH):
            s, m = j % SUBLANES, j // SUBLANES
            z = z + dww_ref[j:j + 1, :] * shifted[part, s, m * SUBLANES:m * SUBLANES + n_out, :]
        if seg_len:
            z = jnp.concatenate([z[b * stride:b * stride + seg_len] for b in range(n_seq)], axis=0)
        mu = jnp.mean(z, axis=-1, keepdims=True)
        zc = z - mu
        var = jnp.mean(zc * zc, axis=-1, keepdims=True)
        zh = zc * lax.rsqrt(var + LN_EPS) * clnw_ref[...] + clnb_ref[...]
        yb_out[0, rs, :] = zh * (jnp.tanh(zh) + 1.0)
        yield

    _interleave(stream, n_parts)

    xn_end, h_end = last_rows[n_parts - 1]
    if seg_len:
        xn_tile = last_rows['xn']
        for b in range(n_seq):
            conv_out[b] = up_ext[(b + 1) * stride - CONV_CARRY:(b + 1) * stride, :]
            shift_out[b] = xn_tile[(b + 1) * seg_len - 1:(b + 1) * seg_len]
    else:
        tail = up_ext[pl.ds(tm + CARRY_PAD - CONV_CARRY, CONV_CARRY), :]
        up_ext[CARRY_PAD - CONV_CARRY:CARRY_PAD, :] = tail
        conv_out[0] = tail
        xn_last[...] = xn_end
        h_last[...] = h_end
        shift_out[0] = xn_end


def _mix_in(x, shift0, conv0, wd, tm, n_parts, whole_sequences):
    bsz, t, d = x.shape
    c = wd['w0'].shape[1]
    weights = [wd[n] for n in ('norm_mix', 'w_in', 'mu_rkv', 'w0', 'a0', 'w_lora1', 'w_lora2', 'g2',
                               'k_k', 'k_a', 'r_k', 'seg', 'dw_w', 'dw_b', 'cln_w', 'cln_b')]
    d_w, d_a = wd['lora_dims']
    if whole_sequences:
        tm, seg_len, n_seq, rows_b = bsz * t, t, bsz, 1
        assert t & (t - 1) == 0 and n_parts == 1
        x = x.reshape(1, tm, d)
        shift0 = jnp.repeat(shift0.reshape(bsz, d), t, axis=0).reshape(1, tm, d)
        grid = (1, 1)
        state = lambda rows, w: pl.BlockSpec((n_seq, rows, w), lambda b, i: (0, 0, 0))
        shift_in = pl.BlockSpec((1, tm, d), lambda b, i: (0, 0, 0))
        window_rows = n_seq * (CARRY_PAD + seg_len)
    else:
        seg_len, rows_b = 0, bsz
        grid = (bsz, t // tm)
        state = lambda rows, w: pl.BlockSpec((1, rows, w), lambda b, i: (b, 0, 0))
        shift_in = state(1, d)
        window_rows = tm + CARRY_PAD
    tok = lambda w: pl.BlockSpec((1, tm, w), lambda b, i: (b, i, 0))
    out_tok = jax.ShapeDtypeStruct((rows_b, x.shape[1], c), F32)
    outs = pl.pallas_call(
        functools.partial(_mix_in_kernel, tm=tm, c=c, n_parts=n_parts, d_w=d_w, d_a=d_a, seg_len=seg_len),
        grid=grid,
        in_specs=[tok(d), shift_in, state(CONV_CARRY, c)] + [_full(w.shape) for w in weights],
        out_specs=[tok(c)] * 9 + [state(1, d), state(CONV_CARRY, c)],
        out_shape=[out_tok] * 9 + [jax.ShapeDtypeStruct((bsz, 1, d), F32),
                                   jax.ShapeDtypeStruct((bsz, CONV_CARRY, c), F32)],
        scratch_shapes=[pltpu.VMEM((1, d), F32), pltpu.VMEM((1, 3 * c), F32),
                        pltpu.VMEM((window_rows, c), F32),
                        pltpu.VMEM((n_parts, SUBLANES, (window_rows - CARRY_PAD) // n_parts + CARRY_PAD - SUBLANES, c),
                                   F32)],
        compiler_params=pltpu.CompilerParams(dimension_semantics=("arbitrary", "arbitrary"),
                                             vmem_limit_bytes=VMEM_LIMIT),
        name="mix_in",
    )(x, shift0, conv0, *weights)
    return [o.reshape(bsz, t, c) for o in outs[:9]] + list(outs[9:])


def _tri_inverse(n_strict, row, col, lg_chunk):
    lg_base = INV_BASE.bit_length() - 1
    same = lambda sh: (row >> sh) == (col >> sh)
    lg0 = min(lg_base, lg_chunk)
    n = row.shape[0]

    def expand(c, lg):
        return jnp.where(same(lg), jnp.concatenate([c] * (n >> lg), axis=0), 0.0).astype(BF16)

    def fold(x, lg):
        b = 1 << lg
        return functools.reduce(lambda u, v: u + v, [x[i:i + b] for i in range(0, n, b)])

    b0 = 1 << lg0
    row_c = lax.broadcasted_iota(I32, (b0, n), 0)
    col_c = lax.broadcasted_iota(I32, (b0, n), 1)
    p_full = [jnp.where(same(lg0), x, 0.0) for x in n_strict]
    p = [fold(x, lg0) for x in p_full]
    t = [jnp.where((col_c & (b0 - 1)) == row_c, 1.0, 0.0) + x for x in p]
    p_full = [x.astype(BF16) for x in p_full]
    for _ in range(lg0 - 1):
        p = [_dg(x.astype(BF16), y) for x, y in zip(p, p_full)]
        p_full = [expand(x, lg0) for x in p]
        t = [x + _dg(x.astype(BF16), y) for x, y in zip(t, p_full)]
    for lg in range(lg0, lg_chunk):
        off_mask = same(lg + 1) & jnp.logical_not(same(lg))
        t_full = [expand(x, lg) for x in t]
        u = [_dg(x.astype(BF16), jnp.where(off_mask, m, 0.0).astype(BF16)) for x, m in zip(t, n_strict)]
        add = [_dg(x.astype(BF16), y) for x, y in zip(u, t_full)]
        even = ((lax.broadcasted_iota(I32, (1 << lg, n), 1) >> lg) & 1) == 0
        t = [jnp.concatenate([jnp.where(even, x, 0.0), jnp.where(even, a, x)], axis=0) for x, a in zip(t, add)]
    return [expand(x, lg_chunk) for x in t]


def _wkv_kernel(r_ref, lw_ref, k_ref, v_ref, a_ref, b_ref, s0_ref, y_ref, s_out, s_scr,
                *, tt, n_sub, chunk, chained):
    ti = pl.program_id(1)
    n_heads = r_ref.shape[2] // HEAD_SIZE
    n_chunks = tt // chunk
    lg_chunk = chunk.bit_length() - 1
    row = lax.broadcasted_iota(I32, (tt, tt), 0)
    col = lax.broadcasted_iota(I32, (tt, tt), 1)
    in_chunk = (row >> lg_chunk) == (col >> lg_chunk)
    tri_incl = in_chunk & (col <= row)
    tri_strict = in_chunk & (col < row)
    m_cum = jnp.where(tri_incl, 1.0, 0.0).astype(BF16)
    m_tot = jnp.where(in_chunk, 1.0, 0.0).astype(BF16)
    row_h = lax.broadcasted_iota(I32, (HEAD_SIZE, HEAD_SIZE), 0)
    col_h = lax.broadcasted_iota(I32, (HEAD_SIZE, HEAD_SIZE), 1)
    eye_h = row_h == col_h

    if chained:
        @pl.when(ti == 0)
        def _():
            s_scr[...] = s0_ref[0]

    rt_all, g_end_all, cols = [], [], {name: [] for name in ('v', 'rt', 'at', 'kt', 'bt', 'bd', 'kd')}
    hsl = [slice(HEAD_SIZE * h, HEAD_SIZE * (h + 1)) for h in range(n_heads)]
    for sub in range(n_sub):
        rows = pl.ds(sub * tt, tt)
        lw_all = lw_ref[0, rows, :]
        k_all = k_ref[0, rows, :]
        b_all = b_ref[0, rows, :]
        cum = _mask_dot(m_cum, lw_all)
        tot = _mask_dot(m_tot, lw_all)
        e_neg = jnp.exp(-cum)
        e_end = jnp.exp(tot - cum)
        full = dict(v=v_ref[0, rows, :], rt=r_ref[0, rows, :] * jnp.exp(cum),
                    at=a_ref[0, rows, :] * jnp.exp(cum - lw_all), kt=k_all * e_neg, bt=b_all * e_neg,
                    bd=b_all * e_end, kd=k_all * e_end)
        rt_all.append(full['rt'])
        g_end_all.append(jnp.exp(tot))
        for name, z in full.items():
            cols[name] += [z[:, s_].astype(BF16) for s_ in hsl]
    v, rt, at, kt, bt, bd, kd = (cols[name] for name in ('v', 'rt', 'at', 'kt', 'bt', 'bd', 'kd'))

    units = range(n_sub * n_heads)
    mm = [_dg(jnp.concatenate([at[u], rt[u]], axis=0), jnp.concatenate([bt[u], kt[u]], axis=0), NT) for u in units]
    m_ab = [jnp.where(tri_strict, mm[u][:tt, :tt], 0.0) for u in units]
    m_ak = [jnp.where(tri_strict, mm[u][:tt, tt:], 0.0).astype(BF16) for u in units]
    m_rb = [jnp.where(tri_incl, mm[u][tt:, :tt], 0.0).astype(BF16) for u in units]
    m_rk = [jnp.where(tri_incl, mm[u][tt:, tt:], 0.0).astype(BF16) for u in units]
    tinv = _tri_inverse(m_ab, row, col, lg_chunk)
    akv = [_dg(m_ak[u], v[u]).astype(BF16) for u in units]
    w1 = [_dg(tinv[u], at[u]).astype(BF16) for u in units]
    w2 = [_dg(tinv[u], akv[u]).astype(BF16) for u in units]
    q = [(rt_all[u // n_heads][:, hsl[u % n_heads]] + _dg(m_rb[u], w1[u])).astype(BF16) for u in units]
    y0 = [_dg(m_rb[u], w2[u]) + _dg(m_rk[u], v[u]) for u in units]

    heads = range(n_heads)
    if chained:
        s = [s_scr[h] for h in heads]
    for sub in range(n_sub):
        for ci in range(n_chunks):
            cs = slice(ci * chunk, (ci + 1) * chunk)
            seq = sub * n_chunks + ci
            if not chained:
                s = [s0_ref[seq, h] for h in heads]
            g_row = g_end_all[sub][ci * chunk:ci * chunk + 1]
            un = [sub * n_heads + h for h in heads]
            gm = [jnp.where(eye_h, jnp.broadcast_to(g_row[:, hsl[h]], (HEAD_SIZE, HEAD_SIZE)), 0.0)
                  + _dg(bd[un[h]][cs], w1[un[h]][cs], TN) for h in heads]
            hm = [_dg(bd[un[h]][cs], w2[un[h]][cs], TN) + _dg(kd[un[h]][cs], v[un[h]][cs], TN) for h in heads]
            for h in heads:
                y_ref[0, pl.ds(sub * tt + ci * chunk, chunk), hsl[h]] = _bdot(q[un[h]][cs], s[h]) + y0[un[h]][cs]
            s = [_bdot(gm[h], s[h]) + hm[h] for h in heads]
            if not chained:
                for h in heads:
                    s_out[seq, h] = s[h]
    if chained:
        for h in heads:
            s_scr[h] = s[h]
            s_out[0, h] = s[h]


def _wkv(r, lw, k, v, a, b, s0t, tt, n_sub, chunk, chained):
    bsz, t, c = r.shape
    n_heads = c // HEAD_SIZE
    hs = HEAD_SIZE
    step = tt * n_sub
    if chained:
        grid = (bsz, t // step)
        tok = pl.BlockSpec((1, step, c), lambda bi, ti: (bi, ti, 0))
        st = pl.BlockSpec((1, n_heads, hs, hs), lambda bi, ti: (bi, 0, 0, 0))
        args = (r, lw, k, v, a, b)
        y_shape = (bsz, t, c)
    else:
        assert t == chunk and (bsz * t) % step == 0
        n_seq = step // chunk
        grid = (1, bsz * t // step)
        tok = pl.BlockSpec((1, step, c), lambda bi, ti: (0, ti, 0))
        st = pl.BlockSpec((n_seq, n_heads, hs, hs), lambda bi, ti: (ti, 0, 0, 0))
        args = tuple(z.reshape(1, bsz * t, c) for z in (r, lw, k, v, a, b))
        y_shape = (1, bsz * t, c)
    y, s_new = pl.pallas_call(
        functools.partial(_wkv_kernel, tt=tt, n_sub=n_sub, chunk=chunk, chained=chained),
        grid=grid,
        in_specs=[tok] * 6 + [st],
        out_specs=[tok, st],
        out_shape=[jax.ShapeDtypeStruct(y_shape, F32), jax.ShapeDtypeStruct(s0t.shape, F32)],
        scratch_shapes=[pltpu.VMEM((n_heads, hs, hs), F32)],
        compiler_params=pltpu.CompilerParams(dimension_semantics=("arbitrary",) * 2,
                                             vmem_limit_bytes=VMEM_LIMIT),
        name="wkv",
    )(*args, s0t)
    return y.reshape(bsz, t, c), s_new


def _interleave(make_stream, n_parts):
    for _ in zip(*[make_stream(part) for part in range(n_parts)]):
        pass


def _mix_out_kernel(x_ref, y_ref, bonus_ref, g_ref, yb_ref, lnw_ref, lnb_ref, seg_ref, wout_ref, nffn_ref,
                    wr_ref, h1_out, xn_out, eid_out, eidt_out, gate_out, *, c, n_parts):
    rows_per = x_ref.shape[0] // n_parts

    def stream(part):
        rs = pl.ds(part * rows_per, rows_per)
        seg = seg_ref[...]
        y = y_ref[rs, :]
        inv_n = 1.0 / HEAD_SIZE
        mu = _seg_sum(y, seg) * inv_n
        yield
        yc = y - mu
        var = _seg_sum(yc * yc, seg) * inv_n
        yield
        yn = yc * lax.rsqrt(var + GN_EPS) * lnw_ref[...] + lnb_ref[...]
        ya = (yn + bonus_ref[rs, :]) * g_ref[rs, :]
        mix = _dg(ya.astype(BF16), wout_ref[:c, :]) + _dg(yb_ref[rs, :].astype(BF16), wout_ref[c:, :])
        yield
        h1 = x_ref[rs, :] + mix
        h1_out[rs, :] = h1
        xn = _rms(h1, nffn_ref[...])
        xn_out[rs, :] = xn.astype(BF16)
        xh, xl = _split2(xn)
        hi_lo = _dg(xh, wr_ref[...])
        logits = hi_lo[:, :ROUTER_LANES] + (hi_lo[:, ROUTER_LANES:] + _dg(xl, wr_ref[:, :ROUTER_LANES]))
        yield
        lane = lax.broadcasted_iota(I32, logits.shape, 1)
        neg = jnp.float32(-jnp.inf)
        is_g = (lane >= N_EXPERTS) & (lane < N_EXPERTS + N_EXPERT_GROUPS)
        glog = jnp.where(is_g, logits, neg)
        gmax = jnp.max(glog, axis=-1, keepdims=True)
        gsel = jnp.min(jnp.where(glog == gmax, lane, 4 * ROUTER_LANES), axis=-1, keepdims=True) - N_EXPERTS
        gp = 1.0 / jnp.sum(jnp.where(is_g, jnp.exp(glog - gmax), 0.0), axis=-1, keepdims=True)
        in_grp = (lane >= gsel * EXPERTS_PER_GROUP) & (lane < (gsel + 1) * EXPERTS_PER_GROUP)
        elog = jnp.where(in_grp, logits, neg)
        emax = jnp.max(elog, axis=-1, keepdims=True)
        ex = jnp.where(in_grp, jnp.exp(elog - emax), 0.0)
        eprob = ex / jnp.sum(ex, axis=-1, keepdims=True)
        eprob = jnp.where(in_grp, eprob, -1.0)
        yield
        v1 = jnp.max(eprob, axis=-1, keepdims=True)
        i1 = jnp.min(jnp.where(eprob == v1, lane, 4 * ROUTER_LANES), axis=-1, keepdims=True)
        rest = jnp.where(lane == i1, -1.0, eprob)
        v2 = jnp.max(rest, axis=-1, keepdims=True)
        i2 = jnp.min(jnp.where(rest == v2, lane, 4 * ROUTER_LANES), axis=-1, keepdims=True)
        denom = v1 + v2
        eid = jnp.where(lane == 0, i1, jnp.where(lane == 1, i2, 0))
        eid_out[rs, :] = eid
        eidt_out[:, rs] = jnp.transpose(eid)[:SUBLANES]
        gate_out[rs, :] =jnp.where(lane == 0, gp * v1 / denom, jnp.where(lane == 1, gp * v2 / denom, 0.0))
        yield

    _interleave(stream, n_parts)


def _mix_out(x2, y2, bonus2, g2, yb2, wd, tm, n_parts):
    m, d = x2.shape
    c = y2.shape[1]
    tokd = pl.BlockSpec((tm, d), lambda i: (i, 0))
    tokc = pl.BlockSpec((tm, c), lambda i: (i, 0))
    tokr = pl.BlockSpec((tm, ROUTER_LANES), lambda i: (i, 0))
    weights = [wd[n] for n in ('ln_x_w', 'ln_x_b', 'seg', 'w_out', 'norm_ffn', 'w_router')]
    return pl.pallas_call(
        functools.partial(_mix_out_kernel, c=c, n_parts=n_parts),
        grid=(m // tm,),
        in_specs=[tokd, tokc, tokc, tokc, tokc] + [_full(w.shape) for w in weights],
        out_specs=[tokd, tokd, tokr, pl.BlockSpec((SUBLANES, tm), lambda i: (0, i)), tokr],
        out_shape=[jax.ShapeDtypeStruct((m, d), F32), jax.ShapeDtypeStruct((m, d), BF16),
                   jax.ShapeDtypeStruct((m, ROUTER_LANES), I32), jax.ShapeDtypeStruct((SUBLANES, m), I32),
                   jax.ShapeDtypeStruct((m, ROUTER_LANES), F32)],
        compiler_params=pltpu.CompilerParams(dimension_semantics=("arbitrary",), vmem_limit_bytes=VMEM_LIMIT),
        name="mix_out",
    )(x2, y2, bonus2, g2, yb2, *weights)


def _pow2_pieces(count, max_rows, fn):
    off = 0
    rows = max_rows
    while rows >= RUN_ALIGN:
        has = (count & (rows // RUN_ALIGN)) != 0

        @pl.when(has)
        def _(off=off, rows=rows):
            fn(off, rows)
        off = off + jnp.where(has, rows, 0)
        rows //= 2


def _for_each_expert(fn):
    def body(e, carry):
        fn(e)
        return carry
    lax.fori_loop(0, N_EXPERTS, body, 0)


def _pow2_floor(n):
    return 1 << (n.bit_length() - 1)


def _dispatch_kernel(nch_ref, tot_ref, off_ref, loff_ref, tn_ref, toff_ref, nu_ref, *refs, groups, bm, n_blocks):
    n_g = len(groups)
    eid_refs, xn_refs = refs[:n_g], refs[n_g + 1:2 * n_g + 1]
    loffc_ref = refs[n_g]
    xs_out, buf, sem, zbuf, zsem = refs[2 * n_g + 1:]
    i = pl.program_id(0)
    last = pl.num_programs(0) - 1
    max_run = max(tm for tm, _ in groups)

    def start_tile(tile):
        def per_expert(e):
            src0 = loff_ref[tile * N_EXPERTS + e]
            dst0 = off_ref[tile * N_EXPERTS + e]

            def piece(o, rows):
                src = buf.at[tile % 2, pl.ds(pl.multiple_of(src0 + o, RUN_ALIGN), rows)]
                dst = xs_out.at[pl.ds(pl.multiple_of(dst0 + o, RUN_ALIGN), rows)]
                pltpu.make_async_copy(src, dst, sem.at[tile % 2]).start()
            _pow2_pieces(nch_ref[tile * N_EXPERTS + e], max_run, piece)
        _for_each_expert(per_expert)

    def wait_tile(tile):
        def piece(o, rows):
            pltpu.make_async_copy(buf.at[tile % 2, pl.ds(0, rows)], xs_out.at[pl.ds(0, rows)], sem.at[tile % 2]).wait()
        _pow2_pieces(tot_ref[tile], _pow2_floor(buf.shape[1]), piece)

    @pl.when(i >= 2)
    def _():
        wait_tile(i - 2)

    def sort_tile(eid_ref, xn_ref, tm):
        e_rows = eid_ref[...]
        sub = lax.broadcasted_iota(I32, (N_EXPERTS, tm), 0)
        e1 = jnp.where(sub == e_rows[0:1], 1.0, 0.0)
        e2 = jnp.where(sub == e_rows[1:2], 1.0, 0.0)
        before = lax.broadcasted_iota(I32, (tm, tm), 0) < lax.broadcasted_iota(I32, (tm, tm), 1)
        slot = _dg((e1 + e2).astype(BF16), jnp.where(before, 1.0, 0.0).astype(BF16)) + loffc_ref[0]
        l1 = jnp.sum(slot * e1, axis=0, keepdims=True).astype(I32)
        l2 = jnp.sum(slot * e2, axis=0, keepdims=True).astype(I32)
        n_rows = 2 * tm + LOCAL_PAD
        rows = lax.broadcasted_iota(I32, (n_rows, tm), 0)
        perm = jnp.where((rows == l1) | (rows == l2), 1.0, 0.0).astype(BF16)
        buf[i % 2, 0:n_rows, :] = _dg(perm, xn_ref[...]).astype(BF16)

    first = 0
    for g, (tm, n_tiles) in enumerate(groups):
        pl.when((i >= first) & (i < first + n_tiles))(functools.partial(sort_tile, eid_refs[g], xn_refs[g], tm))
        first += n_tiles
    start_tile(i)

    @pl.when(i == 0)
    def _():
        zbuf[...] = jnp.zeros_like(zbuf)
        half = zbuf.shape[0]

        def zero_copy(off, rows):
            return pltpu.make_async_copy(zbuf.at[pl.ds(0, rows)], xs_out.at[pl.ds(off, rows)], zsem)

        def tails(fn):
            _for_each_expert(lambda e: _pow2_pieces(
                tn_ref[e], half, lambda o, rows: fn(pl.multiple_of(toff_ref[e] + o, RUN_ALIGN), rows)))

        def unused_blocks(fn):
            def body(b, carry):
                fn(pl.multiple_of(b * bm, bm), half)
                fn(pl.multiple_of(b * bm + half, half), half)
                return carry
            lax.fori_loop(nu_ref[0], n_blocks, body, 0)

        tails(lambda off, rows: zero_copy(off, rows).start())
        unused_blocks(lambda off, rows: zero_copy(off, rows).start())
        tails(lambda off, rows: zero_copy(off, rows).wait())
        unused_blocks(lambda off, rows: zero_copy(off, rows).wait())

    @pl.when(i == last)
    def _():
        @pl.when(i >= 1)
        def _():
            wait_tile(i - 1)
        wait_tile(i)


def _dispatch(plan, n_used, eids_t, xns, tms, n_blocks, bm):
    d = xns[0].shape[1]
    groups = tuple((tm, xn.shape[0] // tm) for xn, tm in zip(xns, tms))
    firsts = [sum(n for _, n in groups[:g]) for g in range(len(groups))]
    lbuf = 2 * max(tms) + LOCAL_PAD

    def tile_of(g):
        return lambda i: jnp.clip(i - firsts[g], 0, groups[g][1] - 1)

    imap = lambda f: (lambda i, *_: f(i))
    in_specs = ([pl.BlockSpec((SUBLANES, tm), imap(lambda i, g=g: (0, tile_of(g)(i)))) for g, tm in enumerate(tms)] +
                [pl.BlockSpec((1, N_EXPERTS, 1), imap(lambda i: (i, 0, 0)))] +
                [pl.BlockSpec((tm, d), imap(lambda i, g=g: (tile_of(g)(i), 0))) for g, tm in enumerate(tms)])
    return pl.pallas_call(
        functools.partial(_dispatch_kernel, groups=groups, bm=bm, n_blocks=n_blocks),
        grid_spec=pltpu.PrefetchScalarGridSpec(
            num_scalar_prefetch=7,
            grid=(sum(n for _, n in groups),),
            in_specs=in_specs,
            out_specs=pl.BlockSpec(memory_space=pl.ANY),
            scratch_shapes=[pltpu.VMEM((2, lbuf, d), BF16), pltpu.SemaphoreType.DMA((2,)),
                            pltpu.VMEM((bm // 2, d), BF16), pltpu.SemaphoreType.DMA],
        ),
        out_shape=jax.ShapeDtypeStruct((n_blocks * bm, d), BF16),
        compiler_params=pltpu.CompilerParams(dimension_semantics=("arbitrary",), vmem_limit_bytes=VMEM_LIMIT),
        name="moe_dispatch",
    )(plan['nch'], plan['tot'], plan['off'], plan['loff'], plan['tail_n'], plan['tail_off'], n_used, *eids_t,
      plan['loff_col'], *xns)


def _experts_kernel(be_ref, slot_ref, nxt_ref, nu_ref, xs_ref, wg_hbm, wu_hbm, wd_hbm, yb_ref,
                    wg_f, wu_f, wd_f, wg_b, wu_b, wd_b, sem):
    b = pl.program_id(0)

    def weight_copies(e, slot):
        pairs = ((wg_hbm, wg_f), (wu_hbm, wu_f), (wd_hbm, wd_f))
        return [pltpu.make_async_copy(src.at[e], dst.at[slot], sem.at[slot, j]) for j, (src, dst) in enumerate(pairs)]

    @pl.when(b == 0)
    def _():
        for cp in weight_copies(be_ref[0], slot_ref[0]):
            cp.start()

    @pl.when((b < nu_ref[0]) & ((b == 0) | (be_ref[b] != be_ref[jnp.maximum(b - 1, 0)])))
    def _():
        slot = slot_ref[b]
        for cp in weight_copies(be_ref[b], slot):
            cp.wait()

        @pl.when(nxt_ref[b] >= 0)
        def _():
            for cp in weight_copies(nxt_ref[b], 1 - slot):
                cp.start()
        wg_b[...] = wg_f[slot].astype(BF16)
        wu_b[...] = wu_f[slot].astype(BF16)
        wd_b[...] = wd_f[slot].astype(BF16)

    @pl.when(b < nu_ref[0])
    def _():
        xb = xs_ref[...]
        hg = _dg(xb, wg_b[...])
        hu = _dg(xb, wu_b[...])
        act = (hg * _sigmoid(hg) * hu).astype(BF16)
        yb_ref[...] = _dg(act, wd_b[...]).astype(BF16)

    @pl.when(pl.program_id(0) >= nu_ref[0])
    def _():
        yb_ref[...] = jnp.zeros_like(yb_ref)


def _experts(sched, n_used, xs, wg, wu, wdn, bm):
    p, d = xs.shape
    ff = wg.shape[2]
    n_blocks = p // bm
    return pl.pallas_call(
        _experts_kernel,
        grid_spec=pltpu.PrefetchScalarGridSpec(
            num_scalar_prefetch=4,
            grid=(n_blocks,),
            in_specs=[pl.BlockSpec((bm, d), lambda b, be, sl, nx, nu: (jnp.minimum(b, nu[0] - 1), 0)),
                      pl.BlockSpec(memory_space=pl.ANY), pl.BlockSpec(memory_space=pl.ANY),
                      pl.BlockSpec(memory_space=pl.ANY)],
            out_specs=pl.BlockSpec((bm, d), lambda b, *_: (b, 0)),
            scratch_shapes=[pltpu.VMEM((2, d, ff), F32), pltpu.VMEM((2, d, ff), F32), pltpu.VMEM((2, ff, d), F32),
                            pltpu.VMEM((d, ff), BF16), pltpu.VMEM((d, ff), BF16), pltpu.VMEM((ff, d), BF16),
                            pltpu.SemaphoreType.DMA((2, 3))],
        ),
        out_shape=jax.ShapeDtypeStruct((p, d), BF16),
        compiler_params=pltpu.CompilerParams(dimension_semantics=("arbitrary",), vmem_limit_bytes=VMEM_LIMIT),
        name="moe_experts",
    )(sched['expert'], sched['slot'], sched['next'], n_used, xs, wg, wu, wdn)


def _final_kernel(nch_ref, tot_ref, off_ref, loff_ref, h1_ref, eid_ref, gate_ref, loffr_ref, p_ref, yb_hbm,
                  nple_ref, wpg_ref, wpp_ref, nfin_ref, y_out, buf, sem, *, tm, lbuf, tile0, n_parts):
    i = pl.program_id(0)

    def fetch(tile):
        base = (tile0 + tile) * N_EXPERTS

        def per_expert(e):
            src0 = off_ref[base + e]
            dst0 = loff_ref[base + e]

            def piece(o, rows):
                src = yb_hbm.at[pl.ds(pl.multiple_of(src0 + o, RUN_ALIGN), rows)]
                dst = buf.at[tile % 2, pl.ds(pl.multiple_of(dst0 + o, RUN_ALIGN), rows)]
                pltpu.make_async_copy(src, dst, sem.at[tile % 2]).start()
            _pow2_pieces(nch_ref[base + e], tm, piece)
        _for_each_expert(per_expert)

    def wait_fetch(tile):
        def piece(o, rows):
            pltpu.make_async_copy(yb_hbm.at[pl.ds(0, rows)], buf.at[tile % 2, pl.ds(0, rows)], sem.at[tile % 2]).wait()
        _pow2_pieces(tot_ref[tile0 + tile], _pow2_floor(lbuf), piece)

    @pl.when(i == 0)
    def _():
        buf[...] = jnp.zeros_like(buf)
        fetch(i)

    @pl.when(i + 1 < pl.num_programs(0))
    def _():
        fetch(i + 1)

    eid = eid_ref[...]
    lane = lax.broadcasted_iota(I32, (tm, N_EXPERTS), 1)
    e12 = (jnp.where(lane == eid[:, 0:1], 1.0, 0.0) + jnp.where(lane == eid[:, 1:2], 1.0, 0.0)).astype(BF16)
    rows_per = tm // n_parts
    picks = []
    for part in range(n_parts):
        rs = pl.ds(part * rows_per, rows_per)
        lane_p = lax.broadcasted_iota(I32, (rows_per, N_EXPERTS), 1)
        eid_p = eid_ref[rs, :]
        e1 = jnp.where(lane_p == eid_p[:, 0:1], 1.0, 0.0)
        e2 = jnp.where(lane_p == eid_p[:, 1:2], 1.0, 0.0)
        before = (lax.broadcasted_iota(I32, (rows_per, tm), 1)
                  < lax.broadcasted_iota(I32, (rows_per, tm), 0) + part * rows_per)
        slot = _dg(jnp.where(before, 1.0, 0.0).astype(BF16), e12) + loffr_ref[0]
        l1 = jnp.sum(slot * e1, axis=1, keepdims=True).astype(I32)
        l2 = jnp.sum(slot * e2, axis=1, keepdims=True).astype(I32)
        cols = lax.broadcasted_iota(I32, (rows_per, lbuf), 1)
        gate = gate_ref[rs, :]
        picks.append(jnp.where(cols == l1, gate[:, 0:1], jnp.where(cols == l2, gate[:, 1:2], 0.0)).astype(BF16))

    wait_fetch(i)
    sorted_rows = buf[i % 2]

    def stream(part):
        rs = pl.ds(part * rows_per, rows_per)
        h2 = h1_ref[rs, :] + _dg(picks[part], sorted_rows)
        yield
        gate_in = _rms(h2, nple_ref[...]).astype(BF16)
        pg = _sigmoid(_dg(gate_in, wpg_ref[...]))
        yield
        h3 = h2 + pg * _dg(p_ref[rs, :].astype(BF16), wpp_ref[...])
        y_out[rs, :] = _rms(h3, nfin_ref[...])
        yield

    _interleave(stream, n_parts)


def _final(plan, h1, eid, gate, p2, yb, wd, tm, tile0):
    m, d = h1.shape
    pd = p2.shape[1]
    lbuf = 2 * tm + LOCAL_PAD
    weights = [wd[n] for n in ('norm_ple', 'w_ple_gate', 'w_ple_proj', 'norm_final')]
    imap = lambda f: (lambda i, *_: f(i))
    return pl.pallas_call(
        functools.partial(_final_kernel, tm=tm, lbuf=lbuf, tile0=tile0, n_parts=2),
        grid_spec=pltpu.PrefetchScalarGridSpec(
            num_scalar_prefetch=4,
            grid=(m // tm,),
            in_specs=[pl.BlockSpec((tm, d), imap(lambda i: (i, 0))),
                      pl.BlockSpec((tm, ROUTER_LANES), imap(lambda i: (i, 0))),
                      pl.BlockSpec((tm, ROUTER_LANES), imap(lambda i: (i, 0))),
                      pl.BlockSpec((1, 1, N_EXPERTS), imap(lambda i: (tile0 + i, 0, 0))),
                      pl.BlockSpec((tm, pd), imap(lambda i: (i, 0))),
                      pl.BlockSpec(memory_space=pl.ANY)] +
                     [pl.BlockSpec(w.shape, imap(lambda i, n=len(w.shape): (0,) * n)) for w in weights],
            out_specs=pl.BlockSpec((tm, d), imap(lambda i: (i, 0))),
            scratch_shapes=[pltpu.VMEM((2, lbuf, d), BF16), pltpu.SemaphoreType.DMA((2,))],
        ),
        out_shape=jax.ShapeDtypeStruct((m, d), F32),
        compiler_params=pltpu.CompilerParams(dimension_semantics=("arbitrary",), vmem_limit_bytes=VMEM_LIMIT),
        name="moe_final",
    )(plan['nch'], plan['tot'], plan['off'], plan['loff'], h1, eid, gate, plan['loff_row'], p2, yb, *weights)


def _route_plan(eids, tms, bm):
    experts = jnp.arange(N_EXPERTS, dtype=I32)
    counts = []
    for eid, tm in zip(eids, tms):
        onehot = (eid[:2, :, None] == experts).astype(I32)
        counts.append(onehot.reshape(2, -1, tm, N_EXPERTS).sum(axis=(0, 2)))
    n = jnp.concatenate(counts)
    n_al = (n + RUN_ALIGN - 1) // RUN_ALIGN * RUN_ALIGN
    loff = jnp.cumsum(n_al, axis=1) - n_al
    used = n_al.sum(axis=0)
    region = (used + bm - 1) // bm * bm
    pend = jnp.cumsum(region)
    off = (pend - region)[None, :] + jnp.cumsum(n_al, axis=0) - n_al
    n_assign = sum(2 * e.shape[1] for e in eids)
    n_blocks = -(-(n_assign + (RUN_ALIGN - 1) * N_EXPERTS * n.shape[0] + N_EXPERTS * (bm - 1)) // bm)
    block_start = jnp.arange(n_blocks, dtype=I32) * bm
    block_expert = jnp.minimum(jnp.sum((pend[None, :] <= block_start[:, None]).astype(I32), axis=1), N_EXPERTS - 1)
    plan = dict(nch=(n_al // RUN_ALIGN).reshape(-1).astype(I32), off=off.reshape(-1).astype(I32),
                tot=(n_al.sum(axis=1) // RUN_ALIGN).astype(I32),
                loff=loff.reshape(-1).astype(I32), tail_n=((region - used) // RUN_ALIGN).astype(I32),
                tail_off=(pend - region + used).astype(I32), loff_col=loff.astype(F32)[:, :, None],
                loff_row=loff.astype(F32)[:, None, :])
    n_used = pend[-1] // bm
    prev = jnp.concatenate([jnp.full((1,), -1, I32), block_expert[:-1].astype(I32)])
    slot = (jnp.cumsum((block_expert != prev).astype(I32)) - 1) % 2
    after = pend[block_expert] // bm
    nxt = jnp.where(after < n_used, block_expert[jnp.minimum(after, n_blocks - 1)], -1)
    sched = dict(expert=block_expert.astype(I32), slot=slot.astype(I32), next=nxt.astype(I32))
    return plan, sched, n_used.astype(I32).reshape(1), n_blocks


def _layer_front(x, shift0, wkv0, conv0, wd, tm_in, tm_tok, wkv_tile, wkv_sub, wkv_chunk, chained):
    bsz, t, d = x.shape
    r, lw, k2, v, ah, bh, g, bonus, yb, shift_new, conv_new = _mix_in(x, shift0, conv0, wd, tm_in, 4 if tm_in >= 512 else 1, not chained)
    y, s_new = _wkv(r, lw, k2, v, ah, bh, jnp.swapaxes(wkv0, -1, -2), wkv_tile, wkv_sub, wkv_chunk, chained)
    flat = lambda z: z.reshape(bsz * t, z.shape[-1])
    h1, xn2, eid, eid_t, gate = _mix_out(flat(x), flat(y), flat(bonus), flat(g), flat(yb), wd, tm_tok, 2)
    return h1, xn2, (eid, eid_t), gate, shift_new.reshape(bsz, d), jnp.swapaxes(s_new, -1, -2), conv_new


def kernel(x_prompt, x_sample, state_shift, state_wkv, cache_conv, p_prompt, p_sample, norm_mix, w_in, mu_rkv, mu_w, mu_a, mu_g, w0, w1, w2, a0, a1, a2, g1, g2, k_k, k_a, r_k, ln_x_w, ln_x_b, dw_w, dw_b, cln_w, cln_b, w_out, norm_ffn, w_router_group, w_router_expert, w_exp_gate, w_exp_up, w_exp_down, norm_ple, w_ple_gate, w_ple_proj, norm_final):
    depth = norm_mix.shape[0]
    assert depth == 1
    d = x_prompt.shape[-1]
    c = w0.shape[-1]
    row = lambda z: z[0].reshape(1, -1).astype(F32)
    lane = jnp.arange(LANES, dtype=I32) // HEAD_SIZE
    w_router = jnp.concatenate([w_router_expert[0], w_router_group[0],
                                jnp.zeros((d, ROUTER_LANES - N_EXPERTS - N_EXPERT_GROUPS), F32)], axis=1)
    first = jnp.concatenate([w1[0], a1[0], g1[0]], axis=1)
    mixed = jnp.concatenate([mu_w[0][:, None] * w1[0], mu_a[0][:, None] * a1[0], mu_g[0][:, None] * g1[0]], axis=1)
    d_w, d_a = w1.shape[2], a1.shape[2]
    second = jnp.concatenate([jnp.concatenate([w2[0], jnp.zeros((d_w, c), F32)], axis=1),
                              jnp.concatenate([jnp.zeros((d_a, c), F32), a2[0]], axis=1)], axis=0)
    glu_half = jnp.concatenate([jnp.ones((3 * c,), F32), jnp.full((w_in.shape[2] - 3 * c,), 0.5, F32)])
    wd = dict(
        norm_mix=row(norm_mix), w_in=(w_in[0] * glu_half).astype(BF16), mu_rkv=row(mu_rkv),
        w0=0.5 * row(w0), a0=0.5 * row(a0),
        w_lora1=jnp.concatenate([first, mixed], axis=0).astype(BF16), w_lora2=(0.5 * second).astype(BF16),
        g2=g2[0].astype(BF16), lora_dims=(d_w, d_a),
        k_k=row(k_k), k_a=row(k_a), r_k=row(r_k), ln_x_w=row(ln_x_w), ln_x_b=row(ln_x_b),
        seg=(lane[:, None] == lane[None, :]).astype(BF16),
        dw_w=dw_w[0].astype(F32), dw_b=row(dw_b), cln_w=0.5 * row(cln_w), cln_b=0.5 * row(cln_b),
        w_out=w_out[0].astype(BF16), norm_ffn=row(norm_ffn),
        w_router=jnp.concatenate(_split2(w_router), axis=1),
        norm_ple=row(norm_ple), w_ple_gate=w_ple_gate[0].astype(BF16), w_ple_proj=w_ple_proj[0].astype(BF16),
        norm_final=norm_final.reshape(1, -1).astype(F32),
    )
    bp, tp, _ = x_prompt.shape
    bs, ts, _ = x_sample.shape
    mp, ms = bp * tp, bs * ts
    tm_p = min(512, tp)
    tm_s = min(256, ms)

    zeros = lambda *s: jnp.zeros(s, F32)
    h1_p, xn_p, eid_p, gate_p, shift_p, wkv_p, conv_p = _layer_front(
        x_prompt, zeros(bp, 1, d), zeros(bp, c // HEAD_SIZE, HEAD_SIZE, HEAD_SIZE), zeros(bp, CONV_CARRY, c),
        wd, tm_p, min(512, mp), min(128, tp), 2 if tp >= 256 else 1, min(64, tp), True)
    h1_s, xn_s, eid_s, gate_s, shift_s, wkv_s, conv_s = _layer_front(
        x_sample, state_shift[0][:, None, :], state_wkv[0], cache_conv[0],
        wd, ts, tm_s, min(128, ms), 2 if ms >= 256 else 1, ts, False)

    bm = 512
    tr_p, tr_s = min(512, mp), min(512, ms)
    plan, sched, n_used, n_blocks = _route_plan([eid_p[1], eid_s[1]], [tr_p, tr_s], bm)
    tiles_p = mp // tr_p
    xs = _dispatch(plan, n_used, [eid_p[1], eid_s[1]], [xn_p, xn_s], [tr_p, tr_s], n_blocks, bm)
    yb = _experts(sched, n_used, xs, w_exp_gate[0], w_exp_up[0], w_exp_down[0], bm)
    y_p = _final(plan, h1_p, eid_p[0], gate_p, p_prompt[0].reshape(mp, -1), yb, wd, tr_p, 0)
    y_s = _final(plan, h1_s, eid_s[0], gate_s, p_sample[0].reshape(ms, -1), yb, wd, tr_s, tiles_p)
    return (y_p.reshape(x_prompt.shape), y_s.reshape(x_sample.shape), shift_p[None], wkv_p[None], conv_p[None],
            shift_s[None], wkv_s[None], conv_s[None])
```

```python
import functools

import jax
import jax.numpy as jnp
from jax import lax
from jax.experimental import pallas as pl
from jax.experimental.pallas import tpu as pltpu

F32 = jnp.float32
BF16 = jnp.bfloat16
I32 = jnp.int32

HEAD_SIZE = 64
CONV_WIDTH = 31
CONV_CARRY = CONV_WIDTH - 1
SUBLANES = 8
LANES = 128
CARRY_PAD = 32
N_EXPERT_GROUPS = 4
EXPERTS_PER_GROUP = 8
N_EXPERTS = N_EXPERT_GROUPS * EXPERTS_PER_GROUP
ROUTER_LANES = 128
RMS_EPS = 1e-6
LN_EPS = 1e-5
GN_EPS = 64e-5
DECAY_SCALE = 0.6065306597126334
INV_BASE = 16
RUN_ALIGN = 16
LOCAL_PAD = N_EXPERTS * RUN_ALIGN
VMEM_LIMIT = 56 * 1024 * 1024

NN = ((1,), (0,))
NT = ((1,), (1,))
TN = ((0,), (0,))


def _dg(a, b, dims=NN):
    return lax.dot_general(a, b, (dims, ((), ())), preferred_element_type=F32)


def _split2(x):
    hi = x.astype(BF16)
    lo = (x - hi.astype(F32)).astype(BF16)
    return hi, lo


def _bdot(a, b, dims=NN):
    return _dg(a.astype(BF16), b.astype(BF16), dims)


def _mask_dot(mask_bf16, x):
    h, l = _split2(x)
    return _dg(mask_bf16, h) + _dg(mask_bf16, l)


def _seg_sum(x, seg_bf16):
    h, l = _split2(x)
    w = seg_bf16.shape[0]
    return jnp.concatenate([_dg(h[:, j:j + w], seg_bf16) + _dg(l[:, j:j + w], seg_bf16)
                            for j in range(0, x.shape[1], w)], axis=1)


def _rms(x, g):
    return x * lax.rsqrt(jnp.mean(x * x, axis=-1, keepdims=True) + RMS_EPS) * g


def _sigmoid(x):
    return 0.5 * jnp.tanh(0.5 * x) + 0.5


def _full(shape):
    n = len(shape)
    return pl.BlockSpec(shape, lambda *_: (0,) * n, pipeline_mode=pl.Buffered(1))


def _mix_in_kernel(x_ref, shift_ref, conv_ref, nm_ref, win_ref, murkv_ref, w0_ref, a0_ref, wl1_ref, wl2_ref, g2_ref,
                   kk_ref, ka_ref, rk_ref, seg_ref, dww_ref, dwb_ref, clnw_ref, clnb_ref,
                   r_out, lw_out, k_out, v_out, a_out, b_out, g_out, bonus_out, yb_out, shift_out, conv_out,
                   xn_last, h_last, up_ext, shifted, *, tm, c, n_parts, d_w, d_a, seg_len):
    i = pl.program_id(1)
    n_seq = tm // seg_len if seg_len else 1
    stride = CARRY_PAD + seg_len

    if seg_len:
        for b in range(n_seq):
            up_ext[b * stride:b * stride + CARRY_PAD - CONV_CARRY, :] = jnp.zeros((CARRY_PAD - CONV_CARRY, c), F32)
            up_ext[b * stride + CARRY_PAD - CONV_CARRY:b * stride + CARRY_PAD, :] = conv_ref[b]
    else:
        @pl.when(i == 0)
        def _():
            sp = shift_ref[0]
            xn_last[...] = sp
            sp8 = jnp.broadcast_to(sp, (8, sp.shape[1])).astype(BF16)
            h_last[...] = _dg(sp8, win_ref[:, :3 * c])[0:1]
            up_ext[CARRY_PAD - CONV_CARRY:CARRY_PAD, :] = conv_ref[0]

    rp = tm // n_parts
    last_rows = {}

    def stream(part):
        rs = pl.ds(part * rp, rp)
        xn = _rms(x_ref[0, rs, :], nm_ref[...])
        hin = _dg(xn.astype(BF16), win_ref[...])
        h_rkv = hin[:, :3 * c]
        last_rows[part] = (xn[rp - 1:rp], h_rkv[rp - 1:rp])
        last_rows['xn'] = xn
        yield
        if seg_len:
            xn_prev = shift_ref[0]
            h_prev = _dg(xn_prev.astype(BF16), win_ref[:, :3 * c])
            first = (lax.broadcasted_iota(I32, (rp, 1), 0) & (seg_len - 1)) == 0
        else:
            xn_prev, h_prev = (xn_last[...], h_last[...]) if part == 0 else last_rows[part - 1]
            first = lax.broadcasted_iota(I32, (rp, 1), 0) == 0
        dx = jnp.where(first, xn_prev, pltpu.roll(xn, 1, 0)) - xn
        hprev = jnp.where(first, h_prev, pltpu.roll(h_rkv, 1, 0))
        rkv = h_rkv + (hprev - h_rkv) * murkv_ref[...]
        r = rkv[:, :c]
        k = rkv[:, c:2 * c]
        v = rkv[:, 2 * c:]
        l1 = _dg(jnp.concatenate([xn.astype(BF16), dx.astype(BF16)], axis=1), wl1_ref[...])
        lane = lax.broadcasted_iota(I32, l1.shape, 1)
        act = jnp.where(lane < d_w, jnp.tanh(l1), jnp.where(lane < d_w + d_a, l1, _sigmoid(l1))).astype(BF16)
        yield
        za = _dg(act[:, :d_w + d_a], wl2_ref[...])
        a = 0.5 * jnp.tanh(a0_ref[...] + za[:, c:]) + 0.5
        g_out[0, rs, :] = _dg(act[:, d_w + d_a:], g2_ref[...])
        seg = seg_ref[...]
        kk = k * kk_ref[...]
        kk = kk * jnp.minimum(lax.rsqrt(_seg_sum(kk * kk, seg)), 1e12)
        k2 = k * (1.0 + (a - 1.0) * ka_ref[...])
        r_out[0, rs, :] = r
        lw_out[0, rs, :] = (-0.5 * DECAY_SCALE) * jnp.tanh(w0_ref[...] + za[:, :c]) - 0.5 * DECAY_SCALE
        k_out[0, rs, :] = k2
        v_out[0, rs, :] = v
        a_out[0, rs, :] = -kk
        b_out[0, rs, :] = kk * a
        yield
        bonus_out[0, rs, :] = _seg_sum(r * k2 * rk_ref[...], seg) * v
        u = hin[:, 3 * c:4 * c] * (jnp.tanh(hin[:, 4 * c:]) + 1.0)
        if seg_len:
            for b in range(n_seq):
                up_ext[b * stride + CARRY_PAD:(b + 1) * stride, :] = u[b * seg_len:(b + 1) * seg_len]
        else:
            up_ext[pl.ds(CARRY_PAD + part * rp, rp), :] = u
        yield
        first_row = CARRY_PAD - CONV_CARRY + part * rp
        n_out = n_seq * stride - CARRY_PAD if seg_len else rp
        for s in range(SUBLANES):
            span = n_out + (CONV_WIDTH - 1 - s) // SUBLANES * SUBLANES
            shifted[part, s, 0:span, :] = up_ext[pl.ds(first_row + s, span), :]
        z = jnp.zeros((n_out, c), F32) + dwb_ref[...]
        for j in range(CONV_WIDTH):
            s, m = j % SUBLANES, j // SUBLANES
            z = z + dww_ref[j:j + 1, :] * shifted[part, s, m * SUBLANES:m * SUBLANES + n_out, :]
        if seg_len:
            z = jnp.concatenate([z[b * stride:b * stride + seg_len] for b in range(n_seq)], axis=0)
        mu = jnp.mean(z, axis=-1, keepdims=True)
        zc = z - mu
        var = jnp.mean(zc * zc, axis=-1, keepdims=True)
        zh = zc * lax.rsqrt(var + LN_EPS) * clnw_ref[...] + clnb_ref[...]
        yb_out[0, rs, :] = zh * (jnp.tanh(zh) + 1.0)
        yield

    _interleave(stream, n_parts)

    xn_end, h_end = last_rows[n_parts - 1]
    if seg_len:
        xn_tile = last_rows['xn']
        for b in range(n_seq):
            conv_out[b] = up_ext[(b + 1) * stride - CONV_CARRY:(b + 1) * stride, :]
            shift_out[b] = xn_tile[(b + 1) * seg_len - 1:(b + 1) * seg_len]
    else:
        tail = up_ext[pl.ds(tm + CARRY_PAD - CONV_CARRY, CONV_CARRY), :]
        up_ext[CARRY_PAD - CONV_CARRY:CARRY_PAD, :] = tail
        conv_out[0] = tail
        xn_last[...] = xn_end
        h_last[...] = h_end
        shift_out[0] = xn_end


def _mix_in(x, shift0, conv0, wd, tm, n_parts, whole_sequences):
    bsz, t, d = x.shape
    c = wd['w0'].shape[1]
    weights = [wd[n] for n in ('norm_mix', 'w_in', 'mu_rkv', 'w0', 'a0', 'w_lora1', 'w_lora2', 'g2',
                               'k_k', 'k_a', 'r_k', 'seg', 'dw_w', 'dw_b', 'cln_w', 'cln_b')]
    d_w, d_a = wd['lora_dims']
    if whole_sequences:
        tm, seg_len, n_seq, rows_b = bsz * t, t, bsz, 1
        assert t & (t - 1) == 0 and n_parts == 1
        x = x.reshape(1, tm, d)
        shift0 = jnp.repeat(shift0.reshape(bsz, d), t, axis=0).reshape(1, tm, d)
        grid = (1, 1)
        state = lambda rows, w: pl.BlockSpec((n_seq, rows, w), lambda b, i: (0, 0, 0))
        shift_in = pl.BlockSpec((1, tm, d), lambda b, i: (0, 0, 0))
        window_rows = n_seq * (CARRY_PAD + seg_len)
    else:
        seg_len, rows_b = 0, bsz
        grid = (bsz, t // tm)
        state = lambda rows, w: pl.BlockSpec((1, rows, w), lambda b, i: (b, 0, 0))
        shift_in = state(1, d)
        window_rows = tm + CARRY_PAD
    tok = lambda w: pl.BlockSpec((1, tm, w), lambda b, i: (b, i, 0))
    out_tok = jax.ShapeDtypeStruct((rows_b, x.shape[1], c), F32)
    outs = pl.pallas_call(
        functools.partial(_mix_in_kernel, tm=tm, c=c, n_parts=n_parts, d_w=d_w, d_a=d_a, seg_len=seg_len),
        grid=grid,
        in_specs=[tok(d), shift_in, state(CONV_CARRY, c)] + [_full(w.shape) for w in weights],
        out_specs=[tok(c)] * 9 + [state(1, d), state(CONV_CARRY, c)],
        out_shape=[out_tok] * 9 + [jax.ShapeDtypeStruct((bsz, 1, d), F32),
                                   jax.ShapeDtypeStruct((bsz, CONV_CARRY, c), F32)],
        scratch_shapes=[pltpu.VMEM((1, d), F32), pltpu.VMEM((1, 3 * c), F32),
                        pltpu.VMEM((window_rows, c), F32),
                        pltpu.VMEM((n_parts, SUBLANES, (window_rows - CARRY_PAD) // n_parts + CARRY_PAD - SUBLANES, c),
                                   F32)],
        compiler_params=pltpu.CompilerParams(dimension_semantics=("arbitrary", "arbitrary"),
                                             vmem_limit_bytes=VMEM_LIMIT),
        name="mix_in",
    )(x, shift0, conv0, *weights)
    return [o.reshape(bsz, t, c) for o in outs[:9]] + list(outs[9:])


def _tri_inverse(n_strict, row, col, lg_chunk):
    lg_base = INV_BASE.bit_length() - 1
    same = lambda sh: (row >> sh) == (col >> sh)
    lg0 = min(lg_base, lg_chunk)
    n = row.shape[0]

    def expand(c, lg):
        return jnp.where(same(lg), jnp.concatenate([c] * (n >> lg), axis=0), 0.0).astype(BF16)

    def fold(x, lg):
        b = 1 << lg
        return functools.reduce(lambda u, v: u + v, [x[i:i + b] for i in range(0, n, b)])

    b0 = 1 << lg0
    row_c = lax.broadcasted_iota(I32, (b0, n), 0)
    col_c = lax.broadcasted_iota(I32, (b0, n), 1)
    p_full = [jnp.where(same(lg0), x, 0.0) for x in n_strict]
    p = [fold(x, lg0) for x in p_full]
    t = [jnp.where((col_c & (b0 - 1)) == row_c, 1.0, 0.0) + x for x in p]
    p_full = [x.astype(BF16) for x in p_full]
    for _ in range(lg0 - 1):
        p = [_dg(x.astype(BF16), y) for x, y in zip(p, p_full)]
        p_full = [expand(x, lg0) for x in p]
        t = [x + _dg(x.astype(BF16), y) for x, y in zip(t, p_full)]
    for lg in range(lg0, lg_chunk):
        off_mask = same(lg + 1) & jnp.logical_not(same(lg))
        t_full = [expand(x, lg) for x in t]
        u = [_dg(x.astype(BF16), jnp.where(off_mask, m, 0.0).astype(BF16)) for x, m in zip(t, n_strict)]
        add = [_dg(x.astype(BF16), y) for x, y in zip(u, t_full)]
        even = ((lax.broadcasted_iota(I32, (1 << lg, n), 1) >> lg) & 1) == 0
        t = [jnp.concatenate([jnp.where(even, x, 0.0), jnp.where(even, a, x)], axis=0) for x, a in zip(t, add)]
    return [expand(x, lg_chunk) for x in t]


def _wkv_kernel(r_ref, lw_ref, k_ref, v_ref, a_ref, b_ref, s0_ref, y_ref, s_out, s_scr,
                *, tt, n_sub, chunk, chained):
    ti = pl.program_id(1)
    n_heads = r_ref.shape[2] // HEAD_SIZE
    n_chunks = tt // chunk
    lg_chunk = chunk.bit_length() - 1
    row = lax.broadcasted_iota(I32, (tt, tt), 0)
    col = lax.broadcasted_iota(I32, (tt, tt), 1)
    in_chunk = (row >> lg_chunk) == (col >> lg_chunk)
    tri_incl = in_chunk & (col <= row)
    tri_strict = in_chunk & (col < row)
    m_cum = jnp.where(tri_incl, 1.0, 0.0).astype(BF16)
    row_h = lax.broadcasted_iota(I32, (HEAD_SIZE, HEAD_SIZE), 0)
    col_h = lax.broadcasted_iota(I32, (HEAD_SIZE, HEAD_SIZE), 1)
    eye_h = row_h == col_h

    if chained:
        @pl.when(ti == 0)
        def _():
            s_scr[...] = s0_ref[0]

    rt_all, g_end_all, cols = [], [], {name: [] for name in ('v', 'rt', 'at', 'kt', 'bt', 'bd', 'kd')}
    hsl = [slice(HEAD_SIZE * h, HEAD_SIZE * (h + 1)) for h in range(n_heads)]
    for sub in range(n_sub):
        rows = pl.ds(sub * tt, tt)
        lw_all = lw_ref[0, rows, :]
        k_all = k_ref[0, rows, :]
        b_all = b_ref[0, rows, :]
        cum = _mask_dot(m_cum, lw_all)
        tot = jnp.concatenate([jnp.broadcast_to(cum[(ci + 1) * chunk - 1:(ci + 1) * chunk], (chunk, cum.shape[1]))
                               for ci in range(n_chunks)], axis=0)
        e_neg = jnp.exp(-cum)
        e_end = jnp.exp(tot - cum)
        full = dict(v=v_ref[0, rows, :], rt=r_ref[0, rows, :] * jnp.exp(cum),
                    at=a_ref[0, rows, :] * jnp.exp(cum - lw_all), kt=k_all * e_neg, bt=b_all * e_neg,
                    bd=b_all * e_end, kd=k_all * e_end)
        rt_all.append(full['rt'])
        g_end_all.append(jnp.exp(tot))
        for name, z in full.items():
            cols[name] += [z[:, s_].astype(BF16) for s_ in hsl]
    v, rt, at, kt, bt, bd, kd = (cols[name] for name in ('v', 'rt', 'at', 'kt', 'bt', 'bd', 'kd'))

    units = range(n_sub * n_heads)
    mm = [_dg(jnp.concatenate([at[u], rt[u]], axis=0), jnp.concatenate([bt[u], kt[u]], axis=0), NT) for u in units]
    m_ab = [jnp.where(tri_strict, mm[u][:tt, :tt], 0.0) for u in units]
    m_ak = [jnp.where(tri_strict, mm[u][:tt, tt:], 0.0).astype(BF16) for u in units]
    m_rb = [jnp.where(tri_incl, mm[u][tt:, :tt], 0.0).astype(BF16) for u in units]
    m_rk = [jnp.where(tri_incl, mm[u][tt:, tt:], 0.0).astype(BF16) for u in units]
    tinv = _tri_inverse(m_ab, row, col, lg_chunk)
    akv = [_dg(m_ak[u], v[u]).astype(BF16) for u in units]
    w1 = [_dg(tinv[u], at[u]).astype(BF16) for u in units]
    w2 = [_dg(tinv[u], akv[u]).astype(BF16) for u in units]
    q = [(rt_all[u // n_heads][:, hsl[u % n_heads]] + _dg(m_rb[u], w1[u])).astype(BF16) for u in units]
    y0 = [_dg(m_rb[u], w2[u]) + _dg(m_rk[u], v[u]) for u in units]

    heads = range(n_heads)
    if chained:
        s = [s_scr[h] for h in heads]
    for sub in range(n_sub):
        for ci in range(n_chunks):
            cs = slice(ci * chunk, (ci + 1) * chunk)
            seq = sub * n_chunks + ci
            if not chained:
                s = [s0_ref[seq, h] for h in heads]
            g_row = g_end_all[sub][ci * chunk:ci * chunk + 1]
            un = [sub * n_heads + h for h in heads]
            gm = [jnp.where(eye_h, jnp.broadcast_to(g_row[:, hsl[h]], (HEAD_SIZE, HEAD_SIZE)), 0.0)
                  + _dg(bd[un[h]][cs], w1[un[h]][cs], TN) for h in heads]
            hm = [_dg(bd[un[h]][cs], w2[un[h]][cs], TN) + _dg(kd[un[h]][cs], v[un[h]][cs], TN) for h in heads]
            for h in heads:
                y_ref[0, pl.ds(sub * tt + ci * chunk, chunk), hsl[h]] = _bdot(q[un[h]][cs], s[h]) + y0[un[h]][cs]
            s = [_bdot(gm[h], s[h]) + hm[h] for h in heads]
            if not chained:
                for h in heads:
                    s_out[seq, h] = s[h]
    if chained:
        for h in heads:
            s_scr[h] = s[h]
            s_out[0, h] = s[h]


def _wkv(r, lw, k, v, a, b, s0t, tt, n_sub, chunk, chained):
    bsz, t, c = r.shape
    n_heads = c // HEAD_SIZE
    hs = HEAD_SIZE
    step = tt * n_sub
    if chained:
        grid = (bsz, t // step)
        tok = pl.BlockSpec((1, step, c), lambda bi, ti: (bi, ti, 0))
        st = pl.BlockSpec((1, n_heads, hs, hs), lambda bi, ti: (bi, 0, 0, 0))
        args = (r, lw, k, v, a, b)
        y_shape = (bsz, t, c)
    else:
        assert t == chunk and (bsz * t) % step == 0
        n_seq = step // chunk
        grid = (1, bsz * t // step)
        tok = pl.BlockSpec((1, step, c), lambda bi, ti: (0, ti, 0))
        st = pl.BlockSpec((n_seq, n_heads, hs, hs), lambda bi, ti: (ti, 0, 0, 0))
        args = tuple(z.reshape(1, bsz * t, c) for z in (r, lw, k, v, a, b))
        y_shape = (1, bsz * t, c)
    y, s_new = pl.pallas_call(
        functools.partial(_wkv_kernel, tt=tt, n_sub=n_sub, chunk=chunk, chained=chained),
        grid=grid,
        in_specs=[tok] * 6 + [st],
        out_specs=[tok, st],
        out_shape=[jax.ShapeDtypeStruct(y_shape, F32), jax.ShapeDtypeStruct(s0t.shape, F32)],
        scratch_shapes=[pltpu.VMEM((n_heads, hs, hs), F32)],
        compiler_params=pltpu.CompilerParams(dimension_semantics=("arbitrary",) * 2,
                                             vmem_limit_bytes=VMEM_LIMIT),
        name="wkv",
    )(*args, s0t)
    return y.reshape(bsz, t, c), s_new


def _interleave(make_stream, n_parts):
    for _ in zip(*[make_stream(part) for part in range(n_parts)]):
        pass


def _mix_out_kernel(x_ref, y_ref, bonus_ref, g_ref, yb_ref, lnw_ref, lnb_ref, seg_ref, wout_ref, nffn_ref,
                    wr_ref, h1_out, xn_out, eid_out, eidt_out, gate_out, *, c, n_parts):
    rows_per = x_ref.shape[0] // n_parts

    def stream(part):
        rs = pl.ds(part * rows_per, rows_per)
        seg = seg_ref[...]
        y = y_ref[rs, :]
        inv_n = 1.0 / HEAD_SIZE
        mu = _seg_sum(y, seg) * inv_n
        yield
        yc = y - mu
        var = _seg_sum(yc * yc, seg) * inv_n
        yield
        yn = yc * lax.rsqrt(var + GN_EPS) * lnw_ref[...] + lnb_ref[...]
        ya = (yn + bonus_ref[rs, :]) * g_ref[rs, :]
        mix = _dg(ya.astype(BF16), wout_ref[:c, :]) + _dg(yb_ref[rs, :].astype(BF16), wout_ref[c:, :])
        yield
        h1 = x_ref[rs, :] + mix
        h1_out[rs, :] = h1
        xn = _rms(h1, nffn_ref[...])
        xn_out[rs, :] = xn.astype(BF16)
        xh, xl = _split2(xn)
        hi_lo = _dg(xh, wr_ref[...])
        logits = hi_lo[:, :ROUTER_LANES] + (hi_lo[:, ROUTER_LANES:] + _dg(xl, wr_ref[:, :ROUTER_LANES]))
        yield
        lane = lax.broadcasted_iota(I32, logits.shape, 1)
        neg = jnp.float32(-jnp.inf)
        is_g = (lane >= N_EXPERTS) & (lane < N_EXPERTS + N_EXPERT_GROUPS)
        glog = jnp.where(is_g, logits, neg)
        gmax = jnp.max(glog, axis=-1, keepdims=True)
        gsel = jnp.min(jnp.where(glog == gmax, lane, 4 * ROUTER_LANES), axis=-1, keepdims=True) - N_EXPERTS
        gp = 1.0 / jnp.sum(jnp.where(is_g, jnp.exp(glog - gmax), 0.0), axis=-1, keepdims=True)
        in_grp = (lane >= gsel * EXPERTS_PER_GROUP) & (lane < (gsel + 1) * EXPERTS_PER_GROUP)
        elog = jnp.where(in_grp, logits, neg)
        emax = jnp.max(elog, axis=-1, keepdims=True)
        ex = jnp.where(in_grp, jnp.exp(elog - emax), 0.0)
        eprob = ex / jnp.sum(ex, axis=-1, keepdims=True)
        eprob = jnp.where(in_grp, eprob, -1.0)
        yield
        v1 = jnp.max(eprob, axis=-1, keepdims=True)
        i1 = jnp.min(jnp.where(eprob == v1, lane, 4 * ROUTER_LANES), axis=-1, keepdims=True)
        rest = jnp.where(lane == i1, -1.0, eprob)
        v2 = jnp.max(rest, axis=-1, keepdims=True)
        i2 = jnp.min(jnp.where(rest == v2, lane, 4 * ROUTER_LANES), axis=-1, keepdims=True)
        denom = v1 + v2
        eid = jnp.where(lane == 0, i1, jnp.where(lane == 1, i2, 0))
        eid_out[rs, :] = eid
        eidt_out[:, rs] = jnp.transpose(eid)[:SUBLANES]
        gate_out[rs, :] =jnp.where(lane == 0, gp * v1 / denom, jnp.where(lane == 1, gp * v2 / denom, 0.0))
        yield

    _interleave(stream, n_parts)


def _mix_out(x2, y2, bonus2, g2, yb2, wd, tm, n_parts):
    m, d = x2.shape
    c = y2.shape[1]
    tokd = pl.BlockSpec((tm, d), lambda i: (i, 0))
    tokc = pl.BlockSpec((tm, c), lambda i: (i, 0))
    tokr = pl.BlockSpec((tm, ROUTER_LANES), lambda i: (i, 0))
    weights = [wd[n] for n in ('ln_x_w', 'ln_x_b', 'seg', 'w_out', 'norm_ffn', 'w_router')]
    return pl.pallas_call(
        functools.partial(_mix_out_kernel, c=c, n_parts=n_parts),
        grid=(m // tm,),
        in_specs=[tokd, tokc, tokc, tokc, tokc] + [_full(w.shape) for w in weights],
        out_specs=[tokd, tokd, tokr, pl.BlockSpec((SUBLANES, tm), lambda i: (0, i)), tokr],
        out_shape=[jax.ShapeDtypeStruct((m, d), F32), jax.ShapeDtypeStruct((m, d), BF16),
                   jax.ShapeDtypeStruct((m, ROUTER_LANES), I32), jax.ShapeDtypeStruct((SUBLANES, m), I32),
                   jax.ShapeDtypeStruct((m, ROUTER_LANES), F32)],
        compiler_params=pltpu.CompilerParams(dimension_semantics=("arbitrary",), vmem_limit_bytes=VMEM_LIMIT),
        name="mix_out",
    )(x2, y2, bonus2, g2, yb2, *weights)


def _pow2_pieces(count, max_rows, fn):
    off = 0
    rows = max_rows
    while rows >= RUN_ALIGN:
        has = (count & (rows // RUN_ALIGN)) != 0

        @pl.when(has)
        def _(off=off, rows=rows):
            fn(off, rows)
        off = off + jnp.where(has, rows, 0)
        rows //= 2


def _for_each_expert(fn):
    def body(e, carry):
        fn(e)
        return carry
    lax.fori_loop(0, N_EXPERTS, body, 0)


def _pow2_floor(n):
    return 1 << (n.bit_length() - 1)


def _dispatch_kernel(nch_ref, tot_ref, off_ref, loff_ref, tn_ref, toff_ref, nu_ref, *refs, groups, bm, n_blocks):
    n_g = len(groups)
    eid_refs, xn_refs = refs[:n_g], refs[n_g + 1:2 * n_g + 1]
    loffc_ref = refs[n_g]
    xs_out, buf, sem, zbuf, zsem = refs[2 * n_g + 1:]
    i = pl.program_id(0)
    last = pl.num_programs(0) - 1
    max_run = max(tm for tm, _ in groups)

    def start_tile(tile):
        def per_expert(e):
            src0 = loff_ref[tile * N_EXPERTS + e]
            dst0 = off_ref[tile * N_EXPERTS + e]

            def piece(o, rows):
                src = buf.at[tile % 2, pl.ds(pl.multiple_of(src0 + o, RUN_ALIGN), rows)]
                dst = xs_out.at[pl.ds(pl.multiple_of(dst0 + o, RUN_ALIGN), rows)]
                pltpu.make_async_copy(src, dst, sem.at[tile % 2]).start()
            _pow2_pieces(nch_ref[tile * N_EXPERTS + e], max_run, piece)
        _for_each_expert(per_expert)

    def wait_tile(tile):
        def piece(o, rows):
            pltpu.make_async_copy(buf.at[tile % 2, pl.ds(0, rows)], xs_out.at[pl.ds(0, rows)], sem.at[tile % 2]).wait()
        _pow2_pieces(tot_ref[tile], _pow2_floor(buf.shape[1]), piece)

    @pl.when(i >= 2)
    def _():
        wait_tile(i - 2)

    def sort_tile(eid_ref, xn_ref, tm):
        e_rows = eid_ref[...]
        sub = lax.broadcasted_iota(I32, (N_EXPERTS, tm), 0)
        e1 = jnp.where(sub == e_rows[0:1], 1.0, 0.0)
        e2 = jnp.where(sub == e_rows[1:2], 1.0, 0.0)
        before = lax.broadcasted_iota(I32, (tm, tm), 0) < lax.broadcasted_iota(I32, (tm, tm), 1)
        slot = _dg((e1 + e2).astype(BF16), jnp.where(before, 1.0, 0.0).astype(BF16)) + loffc_ref[0]
        l1 = jnp.sum(slot * e1, axis=0, keepdims=True).astype(I32)
        l2 = jnp.sum(slot * e2, axis=0, keepdims=True).astype(I32)
        n_rows = 2 * tm + LOCAL_PAD
        rows = lax.broadcasted_iota(I32, (n_rows, tm), 0)
        perm = jnp.where((rows == l1) | (rows == l2), 1.0, 0.0).astype(BF16)
        buf[i % 2, 0:n_rows, :] = _dg(perm, xn_ref[...]).astype(BF16)

    first = 0
    for g, (tm, n_tiles) in enumerate(groups):
        pl.when((i >= first) & (i < first + n_tiles))(functools.partial(sort_tile, eid_refs[g], xn_refs[g], tm))
        first += n_tiles
    start_tile(i)

    @pl.when(i == 0)
    def _():
        zbuf[...] = jnp.zeros_like(zbuf)
        half = zbuf.shape[0]

        def zero_copy(off, rows):
            return pltpu.make_async_copy(zbuf.at[pl.ds(0, rows)], xs_out.at[pl.ds(off, rows)], zsem)

        def tails(fn):
            _for_each_expert(lambda e: _pow2_pieces(
                tn_ref[e], half, lambda o, rows: fn(pl.multiple_of(toff_ref[e] + o, RUN_ALIGN), rows)))

        def unused_blocks(fn):
            def body(b, carry):
                fn(pl.multiple_of(b * bm, bm), half)
                fn(pl.multiple_of(b * bm + half, half), half)
                return carry
            lax.fori_loop(nu_ref[0], n_blocks, body, 0)

        tails(lambda off, rows: zero_copy(off, rows).start())
        unused_blocks(lambda off, rows: zero_copy(off, rows).start())
        tails(lambda off, rows: zero_copy(off, rows).wait())
        unused_blocks(lambda off, rows: zero_copy(off, rows).wait())

    @pl.when(i == last)
    def _():
        @pl.when(i >= 1)
        def _():
            wait_tile(i - 1)
        wait_tile(i)


def _dispatch(plan, n_used, eids_t, xns, tms, n_blocks, bm):
    d = xns[0].shape[1]
    groups = tuple((tm, xn.shape[0] // tm) for xn, tm in zip(xns, tms))
    firsts = [sum(n for _, n in groups[:g]) for g in range(len(groups))]
    lbuf = 2 * max(tms) + LOCAL_PAD

    def tile_of(g):
        return lambda i: jnp.clip(i - firsts[g], 0, groups[g][1] - 1)

    imap = lambda f: (lambda i, *_: f(i))
    in_specs = ([pl.BlockSpec((SUBLANES, tm), imap(lambda i, g=g: (0, tile_of(g)(i)))) for g, tm in enumerate(tms)] +
                [pl.BlockSpec((1, N_EXPERTS, 1), imap(lambda i: (i, 0, 0)))] +
                [pl.BlockSpec((tm, d), imap(lambda i, g=g: (tile_of(g)(i), 0))) for g, tm in enumerate(tms)])
    return pl.pallas_call(
        functools.partial(_dispatch_kernel, groups=groups, bm=bm, n_blocks=n_blocks),
        grid_spec=pltpu.PrefetchScalarGridSpec(
            num_scalar_prefetch=7,
            grid=(sum(n for _, n in groups),),
            in_specs=in_specs,
            out_specs=pl.BlockSpec(memory_space=pl.ANY),
            scratch_shapes=[pltpu.VMEM((2, lbuf, d), BF16), pltpu.SemaphoreType.DMA((2,)),
                            pltpu.VMEM((bm // 2, d), BF16), pltpu.SemaphoreType.DMA],
        ),
        out_shape=jax.ShapeDtypeStruct((n_blocks * bm, d), BF16),
        compiler_params=pltpu.CompilerParams(dimension_semantics=("arbitrary",), vmem_limit_bytes=VMEM_LIMIT),
        name="moe_dispatch",
    )(plan['nch'], plan['tot'], plan['off'], plan['loff'], plan['tail_n'], plan['tail_off'], n_used, *eids_t,
      plan['loff_col'], *xns)


def _experts_kernel(be_ref, slot_ref, nxt_ref, nu_ref, xs_ref, wg_hbm, wu_hbm, wd_hbm, yb_ref,
                    wg_f, wu_f, wd_f, wg_b, wu_b, wd_b, sem):
    b = pl.program_id(0)

    def weight_copies(e, slot):
        pairs = ((wg_hbm, wg_f), (wu_hbm, wu_f), (wd_hbm, wd_f))
        return [pltpu.make_async_copy(src.at[e], dst.at[slot], sem.at[slot, j]) for j, (src, dst) in enumerate(pairs)]

    @pl.when(b == 0)
    def _():
        for cp in weight_copies(be_ref[0], slot_ref[0]):
            cp.start()

    @pl.when((b < nu_ref[0]) & ((b == 0) | (be_ref[b] != be_ref[jnp.maximum(b - 1, 0)])))
    def _():
        slot = slot_ref[b]
        for cp in weight_copies(be_ref[b], slot):
            cp.wait()

        @pl.when(nxt_ref[b] >= 0)
        def _():
            for cp in weight_copies(nxt_ref[b], 1 - slot):
                cp.start()
        wg_b[...] = wg_f[slot].astype(BF16)
        wu_b[...] = wu_f[slot].astype(BF16)
        wd_b[...] = wd_f[slot].astype(BF16)

    @pl.when(b < nu_ref[0])
    def _():
        xb = xs_ref[...]
        hg = _dg(xb, wg_b[...])
        hu = _dg(xb, wu_b[...])
        act = (hg * _sigmoid(hg) * hu).astype(BF16)
        yb_ref[...] = _dg(act, wd_b[...]).astype(BF16)

    @pl.when(pl.program_id(0) >= nu_ref[0])
    def _():
        yb_ref[...] = jnp.zeros_like(yb_ref)


def _experts(sched, n_used, xs, wg, wu, wdn, bm):
    p, d = xs.shape
    ff = wg.shape[2]
    n_blocks = p // bm
    return pl.pallas_call(
        _experts_kernel,
        grid_spec=pltpu.PrefetchScalarGridSpec(
            num_scalar_prefetch=4,
            grid=(n_blocks,),
            in_specs=[pl.BlockSpec((bm, d), lambda b, be, sl, nx, nu: (jnp.minimum(b, nu[0] - 1), 0)),
                      pl.BlockSpec(memory_space=pl.ANY), pl.BlockSpec(memory_space=pl.ANY),
                      pl.BlockSpec(memory_space=pl.ANY)],
            out_specs=pl.BlockSpec((bm, d), lambda b, *_: (b, 0)),
            scratch_shapes=[pltpu.VMEM((2, d, ff), F32), pltpu.VMEM((2, d, ff), F32), pltpu.VMEM((2, ff, d), F32),
                            pltpu.VMEM((d, ff), BF16), pltpu.VMEM((d, ff), BF16), pltpu.VMEM((ff, d), BF16),
                            pltpu.SemaphoreType.DMA((2, 3))],
        ),
        out_shape=jax.ShapeDtypeStruct((p, d), BF16),
        compiler_params=pltpu.CompilerParams(dimension_semantics=("arbitrary",), vmem_limit_bytes=VMEM_LIMIT),
        name="moe_experts",
    )(sched['expert'], sched['slot'], sched['next'], n_used, xs, wg, wu, wdn)


def _final_kernel(nch_ref, tot_ref, off_ref, loff_ref, h1_ref, eid_ref, gate_ref, loffr_ref, p_ref, yb_hbm,
                  nple_ref, wpg_ref, wpp_ref, nfin_ref, y_out, buf, sem, *, tm, lbuf, tile0, n_parts):
    i = pl.program_id(0)

    def fetch(tile):
        base = (tile0 + tile) * N_EXPERTS

        def per_expert(e):
            src0 = off_ref[base + e]
            dst0 = loff_ref[base + e]

            def piece(o, rows):
                src = yb_hbm.at[pl.ds(pl.multiple_of(src0 + o, RUN_ALIGN), rows)]
                dst = buf.at[tile % 2, pl.ds(pl.multiple_of(dst0 + o, RUN_ALIGN), rows)]
                pltpu.make_async_copy(src, dst, sem.at[tile % 2]).start()
            _pow2_pieces(nch_ref[base + e], tm, piece)
        _for_each_expert(per_expert)

    def wait_fetch(tile):
        def piece(o, rows):
            pltpu.make_async_copy(yb_hbm.at[pl.ds(0, rows)], buf.at[tile % 2, pl.ds(0, rows)], sem.at[tile % 2]).wait()
        _pow2_pieces(tot_ref[tile0 + tile], _pow2_floor(lbuf), piece)

    @pl.when(i == 0)
    def _():
        buf[...] = jnp.zeros_like(buf)
        fetch(i)

    @pl.when(i + 1 < pl.num_programs(0))
    def _():
        fetch(i + 1)

    eid = eid_ref[...]
    lane = lax.broadcasted_iota(I32, (tm, N_EXPERTS), 1)
    e12 = (jnp.where(lane == eid[:, 0:1], 1.0, 0.0) + jnp.where(lane == eid[:, 1:2], 1.0, 0.0)).astype(BF16)
    rows_per = tm // n_parts
    picks = []
    for part in range(n_parts):
        rs = pl.ds(part * rows_per, rows_per)
        lane_p = lax.broadcasted_iota(I32, (rows_per, N_EXPERTS), 1)
        eid_p = eid_ref[rs, :]
        e1 = jnp.where(lane_p == eid_p[:, 0:1], 1.0, 0.0)
        e2 = jnp.where(lane_p == eid_p[:, 1:2], 1.0, 0.0)
        before = (lax.broadcasted_iota(I32, (rows_per, tm), 1)
                  < lax.broadcasted_iota(I32, (rows_per, tm), 0) + part * rows_per)
        slot = _dg(jnp.where(before, 1.0, 0.0).astype(BF16), e12) + loffr_ref[0]
        l1 = jnp.sum(slot * e1, axis=1, keepdims=True).astype(I32)
        l2 = jnp.sum(slot * e2, axis=1, keepdims=True).astype(I32)
        cols = lax.broadcasted_iota(I32, (rows_per, lbuf), 1)
        gate = gate_ref[rs, :]
        picks.append(jnp.where(cols == l1, gate[:, 0:1], jnp.where(cols == l2, gate[:, 1:2], 0.0)).astype(BF16))

    wait_fetch(i)
    sorted_rows = buf[i % 2]

    def stream(part):
        rs = pl.ds(part * rows_per, rows_per)
        h2 = h1_ref[rs, :] + _dg(picks[part], sorted_rows)
        yield
        gate_in = _rms(h2, nple_ref[...]).astype(BF16)
        pg = _sigmoid(_dg(gate_in, wpg_ref[...]))
        yield
        h3 = h2 + pg * _dg(p_ref[rs, :].astype(BF16), wpp_ref[...])
        y_out[rs, :] = _rms(h3, nfin_ref[...])
        yield

    _interleave(stream, n_parts)


def _final(plan, h1, eid, gate, p2, yb, wd, tm, tile0):
    m, d = h1.shape
    pd = p2.shape[1]
    lbuf = 2 * tm + LOCAL_PAD
    weights = [wd[n] for n in ('norm_ple', 'w_ple_gate', 'w_ple_proj', 'norm_final')]
    imap = lambda f: (lambda i, *_: f(i))
    return pl.pallas_call(
        functools.partial(_final_kernel, tm=tm, lbuf=lbuf, tile0=tile0, n_parts=2),
        grid_spec=pltpu.PrefetchScalarGridSpec(
            num_scalar_prefetch=4,
            grid=(m // tm,),
            in_specs=[pl.BlockSpec((tm, d), imap(lambda i: (i, 0))),
                      pl.BlockSpec((tm, ROUTER_LANES), imap(lambda i: (i, 0))),
                      pl.BlockSpec((tm, ROUTER_LANES), imap(lambda i: (i, 0))),
                      pl.BlockSpec((1, 1, N_EXPERTS), imap(lambda i: (tile0 + i, 0, 0))),
                      pl.BlockSpec((tm, pd), imap(lambda i: (i, 0))),
                      pl.BlockSpec(memory_space=pl.ANY)] +
                     [pl.BlockSpec(w.shape, imap(lambda i, n=len(w.shape): (0,) * n)) for w in weights],
            out_specs=pl.BlockSpec((tm, d), imap(lambda i: (i, 0))),
            scratch_shapes=[pltpu.VMEM((2, lbuf, d), BF16), pltpu.SemaphoreType.DMA((2,))],
        ),
        out_shape=jax.ShapeDtypeStruct((m, d), F32),
        compiler_params=pltpu.CompilerParams(dimension_semantics=("arbitrary",), vmem_limit_bytes=VMEM_LIMIT),
        name="moe_final",
    )(plan['nch'], plan['tot'], plan['off'], plan['loff'], h1, eid, gate, plan['loff_row'], p2, yb, *weights)


def _route_plan(eids, tms, bm):
    experts = jnp.arange(N_EXPERTS, dtype=I32)
    counts = []
    for eid, tm in zip(eids, tms):
        onehot = (eid[:2, :, None] == experts).astype(I32)
        counts.append(onehot.reshape(2, -1, tm, N_EXPERTS).sum(axis=(0, 2)))
    n = jnp.concatenate(counts)
    n_al = (n + RUN_ALIGN - 1) // RUN_ALIGN * RUN_ALIGN
    loff = jnp.cumsum(n_al, axis=1) - n_al
    used = n_al.sum(axis=0)
    region = (used + bm - 1) // bm * bm
    pend = jnp.cumsum(region)
    off = (pend - region)[None, :] + jnp.cumsum(n_al, axis=0) - n_al
    n_assign = sum(2 * e.shape[1] for e in eids)
    n_blocks = -(-(n_assign + (RUN_ALIGN - 1) * N_EXPERTS * n.shape[0] + N_EXPERTS * (bm - 1)) // bm)
    block_start = jnp.arange(n_blocks, dtype=I32) * bm
    block_expert = jnp.minimum(jnp.sum((pend[None, :] <= block_start[:, None]).astype(I32), axis=1), N_EXPERTS - 1)
    plan = dict(nch=(n_al // RUN_ALIGN).reshape(-1).astype(I32), off=off.reshape(-1).astype(I32),
                tot=(n_al.sum(axis=1) // RUN_ALIGN).astype(I32),
                loff=loff.reshape(-1).astype(I32), tail_n=((region - used) // RUN_ALIGN).astype(I32),
                tail_off=(pend - region + used).astype(I32), loff_col=loff.astype(F32)[:, :, None],
                loff_row=loff.astype(F32)[:, None, :])
    n_used = pend[-1] // bm
    prev = jnp.concatenate([jnp.full((1,), -1, I32), block_expert[:-1].astype(I32)])
    slot = (jnp.cumsum((block_expert != prev).astype(I32)) - 1) % 2
    after = pend[block_expert] // bm
    nxt = jnp.where(after < n_used, block_expert[jnp.minimum(after, n_blocks - 1)], -1)
    sched = dict(expert=block_expert.astype(I32), slot=slot.astype(I32), next=nxt.astype(I32))
    return plan, sched, n_used.astype(I32).reshape(1), n_blocks


def _layer_front(x, shift0, wkv0, conv0, wd, tm_in, tm_tok, wkv_tile, wkv_sub, wkv_chunk, chained):
    bsz, t, d = x.shape
    r, lw, k2, v, ah, bh, g, bonus, yb, shift_new, conv_new = _mix_in(x, shift0, conv0, wd, tm_in, 4 if tm_in >= 512 else 1, not chained)
    y, s_new = _wkv(r, lw, k2, v, ah, bh, jnp.swapaxes(wkv0, -1, -2), wkv_tile, wkv_sub, wkv_chunk, chained)
    flat = lambda z: z.reshape(bsz * t, z.shape[-1])
    h1, xn2, eid, eid_t, gate = _mix_out(flat(x), flat(y), flat(bonus), flat(g), flat(yb), wd, tm_tok, 2)
    return h1, xn2, (eid, eid_t), gate, shift_new.reshape(bsz, d), jnp.swapaxes(s_new, -1, -2), conv_new


def kernel(x_prompt, x_sample, state_shift, state_wkv, cache_conv, p_prompt, p_sample, norm_mix, w_in, mu_rkv, mu_w, mu_a, mu_g, w0, w1, w2, a0, a1, a2, g1, g2, k_k, k_a, r_k, ln_x_w, ln_x_b, dw_w, dw_b, cln_w, cln_b, w_out, norm_ffn, w_router_group, w_router_expert, w_exp_gate, w_exp_up, w_exp_down, norm_ple, w_ple_gate, w_ple_proj, norm_final):
    depth = norm_mix.shape[0]
    assert depth == 1
    d = x_prompt.shape[-1]
    c = w0.shape[-1]
    row = lambda z: z[0].reshape(1, -1).astype(F32)
    lane = jnp.arange(LANES, dtype=I32) // HEAD_SIZE
    w_router = jnp.concatenate([w_router_expert[0], w_router_group[0],
                                jnp.zeros((d, ROUTER_LANES - N_EXPERTS - N_EXPERT_GROUPS), F32)], axis=1)
    first = jnp.concatenate([w1[0], a1[0], g1[0]], axis=1)
    mixed = jnp.concatenate([mu_w[0][:, None] * w1[0], mu_a[0][:, None] * a1[0], mu_g[0][:, None] * g1[0]], axis=1)
    d_w, d_a = w1.shape[2], a1.shape[2]
    second = jnp.concatenate([jnp.concatenate([w2[0], jnp.zeros((d_w, c), F32)], axis=1),
                              jnp.concatenate([jnp.zeros((d_a, c), F32), a2[0]], axis=1)], axis=0)
    glu_half = jnp.concatenate([jnp.ones((3 * c,), F32), jnp.full((w_in.shape[2] - 3 * c,), 0.5, F32)])
    wd = dict(
        norm_mix=row(norm_mix), w_in=(w_in[0] * glu_half).astype(BF16), mu_rkv=row(mu_rkv),
        w0=0.5 * row(w0), a0=0.5 * row(a0),
        w_lora1=jnp.concatenate([first, mixed], axis=0).astype(BF16), w_lora2=(0.5 * second).astype(BF16),
        g2=g2[0].astype(BF16), lora_dims=(d_w, d_a),
        k_k=row(k_k), k_a=row(k_a), r_k=row(r_k), ln_x_w=row(ln_x_w), ln_x_b=row(ln_x_b),
        seg=(lane[:, None] == lane[None, :]).astype(BF16),
        dw_w=dw_w[0].astype(F32), dw_b=row(dw_b), cln_w=0.5 * row(cln_w), cln_b=0.5 * row(cln_b),
        w_out=w_out[0].astype(BF16), norm_ffn=row(norm_ffn),
        w_router=jnp.concatenate(_split2(w_router), axis=1),
        norm_ple=row(norm_ple), w_ple_gate=w_ple_gate[0].astype(BF16), w_ple_proj=w_ple_proj[0].astype(BF16),
        norm_final=norm_final.reshape(1, -1).astype(F32),
    )
    bp, tp, _ = x_prompt.shape
    bs, ts, _ = x_sample.shape
    mp, ms = bp * tp, bs * ts
    tm_p = min(512, tp)
    tm_s = min(256, ms)

    zeros = lambda *s: jnp.zeros(s, F32)
    h1_p, xn_p, eid_p, gate_p, shift_p, wkv_p, conv_p = _layer_front(
        x_prompt, zeros(bp, 1, d), zeros(bp, c // HEAD_SIZE, HEAD_SIZE, HEAD_SIZE), zeros(bp, CONV_CARRY, c),
        wd, tm_p, min(512, mp), min(128, tp), 2 if tp >= 256 else 1, min(64, tp), True)
    h1_s, xn_s, eid_s, gate_s, shift_s, wkv_s, conv_s = _layer_front(
        x_sample, state_shift[0][:, None, :], state_wkv[0], cache_conv[0],
        wd, ts, tm_s, min(128, ms), 2 if ms >= 256 else 1, ts, False)

    bm = 512
    tr_p, tr_s = min(512, mp), min(512, ms)
    plan, sched, n_used, n_blocks = _route_plan([eid_p[1], eid_s[1]], [tr_p, tr_s], bm)
    tiles_p = mp // tr_p
    xs = _dispatch(plan, n_used, [eid_p[1], eid_s[1]], [xn_p, xn_s], [tr_p, tr_s], n_blocks, bm)
    yb = _experts(sched, n_used, xs, w_exp_gate[0], w_exp_up[0], w_exp_down[0], bm)
    y_p = _final(plan, h1_p, eid_p[0], gate_p, p_prompt[0].reshape(mp, -1), yb, wd, tr_p, 0)
    y_s = _final(plan, h1_s, eid_s[0], gate_s, p_sample[0].reshape(ms, -1), yb, wd, tr_s, tiles_p)
    return (y_p.reshape(x_prompt.shape), y_s.reshape(x_sample.shape), shift_p[None], wkv_p[None], conv_p[None],
            shift_s[None], wkv_s[None], conv_s[None])
```

```python
import functools

import jax
import jax.numpy as jnp
from jax import lax
from jax.experimental import pallas as pl
from jax.experimental.pallas import tpu as pltpu

F32 = jnp.float32
BF16 = jnp.bfloat16
I32 = jnp.int32

HEAD_SIZE = 64
CONV_WIDTH = 31
CONV_CARRY = CONV_WIDTH - 1
SUBLANES = 8
LANES = 128
CARRY_PAD = 32
N_EXPERT_GROUPS = 4
EXPERTS_PER_GROUP = 8
N_EXPERTS = N_EXPERT_GROUPS * EXPERTS_PER_GROUP
ROUTER_LANES = 128
RMS_EPS = 1e-6
LN_EPS = 1e-5
GN_EPS = 64e-5
DECAY_SCALE = 0.6065306597126334
INV_BASE = 16
RUN_ALIGN = 16
LOCAL_PAD = N_EXPERTS * RUN_ALIGN
COMMON_PIECE = 4 * RUN_ALIGN
VMEM_LIMIT = 56 * 1024 * 1024

TOKEN_TILE = 512
MIX_IN_PARTS = 4
TOKEN_PARTS = 2
WKV_SUBTILE = 128
WKV_SUBTILES = 2
WKV_CHUNK = 64
EXPERT_BLOCK = 512

NN = ((1,), (0,))
NT = ((1,), (1,))
TN = ((0,), (0,))


def _dg(a, b, dims=NN):
    return lax.dot_general(a, b, (dims, ((), ())), preferred_element_type=F32)


def _split2(x):
    hi = x.astype(BF16)
    lo = (x - hi.astype(F32)).astype(BF16)
    return hi, lo


def _bdot(a, b, dims=NN):
    return _dg(a.astype(BF16), b.astype(BF16), dims)


def _mask_dot(mask_bf16, x):
    h, l = _split2(x)
    return _dg(mask_bf16, h) + _dg(mask_bf16, l)


def _seg_sum(x, seg_bf16):
    h, l = _split2(x)
    w = seg_bf16.shape[0]
    return jnp.concatenate([_dg(h[:, j:j + w], seg_bf16) + _dg(l[:, j:j + w], seg_bf16)
                            for j in range(0, x.shape[1], w)], axis=1)


def _rms(x, g):
    return x * lax.rsqrt(jnp.mean(x * x, axis=-1, keepdims=True) + RMS_EPS) * g


def _sigmoid(x):
    return 0.5 * jnp.tanh(0.5 * x) + 0.5


def _full(shape):
    n = len(shape)
    return pl.BlockSpec(shape, lambda *_: (0,) * n, pipeline_mode=pl.Buffered(1))


def _mix_in_kernel(x_ref, shift_ref, conv_ref, nm_ref, win_ref, murkv_ref, w0_ref, a0_ref, wl1_ref, wl2_ref, g2_ref,
                   kk_ref, ka_ref, rk_ref, seg_ref, dww_ref, dwb_ref, clnw_ref, clnb_ref,
                   r_out, lw_out, k_out, v_out, a_out, b_out, g_out, bonus_out, yb_out, shift_out, conv_out,
                   xn_last, h_last, up_ext, shifted, *, tm, c, n_parts, d_w, d_a, seg_len):
    i = pl.program_id(1)
    n_seq = tm // seg_len if seg_len else 1
    stride = CARRY_PAD + seg_len

    if seg_len:
        for b in range(n_seq):
            up_ext[b * stride:b * stride + CARRY_PAD - CONV_CARRY, :] = jnp.zeros((CARRY_PAD - CONV_CARRY, c), F32)
            up_ext[b * stride + CARRY_PAD - CONV_CARRY:b * stride + CARRY_PAD, :] = conv_ref[b]
    else:
        @pl.when(i == 0)
        def _():
            sp = shift_ref[0]
            xn_last[...] = sp
            sp8 = jnp.broadcast_to(sp, (8, sp.shape[1])).astype(BF16)
            h_last[...] = _dg(sp8, win_ref[:, :3 * c])[0:1]
            up_ext[CARRY_PAD - CONV_CARRY:CARRY_PAD, :] = conv_ref[0]

    rp = tm // n_parts
    last_rows = {}

    def stream(part):
        rs = pl.ds(part * rp, rp)
        xn = _rms(x_ref[0, rs, :], nm_ref[...])
        hin = _dg(xn.astype(BF16), win_ref[...])
        h_rkv = hin[:, :3 * c]
        last_rows[part] = (xn[rp - 1:rp], h_rkv[rp - 1:rp])
        last_rows['xn'] = xn
        yield
        if seg_len:
            xn_prev = shift_ref[0]
            h_prev = _dg(xn_prev.astype(BF16), win_ref[:, :3 * c])
            first = (lax.broadcasted_iota(I32, (rp, 1), 0) & (seg_len - 1)) == 0
        else:
            xn_prev, h_prev = (xn_last[...], h_last[...]) if part == 0 else last_rows[part - 1]
            first = lax.broadcasted_iota(I32, (rp, 1), 0) == 0
        dx = jnp.where(first, xn_prev, pltpu.roll(xn, 1, 0)) - xn
        hprev = jnp.where(first, h_prev, pltpu.roll(h_rkv, 1, 0))
        rkv = h_rkv + (hprev - h_rkv) * murkv_ref[...]
        r = rkv[:, :c]
        k = rkv[:, c:2 * c]
        v = rkv[:, 2 * c:]
        l1 = _dg(jnp.concatenate([xn.astype(BF16), dx.astype(BF16)], axis=1), wl1_ref[...])
        lane = lax.broadcasted_iota(I32, l1.shape, 1)
        act = jnp.where(lane < d_w, jnp.tanh(l1), jnp.where(lane < d_w + d_a, l1, _sigmoid(l1))).astype(BF16)
        yield
        za = _dg(act[:, :d_w + d_a], wl2_ref[...])
        a = 0.5 * jnp.tanh(a0_ref[...] + za[:, c:]) + 0.5
        g_out[0, rs, :] = _dg(act[:, d_w + d_a:], g2_ref[...])
        seg = seg_ref[...]
        kk = k * kk_ref[...]
        kk = kk * jnp.minimum(lax.rsqrt(_seg_sum(kk * kk, seg)), 1e12)
        k2 = k * (1.0 + (a - 1.0) * ka_ref[...])
        r_out[0, rs, :] = r
        lw_out[0, rs, :] = (-0.5 * DECAY_SCALE) * jnp.tanh(w0_ref[...] + za[:, :c]) - 0.5 * DECAY_SCALE
        k_out[0, rs, :] = k2
        v_out[0, rs, :] = v
        a_out[0, rs, :] = -kk
        b_out[0, rs, :] = kk * a
        yield
        bonus_out[0, rs, :] = _seg_sum(r * k2 * rk_ref[...], seg) * v
        u = hin[:, 3 * c:4 * c] * (jnp.tanh(hin[:, 4 * c:]) + 1.0)
        if seg_len:
            for b in range(n_seq):
                up_ext[b * stride + CARRY_PAD:(b + 1) * stride, :] = u[b * seg_len:(b + 1) * seg_len]
        else:
            up_ext[pl.ds(CARRY_PAD + part * rp, rp), :] = u
        yield
        first_row = CARRY_PAD - CONV_CARRY + part * rp
        n_out = n_seq * stride - CARRY_PAD if seg_len else rp
        for s in range(SUBLANES):
            span = n_out + (CONV_WIDTH - 1 - s) // SUBLANES * SUBLANES
            shifted[part, s, 0:span, :] = up_ext[pl.ds(first_row + s, span), :]
        z = dwb_ref[...]
        for j in range(CONV_WIDTH):
            s, m = j % SUBLANES, j // SUBLANES
            z = z + dww_ref[j:j + 1, :] * shifted[part, s, m * SUBLANES:m * SUBLANES + n_out, :]
        if seg_len:
            z = jnp.concatenate([z[b * stride:b * stride + seg_len] for b in range(n_seq)], axis=0)
        mu = jnp.mean(z, axis=-1, keepdims=True)
        zc = z - mu
        var = jnp.mean(zc * zc, axis=-1, keepdims=True)
        zh = zc * lax.rsqrt(var + LN_EPS) * clnw_ref[...] + clnb_ref[...]
        yb_out[0, rs, :] = zh * (jnp.tanh(zh) + 1.0)
        yield

    _interleave(stream, n_parts)

    xn_end, h_end = last_rows[n_parts - 1]
    if seg_len:
        xn_tile = last_rows['xn']
        for b in range(n_seq):
            conv_out[b] = up_ext[(b + 1) * stride - CONV_CARRY:(b + 1) * stride, :]
            shift_out[b] = xn_tile[(b + 1) * seg_len - 1:(b + 1) * seg_len]
    else:
        tail = up_ext[pl.ds(tm + CARRY_PAD - CONV_CARRY, CONV_CARRY), :]
        up_ext[CARRY_PAD - CONV_CARRY:CARRY_PAD, :] = tail
        conv_out[0] = tail
        xn_last[...] = xn_end
        h_last[...] = h_end
        shift_out[0] = xn_end


def _mix_in(x, shift0, conv0, wd, tm, n_parts, whole_sequences):
    bsz, t, d = x.shape
    c = wd['w0'].shape[1]
    weights = [wd[n] for n in ('norm_mix', 'w_in', 'mu_rkv', 'w0', 'a0', 'w_lora1', 'w_lora2', 'g2',
                               'k_k', 'k_a', 'r_k', 'seg', 'dw_w', 'dw_b', 'cln_w', 'cln_b')]
    d_w, d_a = wd['lora_dims']
    if whole_sequences:
        tm, seg_len, n_seq, rows_b = bsz * t, t, bsz, 1
        assert t & (t - 1) == 0 and n_parts == 1
        x = x.reshape(1, tm, d)
        shift0 = jnp.repeat(shift0.reshape(bsz, d), t, axis=0).reshape(1, tm, d)
        grid = (1, 1)
        state = lambda rows, w: pl.BlockSpec((n_seq, rows, w), lambda b, i: (0, 0, 0))
        shift_in = pl.BlockSpec((1, tm, d), lambda b, i: (0, 0, 0))
        window_rows = n_seq * (CARRY_PAD + seg_len)
    else:
        seg_len, rows_b = 0, bsz
        grid = (bsz, t // tm)
        state = lambda rows, w: pl.BlockSpec((1, rows, w), lambda b, i: (b, 0, 0))
        shift_in = state(1, d)
        window_rows = tm + CARRY_PAD
    tok = lambda w: pl.BlockSpec((1, tm, w), lambda b, i: (b, i, 0))
    out_tok = jax.ShapeDtypeStruct((rows_b, x.shape[1], c), F32)
    outs = pl.pallas_call(
        functools.partial(_mix_in_kernel, tm=tm, c=c, n_parts=n_parts, d_w=d_w, d_a=d_a, seg_len=seg_len),
        grid=grid,
        in_specs=[tok(d), shift_in, state(CONV_CARRY, c)] + [_full(w.shape) for w in weights],
        out_specs=[tok(c)] * 9 + [state(1, d), state(CONV_CARRY, c)],
        out_shape=[out_tok] * 9 + [jax.ShapeDtypeStruct((bsz, 1, d), F32),
                                   jax.ShapeDtypeStruct((bsz, CONV_CARRY, c), F32)],
        scratch_shapes=[pltpu.VMEM((1, d), F32), pltpu.VMEM((1, 3 * c), F32),
                        pltpu.VMEM((window_rows, c), F32),
                        pltpu.VMEM((n_parts, SUBLANES, (window_rows - CARRY_PAD) // n_parts + CARRY_PAD - SUBLANES, c),
                                   F32)],
        compiler_params=pltpu.CompilerParams(dimension_semantics=("arbitrary", "arbitrary"),
                                             vmem_limit_bytes=VMEM_LIMIT),
        name="mix_in",
    )(x, shift0, conv0, *weights)
    return [o.reshape(bsz, t, c) for o in outs[:9]] + list(outs[9:])


def _tri_inverse(n_strict, row, col, lg_chunk):
    lg_base = INV_BASE.bit_length() - 1
    same = lambda sh: (row >> sh) == (col >> sh)
    lg0 = min(lg_base, lg_chunk)
    n = row.shape[0]

    def expand(c, lg):
        return jnp.where(same(lg), jnp.concatenate([c] * (n >> lg), axis=0), 0.0).astype(BF16)

    def fold(x, lg):
        b = 1 << lg
        return functools.reduce(lambda u, v: u + v, [x[i:i + b] for i in range(0, n, b)])

    b0 = 1 << lg0
    row_c = lax.broadcasted_iota(I32, (b0, n), 0)
    col_c = lax.broadcasted_iota(I32, (b0, n), 1)
    p_full = [jnp.where(same(lg0), x, 0.0) for x in n_strict]
    p = [fold(x, lg0) for x in p_full]
    t = [jnp.where((col_c & (b0 - 1)) == row_c, 1.0, 0.0) + x for x in p]
    p_full = [x.astype(BF16) for x in p_full]
    for _ in range(lg0 - 1):
        p = [_dg(x.astype(BF16), y) for x, y in zip(p, p_full)]
        p_full = [expand(x, lg0) for x in p]
        t = [x + _dg(x.astype(BF16), y) for x, y in zip(t, p_full)]
    for lg in range(lg0, lg_chunk):
        off_mask = same(lg + 1) & jnp.logical_not(same(lg))
        t_full = [expand(x, lg) for x in t]
        u = [_dg(x.astype(BF16), jnp.where(off_mask, m, 0.0).astype(BF16)) for x, m in zip(t, n_strict)]
        add = [_dg(x.astype(BF16), y) for x, y in zip(u, t_full)]
        even = ((lax.broadcasted_iota(I32, (1 << lg, n), 1) >> lg) & 1) == 0
        t = [jnp.concatenate([jnp.where(even, x, 0.0), jnp.where(even, a, x)], axis=0) for x, a in zip(t, add)]
    return [expand(x, lg_chunk) for x in t]


def _wkv_kernel(r_ref, lw_ref, k_ref, v_ref, a_ref, b_ref, s0_ref, y_ref, s_out, s_scr,
                *, tt, n_sub, chunk, chained):
    ti = pl.program_id(1)
    n_heads = r_ref.shape[2] // HEAD_SIZE
    n_chunks = tt // chunk
    lg_chunk = chunk.bit_length() - 1
    row = lax.broadcasted_iota(I32, (tt, tt), 0)
    col = lax.broadcasted_iota(I32, (tt, tt), 1)
    in_chunk = (row >> lg_chunk) == (col >> lg_chunk)
    tri_incl = in_chunk & (col <= row)
    tri_strict = in_chunk & (col < row)
    m_cum = jnp.where(tri_incl, 1.0, 0.0).astype(BF16)
    row_h = lax.broadcasted_iota(I32, (HEAD_SIZE, HEAD_SIZE), 0)
    col_h = lax.broadcasted_iota(I32, (HEAD_SIZE, HEAD_SIZE), 1)
    eye_h = row_h == col_h

    if chained:
        @pl.when(ti == 0)
        def _():
            s_scr[...] = s0_ref[0]

    rt_all, g_end_all, cols = [], [], {name: [] for name in ('v', 'rt', 'at', 'kt', 'bt', 'bd', 'kd')}
    hsl = [slice(HEAD_SIZE * h, HEAD_SIZE * (h + 1)) for h in range(n_heads)]
    for sub in range(n_sub):
        rows = pl.ds(sub * tt, tt)
        lw_all = lw_ref[0, rows, :]
        k_all = k_ref[0, rows, :]
        b_all = b_ref[0, rows, :]
        cum = _mask_dot(m_cum, lw_all)
        tot = jnp.concatenate([jnp.broadcast_to(cum[(ci + 1) * chunk - 1:(ci + 1) * chunk], (chunk, cum.shape[1]))
                               for ci in range(n_chunks)], axis=0)
        e_neg = jnp.exp(-cum)
        e_end = jnp.exp(tot - cum)
        full = dict(v=v_ref[0, rows, :], rt=r_ref[0, rows, :] * jnp.exp(cum),
                    at=a_ref[0, rows, :] * jnp.exp(cum - lw_all), kt=k_all * e_neg, bt=b_all * e_neg,
                    bd=b_all * e_end, kd=k_all * e_end)
        rt_all.append(full['rt'])
        g_end_all.append(jnp.exp(tot))
        for name, z in full.items():
            cols[name] += [z[:, s_].astype(BF16) for s_ in hsl]
    v, rt, at, kt, bt, bd, kd = (cols[name] for name in ('v', 'rt', 'at', 'kt', 'bt', 'bd', 'kd'))

    units = range(n_sub * n_heads)
    mm = [_dg(jnp.concatenate([at[u], rt[u]], axis=0), jnp.concatenate([bt[u], kt[u]], axis=0), NT) for u in units]
    m_ab = [jnp.where(tri_strict, mm[u][:tt, :tt], 0.0) for u in units]
    m_ak = [jnp.where(tri_strict, mm[u][:tt, tt:], 0.0).astype(BF16) for u in units]
    m_rb = [jnp.where(tri_incl, mm[u][tt:, :tt], 0.0).astype(BF16) for u in units]
    m_rk = [jnp.where(tri_incl, mm[u][tt:, tt:], 0.0).astype(BF16) for u in units]
    tinv = _tri_inverse(m_ab, row, col, lg_chunk)
    akv = [_dg(m_ak[u], v[u]).astype(BF16) for u in units]
    w1 = [_dg(tinv[u], at[u]).astype(BF16) for u in units]
    w2 = [_dg(tinv[u], akv[u]).astype(BF16) for u in units]
    q = [(rt_all[u // n_heads][:, hsl[u % n_heads]] + _dg(m_rb[u], w1[u])).astype(BF16) for u in units]
    y0 = [_dg(m_rb[u], w2[u]) + _dg(m_rk[u], v[u]) for u in units]

    heads = range(n_heads)
    if chained:
        s = [s_scr[h] for h in heads]
    for sub in range(n_sub):
        for ci in range(n_chunks):
            cs = slice(ci * chunk, (ci + 1) * chunk)
            seq = sub * n_chunks + ci
            if not chained:
                s = [s0_ref[seq, h] for h in heads]
            g_row = g_end_all[sub][ci * chunk:ci * chunk + 1]
            un = [sub * n_heads + h for h in heads]
            gm = [jnp.where(eye_h, jnp.broadcast_to(g_row[:, hsl[h]], (HEAD_SIZE, HEAD_SIZE)), 0.0)
                  + _dg(bd[un[h]][cs], w1[un[h]][cs], TN) for h in heads]
            hm = [_dg(bd[un[h]][cs], w2[un[h]][cs], TN) + _dg(kd[un[h]][cs], v[un[h]][cs], TN) for h in heads]
            for h in heads:
                y_ref[0, pl.ds(sub * tt + ci * chunk, chunk), hsl[h]] = _bdot(q[un[h]][cs], s[h]) + y0[un[h]][cs]
            s = [_bdot(gm[h], s[h]) + hm[h] for h in heads]
            if not chained:
                for h in heads:
                    s_out[seq, h] = s[h]
    if chained:
        for h in heads:
            s_scr[h] = s[h]
            s_out[0, h] = s[h]


def _wkv(r, lw, k, v, a, b, s0t, tt, n_sub, chunk, chained):
    bsz, t, c = r.shape
    n_heads = c // HEAD_SIZE
    hs = HEAD_SIZE
    step = tt * n_sub
    if chained:
        grid = (bsz, t // step)
        tok = pl.BlockSpec((1, step, c), lambda bi, ti: (bi, ti, 0))
        st = pl.BlockSpec((1, n_heads, hs, hs), lambda bi, ti: (bi, 0, 0, 0))
        args = (r, lw, k, v, a, b)
        y_shape = (bsz, t, c)
    else:
        assert t == chunk and (bsz * t) % step == 0
        n_seq = step // chunk
        grid = (1, bsz * t // step)
        tok = pl.BlockSpec((1, step, c), lambda bi, ti: (0, ti, 0))
        st = pl.BlockSpec((n_seq, n_heads, hs, hs), lambda bi, ti: (ti, 0, 0, 0))
        args = tuple(z.reshape(1, bsz * t, c) for z in (r, lw, k, v, a, b))
        y_shape = (1, bsz * t, c)
    y, s_new = pl.pallas_call(
        functools.partial(_wkv_kernel, tt=tt, n_sub=n_sub, chunk=chunk, chained=chained),
        grid=grid,
        in_specs=[tok] * 6 + [st],
        out_specs=[tok, st],
        out_shape=[jax.ShapeDtypeStruct(y_shape, F32), jax.ShapeDtypeStruct(s0t.shape, F32)],
        scratch_shapes=[pltpu.VMEM((n_heads, hs, hs), F32)],
        compiler_params=pltpu.CompilerParams(dimension_semantics=("arbitrary",) * 2,
                                             vmem_limit_bytes=VMEM_LIMIT),
        name="wkv",
    )(*args, s0t)
    return y.reshape(bsz, t, c), s_new


def _interleave(make_stream, n_parts):
    for _ in zip(*[make_stream(part) for part in range(n_parts)]):
        pass


def _mix_out_kernel(x_ref, y_ref, bonus_ref, g_ref, yb_ref, lnw_ref, lnb_ref, seg_ref, wout_ref, nffn_ref,
                    wr_ref, h1_out, xn_out, eid_out, eidt_out, gate_out, *, c, n_parts):
    rows_per = x_ref.shape[0] // n_parts

    def stream(part):
        rs = pl.ds(part * rows_per, rows_per)
        seg = seg_ref[...]
        y = y_ref[rs, :]
        inv_n = 1.0 / HEAD_SIZE
        mu = _seg_sum(y, seg) * inv_n
        yield
        yc = y - mu
        var = _seg_sum(yc * yc, seg) * inv_n
        yield
        yn = yc * lax.rsqrt(var + GN_EPS) * lnw_ref[...] + lnb_ref[...]
        ya = (yn + bonus_ref[rs, :]) * g_ref[rs, :]
        mix = _dg(ya.astype(BF16), wout_ref[:c, :]) + _dg(yb_ref[rs, :].astype(BF16), wout_ref[c:, :])
        yield
        h1 = x_ref[rs, :] + mix
        h1_out[rs, :] = h1
        xn = _rms(h1, nffn_ref[...])
        xn_out[rs, :] = xn.astype(BF16)
        xh, xl = _split2(xn)
        hi_lo = _dg(xh, wr_ref[...])
        logits = hi_lo[:, :ROUTER_LANES] + (hi_lo[:, ROUTER_LANES:] + _dg(xl, wr_ref[:, :ROUTER_LANES]))
        yield
        lane = lax.broadcasted_iota(I32, logits.shape, 1)
        neg = jnp.float32(-jnp.inf)
        is_g = (lane >= N_EXPERTS) & (lane < N_EXPERTS + N_EXPERT_GROUPS)
        glog = jnp.where(is_g, logits, neg)
        gmax = jnp.max(glog, axis=-1, keepdims=True)
        gsel = jnp.min(jnp.where(glog == gmax, lane, 4 * ROUTER_LANES), axis=-1, keepdims=True) - N_EXPERTS
        gp = 1.0 / jnp.sum(jnp.where(is_g, jnp.exp(glog - gmax), 0.0), axis=-1, keepdims=True)
        in_grp = (lane >= gsel * EXPERTS_PER_GROUP) & (lane < (gsel + 1) * EXPERTS_PER_GROUP)
        elog = jnp.where(in_grp, logits, neg)
        emax = jnp.max(elog, axis=-1, keepdims=True)
        ex = jnp.where(in_grp, jnp.exp(elog - emax), 0.0)
        eprob = ex / jnp.sum(ex, axis=-1, keepdims=True)
        eprob = jnp.where(in_grp, eprob, -1.0)
        yield
        v1 = jnp.max(eprob, axis=-1, keepdims=True)
        i1 = jnp.min(jnp.where(eprob == v1, lane, 4 * ROUTER_LANES), axis=-1, keepdims=True)
        rest = jnp.where(lane == i1, -1.0, eprob)
        v2 = jnp.max(rest, axis=-1, keepdims=True)
        i2 = jnp.min(jnp.where(rest == v2, lane, 4 * ROUTER_LANES), axis=-1, keepdims=True)
        denom = v1 + v2
        eid = jnp.where(lane == 0, i1, jnp.where(lane == 1, i2, 0))
        eid_out[rs, :] = eid
        eidt_out[:, rs] = jnp.transpose(eid)[:SUBLANES]
        gate_out[rs, :] =jnp.where(lane == 0, gp * v1 / denom, jnp.where(lane == 1, gp * v2 / denom, 0.0))
        yield

    _interleave(stream, n_parts)


def _mix_out(x2, y2, bonus2, g2, yb2, wd, tm, n_parts):
    m, d = x2.shape
    c = y2.shape[1]
    tokd = pl.BlockSpec((tm, d), lambda i: (i, 0))
    tokc = pl.BlockSpec((tm, c), lambda i: (i, 0))
    tokr = pl.BlockSpec((tm, ROUTER_LANES), lambda i: (i, 0))
    weights = [wd[n] for n in ('ln_x_w', 'ln_x_b', 'seg', 'w_out', 'norm_ffn', 'w_router')]
    return pl.pallas_call(
        functools.partial(_mix_out_kernel, c=c, n_parts=n_parts),
        grid=(m // tm,),
        in_specs=[tokd, tokc, tokc, tokc, tokc] + [_full(w.shape) for w in weights],
        out_specs=[tokd, tokd, tokr, pl.BlockSpec((SUBLANES, tm), lambda i: (0, i)), tokr],
        out_shape=[jax.ShapeDtypeStruct((m, d), F32), jax.ShapeDtypeStruct((m, d), BF16),
                   jax.ShapeDtypeStruct((m, ROUTER_LANES), I32), jax.ShapeDtypeStruct((SUBLANES, m), I32),
                   jax.ShapeDtypeStruct((m, ROUTER_LANES), F32)],
        compiler_params=pltpu.CompilerParams(dimension_semantics=("arbitrary",), vmem_limit_bytes=VMEM_LIMIT),
        name="mix_out",
    )(x2, y2, bonus2, g2, yb2, *weights)


def _pow2_pieces(count, max_rows, fn):
    def split(first_rows, last_rows, off):
        rows = first_rows
        while rows >= last_rows:
            has = (count & (rows // RUN_ALIGN)) != 0

            @pl.when(has)
            def _(off=off, rows=rows):
                fn(off, rows)
            off = off + jnp.where(has, rows, 0)
            rows //= 2

    common = min(max_rows, COMMON_PIECE)
    if max_rows > common:
        pl.when(count >= 2 * common // RUN_ALIGN)(lambda: split(max_rows, 2 * common, 0))
    large = (count // (2 * common // RUN_ALIGN)) * (2 * common)
    split(common, RUN_ALIGN, large)


def _for_each_expert(fn):
    def body(e, carry):
        fn(e)
        return carry
    lax.fori_loop(0, N_EXPERTS, body, 0)


def _pow2_floor(n):
    return 1 << (n.bit_length() - 1)


def _dispatch_kernel(nch_ref, tot_ref, off_ref, loff_ref, tn_ref, toff_ref, nu_ref, *refs, groups, bm, n_blocks):
    n_g = len(groups)
    eid_refs, xn_refs = refs[:n_g], refs[n_g + 1:2 * n_g + 1]
    loffc_ref = refs[n_g]
    xs_out, buf, sem, zbuf, zsem = refs[2 * n_g + 1:]
    i = pl.program_id(0)
    last = pl.num_programs(0) - 1
    max_run = max(tm for tm, _ in groups)

    def start_tile(tile):
        def per_expert(e):
            src0 = loff_ref[tile * N_EXPERTS + e]
            dst0 = off_ref[tile * N_EXPERTS + e]

            def piece(o, rows):
                src = buf.at[tile % 2, pl.ds(pl.multiple_of(src0 + o, RUN_ALIGN), rows)]
                dst = xs_out.at[pl.ds(pl.multiple_of(dst0 + o, RUN_ALIGN), rows)]
                pltpu.make_async_copy(src, dst, sem.at[tile % 2]).start()
            _pow2_pieces(nch_ref[tile * N_EXPERTS + e], max_run, piece)
        _for_each_expert(per_expert)

    def wait_tile(tile):
        def piece(o, rows):
            pltpu.make_async_copy(buf.at[tile % 2, pl.ds(0, rows)], xs_out.at[pl.ds(0, rows)], sem.at[tile % 2]).wait()
        _pow2_pieces(tot_ref[tile], _pow2_floor(buf.shape[1]), piece)

    @pl.when(i >= 2)
    def _():
        wait_tile(i - 2)

    def sort_tile(eid_ref, xn_ref, tm):
        e_rows = eid_ref[...]
        sub = lax.broadcasted_iota(I32, (N_EXPERTS, tm), 0)
        e1 = jnp.where(sub == e_rows[0:1], 1.0, 0.0)
        e2 = jnp.where(sub == e_rows[1:2], 1.0, 0.0)
        before = lax.broadcasted_iota(I32, (tm, tm), 0) < lax.broadcasted_iota(I32, (tm, tm), 1)
        slot = _dg((e1 + e2).astype(BF16), jnp.where(before, 1.0, 0.0).astype(BF16)) + loffc_ref[0]
        l1 = jnp.sum(slot * e1, axis=0, keepdims=True).astype(I32)
        l2 = jnp.sum(slot * e2, axis=0, keepdims=True).astype(I32)
        n_rows = 2 * tm + LOCAL_PAD
        rows = lax.broadcasted_iota(I32, (n_rows, tm), 0)
        perm = jnp.where((rows == l1) | (rows == l2), 1.0, 0.0).astype(BF16)
        buf[i % 2, 0:n_rows, :] = _dg(perm, xn_ref[...]).astype(BF16)

    first = 0
    for g, (tm, n_tiles) in enumerate(groups):
        pl.when((i >= first) & (i < first + n_tiles))(functools.partial(sort_tile, eid_refs[g], xn_refs[g], tm))
        first += n_tiles
    start_tile(i)

    @pl.when(i == 0)
    def _():
        zbuf[...] = jnp.zeros_like(zbuf)
        half = zbuf.shape[0]

        def zero_copy(off, rows):
            return pltpu.make_async_copy(zbuf.at[pl.ds(0, rows)], xs_out.at[pl.ds(off, rows)], zsem)

        def tails(fn):
            _for_each_expert(lambda e: _pow2_pieces(
                tn_ref[e], half, lambda o, rows: fn(pl.multiple_of(toff_ref[e] + o, RUN_ALIGN), rows)))

        def unused_blocks(fn):
            def body(b, carry):
                fn(pl.multiple_of(b * bm, bm), half)
                fn(pl.multiple_of(b * bm + half, half), half)
                return carry
            lax.fori_loop(nu_ref[0], n_blocks, body, 0)

        tails(lambda off, rows: zero_copy(off, rows).start())
        unused_blocks(lambda off, rows: zero_copy(off, rows).start())
        tails(lambda off, rows: zero_copy(off, rows).wait())
        unused_blocks(lambda off, rows: zero_copy(off, rows).wait())

    @pl.when(i == last)
    def _():
        @pl.when(i >= 1)
        def _():
            wait_tile(i - 1)
        wait_tile(i)


def _dispatch(plan, n_used, eids_t, xns, tms, n_blocks, bm):
    d = xns[0].shape[1]
    groups = tuple((tm, xn.shape[0] // tm) for xn, tm in zip(xns, tms))
    firsts = [sum(n for _, n in groups[:g]) for g in range(len(groups))]
    lbuf = 2 * max(tms) + LOCAL_PAD

    def tile_of(g):
        return lambda i: jnp.clip(i - firsts[g], 0, groups[g][1] - 1)

    imap = lambda f: (lambda i, *_: f(i))
    in_specs = ([pl.BlockSpec((SUBLANES, tm), imap(lambda i, g=g: (0, tile_of(g)(i)))) for g, tm in enumerate(tms)] +
                [pl.BlockSpec((1, N_EXPERTS, 1), imap(lambda i: (i, 0, 0)))] +
                [pl.BlockSpec((tm, d), imap(lambda i, g=g: (tile_of(g)(i), 0))) for g, tm in enumerate(tms)])
    return pl.pallas_call(
        functools.partial(_dispatch_kernel, groups=groups, bm=bm, n_blocks=n_blocks),
        grid_spec=pltpu.PrefetchScalarGridSpec(
            num_scalar_prefetch=7,
            grid=(sum(n for _, n in groups),),
            in_specs=in_specs,
            out_specs=pl.BlockSpec(memory_space=pl.ANY),
            scratch_shapes=[pltpu.VMEM((2, lbuf, d), BF16), pltpu.SemaphoreType.DMA((2,)),
                            pltpu.VMEM((bm // 2, d), BF16), pltpu.SemaphoreType.DMA],
        ),
        out_shape=jax.ShapeDtypeStruct((n_blocks * bm, d), BF16),
        compiler_params=pltpu.CompilerParams(dimension_semantics=("arbitrary",), vmem_limit_bytes=VMEM_LIMIT),
        name="moe_dispatch",
    )(plan['nch'], plan['tot'], plan['off'], plan['loff'], plan['tail_n'], plan['tail_off'], n_used, *eids_t,
      plan['loff_col'], *xns)


def _experts_kernel(be_ref, slot_ref, nxt_ref, nu_ref, xs_ref, wg_hbm, wu_hbm, wd_hbm, yb_ref,
                    wg_f, wu_f, wd_f, wg_b, wu_b, wd_b, sem):
    b = pl.program_id(0)

    def weight_copies(e, slot):
        pairs = ((wg_hbm, wg_f), (wu_hbm, wu_f), (wd_hbm, wd_f))
        return [pltpu.make_async_copy(src.at[e], dst.at[slot], sem.at[slot, j]) for j, (src, dst) in enumerate(pairs)]

    @pl.when(b == 0)
    def _():
        for cp in weight_copies(be_ref[0], slot_ref[0]):
            cp.start()

    @pl.when((b < nu_ref[0]) & ((b == 0) | (be_ref[b] != be_ref[jnp.maximum(b - 1, 0)])))
    def _():
        slot = slot_ref[b]
        for cp in weight_copies(be_ref[b], slot):
            cp.wait()

        @pl.when(nxt_ref[b] >= 0)
        def _():
            for cp in weight_copies(nxt_ref[b], 1 - slot):
                cp.start()
        wg_b[...] = wg_f[slot].astype(BF16)
        wu_b[...] = wu_f[slot].astype(BF16)
        wd_b[...] = wd_f[slot].astype(BF16)

    @pl.when(b < nu_ref[0])
    def _():
        xb = xs_ref[...]
        hg = _dg(xb, wg_b[...])
        hu = _dg(xb, wu_b[...])
        act = (hg * _sigmoid(hg) * hu).astype(BF16)
        yb_ref[...] = _dg(act, wd_b[...]).astype(BF16)

    @pl.when(pl.program_id(0) >= nu_ref[0])
    def _():
        yb_ref[...] = jnp.zeros_like(yb_ref)


def _experts(sched, n_used, xs, wg, wu, wdn, bm):
    p, d = xs.shape
    ff = wg.shape[2]
    n_blocks = p // bm
    return pl.pallas_call(
        _experts_kernel,
        grid_spec=pltpu.PrefetchScalarGridSpec(
            num_scalar_prefetch=4,
            grid=(n_blocks,),
            in_specs=[pl.BlockSpec((bm, d), lambda b, be, sl, nx, nu: (jnp.minimum(b, nu[0] - 1), 0)),
                      pl.BlockSpec(memory_space=pl.ANY), pl.BlockSpec(memory_space=pl.ANY),
                      pl.BlockSpec(memory_space=pl.ANY)],
            out_specs=pl.BlockSpec((bm, d), lambda b, *_: (b, 0)),
            scratch_shapes=[pltpu.VMEM((2, d, ff), F32), pltpu.VMEM((2, d, ff), F32), pltpu.VMEM((2, ff, d), F32),
                            pltpu.VMEM((d, ff), BF16), pltpu.VMEM((d, ff), BF16), pltpu.VMEM((ff, d), BF16),
                            pltpu.SemaphoreType.DMA((2, 3))],
        ),
        out_shape=jax.ShapeDtypeStruct((p, d), BF16),
        compiler_params=pltpu.CompilerParams(dimension_semantics=("arbitrary",), vmem_limit_bytes=VMEM_LIMIT),
        name="moe_experts",
    )(sched['expert'], sched['slot'], sched['next'], n_used, xs, wg, wu, wdn)


def _final_kernel(nch_ref, tot_ref, off_ref, loff_ref, h1_ref, eid_ref, gate_ref, loffr_ref, p_ref, yb_hbm,
                  nple_ref, wpg_ref, wpp_ref, nfin_ref, y_out, buf, sem, *, tm, lbuf, tile0, n_parts):
    i = pl.program_id(0)

    def fetch(tile):
        base = (tile0 + tile) * N_EXPERTS

        def per_expert(e):
            src0 = off_ref[base + e]
            dst0 = loff_ref[base + e]

            def piece(o, rows):
                src = yb_hbm.at[pl.ds(pl.multiple_of(src0 + o, RUN_ALIGN), rows)]
                dst = buf.at[tile % 2, pl.ds(pl.multiple_of(dst0 + o, RUN_ALIGN), rows)]
                pltpu.make_async_copy(src, dst, sem.at[tile % 2]).start()
            _pow2_pieces(nch_ref[base + e], tm, piece)
        _for_each_expert(per_expert)

    def wait_fetch(tile):
        def piece(o, rows):
            pltpu.make_async_copy(yb_hbm.at[pl.ds(0, rows)], buf.at[tile % 2, pl.ds(0, rows)], sem.at[tile % 2]).wait()
        _pow2_pieces(tot_ref[tile0 + tile], _pow2_floor(lbuf), piece)

    @pl.when(i == 0)
    def _():
        buf[...] = jnp.zeros_like(buf)
        fetch(i)

    @pl.when(i + 1 < pl.num_programs(0))
    def _():
        fetch(i + 1)

    eid = eid_ref[...]
    lane = lax.broadcasted_iota(I32, (tm, N_EXPERTS), 1)
    e12 = (jnp.where(lane == eid[:, 0:1], 1.0, 0.0) + jnp.where(lane == eid[:, 1:2], 1.0, 0.0)).astype(BF16)
    rows_per = tm // n_parts
    picks = []
    for part in range(n_parts):
        rs = pl.ds(part * rows_per, rows_per)
        lane_p = lax.broadcasted_iota(I32, (rows_per, N_EXPERTS), 1)
        eid_p = eid_ref[rs, :]
        e1 = jnp.where(lane_p == eid_p[:, 0:1], 1.0, 0.0)
        e2 = jnp.where(lane_p == eid_p[:, 1:2], 1.0, 0.0)
        before = (lax.broadcasted_iota(I32, (rows_per, tm), 1)
                  < lax.broadcasted_iota(I32, (rows_per, tm), 0) + part * rows_per)
        slot = _dg(jnp.where(before, 1.0, 0.0).astype(BF16), e12) + loffr_ref[0]
        l1 = jnp.sum(slot * e1, axis=1, keepdims=True).astype(I32)
        l2 = jnp.sum(slot * e2, axis=1, keepdims=True).astype(I32)
        cols = lax.broadcasted_iota(I32, (rows_per, lbuf), 1)
        gate = gate_ref[rs, :]
        picks.append(jnp.where(cols == l1, gate[:, 0:1], jnp.where(cols == l2, gate[:, 1:2], 0.0)).astype(BF16))

    wait_fetch(i)
    sorted_rows = buf[i % 2]

    def stream(part):
        rs = pl.ds(part * rows_per, rows_per)
        h2 = h1_ref[rs, :] + _dg(picks[part], sorted_rows)
        yield
        gate_in = _rms(h2, nple_ref[...]).astype(BF16)
        pg = _sigmoid(_dg(gate_in, wpg_ref[...]))
        yield
        h3 = h2 + pg * _dg(p_ref[rs, :].astype(BF16), wpp_ref[...])
        y_out[rs, :] = _rms(h3, nfin_ref[...])
        yield

    _interleave(stream, n_parts)


def _final(plan, h1, eid, gate, p2, yb, wd, tm, tile0):
    m, d = h1.shape
    pd = p2.shape[1]
    lbuf = 2 * tm + LOCAL_PAD
    weights = [wd[n] for n in ('norm_ple', 'w_ple_gate', 'w_ple_proj', 'norm_final')]
    imap = lambda f: (lambda i, *_: f(i))
    return pl.pallas_call(
        functools.partial(_final_kernel, tm=tm, lbuf=lbuf, tile0=tile0, n_parts=TOKEN_PARTS),
        grid_spec=pltpu.PrefetchScalarGridSpec(
            num_scalar_prefetch=4,
            grid=(m // tm,),
            in_specs=[pl.BlockSpec((tm, d), imap(lambda i: (i, 0))),
                      pl.BlockSpec((tm, ROUTER_LANES), imap(lambda i: (i, 0))),
                      pl.BlockSpec((tm, ROUTER_LANES), imap(lambda i: (i, 0))),
                      pl.BlockSpec((1, 1, N_EXPERTS), imap(lambda i: (tile0 + i, 0, 0))),
                      pl.BlockSpec((tm, pd), imap(lambda i: (i, 0))),
                      pl.BlockSpec(memory_space=pl.ANY)] +
                     [pl.BlockSpec(w.shape, imap(lambda i, n=len(w.shape): (0,) * n)) for w in weights],
            out_specs=pl.BlockSpec((tm, d), imap(lambda i: (i, 0))),
            scratch_shapes=[pltpu.VMEM((2, lbuf, d), BF16), pltpu.SemaphoreType.DMA((2,))],
        ),
        out_shape=jax.ShapeDtypeStruct((m, d), F32),
        compiler_params=pltpu.CompilerParams(dimension_semantics=("arbitrary",), vmem_limit_bytes=VMEM_LIMIT),
        name="moe_final",
    )(plan['nch'], plan['tot'], plan['off'], plan['loff'], h1, eid, gate, plan['loff_row'], p2, yb, *weights)


def _route_plan(eids, tms, bm):
    experts = jnp.arange(N_EXPERTS, dtype=I32)
    counts = []
    for eid, tm in zip(eids, tms):
        onehot = (eid[:2, :, None] == experts).astype(I32)
        counts.append(onehot.reshape(2, -1, tm, N_EXPERTS).sum(axis=(0, 2)))
    n = jnp.concatenate(counts)
    n_al = (n + RUN_ALIGN - 1) // RUN_ALIGN * RUN_ALIGN
    loff = jnp.cumsum(n_al, axis=1) - n_al
    used = n_al.sum(axis=0)
    region = (used + bm - 1) // bm * bm
    pend = jnp.cumsum(region)
    off = (pend - region)[None, :] + jnp.cumsum(n_al, axis=0) - n_al
    n_assign = sum(2 * e.shape[1] for e in eids)
    n_blocks = -(-(n_assign + (RUN_ALIGN - 1) * N_EXPERTS * n.shape[0] + N_EXPERTS * (bm - 1)) // bm)
    block_start = jnp.arange(n_blocks, dtype=I32) * bm
    block_expert = jnp.minimum(jnp.sum((pend[None, :] <= block_start[:, None]).astype(I32), axis=1), N_EXPERTS - 1)
    plan = dict(nch=(n_al // RUN_ALIGN).reshape(-1).astype(I32), off=off.reshape(-1).astype(I32),
                tot=(n_al.sum(axis=1) // RUN_ALIGN).astype(I32),
                loff=loff.reshape(-1).astype(I32), tail_n=((region - used) // RUN_ALIGN).astype(I32),
                tail_off=(pend - region + used).astype(I32), loff_col=loff.astype(F32)[:, :, None],
                loff_row=loff.astype(F32)[:, None, :])
    n_used = pend[-1] // bm
    prev = jnp.concatenate([jnp.full((1,), -1, I32), block_expert[:-1].astype(I32)])
    slot = (jnp.cumsum((block_expert != prev).astype(I32)) - 1) % 2
    after = pend[block_expert] // bm
    nxt = jnp.where(after < n_used, block_expert[jnp.minimum(after, n_blocks - 1)], -1)
    sched = dict(expert=block_expert.astype(I32), slot=slot.astype(I32), next=nxt.astype(I32))
    return plan, sched, n_used.astype(I32).reshape(1), n_blocks


def _tile_plan(bsz, t, long_sequence):
    rows = bsz * t
    span = t if long_sequence else rows
    wkv_sub = min(WKV_SUBTILE, span)
    return dict(
        mix_in=min(TOKEN_TILE, t), mix_in_parts=MIX_IN_PARTS if (long_sequence and t >= TOKEN_TILE) else 1,
        wkv_sub=wkv_sub, wkv_n_sub=min(WKV_SUBTILES, span // wkv_sub),
        wkv_chunk=min(WKV_CHUNK, t),
        token=min(TOKEN_TILE, rows), token_parts=TOKEN_PARTS)


def _layer_front(x, shift0, wkv0, conv0, wd, tiles, long_sequence):
    bsz, t, d = x.shape
    r, lw, k2, v, ah, bh, g, bonus, yb, shift_new, conv_new = _mix_in(
        x, shift0, conv0, wd, tiles['mix_in'], tiles['mix_in_parts'], not long_sequence)
    y, s_new = _wkv(r, lw, k2, v, ah, bh, jnp.swapaxes(wkv0, -1, -2), tiles['wkv_sub'], tiles['wkv_n_sub'],
                    tiles['wkv_chunk'], long_sequence)
    flat = lambda z: z.reshape(bsz * t, z.shape[-1])
    h1, xn2, eid, eid_t, gate = _mix_out(flat(x), flat(y), flat(bonus), flat(g), flat(yb), wd, tiles['token'],
                                         tiles['token_parts'])
    return h1, xn2, (eid, eid_t), gate, shift_new.reshape(bsz, d), jnp.swapaxes(s_new, -1, -2), conv_new


def kernel(x_prompt, x_sample, state_shift, state_wkv, cache_conv, p_prompt, p_sample, norm_mix, w_in, mu_rkv, mu_w, mu_a, mu_g, w0, w1, w2, a0, a1, a2, g1, g2, k_k, k_a, r_k, ln_x_w, ln_x_b, dw_w, dw_b, cln_w, cln_b, w_out, norm_ffn, w_router_group, w_router_expert, w_exp_gate, w_exp_up, w_exp_down, norm_ple, w_ple_gate, w_ple_proj, norm_final):
    depth = norm_mix.shape[0]
    assert depth == 1
    d = x_prompt.shape[-1]
    c = w0.shape[-1]
    row = lambda z: z[0].reshape(1, -1).astype(F32)
    lane = jnp.arange(LANES, dtype=I32) // HEAD_SIZE
    w_router = jnp.concatenate([w_router_expert[0], w_router_group[0],
                                jnp.zeros((d, ROUTER_LANES - N_EXPERTS - N_EXPERT_GROUPS), F32)], axis=1)
    first = jnp.concatenate([w1[0], a1[0], g1[0]], axis=1)
    mixed = jnp.concatenate([mu_w[0][:, None] * w1[0], mu_a[0][:, None] * a1[0], mu_g[0][:, None] * g1[0]], axis=1)
    d_w, d_a = w1.shape[2], a1.shape[2]
    second = jnp.concatenate([jnp.concatenate([w2[0], jnp.zeros((d_w, c), F32)], axis=1),
                              jnp.concatenate([jnp.zeros((d_a, c), F32), a2[0]], axis=1)], axis=0)
    glu_half = jnp.concatenate([jnp.ones((3 * c,), F32), jnp.full((w_in.shape[2] - 3 * c,), 0.5, F32)])
    wd = dict(
        norm_mix=row(norm_mix), w_in=(w_in[0] * glu_half).astype(BF16), mu_rkv=row(mu_rkv),
        w0=0.5 * row(w0), a0=0.5 * row(a0),
        w_lora1=jnp.concatenate([first, mixed], axis=0).astype(BF16), w_lora2=(0.5 * second).astype(BF16),
        g2=g2[0].astype(BF16), lora_dims=(d_w, d_a),
        k_k=row(k_k), k_a=row(k_a), r_k=row(r_k), ln_x_w=row(ln_x_w), ln_x_b=row(ln_x_b),
        seg=(lane[:, None] == lane[None, :]).astype(BF16),
        dw_w=dw_w[0].astype(F32), dw_b=row(dw_b), cln_w=0.5 * row(cln_w), cln_b=0.5 * row(cln_b),
        w_out=w_out[0].astype(BF16), norm_ffn=row(norm_ffn),
        w_router=jnp.concatenate(_split2(w_router), axis=1),
        norm_ple=row(norm_ple), w_ple_gate=w_ple_gate[0].astype(BF16), w_ple_proj=w_ple_proj[0].astype(BF16),
        norm_final=norm_final.reshape(1, -1).astype(F32),
    )
    bp, tp, _ = x_prompt.shape
    bs, ts, _ = x_sample.shape
    mp, ms = bp * tp, bs * ts

    zeros = lambda *s: jnp.zeros(s, F32)
    h1_p, xn_p, eid_p, gate_p, shift_p, wkv_p, conv_p = _layer_front(
        x_prompt, zeros(bp, 1, d), zeros(bp, c // HEAD_SIZE, HEAD_SIZE, HEAD_SIZE), zeros(bp, CONV_CARRY, c),
        wd, _tile_plan(bp, tp, True), True)
    h1_s, xn_s, eid_s, gate_s, shift_s, wkv_s, conv_s = _layer_front(
        x_sample, state_shift[0][:, None, :], state_wkv[0], cache_conv[0],
        wd, _tile_plan(bs, ts, False), False)

    bm = EXPERT_BLOCK
    tr_p, tr_s = min(TOKEN_TILE, mp), min(TOKEN_TILE, ms)
    plan, sched, n_used, n_blocks = _route_plan([eid_p[1], eid_s[1]], [tr_p, tr_s], bm)
    tiles_p = mp // tr_p
    xs = _dispatch(plan, n_used, [eid_p[1], eid_s[1]], [xn_p, xn_s], [tr_p, tr_s], n_blocks, bm)
    yb = _experts(sched, n_used, xs, w_exp_gate[0], w_exp_up[0], w_exp_down[0], bm)
    y_p = _final(plan, h1_p, eid_p[0], gate_p, p_prompt[0].reshape(mp, -1), yb, wd, tr_p, 0)
    y_s = _final(plan, h1_s, eid_s[0], gate_s, p_sample[0].reshape(ms, -1), yb, wd, tr_s, tiles_p)
    return (y_p.reshape(x_prompt.shape), y_s.reshape(x_sample.shape), shift_p[None], wkv_p[None], conv_p[None],
            shift_s[None], wkv_s[None], conv_s[None])
```

```python
import functools

import jax
import jax.numpy as jnp
from jax import lax
from jax.experimental import pallas as pl
from jax.experimental.pallas import tpu as pltpu

F32 = jnp.float32
BF16 = jnp.bfloat16
I32 = jnp.int32

HEAD_SIZE = 64
CONV_WIDTH = 31
CONV_CARRY = CONV_WIDTH - 1
SUBLANES = 8
LANES = 128
CARRY_PAD = 32
N_EXPERT_GROUPS = 4
EXPERTS_PER_GROUP = 8
N_EXPERTS = N_EXPERT_GROUPS * EXPERTS_PER_GROUP
ROUTER_LANES = 128
RMS_EPS = 1e-6
LN_EPS = 1e-5
GN_EPS = 64e-5
DECAY_SCALE = 0.6065306597126334
INV_BASE = 16
RUN_ALIGN = 16
LOCAL_PAD = N_EXPERTS * RUN_ALIGN
VMEM_LIMIT = 56 * 1024 * 1024

TOKEN_TILE = 512
MIX_IN_PARTS = 4
TOKEN_PARTS = 2
WKV_SUBTILE = 128
WKV_SUBTILES = 2
WKV_CHUNK = 64
EXPERT_BLOCK = 512

NN = ((1,), (0,))
NT = ((1,), (1,))
TN = ((0,), (0,))


def _dg(a, b, dims=NN):
    return lax.dot_general(a, b, (dims, ((), ())), preferred_element_type=F32)


def _split2(x):
    hi = x.astype(BF16)
    lo = (x - hi.astype(F32)).astype(BF16)
    return hi, lo


def _bdot(a, b, dims=NN):
    return _dg(a.astype(BF16), b.astype(BF16), dims)


def _mask_dot(mask_bf16, x):
    h, l = _split2(x)
    return _dg(mask_bf16, h) + _dg(mask_bf16, l)


def _seg_sum(x, seg_bf16):
    h, l = _split2(x)
    w = seg_bf16.shape[0]
    return jnp.concatenate([_dg(h[:, j:j + w], seg_bf16) + _dg(l[:, j:j + w], seg_bf16)
                            for j in range(0, x.shape[1], w)], axis=1)


def _rms(x, g):
    return x * lax.rsqrt(jnp.mean(x * x, axis=-1, keepdims=True) + RMS_EPS) * g


def _sigmoid(x):
    return 0.5 * jnp.tanh(0.5 * x) + 0.5


def _full(shape):
    n = len(shape)
    return pl.BlockSpec(shape, lambda *_: (0,) * n, pipeline_mode=pl.Buffered(1))


def _mix_in_kernel(x_ref, shift_ref, conv_ref, nm_ref, win_ref, murkv_ref, w0_ref, a0_ref, wl1_ref, wl2_ref, g2_ref,
                   kk_ref, ka_ref, rk_ref, seg_ref, dww_ref, dwb_ref, clnw_ref, clnb_ref,
                   r_out, lw_out, k_out, v_out, a_out, b_out, g_out, bonus_out, yb_out, shift_out, conv_out,
                   xn_last, h_last, up_ext, shifted, *, tm, c, n_parts, d_w, d_a, seg_len):
    i = pl.program_id(1)
    n_seq = tm // seg_len if seg_len else 1
    stride = CARRY_PAD + seg_len

    if seg_len:
        for b in range(n_seq):
            up_ext[b * stride:b * stride + CARRY_PAD - CONV_CARRY, :] = jnp.zeros((CARRY_PAD - CONV_CARRY, c), F32)
            up_ext[b * stride + CARRY_PAD - CONV_CARRY:b * stride + CARRY_PAD, :] = conv_ref[b]
    else:
        @pl.when(i == 0)
        def _():
            sp = shift_ref[0]
            xn_last[...] = sp
            sp8 = jnp.broadcast_to(sp, (8, sp.shape[1])).astype(BF16)
            h_last[...] = _dg(sp8, win_ref[:, :3 * c])[0:1]
            up_ext[CARRY_PAD - CONV_CARRY:CARRY_PAD, :] = conv_ref[0]

    rp = tm // n_parts
    last_rows = {}

    def stream(part):
        rs = pl.ds(part * rp, rp)
        xn = _rms(x_ref[0, rs, :], nm_ref[...])
        hin = _dg(xn.astype(BF16), win_ref[...])
        h_rkv = hin[:, :3 * c]
        last_rows[part] = (xn[rp - 1:rp], h_rkv[rp - 1:rp])
        last_rows['xn'] = xn
        yield
        if seg_len:
            xn_prev = shift_ref[0]
            h_prev = _dg(xn_prev.astype(BF16), win_ref[:, :3 * c])
            first = (lax.broadcasted_iota(I32, (rp, 1), 0) & (seg_len - 1)) == 0
        else:
            xn_prev, h_prev = (xn_last[...], h_last[...]) if part == 0 else last_rows[part - 1]
            first = lax.broadcasted_iota(I32, (rp, 1), 0) == 0
        dx = jnp.where(first, xn_prev, pltpu.roll(xn, 1, 0)) - xn
        hprev = jnp.where(first, h_prev, pltpu.roll(h_rkv, 1, 0))
        rkv = h_rkv + (hprev - h_rkv) * murkv_ref[...]
        r = rkv[:, :c]
        k = rkv[:, c:2 * c]
        v = rkv[:, 2 * c:]
        l1 = _dg(jnp.concatenate([xn.astype(BF16), dx.astype(BF16)], axis=1), wl1_ref[...])
        lane = lax.broadcasted_iota(I32, l1.shape, 1)
        act = jnp.where(lane < d_w, jnp.tanh(l1), jnp.where(lane < d_w + d_a, l1, _sigmoid(l1))).astype(BF16)
        yield
        za = _dg(act[:, :d_w + d_a], wl2_ref[...])
        a = 0.5 * jnp.tanh(a0_ref[...] + za[:, c:]) + 0.5
        g_out[0, rs, :] = _dg(act[:, d_w + d_a:], g2_ref[...])
        seg = seg_ref[...]
        kk = k * kk_ref[...]
        kk = kk * jnp.minimum(lax.rsqrt(_seg_sum(kk * kk, seg)), 1e12)
        k2 = k * (1.0 + (a - 1.0) * ka_ref[...])
        r_out[0, rs, :] = r
        lw_out[0, rs, :] = (-0.5 * DECAY_SCALE) * jnp.tanh(w0_ref[...] + za[:, :c]) - 0.5 * DECAY_SCALE
        k_out[0, rs, :] = k2
        v_out[0, rs, :] = v
        a_out[0, rs, :] = -kk
        b_out[0, rs, :] = kk * a
        yield
        bonus_out[0, rs, :] = _seg_sum(r * k2 * rk_ref[...], seg) * v
        u = hin[:, 3 * c:4 * c] * (jnp.tanh(hin[:, 4 * c:]) + 1.0)
        if seg_len:
            for b in range(n_seq):
                up_ext[b * stride + CARRY_PAD:(b + 1) * stride, :] = u[b * seg_len:(b + 1) * seg_len]
        else:
            up_ext[pl.ds(CARRY_PAD + part * rp, rp), :] = u
        yield
        first_row = CARRY_PAD - CONV_CARRY + part * rp
        n_out = n_seq * stride - CARRY_PAD if seg_len else rp
        for s in range(SUBLANES):
            span = n_out + (CONV_WIDTH - 1 - s) // SUBLANES * SUBLANES
            shifted[part, s, 0:span, :] = up_ext[pl.ds(first_row + s, span), :]
        z = dwb_ref[...]
        for j in range(CONV_WIDTH):
            s, m = j % SUBLANES, j // SUBLANES
            z = z + dww_ref[j:j + 1, :] * shifted[part, s, m * SUBLANES:m * SUBLANES + n_out, :]
        if seg_len:
            z = jnp.concatenate([z[b * stride:b * stride + seg_len] for b in range(n_seq)], axis=0)
        mu = jnp.mean(z, axis=-1, keepdims=True)
        zc = z - mu
        var = jnp.mean(zc * zc, axis=-1, keepdims=True)
        zh = zc * lax.rsqrt(var + LN_EPS) * clnw_ref[...] + clnb_ref[...]
        yb_out[0, rs, :] = zh * (jnp.tanh(zh) + 1.0)
        yield

    _interleave(stream, n_parts)

    xn_end, h_end = last_rows[n_parts - 1]
    if seg_len:
        xn_tile = last_rows['xn']
        for b in range(n_seq):
            conv_out[b] = up_ext[(b + 1) * stride - CONV_CARRY:(b + 1) * stride, :]
            shift_out[b] = xn_tile[(b + 1) * seg_len - 1:(b + 1) * seg_len]
    else:
        tail = up_ext[pl.ds(tm + CARRY_PAD - CONV_CARRY, CONV_CARRY), :]
        up_ext[CARRY_PAD - CONV_CARRY:CARRY_PAD, :] = tail
        conv_out[0] = tail
        xn_last[...] = xn_end
        h_last[...] = h_end
        shift_out[0] = xn_end


def _mix_in(x, shift0, conv0, wd, tm, n_parts, whole_sequences):
    bsz, t, d = x.shape
    c = wd['w0'].shape[1]
    weights = [wd[n] for n in ('norm_mix', 'w_in', 'mu_rkv', 'w0', 'a0', 'w_lora1', 'w_lora2', 'g2',
                               'k_k', 'k_a', 'r_k', 'seg', 'dw_w', 'dw_b', 'cln_w', 'cln_b')]
    d_w, d_a = wd['lora_dims']
    if whole_sequences:
        tm, seg_len, n_seq, rows_b = bsz * t, t, bsz, 1
        assert t & (t - 1) == 0 and n_parts == 1
        x = x.reshape(1, tm, d)
        shift0 = jnp.repeat(shift0.reshape(bsz, d), t, axis=0).reshape(1, tm, d)
        grid = (1, 1)
        state = lambda rows, w: pl.BlockSpec((n_seq, rows, w), lambda b, i: (0, 0, 0))
        shift_in = pl.BlockSpec((1, tm, d), lambda b, i: (0, 0, 0))
        window_rows = n_seq * (CARRY_PAD + seg_len)
    else:
        seg_len, rows_b = 0, bsz
        grid = (bsz, t // tm)
        state = lambda rows, w: pl.BlockSpec((1, rows, w), lambda b, i: (b, 0, 0))
        shift_in = state(1, d)
        window_rows = tm + CARRY_PAD
    tok = lambda w: pl.BlockSpec((1, tm, w), lambda b, i: (b, i, 0))
    out_tok = jax.ShapeDtypeStruct((rows_b, x.shape[1], c), F32)
    outs = pl.pallas_call(
        functools.partial(_mix_in_kernel, tm=tm, c=c, n_parts=n_parts, d_w=d_w, d_a=d_a, seg_len=seg_len),
        grid=grid,
        in_specs=[tok(d), shift_in, state(CONV_CARRY, c)] + [_full(w.shape) for w in weights],
        out_specs=[tok(c)] * 9 + [state(1, d), state(CONV_CARRY, c)],
        out_shape=[out_tok] * 9 + [jax.ShapeDtypeStruct((bsz, 1, d), F32),
                                   jax.ShapeDtypeStruct((bsz, CONV_CARRY, c), F32)],
        scratch_shapes=[pltpu.VMEM((1, d), F32), pltpu.VMEM((1, 3 * c), F32),
                        pltpu.VMEM((window_rows, c), F32),
                        pltpu.VMEM((n_parts, SUBLANES, (window_rows - CARRY_PAD) // n_parts + CARRY_PAD - SUBLANES, c),
                                   F32)],
        compiler_params=pltpu.CompilerParams(dimension_semantics=("arbitrary", "arbitrary"),
                                             vmem_limit_bytes=VMEM_LIMIT),
        name="mix_in",
    )(x, shift0, conv0, *weights)
    return [o.reshape(bsz, t, c) for o in outs[:9]] + list(outs[9:])


def _tri_inverse(n_strict, row, col, lg_chunk):
    lg_base = INV_BASE.bit_length() - 1
    same = lambda sh: (row >> sh) == (col >> sh)
    lg0 = min(lg_base, lg_chunk)
    n = row.shape[0]

    def expand(c, lg):
        return jnp.where(same(lg), jnp.concatenate([c] * (n >> lg), axis=0), 0.0).astype(BF16)

    def fold(x, lg):
        b = 1 << lg
        return functools.reduce(lambda u, v: u + v, [x[i:i + b] for i in range(0, n, b)])

    b0 = 1 << lg0
    row_c = lax.broadcasted_iota(I32, (b0, n), 0)
    col_c = lax.broadcasted_iota(I32, (b0, n), 1)
    p_full = [jnp.where(same(lg0), x, 0.0) for x in n_strict]
    p = [fold(x, lg0) for x in p_full]
    t = [jnp.where((col_c & (b0 - 1)) == row_c, 1.0, 0.0) + x for x in p]
    p_full = [x.astype(BF16) for x in p_full]
    for _ in range(lg0 - 1):
        p = [_dg(x.astype(BF16), y) for x, y in zip(p, p_full)]
        p_full = [expand(x, lg0) for x in p]
        t = [x + _dg(x.astype(BF16), y) for x, y in zip(t, p_full)]
    for lg in range(lg0, lg_chunk):
        off_mask = same(lg + 1) & jnp.logical_not(same(lg))
        t_full = [expand(x, lg) for x in t]
        u = [_dg(x.astype(BF16), jnp.where(off_mask, m, 0.0).astype(BF16)) for x, m in zip(t, n_strict)]
        add = [_dg(x.astype(BF16), y) for x, y in zip(u, t_full)]
        even = ((lax.broadcasted_iota(I32, (1 << lg, n), 1) >> lg) & 1) == 0
        t = [jnp.concatenate([jnp.where(even, x, 0.0), jnp.where(even, a, x)], axis=0) for x, a in zip(t, add)]
    return [expand(x, lg_chunk) for x in t]


def _wkv_kernel(r_ref, lw_ref, k_ref, v_ref, a_ref, b_ref, s0_ref, y_ref, s_out, s_scr,
                *, tt, n_sub, chunk, chained):
    ti = pl.program_id(1)
    n_heads = r_ref.shape[2] // HEAD_SIZE
    n_chunks = tt // chunk
    lg_chunk = chunk.bit_length() - 1
    row = lax.broadcasted_iota(I32, (tt, tt), 0)
    col = lax.broadcasted_iota(I32, (tt, tt), 1)
    in_chunk = (row >> lg_chunk) == (col >> lg_chunk)
    tri_incl = in_chunk & (col <= row)
    tri_strict = in_chunk & (col < row)
    m_cum = jnp.where(tri_incl, 1.0, 0.0).astype(BF16)
    row_h = lax.broadcasted_iota(I32, (HEAD_SIZE, HEAD_SIZE), 0)
    col_h = lax.broadcasted_iota(I32, (HEAD_SIZE, HEAD_SIZE), 1)
    eye_h = row_h == col_h

    if chained:
        @pl.when(ti == 0)
        def _():
            s_scr[...] = s0_ref[0]

    rt_all, g_end_all, cols = [], [], {name: [] for name in ('v', 'rt', 'at', 'kt', 'bt', 'bd', 'kd')}
    hsl = [slice(HEAD_SIZE * h, HEAD_SIZE * (h + 1)) for h in range(n_heads)]
    for sub in range(n_sub):
        rows = pl.ds(sub * tt, tt)
        lw_all = lw_ref[0, rows, :]
        k_all = k_ref[0, rows, :]
        b_all = b_ref[0, rows, :]
        cum = _mask_dot(m_cum, lw_all)
        tot = jnp.concatenate([jnp.broadcast_to(cum[(ci + 1) * chunk - 1:(ci + 1) * chunk], (chunk, cum.shape[1]))
                               for ci in range(n_chunks)], axis=0)
        e_neg = jnp.exp(-cum)
        e_end = jnp.exp(tot - cum)
        full = dict(v=v_ref[0, rows, :], rt=r_ref[0, rows, :] * jnp.exp(cum),
                    at=a_ref[0, rows, :] * jnp.exp(cum - lw_all), kt=k_all * e_neg, bt=b_all * e_neg,
                    bd=b_all * e_end, kd=k_all * e_end)
        rt_all.append(full['rt'])
        g_end_all.append(jnp.exp(tot))
        for name, z in full.items():
            cols[name] += [z[:, s_].astype(BF16) for s_ in hsl]
    v, rt, at, kt, bt, bd, kd = (cols[name] for name in ('v', 'rt', 'at', 'kt', 'bt', 'bd', 'kd'))

    units = range(n_sub * n_heads)
    mm = [_dg(jnp.concatenate([at[u], rt[u]], axis=0), jnp.concatenate([bt[u], kt[u]], axis=0), NT) for u in units]
    m_ab = [jnp.where(tri_strict, mm[u][:tt, :tt], 0.0) for u in units]
    m_ak = [jnp.where(tri_strict, mm[u][:tt, tt:], 0.0).astype(BF16) for u in units]
    m_rb = [jnp.where(tri_incl, mm[u][tt:, :tt], 0.0).astype(BF16) for u in units]
    m_rk = [jnp.where(tri_incl, mm[u][tt:, tt:], 0.0).astype(BF16) for u in units]
    tinv = _tri_inverse(m_ab, row, col, lg_chunk)
    akv = [_dg(m_ak[u], v[u]).astype(BF16) for u in units]
    w1 = [_dg(tinv[u], at[u]).astype(BF16) for u in units]
    w2 = [_dg(tinv[u], akv[u]).astype(BF16) for u in units]
    q = [(rt_all[u // n_heads][:, hsl[u % n_heads]] + _dg(m_rb[u], w1[u])).astype(BF16) for u in units]
    y0 = [_dg(jnp.concatenate([m_rb[u], m_rk[u]], axis=1), jnp.concatenate([w2[u], v[u]], axis=0)) for u in units]

    heads = range(n_heads)
    if chained:
        s = [s_scr[h] for h in heads]
    for sub in range(n_sub):
        for ci in range(n_chunks):
            cs = slice(ci * chunk, (ci + 1) * chunk)
            seq = sub * n_chunks + ci
            if not chained:
                s = [s0_ref[seq, h] for h in heads]
            g_row = g_end_all[sub][ci * chunk:ci * chunk + 1]
            un = [sub * n_heads + h for h in heads]
            gm = [jnp.where(eye_h, jnp.broadcast_to(g_row[:, hsl[h]], (HEAD_SIZE, HEAD_SIZE)), 0.0)
                  + _dg(bd[un[h]][cs], w1[un[h]][cs], TN) for h in heads]
            hm = [_dg(jnp.concatenate([bd[un[h]][cs], kd[un[h]][cs]], axis=0),
                      jnp.concatenate([w2[un[h]][cs], v[un[h]][cs]], axis=0), TN) for h in heads]
            for h in heads:
                y_ref[0, pl.ds(sub * tt + ci * chunk, chunk), hsl[h]] = _bdot(q[un[h]][cs], s[h]) + y0[un[h]][cs]
            s = [_bdot(gm[h], s[h]) + hm[h] for h in heads]
            if not chained:
                for h in heads:
                    s_out[seq, h] = s[h]
    if chained:
        for h in heads:
            s_scr[h] = s[h]
            s_out[0, h] = s[h]


def _wkv(r, lw, k, v, a, b, s0t, tt, n_sub, chunk, chained):
    bsz, t, c = r.shape
    n_heads = c // HEAD_SIZE
    hs = HEAD_SIZE
    step = tt * n_sub
    if chained:
        grid = (bsz, t // step)
        tok = pl.BlockSpec((1, step, c), lambda bi, ti: (bi, ti, 0))
        st = pl.BlockSpec((1, n_heads, hs, hs), lambda bi, ti: (bi, 0, 0, 0))
        args = (r, lw, k, v, a, b)
        y_shape = (bsz, t, c)
    else:
        assert t == chunk and (bsz * t) % step == 0
        n_seq = step // chunk
        grid = (1, bsz * t // step)
        tok = pl.BlockSpec((1, step, c), lambda bi, ti: (0, ti, 0))
        st = pl.BlockSpec((n_seq, n_heads, hs, hs), lambda bi, ti: (ti, 0, 0, 0))
        args = tuple(z.reshape(1, bsz * t, c) for z in (r, lw, k, v, a, b))
        y_shape = (1, bsz * t, c)
    y, s_new = pl.pallas_call(
        functools.partial(_wkv_kernel, tt=tt, n_sub=n_sub, chunk=chunk, chained=chained),
        grid=grid,
        in_specs=[tok] * 6 + [st],
        out_specs=[tok, st],
        out_shape=[jax.ShapeDtypeStruct(y_shape, F32), jax.ShapeDtypeStruct(s0t.shape, F32)],
        scratch_shapes=[pltpu.VMEM((n_heads, hs, hs), F32)],
        compiler_params=pltpu.CompilerParams(dimension_semantics=("arbitrary",) * 2,
                                             vmem_limit_bytes=VMEM_LIMIT),
        name="wkv",
    )(*args, s0t)
    return y.reshape(bsz, t, c), s_new


def _interleave(make_stream, n_parts):
    for _ in zip(*[make_stream(part) for part in range(n_parts)]):
        pass


def _mix_out_kernel(x_ref, y_ref, bonus_ref, g_ref, yb_ref, lnw_ref, lnb_ref, seg_ref, wout_ref, nffn_ref,
                    wr_ref, h1_out, xn_out, eid_out, eidt_out, gate_out, *, c, n_parts):
    rows_per = x_ref.shape[0] // n_parts

    def stream(part):
        rs = pl.ds(part * rows_per, rows_per)
        seg = seg_ref[...]
        y = y_ref[rs, :]
        inv_n = 1.0 / HEAD_SIZE
        mu = _seg_sum(y, seg) * inv_n
        yield
        yc = y - mu
        var = _seg_sum(yc * yc, seg) * inv_n
        yield
        yn = yc * lax.rsqrt(var + GN_EPS) * lnw_ref[...] + lnb_ref[...]
        ya = (yn + bonus_ref[rs, :]) * g_ref[rs, :]
        mix = _dg(ya.astype(BF16), wout_ref[:c, :]) + _dg(yb_ref[rs, :].astype(BF16), wout_ref[c:, :])
        yield
        h1 = x_ref[rs, :] + mix
        h1_out[rs, :] = h1
        xn = _rms(h1, nffn_ref[...])
        xn_out[rs, :] = xn.astype(BF16)
        xh, xl = _split2(xn)
        hi_lo = _dg(xh, wr_ref[...])
        logits = hi_lo[:, :ROUTER_LANES] + (hi_lo[:, ROUTER_LANES:] + _dg(xl, wr_ref[:, :ROUTER_LANES]))
        yield
        lane = lax.broadcasted_iota(I32, logits.shape, 1)
        neg = jnp.float32(-jnp.inf)
        is_g = (lane >= N_EXPERTS) & (lane < N_EXPERTS + N_EXPERT_GROUPS)
        glog = jnp.where(is_g, logits, neg)
        gmax = jnp.max(glog, axis=-1, keepdims=True)
        gsel = jnp.min(jnp.where(glog == gmax, lane, 4 * ROUTER_LANES), axis=-1, keepdims=True) - N_EXPERTS
        gp = 1.0 / jnp.sum(jnp.where(is_g, jnp.exp(glog - gmax), 0.0), axis=-1, keepdims=True)
        in_grp = (lane >= gsel * EXPERTS_PER_GROUP) & (lane < (gsel + 1) * EXPERTS_PER_GROUP)
        elog = jnp.where(in_grp, logits, neg)
        emax = jnp.max(elog, axis=-1, keepdims=True)
        ex = jnp.where(in_grp, jnp.exp(elog - emax), 0.0)
        eprob = ex / jnp.sum(ex, axis=-1, keepdims=True)
        eprob = jnp.where(in_grp, eprob, -1.0)
        yield
        v1 = jnp.max(eprob, axis=-1, keepdims=True)
        i1 = jnp.min(jnp.where(eprob == v1, lane, 4 * ROUTER_LANES), axis=-1, keepdims=True)
        rest = jnp.where(lane == i1, -1.0, eprob)
        v2 = jnp.max(rest, axis=-1, keepdims=True)
        i2 = jnp.min(jnp.where(rest == v2, lane, 4 * ROUTER_LANES), axis=-1, keepdims=True)
        denom = v1 + v2
        eid = jnp.where(lane == 0, i1, jnp.where(lane == 1, i2, 0))
        eid_out[rs, :] = eid
        eidt_out[:, rs] = jnp.transpose(eid)[:SUBLANES]
        gate_out[rs, :] =jnp.where(lane == 0, gp * v1 / denom, jnp.where(lane == 1, gp * v2 / denom, 0.0))
        yield

    _interleave(stream, n_parts)


def _mix_out(x2, y2, bonus2, g2, yb2, wd, tm, n_parts):
    m, d = x2.shape
    c = y2.shape[1]
    tokd = pl.BlockSpec((tm, d), lambda i: (i, 0))
    tokc = pl.BlockSpec((tm, c), lambda i: (i, 0))
    tokr = pl.BlockSpec((tm, ROUTER_LANES), lambda i: (i, 0))
    weights = [wd[n] for n in ('ln_x_w', 'ln_x_b', 'seg', 'w_out', 'norm_ffn', 'w_router')]
    return pl.pallas_call(
        functools.partial(_mix_out_kernel, c=c, n_parts=n_parts),
        grid=(m // tm,),
        in_specs=[tokd, tokc, tokc, tokc, tokc] + [_full(w.shape) for w in weights],
        out_specs=[tokd, tokd, tokr, pl.BlockSpec((SUBLANES, tm), lambda i: (0, i)), tokr],
        out_shape=[jax.ShapeDtypeStruct((m, d), F32), jax.ShapeDtypeStruct((m, d), BF16),
                   jax.ShapeDtypeStruct((m, ROUTER_LANES), I32), jax.ShapeDtypeStruct((SUBLANES, m), I32),
                   jax.ShapeDtypeStruct((m, ROUTER_LANES), F32)],
        compiler_params=pltpu.CompilerParams(dimension_semantics=("arbitrary",), vmem_limit_bytes=VMEM_LIMIT),
        name="mix_out",
    )(x2, y2, bonus2, g2, yb2, *weights)


def _pow2_pieces(count, max_rows, fn):
    off = 0
    rows = max_rows
    while rows >= RUN_ALIGN:
        has = (count & (rows // RUN_ALIGN)) != 0

        @pl.when(has)
        def _(off=off, rows=rows):
            fn(off, rows)
        off = off + jnp.where(has, rows, 0)
        rows //= 2


def _for_each_expert(fn):
    def body(e, carry):
        fn(e)
        return carry
    lax.fori_loop(0, N_EXPERTS, body, 0)


def _pow2_floor(n):
    return 1 << (n.bit_length() - 1)


def _dispatch_kernel(nch_ref, tot_ref, off_ref, loff_ref, tn_ref, toff_ref, nu_ref, *refs, groups, bm, n_blocks):
    n_g = len(groups)
    eid_refs, xn_refs = refs[:n_g], refs[n_g + 1:2 * n_g + 1]
    loffc_ref = refs[n_g]
    xs_out, buf, sem, zbuf, zsem = refs[2 * n_g + 1:]
    i = pl.program_id(0)
    last = pl.num_programs(0) - 1
    max_run = max(tm for tm, _ in groups)

    def start_tile(tile):
        def per_expert(e):
            src0 = loff_ref[tile * N_EXPERTS + e]
            dst0 = off_ref[tile * N_EXPERTS + e]

            def piece(o, rows):
                src = buf.at[tile % 2, pl.ds(pl.multiple_of(src0 + o, RUN_ALIGN), rows)]
                dst = xs_out.at[pl.ds(pl.multiple_of(dst0 + o, RUN_ALIGN), rows)]
                pltpu.make_async_copy(src, dst, sem.at[tile % 2]).start()
            _pow2_pieces(nch_ref[tile * N_EXPERTS + e], max_run, piece)
        _for_each_expert(per_expert)

    def wait_tile(tile):
        def piece(o, rows):
            pltpu.make_async_copy(buf.at[tile % 2, pl.ds(0, rows)], xs_out.at[pl.ds(0, rows)], sem.at[tile % 2]).wait()
        _pow2_pieces(tot_ref[tile], _pow2_floor(buf.shape[1]), piece)

    @pl.when(i >= 2)
    def _():
        wait_tile(i - 2)

    def sort_tile(eid_ref, xn_ref, tm):
        e_rows = eid_ref[...]
        sub = lax.broadcasted_iota(I32, (N_EXPERTS, tm), 0)
        e1 = jnp.where(sub == e_rows[0:1], 1.0, 0.0)
        e2 = jnp.where(sub == e_rows[1:2], 1.0, 0.0)
        before = lax.broadcasted_iota(I32, (tm, tm), 0) < lax.broadcasted_iota(I32, (tm, tm), 1)
        slot = _dg((e1 + e2).astype(BF16), jnp.where(before, 1.0, 0.0).astype(BF16)) + loffc_ref[0]
        l1 = jnp.sum(slot * e1, axis=0, keepdims=True).astype(I32)
        l2 = jnp.sum(slot * e2, axis=0, keepdims=True).astype(I32)
        n_rows = 2 * tm + LOCAL_PAD
        rows = lax.broadcasted_iota(I32, (n_rows, tm), 0)
        perm = jnp.where((rows == l1) | (rows == l2), 1.0, 0.0).astype(BF16)
        buf[i % 2, 0:n_rows, :] = _dg(perm, xn_ref[...]).astype(BF16)

    first = 0
    for g, (tm, n_tiles) in enumerate(groups):
        pl.when((i >= first) & (i < first + n_tiles))(functools.partial(sort_tile, eid_refs[g], xn_refs[g], tm))
        first += n_tiles
    start_tile(i)

    @pl.when(i == 0)
    def _():
        zbuf[...] = jnp.zeros_like(zbuf)
        half = zbuf.shape[0]

        def zero_copy(off, rows):
            return pltpu.make_async_copy(zbuf.at[pl.ds(0, rows)], xs_out.at[pl.ds(off, rows)], zsem)

        def tails(fn):
            _for_each_expert(lambda e: _pow2_pieces(
                tn_ref[e], half, lambda o, rows: fn(pl.multiple_of(toff_ref[e] + o, RUN_ALIGN), rows)))

        def unused_blocks(fn):
            def body(b, carry):
                fn(pl.multiple_of(b * bm, bm), half)
                fn(pl.multiple_of(b * bm + half, half), half)
                return carry
            lax.fori_loop(nu_ref[0], n_blocks, body, 0)

        tails(lambda off, rows: zero_copy(off, rows).start())
        unused_blocks(lambda off, rows: zero_copy(off, rows).start())
        tails(lambda off, rows: zero_copy(off, rows).wait())
        unused_blocks(lambda off, rows: zero_copy(off, rows).wait())

    @pl.when(i == last)
    def _():
        @pl.when(i >= 1)
        def _():
            wait_tile(i - 1)
        wait_tile(i)


def _dispatch(plan, n_used, eids_t, xns, tms, n_blocks, bm):
    d = xns[0].shape[1]
    groups = tuple((tm, xn.shape[0] // tm) for xn, tm in zip(xns, tms))
    firsts = [sum(n for _, n in groups[:g]) for g in range(len(groups))]
    lbuf = 2 * max(tms) + LOCAL_PAD

    def tile_of(g):
        return lambda i: jnp.clip(i - firsts[g], 0, groups[g][1] - 1)

    imap = lambda f: (lambda i, *_: f(i))
    in_specs = ([pl.BlockSpec((SUBLANES, tm), imap(lambda i, g=g: (0, tile_of(g)(i)))) for g, tm in enumerate(tms)] +
                [pl.BlockSpec((1, N_EXPERTS, 1), imap(lambda i: (i, 0, 0)))] +
                [pl.BlockSpec((tm, d), imap(lambda i, g=g: (tile_of(g)(i), 0))) for g, tm in enumerate(tms)])
    return pl.pallas_call(
        functools.partial(_dispatch_kernel, groups=groups, bm=bm, n_blocks=n_blocks),
        grid_spec=pltpu.PrefetchScalarGridSpec(
            num_scalar_prefetch=7,
            grid=(sum(n for _, n in groups),),
            in_specs=in_specs,
            out_specs=pl.BlockSpec(memory_space=pl.ANY),
            scratch_shapes=[pltpu.VMEM((2, lbuf, d), BF16), pltpu.SemaphoreType.DMA((2,)),
                            pltpu.VMEM((bm // 2, d), BF16), pltpu.SemaphoreType.DMA],
        ),
        out_shape=jax.ShapeDtypeStruct((n_blocks * bm, d), BF16),
        compiler_params=pltpu.CompilerParams(dimension_semantics=("arbitrary",), vmem_limit_bytes=VMEM_LIMIT),
        name="moe_dispatch",
    )(plan['nch'], plan['tot'], plan['off'], plan['loff'], plan['tail_n'], plan['tail_off'], n_used, *eids_t,
      plan['loff_col'], *xns)


def _experts_kernel(be_ref, slot_ref, nxt_ref, nu_ref, xs_ref, wg_hbm, wu_hbm, wd_hbm, yb_ref,
                    wg_f, wu_f, wd_f, wg_b, wu_b, wd_b, sem):
    b = pl.program_id(0)

    def weight_copies(e, slot):
        pairs = ((wg_hbm, wg_f), (wu_hbm, wu_f), (wd_hbm, wd_f))
        return [pltpu.make_async_copy(src.at[e], dst.at[slot], sem.at[slot, j]) for j, (src, dst) in enumerate(pairs)]

    @pl.when(b == 0)
    def _():
        for cp in weight_copies(be_ref[0], slot_ref[0]):
            cp.start()

    @pl.when((b < nu_ref[0]) & ((b == 0) | (be_ref[b] != be_ref[jnp.maximum(b - 1, 0)])))
    def _():
        slot = slot_ref[b]
        for cp in weight_copies(be_ref[b], slot):
            cp.wait()

        @pl.when(nxt_ref[b] >= 0)
        def _():
            for cp in weight_copies(nxt_ref[b], 1 - slot):
                cp.start()
        wg_b[...] = wg_f[slot].astype(BF16)
        wu_b[...] = wu_f[slot].astype(BF16)
        wd_b[...] = wd_f[slot].astype(BF16)

    @pl.when(b < nu_ref[0])
    def _():
        xb = xs_ref[...]
        hg = _dg(xb, wg_b[...])
        hu = _dg(xb, wu_b[...])
        act = (hg * _sigmoid(hg) * hu).astype(BF16)
        yb_ref[...] = _dg(act, wd_b[...]).astype(BF16)

    @pl.when(pl.program_id(0) >= nu_ref[0])
    def _():
        yb_ref[...] = jnp.zeros_like(yb_ref)


def _experts(sched, n_used, xs, wg, wu, wdn, bm):
    p, d = xs.shape
    ff = wg.shape[2]
    n_blocks = p // bm
    return pl.pallas_call(
        _experts_kernel,
        grid_spec=pltpu.PrefetchScalarGridSpec(
            num_scalar_prefetch=4,
            grid=(n_blocks,),
            in_specs=[pl.BlockSpec((bm, d), lambda b, be, sl, nx, nu: (jnp.minimum(b, nu[0] - 1), 0)),
                      pl.BlockSpec(memory_space=pl.ANY), pl.BlockSpec(memory_space=pl.ANY),
                      pl.BlockSpec(memory_space=pl.ANY)],
            out_specs=pl.BlockSpec((bm, d), lambda b, *_: (b, 0)),
            scratch_shapes=[pltpu.VMEM((2, d, ff), F32), pltpu.VMEM((2, d, ff), F32), pltpu.VMEM((2, ff, d), F32),
                            pltpu.VMEM((d, ff), BF16), pltpu.VMEM((d, ff), BF16), pltpu.VMEM((ff, d), BF16),
                            pltpu.SemaphoreType.DMA((2, 3))],
        ),
        out_shape=jax.ShapeDtypeStruct((p, d), BF16),
        compiler_params=pltpu.CompilerParams(dimension_semantics=("arbitrary",), vmem_limit_bytes=VMEM_LIMIT),
        name="moe_experts",
    )(sched['expert'], sched['slot'], sched['next'], n_used, xs, wg, wu, wdn)


def _final_kernel(nch_ref, tot_ref, off_ref, loff_ref, h1_ref, eid_ref, gate_ref, loffr_ref, p_ref, yb_hbm,
                  nple_ref, wpg_ref, wpp_ref, nfin_ref, y_out, buf, sem, *, tm, lbuf, tile0, n_parts):
    i = pl.program_id(0)

    def fetch(tile):
        base = (tile0 + tile) * N_EXPERTS

        def per_expert(e):
            src0 = off_ref[base + e]
            dst0 = loff_ref[base + e]

            def piece(o, rows):
                src = yb_hbm.at[pl.ds(pl.multiple_of(src0 + o, RUN_ALIGN), rows)]
                dst = buf.at[tile % 2, pl.ds(pl.multiple_of(dst0 + o, RUN_ALIGN), rows)]
                pltpu.make_async_copy(src, dst, sem.at[tile % 2]).start()
            _pow2_pieces(nch_ref[base + e], tm, piece)
        _for_each_expert(per_expert)

    def wait_fetch(tile):
        def piece(o, rows):
            pltpu.make_async_copy(yb_hbm.at[pl.ds(0, rows)], buf.at[tile % 2, pl.ds(0, rows)], sem.at[tile % 2]).wait()
        _pow2_pieces(tot_ref[tile0 + tile], _pow2_floor(lbuf), piece)

    @pl.when(i == 0)
    def _():
        buf[...] = jnp.zeros_like(buf)
        fetch(i)

    @pl.when(i + 1 < pl.num_programs(0))
    def _():
        fetch(i + 1)

    eid = eid_ref[...]
    lane = lax.broadcasted_iota(I32, (tm, N_EXPERTS), 1)
    e12 = (jnp.where(lane == eid[:, 0:1], 1.0, 0.0) + jnp.where(lane == eid[:, 1:2], 1.0, 0.0)).astype(BF16)
    rows_per = tm // n_parts
    picks = []
    for part in range(n_parts):
        rs = pl.ds(part * rows_per, rows_per)
        lane_p = lax.broadcasted_iota(I32, (rows_per, N_EXPERTS), 1)
        eid_p = eid_ref[rs, :]
        e1 = jnp.where(lane_p == eid_p[:, 0:1], 1.0, 0.0)
        e2 = jnp.where(lane_p == eid_p[:, 1:2], 1.0, 0.0)
        before = (lax.broadcasted_iota(I32, (rows_per, tm), 1)
                  < lax.broadcasted_iota(I32, (rows_per, tm), 0) + part * rows_per)
        slot = _dg(jnp.where(before, 1.0, 0.0).astype(BF16), e12) + loffr_ref[0]
        l1 = jnp.sum(slot * e1, axis=1, keepdims=True).astype(I32)
        l2 = jnp.sum(slot * e2, axis=1, keepdims=True).astype(I32)
        cols = lax.broadcasted_iota(I32, (rows_per, lbuf), 1)
        gate = gate_ref[rs, :]
        picks.append(jnp.where(cols == l1, gate[:, 0:1], jnp.where(cols == l2, gate[:, 1:2], 0.0)).astype(BF16))

    wait_fetch(i)
    sorted_rows = buf[i % 2]

    def stream(part):
        rs = pl.ds(part * rows_per, rows_per)
        h2 = h1_ref[rs, :] + _dg(picks[part], sorted_rows)
        yield
        gate_in = _rms(h2, nple_ref[...]).astype(BF16)
        pg = _sigmoid(_dg(gate_in, wpg_ref[...]))
        yield
        h3 = h2 + pg * _dg(p_ref[rs, :].astype(BF16), wpp_ref[...])
        y_out[rs, :] = _rms(h3, nfin_ref[...])
        yield

    _interleave(stream, n_parts)


def _final(plan, h1, eid, gate, p2, yb, wd, tm, tile0):
    m, d = h1.shape
    pd = p2.shape[1]
    lbuf = 2 * tm + LOCAL_PAD
    weights = [wd[n] for n in ('norm_ple', 'w_ple_gate', 'w_ple_proj', 'norm_final')]
    imap = lambda f: (lambda i, *_: f(i))
    return pl.pallas_call(
        functools.partial(_final_kernel, tm=tm, lbuf=lbuf, tile0=tile0, n_parts=TOKEN_PARTS),
        grid_spec=pltpu.PrefetchScalarGridSpec(
            num_scalar_prefetch=4,
            grid=(m // tm,),
            in_specs=[pl.BlockSpec((tm, d), imap(lambda i: (i, 0))),
                      pl.BlockSpec((tm, ROUTER_LANES), imap(lambda i: (i, 0))),
                      pl.BlockSpec((tm, ROUTER_LANES), imap(lambda i: (i, 0))),
                      pl.BlockSpec((1, 1, N_EXPERTS), imap(lambda i: (tile0 + i, 0, 0))),
                      pl.BlockSpec((tm, pd), imap(lambda i: (i, 0))),
                      pl.BlockSpec(memory_space=pl.ANY)] +
                     [pl.BlockSpec(w.shape, imap(lambda i, n=len(w.shape): (0,) * n)) for w in weights],
            out_specs=pl.BlockSpec((tm, d), imap(lambda i: (i, 0))),
            scratch_shapes=[pltpu.VMEM((2, lbuf, d), BF16), pltpu.SemaphoreType.DMA((2,))],
        ),
        out_shape=jax.ShapeDtypeStruct((m, d), F32),
        compiler_params=pltpu.CompilerParams(dimension_semantics=("arbitrary",), vmem_limit_bytes=VMEM_LIMIT),
        name="moe_final",
    )(plan['nch'], plan['tot'], plan['off'], plan['loff'], h1, eid, gate, plan['loff_row'], p2, yb, *weights)


def _route_plan(eids, tms, bm):
    experts = jnp.arange(N_EXPERTS, dtype=I32)
    counts = []
    for eid, tm in zip(eids, tms):
        onehot = (eid[:2, :, None] == experts).astype(I32)
        counts.append(onehot.reshape(2, -1, tm, N_EXPERTS).sum(axis=(0, 2)))
    n = jnp.concatenate(counts)
    n_al = (n + RUN_ALIGN - 1) // RUN_ALIGN * RUN_ALIGN
    loff = jnp.cumsum(n_al, axis=1) - n_al
    used = n_al.sum(axis=0)
    region = (used + bm - 1) // bm * bm
    pend = jnp.cumsum(region)
    off = (pend - region)[None, :] + jnp.cumsum(n_al, axis=0) - n_al
    n_assign = sum(2 * e.shape[1] for e in eids)
    n_blocks = -(-(n_assign + (RUN_ALIGN - 1) * N_EXPERTS * n.shape[0] + N_EXPERTS * (bm - 1)) // bm)
    block_start = jnp.arange(n_blocks, dtype=I32) * bm
    block_expert = jnp.minimum(jnp.sum((pend[None, :] <= block_start[:, None]).astype(I32), axis=1), N_EXPERTS - 1)
    plan = dict(nch=(n_al // RUN_ALIGN).reshape(-1).astype(I32), off=off.reshape(-1).astype(I32),
                tot=(n_al.sum(axis=1) // RUN_ALIGN).astype(I32),
                loff=loff.reshape(-1).astype(I32), tail_n=((region - used) // RUN_ALIGN).astype(I32),
                tail_off=(pend - region + used).astype(I32), loff_col=loff.astype(F32)[:, :, None],
                loff_row=loff.astype(F32)[:, None, :])
    n_used = pend[-1] // bm
    prev = jnp.concatenate([jnp.full((1,), -1, I32), block_expert[:-1].astype(I32)])
    slot = (jnp.cumsum((block_expert != prev).astype(I32)) - 1) % 2
    after = pend[block_expert] // bm
    nxt = jnp.where(after < n_used, block_expert[jnp.minimum(after, n_blocks - 1)], -1)
    sched = dict(expert=block_expert.astype(I32), slot=slot.astype(I32), next=nxt.astype(I32))
    return plan, sched, n_used.astype(I32).reshape(1), n_blocks


def _tile_plan(bsz, t, long_sequence):
    rows = bsz * t
    span = t if long_sequence else rows
    wkv_sub = min(WKV_SUBTILE, span)
    return dict(
        mix_in=min(TOKEN_TILE, t), mix_in_parts=MIX_IN_PARTS if (long_sequence and t >= TOKEN_TILE) else 1,
        wkv_sub=wkv_sub, wkv_n_sub=min(WKV_SUBTILES, span // wkv_sub),
        wkv_chunk=min(WKV_CHUNK, t),
        token=min(TOKEN_TILE, rows), token_parts=TOKEN_PARTS)


def _layer_front(x, shift0, wkv0, conv0, wd, tiles, long_sequence):
    bsz, t, d = x.shape
    r, lw, k2, v, ah, bh, g, bonus, yb, shift_new, conv_new = _mix_in(
        x, shift0, conv0, wd, tiles['mix_in'], tiles['mix_in_parts'], not long_sequence)
    y, s_new = _wkv(r, lw, k2, v, ah, bh, jnp.swapaxes(wkv0, -1, -2), tiles['wkv_sub'], tiles['wkv_n_sub'],
                    tiles['wkv_chunk'], long_sequence)
    flat = lambda z: z.reshape(bsz * t, z.shape[-1])
    h1, xn2, eid, eid_t, gate = _mix_out(flat(x), flat(y), flat(bonus), flat(g), flat(yb), wd, tiles['token'],
                                         tiles['token_parts'])
    return h1, xn2, (eid, eid_t), gate, shift_new.reshape(bsz, d), jnp.swapaxes(s_new, -1, -2), conv_new


def kernel(x_prompt, x_sample, state_shift, state_wkv, cache_conv, p_prompt, p_sample, norm_mix, w_in, mu_rkv, mu_w, mu_a, mu_g, w0, w1, w2, a0, a1, a2, g1, g2, k_k, k_a, r_k, ln_x_w, ln_x_b, dw_w, dw_b, cln_w, cln_b, w_out, norm_ffn, w_router_group, w_router_expert, w_exp_gate, w_exp_up, w_exp_down, norm_ple, w_ple_gate, w_ple_proj, norm_final):
    depth = norm_mix.shape[0]
    assert depth == 1
    d = x_prompt.shape[-1]
    c = w0.shape[-1]
    row = lambda z: z[0].reshape(1, -1).astype(F32)
    lane = jnp.arange(LANES, dtype=I32) // HEAD_SIZE
    w_router = jnp.concatenate([w_router_expert[0], w_router_group[0],
                                jnp.zeros((d, ROUTER_LANES - N_EXPERTS - N_EXPERT_GROUPS), F32)], axis=1)
    first = jnp.concatenate([w1[0], a1[0], g1[0]], axis=1)
    mixed = jnp.concatenate([mu_w[0][:, None] * w1[0], mu_a[0][:, None] * a1[0], mu_g[0][:, None] * g1[0]], axis=1)
    d_w, d_a = w1.shape[2], a1.shape[2]
    second = jnp.concatenate([jnp.concatenate([w2[0], jnp.zeros((d_w, c), F32)], axis=1),
                              jnp.concatenate([jnp.zeros((d_a, c), F32), a2[0]], axis=1)], axis=0)
    glu_half = jnp.concatenate([jnp.ones((3 * c,), F32), jnp.full((w_in.shape[2] - 3 * c,), 0.5, F32)])
    wd = dict(
        norm_mix=row(norm_mix), w_in=(w_in[0] * glu_half).astype(BF16), mu_rkv=row(mu_rkv),
        w0=0.5 * row(w0), a0=0.5 * row(a0),
        w_lora1=jnp.concatenate([first, mixed], axis=0).astype(BF16), w_lora2=(0.5 * second).astype(BF16),
        g2=g2[0].astype(BF16), lora_dims=(d_w, d_a),
        k_k=row(k_k), k_a=row(k_a), r_k=row(r_k), ln_x_w=row(ln_x_w), ln_x_b=row(ln_x_b),
        seg=(lane[:, None] == lane[None, :]).astype(BF16),
        dw_w=dw_w[0].astype(F32), dw_b=row(dw_b), cln_w=0.5 * row(cln_w), cln_b=0.5 * row(cln_b),
        w_out=w_out[0].astype(BF16), norm_ffn=row(norm_ffn),
        w_router=jnp.concatenate(_split2(w_router), axis=1),
        norm_ple=row(norm_ple), w_ple_gate=w_ple_gate[0].astype(BF16), w_ple_proj=w_ple_proj[0].astype(BF16),
        norm_final=norm_final.reshape(1, -1).astype(F32),
    )
    bp, tp, _ = x_prompt.shape
    bs, ts, _ = x_sample.shape
    mp, ms = bp * tp, bs * ts

    zeros = lambda *s: jnp.zeros(s, F32)
    h1_p, xn_p, eid_p, gate_p, shift_p, wkv_p, conv_p = _layer_front(
        x_prompt, zeros(bp, 1, d), zeros(bp, c // HEAD_SIZE, HEAD_SIZE, HEAD_SIZE), zeros(bp, CONV_CARRY, c),
        wd, _tile_plan(bp, tp, True), True)
    h1_s, xn_s, eid_s, gate_s, shift_s, wkv_s, conv_s = _layer_front(
        x_sample, state_shift[0][:, None, :], state_wkv[0], cache_conv[0],
        wd, _tile_plan(bs, ts, False), False)

    bm = EXPERT_BLOCK
    tr_p, tr_s = min(TOKEN_TILE, mp), min(TOKEN_TILE, ms)
    plan, sched, n_used, n_blocks = _route_plan([eid_p[1], eid_s[1]], [tr_p, tr_s], bm)
    tiles_p = mp // tr_p
    xs = _dispatch(plan, n_used, [eid_p[1], eid_s[1]], [xn_p, xn_s], [tr_p, tr_s], n_blocks, bm)
    yb = _experts(sched, n_used, xs, w_exp_gate[0], w_exp_up[0], w_exp_down[0], bm)
    y_p = _final(plan, h1_p, eid_p[0], gate_p, p_prompt[0].reshape(mp, -1), yb, wd, tr_p, 0)
    y_s = _final(plan, h1_s, eid_s[0], gate_s, p_sample[0].reshape(ms, -1), yb, wd, tr_s, tiles_p)
    return (y_p.reshape(x_prompt.shape), y_s.reshape(x_sample.shape), shift_p[None], wkv_p[None], conv_p[None],
            shift_s[None], wkv_s[None], conv_s[None])
```

```python
import functools

import jax
import jax.numpy as jnp
from jax import lax
from jax.experimental import pallas as pl
from jax.experimental.pallas import tpu as pltpu

F32 = jnp.float32
BF16 = jnp.bfloat16
I32 = jnp.int32

HEAD_SIZE = 64
CONV_WIDTH = 31
CONV_CARRY = CONV_WIDTH - 1
SUBLANES = 8
LANES = 128
CARRY_PAD = 32
N_EXPERT_GROUPS = 4
EXPERTS_PER_GROUP = 8
N_EXPERTS = N_EXPERT_GROUPS * EXPERTS_PER_GROUP
ROUTER_LANES = 128
RMS_EPS = 1e-6
LN_EPS = 1e-5
GN_EPS = 64e-5
DECAY_SCALE = 0.6065306597126334
INV_BASE = 16
RUN_ALIGN = 16
LOCAL_PAD = N_EXPERTS * RUN_ALIGN
VMEM_LIMIT = 56 * 1024 * 1024

TOKEN_TILE = 512
MIX_IN_PARTS = 4
TOKEN_PARTS = 2
WKV_SUBTILE = 128
WKV_SUBTILES = 2
WKV_CHUNK = 64
EXPERT_BLOCK = 512

NN = ((1,), (0,))
NT = ((1,), (1,))
TN = ((0,), (0,))


def _dg(a, b, dims=NN):
    return lax.dot_general(a, b, (dims, ((), ())), preferred_element_type=F32)


def _split2(x):
    hi = x.astype(BF16)
    lo = (x - hi.astype(F32)).astype(BF16)
    return hi, lo


def _bdot(a, b, dims=NN):
    return _dg(a.astype(BF16), b.astype(BF16), dims)


def _mask_dot(mask_bf16, x):
    h, l = _split2(x)
    return _dg(mask_bf16, h) + _dg(mask_bf16, l)


def _seg_sum(x, seg_bf16):
    h, l = _split2(x)
    w = seg_bf16.shape[0]
    return jnp.concatenate([_dg(h[:, j:j + w], seg_bf16) + _dg(l[:, j:j + w], seg_bf16)
                            for j in range(0, x.shape[1], w)], axis=1)


def _rms(x, g):
    return x * lax.rsqrt(jnp.mean(x * x, axis=-1, keepdims=True) + RMS_EPS) * g


def _sigmoid(x):
    return 0.5 * jnp.tanh(0.5 * x) + 0.5


def _full(shape):
    n = len(shape)
    return pl.BlockSpec(shape, lambda *_: (0,) * n, pipeline_mode=pl.Buffered(1))


def _mix_in_kernel(x_ref, shift_ref, conv_ref, nm_ref, win_ref, murkv_ref, w0_ref, a0_ref, wl1_ref, wl2_ref, g2_ref,
                   kk_ref, ka_ref, rk_ref, seg_ref, dww_ref, dwb_ref, clnw_ref, clnb_ref,
                   r_out, lw_out, k_out, v_out, a_out, b_out, g_out, bonus_out, yb_out, shift_out, conv_out,
                   xn_last, h_last, up_ext, shifted, *, tm, c, n_parts, d_w, d_a, seg_len):
    i = pl.program_id(1)
    n_seq = tm // seg_len if seg_len else 1
    stride = CARRY_PAD + seg_len

    if seg_len:
        for b in range(n_seq):
            up_ext[b * stride:b * stride + CARRY_PAD - CONV_CARRY, :] = jnp.zeros((CARRY_PAD - CONV_CARRY, c), F32)
            up_ext[b * stride + CARRY_PAD - CONV_CARRY:b * stride + CARRY_PAD, :] = conv_ref[b]
    else:
        @pl.when(i == 0)
        def _():
            sp = shift_ref[0]
            xn_last[...] = sp
            sp8 = jnp.broadcast_to(sp, (8, sp.shape[1])).astype(BF16)
            h_last[...] = _dg(sp8, win_ref[:, :3 * c])[0:1]
            up_ext[CARRY_PAD - CONV_CARRY:CARRY_PAD, :] = conv_ref[0]

    rp = tm // n_parts
    last_rows = {}

    def stream(part):
        rs = pl.ds(part * rp, rp)
        xn = _rms(x_ref[0, rs, :], nm_ref[...])
        hin = _dg(xn.astype(BF16), win_ref[...])
        h_rkv = hin[:, :3 * c]
        last_rows[part] = (xn[rp - 1:rp], h_rkv[rp - 1:rp])
        last_rows['xn'] = xn
        yield
        if seg_len:
            xn_prev = shift_ref[0]
            h_prev = _dg(xn_prev.astype(BF16), win_ref[:, :3 * c])
            first = (lax.broadcasted_iota(I32, (rp, 1), 0) & (seg_len - 1)) == 0
        else:
            xn_prev, h_prev = (xn_last[...], h_last[...]) if part == 0 else last_rows[part - 1]
            first = lax.broadcasted_iota(I32, (rp, 1), 0) == 0
        dx = jnp.where(first, xn_prev, pltpu.roll(xn, 1, 0)) - xn
        hprev = jnp.where(first, h_prev, pltpu.roll(h_rkv, 1, 0))
        rkv = h_rkv + (hprev - h_rkv) * murkv_ref[...]
        r = rkv[:, :c]
        k = rkv[:, c:2 * c]
        v = rkv[:, 2 * c:]
        l1 = _dg(jnp.concatenate([xn.astype(BF16), dx.astype(BF16)], axis=1), wl1_ref[...])
        lane = lax.broadcasted_iota(I32, l1.shape, 1)
        act = jnp.where(lane < d_w, jnp.tanh(l1), jnp.where(lane < d_w + d_a, l1, _sigmoid(l1))).astype(BF16)
        yield
        za = _dg(act[:, :d_w + d_a], wl2_ref[...])
        a = 0.5 * jnp.tanh(a0_ref[...] + za[:, c:]) + 0.5
        g_out[0, rs, :] = _dg(act[:, d_w + d_a:], g2_ref[...])
        seg = seg_ref[...]
        kk = k * kk_ref[...]
        kk = kk * jnp.minimum(lax.rsqrt(_seg_sum(kk * kk, seg)), 1e12)
        k2 = k * (1.0 + (a - 1.0) * ka_ref[...])
        r_out[0, rs, :] = r
        lw_out[0, rs, :] = (-0.5 * DECAY_SCALE) * jnp.tanh(w0_ref[...] + za[:, :c]) - 0.5 * DECAY_SCALE
        k_out[0, rs, :] = k2
        v_out[0, rs, :] = v
        a_out[0, rs, :] = -kk
        b_out[0, rs, :] = kk * a
        yield
        bonus_out[0, rs, :] = _seg_sum(r * k2 * rk_ref[...], seg) * v
        u = hin[:, 3 * c:4 * c] * (jnp.tanh(hin[:, 4 * c:]) + 1.0)
        if seg_len:
            for b in range(n_seq):
                up_ext[b * stride + CARRY_PAD:(b + 1) * stride, :] = u[b * seg_len:(b + 1) * seg_len]
        else:
            up_ext[pl.ds(CARRY_PAD + part * rp, rp), :] = u
        yield
        first_row = CARRY_PAD - CONV_CARRY + part * rp
        n_out = n_seq * stride - CARRY_PAD if seg_len else rp
        for s in range(SUBLANES):
            span = n_out + (CONV_WIDTH - 1 - s) // SUBLANES * SUBLANES
            shifted[part, s, 0:span, :] = up_ext[pl.ds(first_row + s, span), :]
        z = dwb_ref[...]
        for j in range(CONV_WIDTH):
            s, m = j % SUBLANES, j // SUBLANES
            z = z + dww_ref[j:j + 1, :] * shifted[part, s, m * SUBLANES:m * SUBLANES + n_out, :]
        if seg_len:
            z = jnp.concatenate([z[b * stride:b * stride + seg_len] for b in range(n_seq)], axis=0)
        mu = jnp.mean(z, axis=-1, keepdims=True)
        zc = z - mu
        var = jnp.mean(zc * zc, axis=-1, keepdims=True)
        zh = zc * lax.rsqrt(var + LN_EPS) * clnw_ref[...] + clnb_ref[...]
        yb_out[0, rs, :] = zh * (jnp.tanh(zh) + 1.0)
        yield

    _interleave(stream, n_parts)

    xn_end, h_end = last_rows[n_parts - 1]
    if seg_len:
        xn_tile = last_rows['xn']
        for b in range(n_seq):
            conv_out[b] = up_ext[(b + 1) * stride - CONV_CARRY:(b + 1) * stride, :]
            shift_out[b] = xn_tile[(b + 1) * seg_len - 1:(b + 1) * seg_len]
    else:
        tail = up_ext[pl.ds(tm + CARRY_PAD - CONV_CARRY, CONV_CARRY), :]
        up_ext[CARRY_PAD - CONV_CARRY:CARRY_PAD, :] = tail
        conv_out[0] = tail
        xn_last[...] = xn_end
        h_last[...] = h_end
        shift_out[0] = xn_end


def _mix_in(x, shift0, conv0, wd, tm, n_parts, whole_sequences):
    bsz, t, d = x.shape
    c = wd['w0'].shape[1]
    weights = [wd[n] for n in ('norm_mix', 'w_in', 'mu_rkv', 'w0', 'a0', 'w_lora1', 'w_lora2', 'g2',
                               'k_k', 'k_a', 'r_k', 'seg', 'dw_w', 'dw_b', 'cln_w', 'cln_b')]
    d_w, d_a = wd['lora_dims']
    if whole_sequences:
        tm, seg_len, n_seq, rows_b = bsz * t, t, bsz, 1
        assert t & (t - 1) == 0 and n_parts == 1
        x = x.reshape(1, tm, d)
        shift0 = jnp.repeat(shift0.reshape(bsz, d), t, axis=0).reshape(1, tm, d)
        grid = (1, 1)
        state = lambda rows, w: pl.BlockSpec((n_seq, rows, w), lambda b, i: (0, 0, 0))
        shift_in = pl.BlockSpec((1, tm, d), lambda b, i: (0, 0, 0))
        window_rows = n_seq * (CARRY_PAD + seg_len)
    else:
        seg_len, rows_b = 0, bsz
        grid = (bsz, t // tm)
        state = lambda rows, w: pl.BlockSpec((1, rows, w), lambda b, i: (b, 0, 0))
        shift_in = state(1, d)
        window_rows = tm + CARRY_PAD
    tok = lambda w: pl.BlockSpec((1, tm, w), lambda b, i: (b, i, 0))
    out_tok = jax.ShapeDtypeStruct((rows_b, x.shape[1], c), F32)
    outs = pl.pallas_call(
        functools.partial(_mix_in_kernel, tm=tm, c=c, n_parts=n_parts, d_w=d_w, d_a=d_a, seg_len=seg_len),
        grid=grid,
        in_specs=[tok(d), shift_in, state(CONV_CARRY, c)] + [_full(w.shape) for w in weights],
        out_specs=[tok(c)] * 9 + [state(1, d), state(CONV_CARRY, c)],
        out_shape=[out_tok] * 9 + [jax.ShapeDtypeStruct((bsz, 1, d), F32),
                                   jax.ShapeDtypeStruct((bsz, CONV_CARRY, c), F32)],
        scratch_shapes=[pltpu.VMEM((1, d), F32), pltpu.VMEM((1, 3 * c), F32),
                        pltpu.VMEM((window_rows, c), F32),
                        pltpu.VMEM((n_parts, SUBLANES, (window_rows - CARRY_PAD) // n_parts + CARRY_PAD - SUBLANES, c),
                                   F32)],
        compiler_params=pltpu.CompilerParams(dimension_semantics=("arbitrary", "arbitrary"),
                                             vmem_limit_bytes=VMEM_LIMIT),
        name="mix_in",
    )(x, shift0, conv0, *weights)
    return [o.reshape(bsz, t, c) for o in outs[:9]] + list(outs[9:])


def _tri_inverse(n_strict, row, col, lg_chunk):
    lg_base = INV_BASE.bit_length() - 1
    same = lambda sh: (row >> sh) == (col >> sh)
    lg0 = min(lg_base, lg_chunk)
    n = row.shape[0]

    def expand(c, lg):
        return jnp.where(same(lg), jnp.concatenate([c] * (n >> lg), axis=0), 0.0).astype(BF16)

    def fold(x, lg):
        b = 1 << lg
        return functools.reduce(lambda u, v: u + v, [x[i:i + b] for i in range(0, n, b)])

    b0 = 1 << lg0
    row_c = lax.broadcasted_iota(I32, (b0, n), 0)
    col_c = lax.broadcasted_iota(I32, (b0, n), 1)
    p_full = [jnp.where(same(lg0), x, 0.0) for x in n_strict]
    p = [fold(x, lg0) for x in p_full]
    t = [jnp.where((col_c & (b0 - 1)) == row_c, 1.0, 0.0) + x for x in p]
    p_full = [x.astype(BF16) for x in p_full]
    for _ in range(lg0 - 1):
        p = [_dg(x.astype(BF16), y) for x, y in zip(p, p_full)]
        p_full = [expand(x, lg0) for x in p]
        t = [x + _dg(x.astype(BF16), y) for x, y in zip(t, p_full)]
    for lg in range(lg0, lg_chunk):
        off_mask = same(lg + 1) & jnp.logical_not(same(lg))
        t_full = [expand(x, lg) for x in t]
        u = [_dg(x.astype(BF16), jnp.where(off_mask, m, 0.0).astype(BF16)) for x, m in zip(t, n_strict)]
        add = [_dg(x.astype(BF16), y) for x, y in zip(u, t_full)]
        even = ((lax.broadcasted_iota(I32, (1 << lg, n), 1) >> lg) & 1) == 0
        t = [jnp.concatenate([jnp.where(even, x, 0.0), jnp.where(even, a, x)], axis=0) for x, a in zip(t, add)]
    return [expand(x, lg_chunk) for x in t]


def _wkv_kernel(r_ref, lw_ref, k_ref, v_ref, a_ref, b_ref, s0_ref, y_ref, s_out, s_scr,
                *, tt, n_sub, chunk, chained):
    ti = pl.program_id(1)
    n_heads = r_ref.shape[2] // HEAD_SIZE
    n_chunks = tt // chunk
    lg_chunk = chunk.bit_length() - 1
    row = lax.broadcasted_iota(I32, (tt, tt), 0)
    col = lax.broadcasted_iota(I32, (tt, tt), 1)
    in_chunk = (row >> lg_chunk) == (col >> lg_chunk)
    tri_incl = in_chunk & (col <= row)
    tri_strict = in_chunk & (col < row)
    m_cum = jnp.where(tri_incl, 1.0, 0.0).astype(BF16)
    row_h = lax.broadcasted_iota(I32, (HEAD_SIZE, HEAD_SIZE), 0)
    col_h = lax.broadcasted_iota(I32, (HEAD_SIZE, HEAD_SIZE), 1)
    eye_h = row_h == col_h

    if chained:
        @pl.when(ti == 0)
        def _():
            s_scr[...] = s0_ref[0]

    rt_all, g_end_all, cols = [], [], {name: [] for name in ('v', 'rt', 'at', 'kt', 'bt', 'bd', 'kd')}
    hsl = [slice(HEAD_SIZE * h, HEAD_SIZE * (h + 1)) for h in range(n_heads)]
    for sub in range(n_sub):
        rows = pl.ds(sub * tt, tt)
        lw_all = lw_ref[0, rows, :]
        k_all = k_ref[0, rows, :]
        b_all = b_ref[0, rows, :]
        cum = _mask_dot(m_cum, lw_all)
        tot = jnp.concatenate([jnp.broadcast_to(cum[(ci + 1) * chunk - 1:(ci + 1) * chunk], (chunk, cum.shape[1]))
                               for ci in range(n_chunks)], axis=0)
        e_neg = jnp.exp(-cum)
        e_end = jnp.exp(tot - cum)
        full = dict(v=v_ref[0, rows, :], rt=r_ref[0, rows, :] * jnp.exp(cum),
                    at=a_ref[0, rows, :] * jnp.exp(cum - lw_all), kt=k_all * e_neg, bt=b_all * e_neg,
                    bd=b_all * e_end, kd=k_all * e_end)
        rt_all.append(full['rt'])
        g_end_all.append(jnp.exp(tot))
        for name, z in full.items():
            cols[name] += [z[:, s_].astype(BF16) for s_ in hsl]
    v, rt, at, kt, bt, bd, kd = (cols[name] for name in ('v', 'rt', 'at', 'kt', 'bt', 'bd', 'kd'))

    units = range(n_sub * n_heads)
    mm = [_dg(jnp.concatenate([at[u], rt[u]], axis=0), jnp.concatenate([bt[u], kt[u]], axis=0), NT) for u in units]
    m_ab = [jnp.where(tri_strict, mm[u][:tt, :tt], 0.0) for u in units]
    m_ak = [jnp.where(tri_strict, mm[u][:tt, tt:], 0.0).astype(BF16) for u in units]
    m_rb = [jnp.where(tri_incl, mm[u][tt:, :tt], 0.0).astype(BF16) for u in units]
    m_rk = [jnp.where(tri_incl, mm[u][tt:, tt:], 0.0).astype(BF16) for u in units]
    tinv = _tri_inverse(m_ab, row, col, lg_chunk)
    akv = [_dg(m_ak[u], v[u]).astype(BF16) for u in units]
    w1 = [_dg(tinv[u], at[u]).astype(BF16) for u in units]
    w2 = [_dg(tinv[u], akv[u]).astype(BF16) for u in units]
    q = [(rt_all[u // n_heads][:, hsl[u % n_heads]] + _dg(m_rb[u], w1[u])).astype(BF16) for u in units]
    y0 = [_dg(jnp.concatenate([m_rb[u], m_rk[u]], axis=1), jnp.concatenate([w2[u], v[u]], axis=0)) for u in units]

    heads = range(n_heads)
    if chained:
        s = [s_scr[h] for h in heads]
    for sub in range(n_sub):
        for ci in range(n_chunks):
            cs = slice(ci * chunk, (ci + 1) * chunk)
            seq = sub * n_chunks + ci
            if not chained:
                s = [s0_ref[seq, h] for h in heads]
            g_row = g_end_all[sub][ci * chunk:ci * chunk + 1]
            un = [sub * n_heads + h for h in heads]
            gm = [jnp.where(eye_h, jnp.broadcast_to(g_row[:, hsl[h]], (HEAD_SIZE, HEAD_SIZE)), 0.0)
                  + _dg(bd[un[h]][cs], w1[un[h]][cs], TN) for h in heads]
            hm = [_dg(jnp.concatenate([bd[un[h]][cs], kd[un[h]][cs]], axis=0),
                      jnp.concatenate([w2[un[h]][cs], v[un[h]][cs]], axis=0), TN) for h in heads]
            for h in heads:
                y_ref[0, pl.ds(sub * tt + ci * chunk, chunk), hsl[h]] = _bdot(q[un[h]][cs], s[h]) + y0[un[h]][cs]
            s = [_bdot(gm[h], s[h]) + hm[h] for h in heads]
            if not chained:
                for h in heads:
                    s_out[seq, h] = s[h]
    if chained:
        for h in heads:
            s_scr[h] = s[h]
            s_out[0, h] = s[h]


def _wkv(r, lw, k, v, a, b, s0t, tt, n_sub, chunk, chained):
    bsz, t, c = r.shape
    n_heads = c // HEAD_SIZE
    hs = HEAD_SIZE
    step = tt * n_sub
    if chained:
        grid = (bsz, t // step)
        tok = pl.BlockSpec((1, step, c), lambda bi, ti: (bi, ti, 0))
        st = pl.BlockSpec((1, n_heads, hs, hs), lambda bi, ti: (bi, 0, 0, 0))
        args = (r, lw, k, v, a, b)
        y_shape = (bsz, t, c)
    else:
        assert t == chunk and (bsz * t) % step == 0
        n_seq = step // chunk
        grid = (1, bsz * t // step)
        tok = pl.BlockSpec((1, step, c), lambda bi, ti: (0, ti, 0))
        st = pl.BlockSpec((n_seq, n_heads, hs, hs), lambda bi, ti: (ti, 0, 0, 0))
        args = tuple(z.reshape(1, bsz * t, c) for z in (r, lw, k, v, a, b))
        y_shape = (1, bsz * t, c)
    y, s_new = pl.pallas_call(
        functools.partial(_wkv_kernel, tt=tt, n_sub=n_sub, chunk=chunk, chained=chained),
        grid=grid,
        in_specs=[tok] * 6 + [st],
        out_specs=[tok, st],
        out_shape=[jax.ShapeDtypeStruct(y_shape, F32), jax.ShapeDtypeStruct(s0t.shape, F32)],
        scratch_shapes=[pltpu.VMEM((n_heads, hs, hs), F32)],
        compiler_params=pltpu.CompilerParams(dimension_semantics=("arbitrary",) * 2,
                                             vmem_limit_bytes=VMEM_LIMIT),
        name="wkv",
    )(*args, s0t)
    return y.reshape(bsz, t, c), s_new


def _interleave(make_stream, n_parts):
    for _ in zip(*[make_stream(part) for part in range(n_parts)]):
        pass


def _mix_out_kernel(x_ref, y_ref, bonus_ref, g_ref, yb_ref, lnw_ref, lnb_ref, seg_ref, wout_ref, nffn_ref,
                    wr_ref, h1_out, xn_out, eid_out, eidt_out, gate_out, *, c, n_parts):
    rows_per = x_ref.shape[0] // n_parts

    def stream(part):
        rs = pl.ds(part * rows_per, rows_per)
        seg = seg_ref[...]
        y = y_ref[rs, :]
        inv_n = 1.0 / HEAD_SIZE
        mu = _seg_sum(y, seg) * inv_n
        yield
        yc = y - mu
        var = _seg_sum(yc * yc, seg) * inv_n
        yield
        yn = yc * lax.rsqrt(var + GN_EPS) * lnw_ref[...] + lnb_ref[...]
        ya = (yn + bonus_ref[rs, :]) * g_ref[rs, :]
        mix = _dg(ya.astype(BF16), wout_ref[:c, :]) + _dg(yb_ref[rs, :].astype(BF16), wout_ref[c:, :])
        yield
        h1 = x_ref[rs, :] + mix
        h1_out[rs, :] = h1
        xn = _rms(h1, nffn_ref[...])
        xn_out[rs, :] = xn.astype(BF16)
        xh, xl = _split2(xn)
        hi_lo = _dg(xh, wr_ref[...])
        logits = hi_lo[:, :ROUTER_LANES] + (hi_lo[:, ROUTER_LANES:] + _dg(xl, wr_ref[:, :ROUTER_LANES]))
        yield
        lane = lax.broadcasted_iota(I32, logits.shape, 1)
        neg = jnp.float32(-jnp.inf)
        is_g = (lane >= N_EXPERTS) & (lane < N_EXPERTS + N_EXPERT_GROUPS)
        glog = jnp.where(is_g, logits, neg)
        gmax = jnp.max(glog, axis=-1, keepdims=True)
        gsel = jnp.min(jnp.where(glog == gmax, lane, 4 * ROUTER_LANES), axis=-1, keepdims=True) - N_EXPERTS
        gp = 1.0 / jnp.sum(jnp.where(is_g, jnp.exp(glog - gmax), 0.0), axis=-1, keepdims=True)
        in_grp = (lane >= gsel * EXPERTS_PER_GROUP) & (lane < (gsel + 1) * EXPERTS_PER_GROUP)
        elog = jnp.where(in_grp, logits, neg)
        emax = jnp.max(elog, axis=-1, keepdims=True)
        ex = jnp.where(in_grp, jnp.exp(elog - emax), 0.0)
        eprob = ex / jnp.sum(ex, axis=-1, keepdims=True)
        eprob = jnp.where(in_grp, eprob, -1.0)
        yield
        v1 = jnp.max(eprob, axis=-1, keepdims=True)
        i1 = jnp.min(jnp.where(eprob == v1, lane, 4 * ROUTER_LANES), axis=-1, keepdims=True)
        rest = jnp.where(lane == i1, -1.0, eprob)
        v2 = jnp.max(rest, axis=-1, keepdims=True)
        i2 = jnp.min(jnp.where(rest == v2, lane, 4 * ROUTER_LANES), axis=-1, keepdims=True)
        denom = v1 + v2
        eid = jnp.where(lane == 0, i1, jnp.where(lane == 1, i2, 0))
        eid_out[rs, :] = eid
        eidt_out[:, rs] = jnp.transpose(eid)[:SUBLANES]
        gate_out[rs, :] =jnp.where(lane == 0, gp * v1 / denom, jnp.where(lane == 1, gp * v2 / denom, 0.0))
        yield

    _interleave(stream, n_parts)


def _mix_out(x2, y2, bonus2, g2, yb2, wd, tm, n_parts):
    m, d = x2.shape
    c = y2.shape[1]
    tokd = pl.BlockSpec((tm, d), lambda i: (i, 0))
    tokc = pl.BlockSpec((tm, c), lambda i: (i, 0))
    tokr = pl.BlockSpec((tm, ROUTER_LANES), lambda i: (i, 0))
    weights = [wd[n] for n in ('ln_x_w', 'ln_x_b', 'seg', 'w_out', 'norm_ffn', 'w_router')]
    return pl.pallas_call(
        functools.partial(_mix_out_kernel, c=c, n_parts=n_parts),
        grid=(m // tm,),
        in_specs=[tokd, tokc, tokc, tokc, tokc] + [_full(w.shape) for w in weights],
        out_specs=[tokd, tokd, tokr, pl.BlockSpec((SUBLANES, tm), lambda i: (0, i)), tokr],
        out_shape=[jax.ShapeDtypeStruct((m, d), F32), jax.ShapeDtypeStruct((m, d), BF16),
                   jax.ShapeDtypeStruct((m, ROUTER_LANES), I32), jax.ShapeDtypeStruct((SUBLANES, m), I32),
                   jax.ShapeDtypeStruct((m, ROUTER_LANES), F32)],
        compiler_params=pltpu.CompilerParams(dimension_semantics=("arbitrary",), vmem_limit_bytes=VMEM_LIMIT),
        name="mix_out",
    )(x2, y2, bonus2, g2, yb2, *weights)


def _pow2_pieces(count, max_rows, fn):
    off = 0
    rows = max_rows
    while rows >= RUN_ALIGN:
        has = (count & (rows // RUN_ALIGN)) != 0

        @pl.when(has)
        def _(off=off, rows=rows):
            fn(off, rows)
        off = off + jnp.where(has, rows, 0)
        rows //= 2


def _for_each_expert(fn):
    def body(e, carry):
        fn(e)
        return carry
    lax.fori_loop(0, N_EXPERTS, body, 0)


def _pow2_floor(n):
    return 1 << (n.bit_length() - 1)


def _dispatch_kernel(nch_ref, tot_ref, off_ref, loff_ref, tn_ref, toff_ref, nu_ref, *refs, groups, bm, n_blocks):
    n_g = len(groups)
    eid_refs, xn_refs = refs[:n_g], refs[n_g + 1:2 * n_g + 1]
    loffc_ref = refs[n_g]
    xs_out, buf, sem, zbuf, zsem = refs[2 * n_g + 1:]
    i = pl.program_id(0)
    last = pl.num_programs(0) - 1
    max_run = max(tm for tm, _ in groups)

    def start_tile(tile):
        def per_expert(e):
            src0 = loff_ref[tile * N_EXPERTS + e]
            dst0 = off_ref[tile * N_EXPERTS + e]

            def piece(o, rows):
                src = buf.at[tile % 2, pl.ds(pl.multiple_of(src0 + o, RUN_ALIGN), rows)]
                dst = xs_out.at[pl.ds(pl.multiple_of(dst0 + o, RUN_ALIGN), rows)]
                pltpu.make_async_copy(src, dst, sem.at[tile % 2]).start()
            _pow2_pieces(nch_ref[tile * N_EXPERTS + e], max_run, piece)
        _for_each_expert(per_expert)

    def wait_tile(tile):
        def piece(o, rows):
            pltpu.make_async_copy(buf.at[tile % 2, pl.ds(0, rows)], xs_out.at[pl.ds(0, rows)], sem.at[tile % 2]).wait()
        _pow2_pieces(tot_ref[tile], _pow2_floor(buf.shape[1]), piece)

    @pl.when(i >= 2)
    def _():
        wait_tile(i - 2)

    def sort_tile(eid_ref, xn_ref, tm):
        e_rows = eid_ref[...]
        sub = lax.broadcasted_iota(I32, (N_EXPERTS, tm), 0)
        e1 = jnp.where(sub == e_rows[0:1], 1.0, 0.0)
        e2 = jnp.where(sub == e_rows[1:2], 1.0, 0.0)
        before = lax.broadcasted_iota(I32, (tm, tm), 0) < lax.broadcasted_iota(I32, (tm, tm), 1)
        slot = _dg((e1 + e2).astype(BF16), jnp.where(before, 1.0, 0.0).astype(BF16)) + loffc_ref[0]
        l1 = jnp.sum(slot * e1, axis=0, keepdims=True).astype(I32)
        l2 = jnp.sum(slot * e2, axis=0, keepdims=True).astype(I32)
        n_rows = 2 * tm + LOCAL_PAD
        rows = lax.broadcasted_iota(I32, (n_rows, tm), 0)
        perm = jnp.where((rows == l1) | (rows == l2), 1.0, 0.0).astype(BF16)
        buf[i % 2, 0:n_rows, :] = _dg(perm, xn_ref[...]).astype(BF16)

    first = 0
    for g, (tm, n_tiles) in enumerate(groups):
        pl.when((i >= first) & (i < first + n_tiles))(functools.partial(sort_tile, eid_refs[g], xn_refs[g], tm))
        first += n_tiles
    start_tile(i)

    @pl.when(i == 0)
    def _():
        zbuf[...] = jnp.zeros_like(zbuf)
        half = zbuf.shape[0]

        def zero_copy(off, rows):
            return pltpu.make_async_copy(zbuf.at[pl.ds(0, rows)], xs_out.at[pl.ds(off, rows)], zsem)

        def tails(fn):
            _for_each_expert(lambda e: _pow2_pieces(
                tn_ref[e], half, lambda o, rows: fn(pl.multiple_of(toff_ref[e] + o, RUN_ALIGN), rows)))

        def unused_blocks(fn):
            def body(b, carry):
                fn(pl.multiple_of(b * bm, bm), half)
                fn(pl.multiple_of(b * bm + half, half), half)
                return carry
            lax.fori_loop(nu_ref[0], n_blocks, body, 0)

        tails(lambda off, rows: zero_copy(off, rows).start())
        unused_blocks(lambda off, rows: zero_copy(off, rows).start())
        tails(lambda off, rows: zero_copy(off, rows).wait())
        unused_blocks(lambda off, rows: zero_copy(off, rows).wait())

    @pl.when(i == last)
    def _():
        @pl.when(i >= 1)
        def _():
            wait_tile(i - 1)
        wait_tile(i)


def _dispatch(plan, n_used, eids_t, xns, tms, n_blocks, bm):
    d = xns[0].shape[1]
    groups = tuple((tm, xn.shape[0] // tm) for xn, tm in zip(xns, tms))
    firsts = [sum(n for _, n in groups[:g]) for g in range(len(groups))]
    lbuf = 2 * max(tms) + LOCAL_PAD

    def tile_of(g):
        return lambda i: jnp.clip(i - firsts[g], 0, groups[g][1] - 1)

    imap = lambda f: (lambda i, *_: f(i))
    in_specs = ([pl.BlockSpec((SUBLANES, tm), imap(lambda i, g=g: (0, tile_of(g)(i)))) for g, tm in enumerate(tms)] +
                [pl.BlockSpec((1, N_EXPERTS, 1), imap(lambda i: (i, 0, 0)))] +
                [pl.BlockSpec((tm, d), imap(lambda i, g=g: (tile_of(g)(i), 0))) for g, tm in enumerate(tms)])
    return pl.pallas_call(
        functools.partial(_dispatch_kernel, groups=groups, bm=bm, n_blocks=n_blocks),
        grid_spec=pltpu.PrefetchScalarGridSpec(
            num_scalar_prefetch=7,
            grid=(sum(n for _, n in groups),),
            in_specs=in_specs,
            out_specs=pl.BlockSpec(memory_space=pl.ANY),
            scratch_shapes=[pltpu.VMEM((2, lbuf, d), BF16), pltpu.SemaphoreType.DMA((2,)),
                            pltpu.VMEM((bm // 2, d), BF16), pltpu.SemaphoreType.DMA],
        ),
        out_shape=jax.ShapeDtypeStruct((n_blocks * bm, d), BF16),
        compiler_params=pltpu.CompilerParams(dimension_semantics=("arbitrary",), vmem_limit_bytes=VMEM_LIMIT),
        name="moe_dispatch",
    )(plan['nch'], plan['tot'], plan['off'], plan['loff'], plan['tail_n'], plan['tail_off'], n_used, *eids_t,
      plan['loff_col'], *xns)


def _experts_kernel(be_ref, slot_ref, nxt_ref, nu_ref, xs_ref, wg_hbm, wu_hbm, wd_hbm, yb_ref,
                    wg_f, wu_f, wd_f, wg_b, wu_b, wd_b, sem):
    b = pl.program_id(0)

    def weight_copies(e, slot):
        pairs = ((wg_hbm, wg_f), (wu_hbm, wu_f), (wd_hbm, wd_f))
        return [pltpu.make_async_copy(src.at[e], dst.at[slot], sem.at[slot, j]) for j, (src, dst) in enumerate(pairs)]

    @pl.when(b == 0)
    def _():
        for cp in weight_copies(be_ref[0], slot_ref[0]):
            cp.start()

    def gated_mlp(wg, wu, wd):
        xb = xs_ref[...]
        hg = _dg(xb, wg)
        hu = _dg(xb, wu)
        act = (hg * _sigmoid(hg) * hu).astype(BF16)
        yb_ref[...] = _dg(act, wd).astype(BF16)

    active = b < nu_ref[0]
    first = (b == 0) | (be_ref[b] != be_ref[jnp.maximum(b - 1, 0)])

    @pl.when(active & first)
    def _():
        slot = slot_ref[b]
        for cp in weight_copies(be_ref[b], slot):
            cp.wait()

        @pl.when(nxt_ref[b] >= 0)
        def _():
            for cp in weight_copies(nxt_ref[b], 1 - slot):
                cp.start()
        wg, wu, wd = wg_f[slot].astype(BF16), wu_f[slot].astype(BF16), wd_f[slot].astype(BF16)
        wg_b[...] = wg
        wu_b[...] = wu
        wd_b[...] = wd
        gated_mlp(wg, wu, wd)

    @pl.when(active & jnp.logical_not(first))
    def _():
        gated_mlp(wg_b[...], wu_b[...], wd_b[...])

    @pl.when(pl.program_id(0) >= nu_ref[0])
    def _():
        yb_ref[...] = jnp.zeros_like(yb_ref)


def _experts(sched, n_used, xs, wg, wu, wdn, bm):
    p, d = xs.shape
    ff = wg.shape[2]
    n_blocks = p // bm
    return pl.pallas_call(
        _experts_kernel,
        grid_spec=pltpu.PrefetchScalarGridSpec(
            num_scalar_prefetch=4,
            grid=(n_blocks,),
            in_specs=[pl.BlockSpec((bm, d), lambda b, be, sl, nx, nu: (jnp.minimum(b, nu[0] - 1), 0)),
                      pl.BlockSpec(memory_space=pl.ANY), pl.BlockSpec(memory_space=pl.ANY),
                      pl.BlockSpec(memory_space=pl.ANY)],
            out_specs=pl.BlockSpec((bm, d), lambda b, *_: (b, 0)),
            scratch_shapes=[pltpu.VMEM((2, d, ff), F32), pltpu.VMEM((2, d, ff), F32), pltpu.VMEM((2, ff, d), F32),
                            pltpu.VMEM((d, ff), BF16), pltpu.VMEM((d, ff), BF16), pltpu.VMEM((ff, d), BF16),
                            pltpu.SemaphoreType.DMA((2, 3))],
        ),
        out_shape=jax.ShapeDtypeStruct((p, d), BF16),
        compiler_params=pltpu.CompilerParams(dimension_semantics=("arbitrary",), vmem_limit_bytes=VMEM_LIMIT),
        name="moe_experts",
    )(sched['expert'], sched['slot'], sched['next'], n_used, xs, wg, wu, wdn)


def _final_kernel(nch_ref, tot_ref, off_ref, loff_ref, h1_ref, eid_ref, gate_ref, loffr_ref, p_ref, yb_hbm,
                  nple_ref, wpg_ref, wpp_ref, nfin_ref, y_out, buf, sem, *, tm, lbuf, tile0, n_parts):
    i = pl.program_id(0)

    def fetch(tile):
        base = (tile0 + tile) * N_EXPERTS

        def per_expert(e):
            src0 = off_ref[base + e]
            dst0 = loff_ref[base + e]

            def piece(o, rows):
                src = yb_hbm.at[pl.ds(pl.multiple_of(src0 + o, RUN_ALIGN), rows)]
                dst = buf.at[tile % 2, pl.ds(pl.multiple_of(dst0 + o, RUN_ALIGN), rows)]
                pltpu.make_async_copy(src, dst, sem.at[tile % 2]).start()
            _pow2_pieces(nch_ref[base + e], tm, piece)
        _for_each_expert(per_expert)

    def wait_fetch(tile):
        def piece(o, rows):
            pltpu.make_async_copy(yb_hbm.at[pl.ds(0, rows)], buf.at[tile % 2, pl.ds(0, rows)], sem.at[tile % 2]).wait()
        _pow2_pieces(tot_ref[tile0 + tile], _pow2_floor(lbuf), piece)

    @pl.when(i == 0)
    def _():
        buf[...] = jnp.zeros_like(buf)
        fetch(i)

    @pl.when(i + 1 < pl.num_programs(0))
    def _():
        fetch(i + 1)

    eid = eid_ref[...]
    lane = lax.broadcasted_iota(I32, (tm, N_EXPERTS), 1)
    e12 = (jnp.where(lane == eid[:, 0:1], 1.0, 0.0) + jnp.where(lane == eid[:, 1:2], 1.0, 0.0)).astype(BF16)
    rows_per = tm // n_parts
    picks = []
    for part in range(n_parts):
        rs = pl.ds(part * rows_per, rows_per)
        lane_p = lax.broadcasted_iota(I32, (rows_per, N_EXPERTS), 1)
        eid_p = eid_ref[rs, :]
        e1 = jnp.where(lane_p == eid_p[:, 0:1], 1.0, 0.0)
        e2 = jnp.where(lane_p == eid_p[:, 1:2], 1.0, 0.0)
        before = (lax.broadcasted_iota(I32, (rows_per, tm), 1)
                  < lax.broadcasted_iota(I32, (rows_per, tm), 0) + part * rows_per)
        slot = _dg(jnp.where(before, 1.0, 0.0).astype(BF16), e12) + loffr_ref[0]
        l1 = jnp.sum(slot * e1, axis=1, keepdims=True).astype(I32)
        l2 = jnp.sum(slot * e2, axis=1, keepdims=True).astype(I32)
        cols = lax.broadcasted_iota(I32, (rows_per, lbuf), 1)
        gate = gate_ref[rs, :]
        picks.append(jnp.where(cols == l1, gate[:, 0:1], jnp.where(cols == l2, gate[:, 1:2], 0.0)).astype(BF16))

    wait_fetch(i)
    sorted_rows = buf[i % 2]

    def stream(part):
        rs = pl.ds(part * rows_per, rows_per)
        h2 = h1_ref[rs, :] + _dg(picks[part], sorted_rows)
        yield
        gate_in = _rms(h2, nple_ref[...]).astype(BF16)
        pg = _sigmoid(_dg(gate_in, wpg_ref[...]))
        yield
        h3 = h2 + pg * _dg(p_ref[rs, :].astype(BF16), wpp_ref[...])
        y_out[rs, :] = _rms(h3, nfin_ref[...])
        yield

    _interleave(stream, n_parts)


def _final(plan, h1, eid, gate, p2, yb, wd, tm, tile0):
    m, d = h1.shape
    pd = p2.shape[1]
    lbuf = 2 * tm + LOCAL_PAD
    weights = [wd[n] for n in ('norm_ple', 'w_ple_gate', 'w_ple_proj', 'norm_final')]
    imap = lambda f: (lambda i, *_: f(i))
    return pl.pallas_call(
        functools.partial(_final_kernel, tm=tm, lbuf=lbuf, tile0=tile0, n_parts=TOKEN_PARTS),
        grid_spec=pltpu.PrefetchScalarGridSpec(
            num_scalar_prefetch=4,
            grid=(m // tm,),
            in_specs=[pl.BlockSpec((tm, d), imap(lambda i: (i, 0))),
                      pl.BlockSpec((tm, ROUTER_LANES), imap(lambda i: (i, 0))),
                      pl.BlockSpec((tm, ROUTER_LANES), imap(lambda i: (i, 0))),
                      pl.BlockSpec((1, 1, N_EXPERTS), imap(lambda i: (tile0 + i, 0, 0))),
                      pl.BlockSpec((tm, pd), imap(lambda i: (i, 0))),
                      pl.BlockSpec(memory_space=pl.ANY)] +
                     [pl.BlockSpec(w.shape, imap(lambda i, n=len(w.shape): (0,) * n)) for w in weights],
            out_specs=pl.BlockSpec((tm, d), imap(lambda i: (i, 0))),
            scratch_shapes=[pltpu.VMEM((2, lbuf, d), BF16), pltpu.SemaphoreType.DMA((2,))],
        ),
        out_shape=jax.ShapeDtypeStruct((m, d), F32),
        compiler_params=pltpu.CompilerParams(dimension_semantics=("arbitrary",), vmem_limit_bytes=VMEM_LIMIT),
        name="moe_final",
    )(plan['nch'], plan['tot'], plan['off'], plan['loff'], h1, eid, gate, plan['loff_row'], p2, yb, *weights)


def _route_plan(eids, tms, bm):
    experts = jnp.arange(N_EXPERTS, dtype=I32)
    counts = []
    for eid, tm in zip(eids, tms):
        onehot = (eid[:2, :, None] == experts).astype(I32)
        counts.append(onehot.reshape(2, -1, tm, N_EXPERTS).sum(axis=(0, 2)))
    n = jnp.concatenate(counts)
    n_al = (n + RUN_ALIGN - 1) // RUN_ALIGN * RUN_ALIGN
    loff = jnp.cumsum(n_al, axis=1) - n_al
    used = n_al.sum(axis=0)
    region = (used + bm - 1) // bm * bm
    pend = jnp.cumsum(region)
    off = (pend - region)[None, :] + jnp.cumsum(n_al, axis=0) - n_al
    n_assign = sum(2 * e.shape[1] for e in eids)
    n_blocks = -(-(n_assign + (RUN_ALIGN - 1) * N_EXPERTS * n.shape[0] + N_EXPERTS * (bm - 1)) // bm)
    block_start = jnp.arange(n_blocks, dtype=I32) * bm
    block_expert = jnp.minimum(jnp.sum((pend[None, :] <= block_start[:, None]).astype(I32), axis=1), N_EXPERTS - 1)
    plan = dict(nch=(n_al // RUN_ALIGN).reshape(-1).astype(I32), off=off.reshape(-1).astype(I32),
                tot=(n_al.sum(axis=1) // RUN_ALIGN).astype(I32),
                loff=loff.reshape(-1).astype(I32), tail_n=((region - used) // RUN_ALIGN).astype(I32),
                tail_off=(pend - region + used).astype(I32), loff_col=loff.astype(F32)[:, :, None],
                loff_row=loff.astype(F32)[:, None, :])
    n_used = pend[-1] // bm
    prev = jnp.concatenate([jnp.full((1,), -1, I32), block_expert[:-1].astype(I32)])
    slot = (jnp.cumsum((block_expert != prev).astype(I32)) - 1) % 2
    after = pend[block_expert] // bm
    nxt = jnp.where(after < n_used, block_expert[jnp.minimum(after, n_blocks - 1)], -1)
    sched = dict(expert=block_expert.astype(I32), slot=slot.astype(I32), next=nxt.astype(I32))
    return plan, sched, n_used.astype(I32).reshape(1), n_blocks


def _tile_plan(bsz, t, long_sequence):
    rows = bsz * t
    span = t if long_sequence else rows
    wkv_sub = min(WKV_SUBTILE, span)
    return dict(
        mix_in=min(TOKEN_TILE, t), mix_in_parts=MIX_IN_PARTS if (long_sequence and t >= TOKEN_TILE) else 1,
        wkv_sub=wkv_sub, wkv_n_sub=min(WKV_SUBTILES, span // wkv_sub),
        wkv_chunk=min(WKV_CHUNK, t),
        token=min(TOKEN_TILE, rows), token_parts=TOKEN_PARTS)


def _layer_front(x, shift0, wkv0, conv0, wd, tiles, long_sequence):
    bsz, t, d = x.shape
    r, lw, k2, v, ah, bh, g, bonus, yb, shift_new, conv_new = _mix_in(
        x, shift0, conv0, wd, tiles['mix_in'], tiles['mix_in_parts'], not long_sequence)
    y, s_new = _wkv(r, lw, k2, v, ah, bh, jnp.swapaxes(wkv0, -1, -2), tiles['wkv_sub'], tiles['wkv_n_sub'],
                    tiles['wkv_chunk'], long_sequence)
    flat = lambda z: z.reshape(bsz * t, z.shape[-1])
    h1, xn2, eid, eid_t, gate = _mix_out(flat(x), flat(y), flat(bonus), flat(g), flat(yb), wd, tiles['token'],
                                         tiles['token_parts'])
    return h1, xn2, (eid, eid_t), gate, shift_new.reshape(bsz, d), jnp.swapaxes(s_new, -1, -2), conv_new


def kernel(x_prompt, x_sample, state_shift, state_wkv, cache_conv, p_prompt, p_sample, norm_mix, w_in, mu_rkv, mu_w, mu_a, mu_g, w0, w1, w2, a0, a1, a2, g1, g2, k_k, k_a, r_k, ln_x_w, ln_x_b, dw_w, dw_b, cln_w, cln_b, w_out, norm_ffn, w_router_group, w_router_expert, w_exp_gate, w_exp_up, w_exp_down, norm_ple, w_ple_gate, w_ple_proj, norm_final):
    depth = norm_mix.shape[0]
    assert depth == 1
    d = x_prompt.shape[-1]
    c = w0.shape[-1]
    row = lambda z: z[0].reshape(1, -1).astype(F32)
    lane = jnp.arange(LANES, dtype=I32) // HEAD_SIZE
    w_router = jnp.concatenate([w_router_expert[0], w_router_group[0],
                                jnp.zeros((d, ROUTER_LANES - N_EXPERTS - N_EXPERT_GROUPS), F32)], axis=1)
    first = jnp.concatenate([w1[0], a1[0], g1[0]], axis=1)
    mixed = jnp.concatenate([mu_w[0][:, None] * w1[0], mu_a[0][:, None] * a1[0], mu_g[0][:, None] * g1[0]], axis=1)
    d_w, d_a = w1.shape[2], a1.shape[2]
    second = jnp.concatenate([jnp.concatenate([w2[0], jnp.zeros((d_w, c), F32)], axis=1),
                              jnp.concatenate([jnp.zeros((d_a, c), F32), a2[0]], axis=1)], axis=0)
    glu_half = jnp.concatenate([jnp.ones((3 * c,), F32), jnp.full((w_in.shape[2] - 3 * c,), 0.5, F32)])
    wd = dict(
        norm_mix=row(norm_mix), w_in=(w_in[0] * glu_half).astype(BF16), mu_rkv=row(mu_rkv),
        w0=0.5 * row(w0), a0=0.5 * row(a0),
        w_lora1=jnp.concatenate([first, mixed], axis=0).astype(BF16), w_lora2=(0.5 * second).astype(BF16),
        g2=g2[0].astype(BF16), lora_dims=(d_w, d_a),
        k_k=row(k_k), k_a=row(k_a), r_k=row(r_k), ln_x_w=row(ln_x_w), ln_x_b=row(ln_x_b),
        seg=(lane[:, None] == lane[None, :]).astype(BF16),
        dw_w=dw_w[0].astype(F32), dw_b=row(dw_b), cln_w=0.5 * row(cln_w), cln_b=0.5 * row(cln_b),
        w_out=w_out[0].astype(BF16), norm_ffn=row(norm_ffn),
        w_router=jnp.concatenate(_split2(w_router), axis=1),
        norm_ple=row(norm_ple), w_ple_gate=w_ple_gate[0].astype(BF16), w_ple_proj=w_ple_proj[0].astype(BF16),
        norm_final=norm_final.reshape(1, -1).astype(F32),
    )
    bp, tp, _ = x_prompt.shape
    bs, ts, _ = x_sample.shape
    mp, ms = bp * tp, bs * ts

    zeros = lambda *s: jnp.zeros(s, F32)
    h1_p, xn_p, eid_p, gate_p, shift_p, wkv_p, conv_p = _layer_front(
        x_prompt, zeros(bp, 1, d), zeros(bp, c // HEAD_SIZE, HEAD_SIZE, HEAD_SIZE), zeros(bp, CONV_CARRY, c),
        wd, _tile_plan(bp, tp, True), True)
    h1_s, xn_s, eid_s, gate_s, shift_s, wkv_s, conv_s = _layer_front(
        x_sample, state_shift[0][:, None, :], state_wkv[0], cache_conv[0],
        wd, _tile_plan(bs, ts, False), False)

    bm = EXPERT_BLOCK
    tr_p, tr_s = min(TOKEN_TILE, mp), min(TOKEN_TILE, ms)
    plan, sched, n_used, n_blocks = _route_plan([eid_p[1], eid_s[1]], [tr_p, tr_s], bm)
    tiles_p = mp // tr_p
    xs = _dispatch(plan, n_used, [eid_p[1], eid_s[1]], [xn_p, xn_s], [tr_p, tr_s], n_blocks, bm)
    yb = _experts(sched, n_used, xs, w_exp_gate[0], w_exp_up[0], w_exp_down[0], bm)
    y_p = _final(plan, h1_p, eid_p[0], gate_p, p_prompt[0].reshape(mp, -1), yb, wd, tr_p, 0)
    y_s = _final(plan, h1_s, eid_s[0], gate_s, p_sample[0].reshape(ms, -1), yb, wd, tr_s, tiles_p)
    return (y_p.reshape(x_prompt.shape), y_s.reshape(x_sample.shape), shift_p[None], wkv_p[None], conv_p[None],
            shift_s[None], wkv_s[None], conv_s[None])
```

```python
import functools

import jax
import jax.numpy as jnp
from jax import lax
from jax.experimental import pallas as pl
from jax.experimental.pallas import tpu as pltpu

F32 = jnp.float32
BF16 = jnp.bfloat16
I32 = jnp.int32

HEAD_SIZE = 64
CONV_WIDTH = 31
CONV_CARRY = CONV_WIDTH - 1
SUBLANES = 8
LANES = 128
CARRY_PAD = 32
N_EXPERT_GROUPS = 4
EXPERTS_PER_GROUP = 8
N_EXPERTS = N_EXPERT_GROUPS * EXPERTS_PER_GROUP
ROUTER_LANES = 128
RMS_EPS = 1e-6
LN_EPS = 1e-5
GN_EPS = 64e-5
DECAY_SCALE = 0.6065306597126334
INV_BASE = 16
RUN_ALIGN = 16
LOCAL_PAD = N_EXPERTS * RUN_ALIGN
VMEM_LIMIT = 56 * 1024 * 1024

TOKEN_TILE = 512
MIX_IN_PARTS = 4
TOKEN_PARTS = 2
WKV_SUBTILE = 128
WKV_SUBTILES = 2
WKV_CHUNK = 64
EXPERT_BLOCK = 512

NN = ((1,), (0,))
NT = ((1,), (1,))
TN = ((0,), (0,))


def _dg(a, b, dims=NN):
    return lax.dot_general(a, b, (dims, ((), ())), preferred_element_type=F32)


def _split2(x):
    hi = x.astype(BF16)
    lo = (x - hi.astype(F32)).astype(BF16)
    return hi, lo


def _bdot(a, b, dims=NN):
    return _dg(a.astype(BF16), b.astype(BF16), dims)


def _mask_dot(mask_bf16, x):
    h, l = _split2(x)
    return _dg(mask_bf16, h) + _dg(mask_bf16, l)


def _seg_sum(x, seg_bf16):
    h, l = _split2(x)
    w = seg_bf16.shape[0]
    return jnp.concatenate([_dg(h[:, j:j + w], seg_bf16) + _dg(l[:, j:j + w], seg_bf16)
                            for j in range(0, x.shape[1], w)], axis=1)


def _rms(x, g):
    return x * lax.rsqrt(jnp.mean(x * x, axis=-1, keepdims=True) + RMS_EPS) * g


def _sigmoid(x):
    return 0.5 * jnp.tanh(0.5 * x) + 0.5


def _full(shape):
    n = len(shape)
    return pl.BlockSpec(shape, lambda *_: (0,) * n, pipeline_mode=pl.Buffered(1))


def _mix_in_kernel(x_ref, shift_ref, conv_ref, nm_ref, win_ref, murkv_ref, w0_ref, a0_ref, wl1_ref, wl2_ref, g2_ref,
                   kk_ref, ka_ref, rk_ref, seg_ref, dww_ref, dwb_ref, clnw_ref, clnb_ref,
                   wkv_out, aux_out, shift_out, conv_out,
                   xn_last, h_last, up_ext, shifted, *, tm, c, n_parts, d_w, d_a, seg_len):
    i = pl.program_id(1)
    n_seq = tm // seg_len if seg_len else 1
    stride = CARRY_PAD + seg_len

    if seg_len:
        for b in range(n_seq):
            up_ext[b * stride:b * stride + CARRY_PAD - CONV_CARRY, :] = jnp.zeros((CARRY_PAD - CONV_CARRY, c), F32)
            up_ext[b * stride + CARRY_PAD - CONV_CARRY:b * stride + CARRY_PAD, :] = conv_ref[b]
    else:
        @pl.when(i == 0)
        def _():
            sp = shift_ref[0]
            xn_last[...] = sp
            sp8 = jnp.broadcast_to(sp, (8, sp.shape[1])).astype(BF16)
            h_last[...] = _dg(sp8, win_ref[:, :3 * c])[0:1]
            up_ext[CARRY_PAD - CONV_CARRY:CARRY_PAD, :] = conv_ref[0]

    rp = tm // n_parts
    last_rows = {}

    def stream(part):
        rs = pl.ds(part * rp, rp)
        xn = _rms(x_ref[0, rs, :], nm_ref[...])
        hin = _dg(xn.astype(BF16), win_ref[...])
        h_rkv = hin[:, :3 * c]
        last_rows[part] = (xn[rp - 1:rp], h_rkv[rp - 1:rp])
        last_rows['xn'] = xn
        yield
        if seg_len:
            xn_prev = shift_ref[0]
            h_prev = _dg(xn_prev.astype(BF16), win_ref[:, :3 * c])
            first = (lax.broadcasted_iota(I32, (rp, 1), 0) & (seg_len - 1)) == 0
        else:
            xn_prev, h_prev = (xn_last[...], h_last[...]) if part == 0 else last_rows[part - 1]
            first = lax.broadcasted_iota(I32, (rp, 1), 0) == 0
        dx = jnp.where(first, xn_prev, pltpu.roll(xn, 1, 0)) - xn
        hprev = jnp.where(first, h_prev, pltpu.roll(h_rkv, 1, 0))
        rkv = h_rkv + (hprev - h_rkv) * murkv_ref[...]
        r = rkv[:, :c]
        k = rkv[:, c:2 * c]
        v = rkv[:, 2 * c:]
        l1 = _dg(jnp.concatenate([xn.astype(BF16), dx.astype(BF16)], axis=1), wl1_ref[...])
        lane = lax.broadcasted_iota(I32, l1.shape, 1)
        act = jnp.where(lane < d_w, jnp.tanh(l1), jnp.where(lane < d_w + d_a, l1, _sigmoid(l1))).astype(BF16)
        yield
        za = _dg(act[:, :d_w + d_a], wl2_ref[...])
        a = 0.5 * jnp.tanh(a0_ref[...] + za[:, c:]) + 0.5
        aux_out[0, rs, 0:c] = _dg(act[:, d_w + d_a:], g2_ref[...])
        seg = seg_ref[...]
        kk = k * kk_ref[...]
        kk = kk * jnp.minimum(lax.rsqrt(_seg_sum(kk * kk, seg)), 1e12)
        k2 = k * (1.0 + (a - 1.0) * ka_ref[...])
        wkv_out[0, rs, 0:c] = r
        wkv_out[0, rs, c:2 * c] = (-0.5 * DECAY_SCALE) * jnp.tanh(w0_ref[...] + za[:, :c]) - 0.5 * DECAY_SCALE
        wkv_out[0, rs, 2 * c:3 * c] = k2
        wkv_out[0, rs, 3 * c:4 * c] = v
        wkv_out[0, rs, 4 * c:5 * c] = -kk
        wkv_out[0, rs, 5 * c:6 * c] = kk * a
        yield
        aux_out[0, rs, c:2 * c] = _seg_sum(r * k2 * rk_ref[...], seg) * v
        u = hin[:, 3 * c:4 * c] * (jnp.tanh(hin[:, 4 * c:]) + 1.0)
        if seg_len:
            for b in range(n_seq):
                up_ext[b * stride + CARRY_PAD:(b + 1) * stride, :] = u[b * seg_len:(b + 1) * seg_len]
        else:
            up_ext[pl.ds(CARRY_PAD + part * rp, rp), :] = u
        yield
        first_row = CARRY_PAD - CONV_CARRY + part * rp
        n_out = n_seq * stride - CARRY_PAD if seg_len else rp
        for s in range(SUBLANES):
            span = n_out + (CONV_WIDTH - 1 - s) // SUBLANES * SUBLANES
            shifted[part, s, 0:span, :] = up_ext[pl.ds(first_row + s, span), :]
        z = dwb_ref[...]
        for j in range(CONV_WIDTH):
            s, m = j % SUBLANES, j // SUBLANES
            z = z + dww_ref[j:j + 1, :] * shifted[part, s, m * SUBLANES:m * SUBLANES + n_out, :]
        if seg_len:
            z = jnp.concatenate([z[b * stride:b * stride + seg_len] for b in range(n_seq)], axis=0)
        mu = jnp.mean(z, axis=-1, keepdims=True)
        zc = z - mu
        var = jnp.mean(zc * zc, axis=-1, keepdims=True)
        zh = zc * lax.rsqrt(var + LN_EPS) * clnw_ref[...] + clnb_ref[...]
        aux_out[0, rs, 2 * c:3 * c] = zh * (jnp.tanh(zh) + 1.0)
        yield

    _interleave(stream, n_parts)

    xn_end, h_end = last_rows[n_parts - 1]
    if seg_len:
        xn_tile = last_rows['xn']
        for b in range(n_seq):
            conv_out[b] = up_ext[(b + 1) * stride - CONV_CARRY:(b + 1) * stride, :]
            shift_out[b] = xn_tile[(b + 1) * seg_len - 1:(b + 1) * seg_len]
    else:
        tail = up_ext[pl.ds(tm + CARRY_PAD - CONV_CARRY, CONV_CARRY), :]
        up_ext[CARRY_PAD - CONV_CARRY:CARRY_PAD, :] = tail
        conv_out[0] = tail
        xn_last[...] = xn_end
        h_last[...] = h_end
        shift_out[0] = xn_end


def _mix_in(x, shift0, conv0, wd, tm, n_parts, whole_sequences):
    bsz, t, d = x.shape
    c = wd['w0'].shape[1]
    weights = [wd[n] for n in ('norm_mix', 'w_in', 'mu_rkv', 'w0', 'a0', 'w_lora1', 'w_lora2', 'g2',
                               'k_k', 'k_a', 'r_k', 'seg', 'dw_w', 'dw_b', 'cln_w', 'cln_b')]
    d_w, d_a = wd['lora_dims']
    if whole_sequences:
        tm, seg_len, n_seq, rows_b = bsz * t, t, bsz, 1
        assert t & (t - 1) == 0 and n_parts == 1
        x = x.reshape(1, tm, d)
        shift0 = jnp.repeat(shift0.reshape(bsz, d), t, axis=0).reshape(1, tm, d)
        grid = (1, 1)
        state = lambda rows, w: pl.BlockSpec((n_seq, rows, w), lambda b, i: (0, 0, 0))
        shift_in = pl.BlockSpec((1, tm, d), lambda b, i: (0, 0, 0))
        window_rows = n_seq * (CARRY_PAD + seg_len)
    else:
        seg_len, rows_b = 0, bsz
        grid = (bsz, t // tm)
        state = lambda rows, w: pl.BlockSpec((1, rows, w), lambda b, i: (b, 0, 0))
        shift_in = state(1, d)
        window_rows = tm + CARRY_PAD
    tok = lambda w: pl.BlockSpec((1, tm, w), lambda b, i: (b, i, 0))
    out_tok = lambda w: jax.ShapeDtypeStruct((rows_b, x.shape[1], w), F32)
    wkv_in, aux, shift_new, conv_new = pl.pallas_call(
        functools.partial(_mix_in_kernel, tm=tm, c=c, n_parts=n_parts, d_w=d_w, d_a=d_a, seg_len=seg_len),
        grid=grid,
        in_specs=[tok(d), shift_in, state(CONV_CARRY, c)] + [_full(w.shape) for w in weights],
        out_specs=[tok(6 * c), tok(3 * c), state(1, d), state(CONV_CARRY, c)],
        out_shape=[out_tok(6 * c), out_tok(3 * c), jax.ShapeDtypeStruct((bsz, 1, d), F32),
                   jax.ShapeDtypeStruct((bsz, CONV_CARRY, c), F32)],
        scratch_shapes=[pltpu.VMEM((1, d), F32), pltpu.VMEM((1, 3 * c), F32),
                        pltpu.VMEM((window_rows, c), F32),
                        pltpu.VMEM((n_parts, SUBLANES, (window_rows - CARRY_PAD) // n_parts + CARRY_PAD - SUBLANES, c),
                                   F32)],
        compiler_params=pltpu.CompilerParams(dimension_semantics=("arbitrary", "arbitrary"),
                                             vmem_limit_bytes=VMEM_LIMIT),
        name="mix_in",
    )(x, shift0, conv0, *weights)
    return wkv_in.reshape(bsz, t, 6 * c), aux.reshape(bsz, t, 3 * c), shift_new, conv_new


def _tri_inverse(n_strict, row, col, lg_chunk):
    lg_base = INV_BASE.bit_length() - 1
    same = lambda sh: (row >> sh) == (col >> sh)
    lg0 = min(lg_base, lg_chunk)
    n = row.shape[0]

    def expand(c, lg):
        return jnp.where(same(lg), jnp.concatenate([c] * (n >> lg), axis=0), 0.0).astype(BF16)

    def fold(x, lg):
        b = 1 << lg
        return functools.reduce(lambda u, v: u + v, [x[i:i + b] for i in range(0, n, b)])

    b0 = 1 << lg0
    row_c = lax.broadcasted_iota(I32, (b0, n), 0)
    col_c = lax.broadcasted_iota(I32, (b0, n), 1)
    p_full = [jnp.where(same(lg0), x, 0.0) for x in n_strict]
    p = [fold(x, lg0) for x in p_full]
    t = [jnp.where((col_c & (b0 - 1)) == row_c, 1.0, 0.0) + x for x in p]
    p_full = [x.astype(BF16) for x in p_full]
    for _ in range(lg0 - 1):
        p = [_dg(x.astype(BF16), y) for x, y in zip(p, p_full)]
        p_full = [expand(x, lg0) for x in p]
        t = [x + _dg(x.astype(BF16), y) for x, y in zip(t, p_full)]
    for lg in range(lg0, lg_chunk):
        off_mask = same(lg + 1) & jnp.logical_not(same(lg))
        t_full = [expand(x, lg) for x in t]
        u = [_dg(x.astype(BF16), jnp.where(off_mask, m, 0.0).astype(BF16)) for x, m in zip(t, n_strict)]
        add = [_dg(x.astype(BF16), y) for x, y in zip(u, t_full)]
        even = ((lax.broadcasted_iota(I32, (1 << lg, n), 1) >> lg) & 1) == 0
        t = [jnp.concatenate([jnp.where(even, x, 0.0), jnp.where(even, a, x)], axis=0) for x, a in zip(t, add)]
    return [expand(x, lg_chunk) for x in t]


def _wkv_kernel(in_ref, s0_ref, y_ref, s_out, s_scr, *, tt, n_sub, chunk, chained):
    ti = pl.program_id(1)
    c = in_ref.shape[2] // 6
    n_heads = c // HEAD_SIZE
    field = lambda j, rows: in_ref[0, rows, j * c:(j + 1) * c]
    n_chunks = tt // chunk
    lg_chunk = chunk.bit_length() - 1
    row = lax.broadcasted_iota(I32, (tt, tt), 0)
    col = lax.broadcasted_iota(I32, (tt, tt), 1)
    in_chunk = (row >> lg_chunk) == (col >> lg_chunk)
    tri_incl = in_chunk & (col <= row)
    tri_strict = in_chunk & (col < row)
    m_cum = jnp.where(tri_incl, 1.0, 0.0).astype(BF16)
    row_h = lax.broadcasted_iota(I32, (HEAD_SIZE, HEAD_SIZE), 0)
    col_h = lax.broadcasted_iota(I32, (HEAD_SIZE, HEAD_SIZE), 1)
    eye_h = row_h == col_h

    if chained:
        @pl.when(ti == 0)
        def _():
            s_scr[...] = s0_ref[0]

    rt_all, g_end_all, cols = [], [], {name: [] for name in ('v', 'rt', 'at', 'kt', 'bt', 'bd', 'kd')}
    hsl = [slice(HEAD_SIZE * h, HEAD_SIZE * (h + 1)) for h in range(n_heads)]
    for sub in range(n_sub):
        rows = pl.ds(sub * tt, tt)
        lw_all = field(1, rows)
        k_all = field(2, rows)
        b_all = field(5, rows)
        cum = _mask_dot(m_cum, lw_all)
        tot = jnp.concatenate([jnp.broadcast_to(cum[(ci + 1) * chunk - 1:(ci + 1) * chunk], (chunk, cum.shape[1]))
                               for ci in range(n_chunks)], axis=0)
        e_neg = jnp.exp(-cum)
        e_end = jnp.exp(tot - cum)
        full = dict(v=field(3, rows), rt=field(0, rows) * jnp.exp(cum),
                    at=field(4, rows) * jnp.exp(cum - lw_all), kt=k_all * e_neg, bt=b_all * e_neg,
                    bd=b_all * e_end, kd=k_all * e_end)
        rt_all.append(full['rt'])
        g_end_all.append(jnp.exp(tot))
        for name, z in full.items():
            cols[name] += [z[:, s_].astype(BF16) for s_ in hsl]
    v, rt, at, kt, bt, bd, kd = (cols[name] for name in ('v', 'rt', 'at', 'kt', 'bt', 'bd', 'kd'))

    units = range(n_sub * n_heads)
    mm = [_dg(jnp.concatenate([at[u], rt[u]], axis=0), jnp.concatenate([bt[u], kt[u]], axis=0), NT) for u in units]
    m_ab = [jnp.where(tri_strict, mm[u][:tt, :tt], 0.0) for u in units]
    m_ak = [jnp.where(tri_strict, mm[u][:tt, tt:], 0.0).astype(BF16) for u in units]
    m_rb = [jnp.where(tri_incl, mm[u][tt:, :tt], 0.0).astype(BF16) for u in units]
    m_rk = [jnp.where(tri_incl, mm[u][tt:, tt:], 0.0).astype(BF16) for u in units]
    tinv = _tri_inverse(m_ab, row, col, lg_chunk)
    akv = [_dg(m_ak[u], v[u]).astype(BF16) for u in units]
    w1 = [_dg(tinv[u], at[u]).astype(BF16) for u in units]
    w2 = [_dg(tinv[u], akv[u]).astype(BF16) for u in units]
    q = [(rt_all[u // n_heads][:, hsl[u % n_heads]] + _dg(m_rb[u], w1[u])).astype(BF16) for u in units]
    y0 = [_dg(jnp.concatenate([m_rb[u], m_rk[u]], axis=1), jnp.concatenate([w2[u], v[u]], axis=0)) for u in units]

    heads = range(n_heads)
    if chained:
        s = [s_scr[h] for h in heads]
    for sub in range(n_sub):
        for ci in range(n_chunks):
            cs = slice(ci * chunk, (ci + 1) * chunk)
            seq = sub * n_chunks + ci
            if not chained:
                s = [s0_ref[seq, h] for h in heads]
            g_row = g_end_all[sub][ci * chunk:ci * chunk + 1]
            un = [sub * n_heads + h for h in heads]
            gm = [jnp.where(eye_h, jnp.broadcast_to(g_row[:, hsl[h]], (HEAD_SIZE, HEAD_SIZE)), 0.0)
                  + _dg(bd[un[h]][cs], w1[un[h]][cs], TN) for h in heads]
            hm = [_dg(jnp.concatenate([bd[un[h]][cs], kd[un[h]][cs]], axis=0),
                      jnp.concatenate([w2[un[h]][cs], v[un[h]][cs]], axis=0), TN) for h in heads]
            for h in heads:
                y_ref[0, pl.ds(sub * tt + ci * chunk, chunk), hsl[h]] = _bdot(q[un[h]][cs], s[h]) + y0[un[h]][cs]
            s = [_bdot(gm[h], s[h]) + hm[h] for h in heads]
            if not chained:
                for h in heads:
                    s_out[seq, h] = s[h]
    if chained:
        for h in heads:
            s_scr[h] = s[h]
            s_out[0, h] = s[h]


def _wkv(wkv_in, s0t, tt, n_sub, chunk, chained):
    bsz, t, c6 = wkv_in.shape
    c = c6 // 6
    n_heads = c // HEAD_SIZE
    hs = HEAD_SIZE
    step = tt * n_sub
    if chained:
        grid = (bsz, t // step)
        tok = lambda w: pl.BlockSpec((1, step, w), lambda bi, ti: (bi, ti, 0))
        st = pl.BlockSpec((1, n_heads, hs, hs), lambda bi, ti: (bi, 0, 0, 0))
        y_shape = (bsz, t, c)
    else:
        assert t == chunk and (bsz * t) % step == 0
        n_seq = step // chunk
        grid = (1, bsz * t // step)
        tok = lambda w: pl.BlockSpec((1, step, w), lambda bi, ti: (0, ti, 0))
        st = pl.BlockSpec((n_seq, n_heads, hs, hs), lambda bi, ti: (ti, 0, 0, 0))
        wkv_in = wkv_in.reshape(1, bsz * t, c6)
        y_shape = (1, bsz * t, c)
    y, s_new = pl.pallas_call(
        functools.partial(_wkv_kernel, tt=tt, n_sub=n_sub, chunk=chunk, chained=chained),
        grid=grid,
        in_specs=[tok(c6), st],
        out_specs=[tok(c), st],
        out_shape=[jax.ShapeDtypeStruct(y_shape, F32), jax.ShapeDtypeStruct(s0t.shape, F32)],
        scratch_shapes=[pltpu.VMEM((n_heads, hs, hs), F32)],
        compiler_params=pltpu.CompilerParams(dimension_semantics=("arbitrary",) * 2,
                                             vmem_limit_bytes=VMEM_LIMIT),
        name="wkv",
    )(wkv_in, s0t)
    return y.reshape(bsz, t, c), s_new


def _interleave(make_stream, n_parts):
    for _ in zip(*[make_stream(part) for part in range(n_parts)]):
        pass


def _mix_out_kernel(x_ref, y_ref, aux_ref, lnw_ref, lnb_ref, seg_ref, wout_ref, nffn_ref,
                    wr_ref, h1_out, xn_out, eid_out, eidt_out, gate_out, *, c, n_parts):
    rows_per = x_ref.shape[0] // n_parts

    def stream(part):
        rs = pl.ds(part * rows_per, rows_per)
        seg = seg_ref[...]
        y = y_ref[rs, :]
        inv_n = 1.0 / HEAD_SIZE
        mu = _seg_sum(y, seg) * inv_n
        yield
        yc = y - mu
        var = _seg_sum(yc * yc, seg) * inv_n
        yield
        yn = yc * lax.rsqrt(var + GN_EPS) * lnw_ref[...] + lnb_ref[...]
        ya = (yn + aux_ref[rs, c:2 * c]) * aux_ref[rs, 0:c]
        mix = _dg(ya.astype(BF16), wout_ref[:c, :]) + _dg(aux_ref[rs, 2 * c:3 * c].astype(BF16), wout_ref[c:, :])
        yield
        h1 = x_ref[rs, :] + mix
        h1_out[rs, :] = h1
        xn = _rms(h1, nffn_ref[...])
        xn_out[rs, :] = xn.astype(BF16)
        xh, xl = _split2(xn)
        hi_lo = _dg(xh, wr_ref[...])
        logits = hi_lo[:, :ROUTER_LANES] + (hi_lo[:, ROUTER_LANES:] + _dg(xl, wr_ref[:, :ROUTER_LANES]))
        yield
        lane = lax.broadcasted_iota(I32, logits.shape, 1)
        neg = jnp.float32(-jnp.inf)
        is_g = (lane >= N_EXPERTS) & (lane < N_EXPERTS + N_EXPERT_GROUPS)
        glog = jnp.where(is_g, logits, neg)
        gmax = jnp.max(glog, axis=-1, keepdims=True)
        gsel = jnp.min(jnp.where(glog == gmax, lane, 4 * ROUTER_LANES), axis=-1, keepdims=True) - N_EXPERTS
        gp = 1.0 / jnp.sum(jnp.where(is_g, jnp.exp(glog - gmax), 0.0), axis=-1, keepdims=True)
        in_grp = (lane >= gsel * EXPERTS_PER_GROUP) & (lane < (gsel + 1) * EXPERTS_PER_GROUP)
        elog = jnp.where(in_grp, logits, neg)
        emax = jnp.max(elog, axis=-1, keepdims=True)
        ex = jnp.where(in_grp, jnp.exp(elog - emax), 0.0)
        eprob = ex / jnp.sum(ex, axis=-1, keepdims=True)
        eprob = jnp.where(in_grp, eprob, -1.0)
        yield
        v1 = jnp.max(eprob, axis=-1, keepdims=True)
        i1 = jnp.min(jnp.where(eprob == v1, lane, 4 * ROUTER_LANES), axis=-1, keepdims=True)
        rest = jnp.where(lane == i1, -1.0, eprob)
        v2 = jnp.max(rest, axis=-1, keepdims=True)
        i2 = jnp.min(jnp.where(rest == v2, lane, 4 * ROUTER_LANES), axis=-1, keepdims=True)
        denom = v1 + v2
        eid = jnp.where(lane == 0, i1, jnp.where(lane == 1, i2, 0))
        eid_out[rs, :] = eid
        eidt_out[:, rs] = jnp.transpose(eid)[:SUBLANES]
        gate_out[rs, :] =jnp.where(lane == 0, gp * v1 / denom, jnp.where(lane == 1, gp * v2 / denom, 0.0))
        yield

    _interleave(stream, n_parts)


def _mix_out(x2, y2, aux2, wd, tm, n_parts):
    m, d = x2.shape
    c = y2.shape[1]
    tokd = pl.BlockSpec((tm, d), lambda i: (i, 0))
    tokc = pl.BlockSpec((tm, c), lambda i: (i, 0))
    tokr = pl.BlockSpec((tm, ROUTER_LANES), lambda i: (i, 0))
    weights = [wd[n] for n in ('ln_x_w', 'ln_x_b', 'seg', 'w_out', 'norm_ffn', 'w_router')]
    return pl.pallas_call(
        functools.partial(_mix_out_kernel, c=c, n_parts=n_parts),
        grid=(m // tm,),
        in_specs=[tokd, tokc, pl.BlockSpec((tm, 3 * c), lambda i: (i, 0))] + [_full(w.shape) for w in weights],
        out_specs=[tokd, tokd, tokr, pl.BlockSpec((SUBLANES, tm), lambda i: (0, i)), tokr],
        out_shape=[jax.ShapeDtypeStruct((m, d), F32), jax.ShapeDtypeStruct((m, d), BF16),
                   jax.ShapeDtypeStruct((m, ROUTER_LANES), I32), jax.ShapeDtypeStruct((SUBLANES, m), I32),
                   jax.ShapeDtypeStruct((m, ROUTER_LANES), F32)],
        compiler_params=pltpu.CompilerParams(dimension_semantics=("arbitrary",), vmem_limit_bytes=VMEM_LIMIT),
        name="mix_out",
    )(x2, y2, aux2, *weights)


def _pow2_pieces(count, max_rows, fn):
    off = 0
    rows = max_rows
    while rows >= RUN_ALIGN:
        has = (count & (rows // RUN_ALIGN)) != 0

        @pl.when(has)
        def _(off=off, rows=rows):
            fn(off, rows)
        off = off + jnp.where(has, rows, 0)
        rows //= 2


def _for_each_expert(fn):
    def body(e, carry):
        fn(e)
        return carry
    lax.fori_loop(0, N_EXPERTS, body, 0)


def _pow2_floor(n):
    return 1 << (n.bit_length() - 1)


def _dispatch_kernel(nch_ref, tot_ref, off_ref, loff_ref, tn_ref, toff_ref, nu_ref, *refs, groups, bm, n_blocks):
    n_g = len(groups)
    eid_refs, xn_refs = refs[:n_g], refs[n_g + 1:2 * n_g + 1]
    loffc_ref = refs[n_g]
    xs_out, buf, sem, zbuf, zsem = refs[2 * n_g + 1:]
    i = pl.program_id(0)
    last = pl.num_programs(0) - 1
    max_run = max(tm for tm, _ in groups)

    def start_tile(tile):
        def per_expert(e):
            src0 = loff_ref[tile * N_EXPERTS + e]
            dst0 = off_ref[tile * N_EXPERTS + e]

            def piece(o, rows):
                src = buf.at[tile % 2, pl.ds(pl.multiple_of(src0 + o, RUN_ALIGN), rows)]
                dst = xs_out.at[pl.ds(pl.multiple_of(dst0 + o, RUN_ALIGN), rows)]
                pltpu.make_async_copy(src, dst, sem.at[tile % 2]).start()
            _pow2_pieces(nch_ref[tile * N_EXPERTS + e], max_run, piece)
        _for_each_expert(per_expert)

    def wait_tile(tile):
        def piece(o, rows):
            pltpu.make_async_copy(buf.at[tile % 2, pl.ds(0, rows)], xs_out.at[pl.ds(0, rows)], sem.at[tile % 2]).wait()
        _pow2_pieces(tot_ref[tile], _pow2_floor(buf.shape[1]), piece)

    @pl.when(i >= 2)
    def _():
        wait_tile(i - 2)

    def sort_tile(eid_ref, xn_ref, tm):
        e_rows = eid_ref[...]
        sub = lax.broadcasted_iota(I32, (N_EXPERTS, tm), 0)
        e1 = jnp.where(sub == e_rows[0:1], 1.0, 0.0)
        e2 = jnp.where(sub == e_rows[1:2], 1.0, 0.0)
        before = lax.broadcasted_iota(I32, (tm, tm), 0) < lax.broadcasted_iota(I32, (tm, tm), 1)
        slot = _dg((e1 + e2).astype(BF16), jnp.where(before, 1.0, 0.0).astype(BF16)) + loffc_ref[0]
        l1 = jnp.sum(slot * e1, axis=0, keepdims=True).astype(I32)
        l2 = jnp.sum(slot * e2, axis=0, keepdims=True).astype(I32)
        n_rows = 2 * tm + LOCAL_PAD
        rows = lax.broadcasted_iota(I32, (n_rows, tm), 0)
        perm = jnp.where((rows == l1) | (rows == l2), 1.0, 0.0).astype(BF16)
        buf[i % 2, 0:n_rows, :] = _dg(perm, xn_ref[...]).astype(BF16)

    first = 0
    for g, (tm, n_tiles) in enumerate(groups):
        pl.when((i >= first) & (i < first + n_tiles))(functools.partial(sort_tile, eid_refs[g], xn_refs[g], tm))
        first += n_tiles
    start_tile(i)

    @pl.when(i == 0)
    def _():
        zbuf[...] = jnp.zeros_like(zbuf)
        half = zbuf.shape[0]

        def zero_copy(off, rows):
            return pltpu.make_async_copy(zbuf.at[pl.ds(0, rows)], xs_out.at[pl.ds(off, rows)], zsem)

        def tails(fn):
            _for_each_expert(lambda e: _pow2_pieces(
                tn_ref[e], half, lambda o, rows: fn(pl.multiple_of(toff_ref[e] + o, RUN_ALIGN), rows)))

        def unused_blocks(fn):
            def body(b, carry):
                fn(pl.multiple_of(b * bm, bm), half)
                fn(pl.multiple_of(b * bm + half, half), half)
                return carry
            lax.fori_loop(nu_ref[0], n_blocks, body, 0)

        tails(lambda off, rows: zero_copy(off, rows).start())
        unused_blocks(lambda off, rows: zero_copy(off, rows).start())
        tails(lambda off, rows: zero_copy(off, rows).wait())
        unused_blocks(lambda off, rows: zero_copy(off, rows).wait())

    @pl.when(i == last)
    def _():
        @pl.when(i >= 1)
        def _():
            wait_tile(i - 1)
        wait_tile(i)


def _dispatch(plan, n_used, eids_t, xns, tms, n_blocks, bm):
    d = xns[0].shape[1]
    groups = tuple((tm, xn.shape[0] // tm) for xn, tm in zip(xns, tms))
    firsts = [sum(n for _, n in groups[:g]) for g in range(len(groups))]
    lbuf = 2 * max(tms) + LOCAL_PAD

    def tile_of(g):
        return lambda i: jnp.clip(i - firsts[g], 0, groups[g][1] - 1)

    imap = lambda f: (lambda i, *_: f(i))
    in_specs = ([pl.BlockSpec((SUBLANES, tm), imap(lambda i, g=g: (0, tile_of(g)(i)))) for g, tm in enumerate(tms)] +
                [pl.BlockSpec((1, N_EXPERTS, 1), imap(lambda i: (i, 0, 0)))] +
                [pl.BlockSpec((tm, d), imap(lambda i, g=g: (tile_of(g)(i), 0))) for g, tm in enumerate(tms)])
    return pl.pallas_call(
        functools.partial(_dispatch_kernel, groups=groups, bm=bm, n_blocks=n_blocks),
        grid_spec=pltpu.PrefetchScalarGridSpec(
            num_scalar_prefetch=7,
            grid=(sum(n for _, n in groups),),
            in_specs=in_specs,
            out_specs=pl.BlockSpec(memory_space=pl.ANY),
            scratch_shapes=[pltpu.VMEM((2, lbuf, d), BF16), pltpu.SemaphoreType.DMA((2,)),
                            pltpu.VMEM((bm // 2, d), BF16), pltpu.SemaphoreType.DMA],
        ),
        out_shape=jax.ShapeDtypeStruct((n_blocks * bm, d), BF16),
        compiler_params=pltpu.CompilerParams(dimension_semantics=("arbitrary",), vmem_limit_bytes=VMEM_LIMIT),
        name="moe_dispatch",
    )(plan['nch'], plan['tot'], plan['off'], plan['loff'], plan['tail_n'], plan['tail_off'], n_used, *eids_t,
      plan['loff_col'], *xns)


def _experts_kernel(be_ref, slot_ref, nxt_ref, nu_ref, xs_ref, wg_hbm, wu_hbm, wd_hbm, yb_ref,
                    wg_f, wu_f, wd_f, wg_b, wu_b, wd_b, sem):
    b = pl.program_id(0)

    def weight_copies(e, slot):
        pairs = ((wg_hbm, wg_f), (wu_hbm, wu_f), (wd_hbm, wd_f))
        return [pltpu.make_async_copy(src.at[e], dst.at[slot], sem.at[slot, j]) for j, (src, dst) in enumerate(pairs)]

    @pl.when(b == 0)
    def _():
        for cp in weight_copies(be_ref[0], slot_ref[0]):
            cp.start()

    def gated_mlp(wg, wu, wd):
        xb = xs_ref[...]
        hg = _dg(xb, wg)
        hu = _dg(xb, wu)
        act = (hg * _sigmoid(hg) * hu).astype(BF16)
        yb_ref[...] = _dg(act, wd).astype(BF16)

    active = b < nu_ref[0]
    first = (b == 0) | (be_ref[b] != be_ref[jnp.maximum(b - 1, 0)])

    @pl.when(active & first)
    def _():
        slot = slot_ref[b]
        for cp in weight_copies(be_ref[b], slot):
            cp.wait()

        @pl.when(nxt_ref[b] >= 0)
        def _():
            for cp in weight_copies(nxt_ref[b], 1 - slot):
                cp.start()
        wg, wu, wd = wg_f[slot].astype(BF16), wu_f[slot].astype(BF16), wd_f[slot].astype(BF16)
        wg_b[...] = wg
        wu_b[...] = wu
        wd_b[...] = wd
        gated_mlp(wg, wu, wd)

    @pl.when(active & jnp.logical_not(first))
    def _():
        gated_mlp(wg_b[...], wu_b[...], wd_b[...])

    @pl.when(pl.program_id(0) >= nu_ref[0])
    def _():
        yb_ref[...] = jnp.zeros_like(yb_ref)


def _experts(sched, n_used, xs, wg, wu, wdn, bm):
    p, d = xs.shape
    ff = wg.shape[2]
    n_blocks = p // bm
    return pl.pallas_call(
        _experts_kernel,
        grid_spec=pltpu.PrefetchScalarGridSpec(
            num_scalar_prefetch=4,
            grid=(n_blocks,),
            in_specs=[pl.BlockSpec((bm, d), lambda b, be, sl, nx, nu: (jnp.minimum(b, nu[0] - 1), 0)),
                      pl.BlockSpec(memory_space=pl.ANY), pl.BlockSpec(memory_space=pl.ANY),
                      pl.BlockSpec(memory_space=pl.ANY)],
            out_specs=pl.BlockSpec((bm, d), lambda b, *_: (b, 0)),
            scratch_shapes=[pltpu.VMEM((2, d, ff), F32), pltpu.VMEM((2, d, ff), F32), pltpu.VMEM((2, ff, d), F32),
                            pltpu.VMEM((d, ff), BF16), pltpu.VMEM((d, ff), BF16), pltpu.VMEM((ff, d), BF16),
                            pltpu.SemaphoreType.DMA((2, 3))],
        ),
        out_shape=jax.ShapeDtypeStruct((p, d), BF16),
        compiler_params=pltpu.CompilerParams(dimension_semantics=("arbitrary",), vmem_limit_bytes=VMEM_LIMIT),
        name="moe_experts",
    )(sched['expert'], sched['slot'], sched['next'], n_used, xs, wg, wu, wdn)


def _final_kernel(nch_ref, tot_ref, off_ref, loff_ref, h1_ref, eid_ref, gate_ref, loffr_ref, p_ref, yb_hbm,
                  nple_ref, wpg_ref, wpp_ref, nfin_ref, y_out, buf, sem, *, tm, lbuf, tile0, n_parts):
    i = pl.program_id(0)

    def fetch(tile):
        base = (tile0 + tile) * N_EXPERTS

        def per_expert(e):
            src0 = off_ref[base + e]
            dst0 = loff_ref[base + e]

            def piece(o, rows):
                src = yb_hbm.at[pl.ds(pl.multiple_of(src0 + o, RUN_ALIGN), rows)]
                dst = buf.at[tile % 2, pl.ds(pl.multiple_of(dst0 + o, RUN_ALIGN), rows)]
                pltpu.make_async_copy(src, dst, sem.at[tile % 2]).start()
            _pow2_pieces(nch_ref[base + e], tm, piece)
        _for_each_expert(per_expert)

    def wait_fetch(tile):
        def piece(o, rows):
            pltpu.make_async_copy(yb_hbm.at[pl.ds(0, rows)], buf.at[tile % 2, pl.ds(0, rows)], sem.at[tile % 2]).wait()
        _pow2_pieces(tot_ref[tile0 + tile], _pow2_floor(lbuf), piece)

    @pl.when(i == 0)
    def _():
        buf[...] = jnp.zeros_like(buf)
        fetch(i)

    @pl.when(i + 1 < pl.num_programs(0))
    def _():
        fetch(i + 1)

    eid = eid_ref[...]
    lane = lax.broadcasted_iota(I32, (tm, N_EXPERTS), 1)
    e12 = (jnp.where(lane == eid[:, 0:1], 1.0, 0.0) + jnp.where(lane == eid[:, 1:2], 1.0, 0.0)).astype(BF16)
    rows_per = tm // n_parts
    picks = []
    for part in range(n_parts):
        rs = pl.ds(part * rows_per, rows_per)
        lane_p = lax.broadcasted_iota(I32, (rows_per, N_EXPERTS), 1)
        eid_p = eid_ref[rs, :]
        e1 = jnp.where(lane_p == eid_p[:, 0:1], 1.0, 0.0)
        e2 = jnp.where(lane_p == eid_p[:, 1:2], 1.0, 0.0)
        before = (lax.broadcasted_iota(I32, (rows_per, tm), 1)
                  < lax.broadcasted_iota(I32, (rows_per, tm), 0) + part * rows_per)
        slot = _dg(jnp.where(before, 1.0, 0.0).astype(BF16), e12) + loffr_ref[0]
        l1 = jnp.sum(slot * e1, axis=1, keepdims=True).astype(I32)
        l2 = jnp.sum(slot * e2, axis=1, keepdims=True).astype(I32)
        cols = lax.broadcasted_iota(I32, (rows_per, lbuf), 1)
        gate = gate_ref[rs, :]
        picks.append(jnp.where(cols == l1, gate[:, 0:1], jnp.where(cols == l2, gate[:, 1:2], 0.0)).astype(BF16))

    wait_fetch(i)
    sorted_rows = buf[i % 2]

    def stream(part):
        rs = pl.ds(part * rows_per, rows_per)
        h2 = h1_ref[rs, :] + _dg(picks[part], sorted_rows)
        yield
        gate_in = _rms(h2, nple_ref[...]).astype(BF16)
        pg = _sigmoid(_dg(gate_in, wpg_ref[...]))
        yield
        h3 = h2 + pg * _dg(p_ref[rs, :].astype(BF16), wpp_ref[...])
        y_out[rs, :] = _rms(h3, nfin_ref[...])
        yield

    _interleave(stream, n_parts)


def _final(plan, h1, eid, gate, p2, yb, wd, tm, tile0):
    m, d = h1.shape
    pd = p2.shape[1]
    lbuf = 2 * tm + LOCAL_PAD
    weights = [wd[n] for n in ('norm_ple', 'w_ple_gate', 'w_ple_proj', 'norm_final')]
    imap = lambda f: (lambda i, *_: f(i))
    return pl.pallas_call(
        functools.partial(_final_kernel, tm=tm, lbuf=lbuf, tile0=tile0, n_parts=TOKEN_PARTS),
        grid_spec=pltpu.PrefetchScalarGridSpec(
            num_scalar_prefetch=4,
            grid=(m // tm,),
            in_specs=[pl.BlockSpec((tm, d), imap(lambda i: (i, 0))),
                      pl.BlockSpec((tm, ROUTER_LANES), imap(lambda i: (i, 0))),
                      pl.BlockSpec((tm, ROUTER_LANES), imap(lambda i: (i, 0))),
                      pl.BlockSpec((1, 1, N_EXPERTS), imap(lambda i: (tile0 + i, 0, 0))),
                      pl.BlockSpec((tm, pd), imap(lambda i: (i, 0))),
                      pl.BlockSpec(memory_space=pl.ANY)] +
                     [pl.BlockSpec(w.shape, imap(lambda i, n=len(w.shape): (0,) * n)) for w in weights],
            out_specs=pl.BlockSpec((tm, d), imap(lambda i: (i, 0))),
            scratch_shapes=[pltpu.VMEM((2, lbuf, d), BF16), pltpu.SemaphoreType.DMA((2,))],
        ),
        out_shape=jax.ShapeDtypeStruct((m, d), F32),
        compiler_params=pltpu.CompilerParams(dimension_semantics=("arbitrary",), vmem_limit_bytes=VMEM_LIMIT),
        name="moe_final",
    )(plan['nch'], plan['tot'], plan['off'], plan['loff'], h1, eid, gate, plan['loff_row'], p2, yb, *weights)


def _route_plan(eids, tms, bm):
    experts = jnp.arange(N_EXPERTS, dtype=I32)
    counts = []
    for eid, tm in zip(eids, tms):
        onehot = (eid[:2, :, None] == experts).astype(I32)
        counts.append(onehot.reshape(2, -1, tm, N_EXPERTS).sum(axis=(0, 2)))
    n = jnp.concatenate(counts)
    n_al = (n + RUN_ALIGN - 1) // RUN_ALIGN * RUN_ALIGN
    loff = jnp.cumsum(n_al, axis=1) - n_al
    used = n_al.sum(axis=0)
    region = (used + bm - 1) // bm * bm
    pend = jnp.cumsum(region)
    off = (pend - region)[None, :] + jnp.cumsum(n_al, axis=0) - n_al
    n_assign = sum(2 * e.shape[1] for e in eids)
    n_blocks = -(-(n_assign + (RUN_ALIGN - 1) * N_EXPERTS * n.shape[0] + N_EXPERTS * (bm - 1)) // bm)
    block_start = jnp.arange(n_blocks, dtype=I32) * bm
    block_expert = jnp.minimum(jnp.sum((pend[None, :] <= block_start[:, None]).astype(I32), axis=1), N_EXPERTS - 1)
    plan = dict(nch=(n_al // RUN_ALIGN).reshape(-1).astype(I32), off=off.reshape(-1).astype(I32),
                tot=(n_al.sum(axis=1) // RUN_ALIGN).astype(I32),
                loff=loff.reshape(-1).astype(I32), tail_n=((region - used) // RUN_ALIGN).astype(I32),
                tail_off=(pend - region + used).astype(I32), loff_col=loff.astype(F32)[:, :, None],
                loff_row=loff.astype(F32)[:, None, :])
    n_used = pend[-1] // bm
    prev = jnp.concatenate([jnp.full((1,), -1, I32), block_expert[:-1].astype(I32)])
    slot = (jnp.cumsum((block_expert != prev).astype(I32)) - 1) % 2
    after = pend[block_expert] // bm
    nxt = jnp.where(after < n_used, block_expert[jnp.minimum(after, n_blocks - 1)], -1)
    sched = dict(expert=block_expert.astype(I32), slot=slot.astype(I32), next=nxt.astype(I32))
    return plan, sched, n_used.astype(I32).reshape(1), n_blocks


def _tile_plan(bsz, t, long_sequence):
    rows = bsz * t
    span = t if long_sequence else rows
    wkv_sub = min(WKV_SUBTILE, span)
    return dict(
        mix_in=min(TOKEN_TILE, t), mix_in_parts=MIX_IN_PARTS if (long_sequence and t >= TOKEN_TILE) else 1,
        wkv_sub=wkv_sub, wkv_n_sub=min(WKV_SUBTILES, span // wkv_sub),
        wkv_chunk=min(WKV_CHUNK, t),
        token=min(TOKEN_TILE, rows), token_parts=TOKEN_PARTS)


def _layer_front(x, shift0, wkv0, conv0, wd, tiles, long_sequence):
    bsz, t, d = x.shape
    wkv_in, aux, shift_new, conv_new = _mix_in(
        x, shift0, conv0, wd, tiles['mix_in'], tiles['mix_in_parts'], not long_sequence)
    y, s_new = _wkv(wkv_in, jnp.swapaxes(wkv0, -1, -2), tiles['wkv_sub'], tiles['wkv_n_sub'], tiles['wkv_chunk'],
                    long_sequence)
    flat = lambda z: z.reshape(bsz * t, z.shape[-1])
    h1, xn2, eid, eid_t, gate = _mix_out(flat(x), flat(y), flat(aux), wd, tiles['token'], tiles['token_parts'])
    return h1, xn2, (eid, eid_t), gate, shift_new.reshape(bsz, d), jnp.swapaxes(s_new, -1, -2), conv_new


def kernel(x_prompt, x_sample, state_shift, state_wkv, cache_conv, p_prompt, p_sample, norm_mix, w_in, mu_rkv, mu_w, mu_a, mu_g, w0, w1, w2, a0, a1, a2, g1, g2, k_k, k_a, r_k, ln_x_w, ln_x_b, dw_w, dw_b, cln_w, cln_b, w_out, norm_ffn, w_router_group, w_router_expert, w_exp_gate, w_exp_up, w_exp_down, norm_ple, w_ple_gate, w_ple_proj, norm_final):
    depth = norm_mix.shape[0]
    assert depth == 1
    d = x_prompt.shape[-1]
    c = w0.shape[-1]
    row = lambda z: z[0].reshape(1, -1).astype(F32)
    lane = jnp.arange(LANES, dtype=I32) // HEAD_SIZE
    w_router = jnp.concatenate([w_router_expert[0], w_router_group[0],
                                jnp.zeros((d, ROUTER_LANES - N_EXPERTS - N_EXPERT_GROUPS), F32)], axis=1)
    first = jnp.concatenate([w1[0], a1[0], g1[0]], axis=1)
    mixed = jnp.concatenate([mu_w[0][:, None] * w1[0], mu_a[0][:, None] * a1[0], mu_g[0][:, None] * g1[0]], axis=1)
    d_w, d_a = w1.shape[2], a1.shape[2]
    second = jnp.concatenate([jnp.concatenate([w2[0], jnp.zeros((d_w, c), F32)], axis=1),
                              jnp.concatenate([jnp.zeros((d_a, c), F32), a2[0]], axis=1)], axis=0)
    glu_half = jnp.concatenate([jnp.ones((3 * c,), F32), jnp.full((w_in.shape[2] - 3 * c,), 0.5, F32)])
    wd = dict(
        norm_mix=row(norm_mix), w_in=(w_in[0] * glu_half).astype(BF16), mu_rkv=row(mu_rkv),
        w0=0.5 * row(w0), a0=0.5 * row(a0),
        w_lora1=jnp.concatenate([first, mixed], axis=0).astype(BF16), w_lora2=(0.5 * second).astype(BF16),
        g2=g2[0].astype(BF16), lora_dims=(d_w, d_a),
        k_k=row(k_k), k_a=row(k_a), r_k=row(r_k), ln_x_w=row(ln_x_w), ln_x_b=row(ln_x_b),
        seg=(lane[:, None] == lane[None, :]).astype(BF16),
        dw_w=dw_w[0].astype(F32), dw_b=row(dw_b), cln_w=0.5 * row(cln_w), cln_b=0.5 * row(cln_b),
        w_out=w_out[0].astype(BF16), norm_ffn=row(norm_ffn),
        w_router=jnp.concatenate(_split2(w_router), axis=1),
        norm_ple=row(norm_ple), w_ple_gate=w_ple_gate[0].astype(BF16), w_ple_proj=w_ple_proj[0].astype(BF16),
        norm_final=norm_final.reshape(1, -1).astype(F32),
    )
    bp, tp, _ = x_prompt.shape
    bs, ts, _ = x_sample.shape
    mp, ms = bp * tp, bs * ts

    zeros = lambda *s: jnp.zeros(s, F32)
    h1_p, xn_p, eid_p, gate_p, shift_p, wkv_p, conv_p = _layer_front(
        x_prompt, zeros(bp, 1, d), zeros(bp, c // HEAD_SIZE, HEAD_SIZE, HEAD_SIZE), zeros(bp, CONV_CARRY, c),
        wd, _tile_plan(bp, tp, True), True)
    h1_s, xn_s, eid_s, gate_s, shift_s, wkv_s, conv_s = _layer_front(
        x_sample, state_shift[0][:, None, :], state_wkv[0], cache_conv[0],
        wd, _tile_plan(bs, ts, False), False)

    bm = EXPERT_BLOCK
    tr_p, tr_s = min(TOKEN_TILE, mp), min(TOKEN_TILE, ms)
    plan, sched, n_used, n_blocks = _route_plan([eid_p[1], eid_s[1]], [tr_p, tr_s], bm)
    tiles_p = mp // tr_p
    xs = _dispatch(plan, n_used, [eid_p[1], eid_s[1]], [xn_p, xn_s], [tr_p, tr_s], n_blocks, bm)
    yb = _experts(sched, n_used, xs, w_exp_gate[0], w_exp_up[0], w_exp_down[0], bm)
    y_p = _final(plan, h1_p, eid_p[0], gate_p, p_prompt[0].reshape(mp, -1), yb, wd, tr_p, 0)
    y_s = _final(plan, h1_s, eid_s[0], gate_s, p_sample[0].reshape(ms, -1), yb, wd, tr_s, tiles_p)
    return (y_p.reshape(x_prompt.shape), y_s.reshape(x_sample.shape), shift_p[None], wkv_p[None], conv_p[None],
            shift_s[None], wkv_s[None], conv_s[None])
```

```python
import functools

import jax
import jax.numpy as jnp
from jax import lax
from jax.experimental import pallas as pl
from jax.experimental.pallas import tpu as pltpu

F32 = jnp.float32
BF16 = jnp.bfloat16
I32 = jnp.int32

HEAD_SIZE = 64
CONV_WIDTH = 31
CONV_CARRY = CONV_WIDTH - 1
SUBLANES = 8
LANES = 128
CARRY_PAD = 32
N_EXPERT_GROUPS = 4
EXPERTS_PER_GROUP = 8
N_EXPERTS = N_EXPERT_GROUPS * EXPERTS_PER_GROUP
ROUTER_LANES = 128
RMS_EPS = 1e-6
LN_EPS = 1e-5
GN_EPS = 64e-5
DECAY_SCALE = 0.6065306597126334
INV_BASE = 16
RUN_ALIGN = 16
LOCAL_PAD = N_EXPERTS * RUN_ALIGN
VMEM_LIMIT = 56 * 1024 * 1024

TOKEN_TILE = 512
MIX_IN_PARTS = 4
TOKEN_PARTS = 2
WKV_SUBTILE = 128
WKV_SUBTILES = 2
WKV_CHUNK = 64
EXPERT_BLOCK = 512

NN = ((1,), (0,))
NT = ((1,), (1,))
TN = ((0,), (0,))


def _dg(a, b, dims=NN):
    return lax.dot_general(a, b, (dims, ((), ())), preferred_element_type=F32)


def _split2(x):
    hi = x.astype(BF16)
    lo = (x - hi.astype(F32)).astype(BF16)
    return hi, lo


def _bdot(a, b, dims=NN):
    return _dg(a.astype(BF16), b.astype(BF16), dims)


def _mask_dot(mask_bf16, x):
    h, l = _split2(x)
    return _dg(mask_bf16, h) + _dg(mask_bf16, l)


def _seg_sum(x, seg_bf16):
    h, l = _split2(x)
    w = seg_bf16.shape[0]
    return jnp.concatenate([_dg(h[:, j:j + w], seg_bf16) + _dg(l[:, j:j + w], seg_bf16)
                            for j in range(0, x.shape[1], w)], axis=1)


def _rms(x, g):
    return x * lax.rsqrt(jnp.mean(x * x, axis=-1, keepdims=True) + RMS_EPS) * g


def _sigmoid(x):
    return 0.5 * jnp.tanh(0.5 * x) + 0.5


def _full(shape):
    n = len(shape)
    return pl.BlockSpec(shape, lambda *_: (0,) * n, pipeline_mode=pl.Buffered(1))


def _mix_in_kernel(x_ref, shift_ref, conv_ref, nm_ref, win_ref, murkv_ref, w0_ref, a0_ref, wl1_ref, wl2_ref, g2_ref,
                   kk_ref, ka_ref, rk_ref, seg_ref, dww_ref, dwb_ref, clnw_ref, clnb_ref,
                   wkv_out, aux_out, shift_out, conv_out,
                   xn_last, h_last, up_ext, shifted, *, tm, c, n_parts, d_w, d_a, seg_len):
    i = pl.program_id(1)
    n_seq = tm // seg_len if seg_len else 1
    stride = CARRY_PAD + seg_len

    if seg_len:
        for b in range(n_seq):
            up_ext[b * stride:b * stride + CARRY_PAD - CONV_CARRY, :] = jnp.zeros((CARRY_PAD - CONV_CARRY, c), F32)
            up_ext[b * stride + CARRY_PAD - CONV_CARRY:b * stride + CARRY_PAD, :] = conv_ref[b]
    else:
        @pl.when(i == 0)
        def _():
            sp = shift_ref[0]
            xn_last[...] = sp
            sp8 = jnp.broadcast_to(sp, (8, sp.shape[1])).astype(BF16)
            h_last[...] = _dg(sp8, win_ref[:, :3 * c])[0:1]
            up_ext[CARRY_PAD - CONV_CARRY:CARRY_PAD, :] = conv_ref[0]

    rp = tm // n_parts
    last_rows = {}

    def stream(part):
        rs = pl.ds(part * rp, rp)
        xn = _rms(x_ref[0, rs, :], nm_ref[...])
        hin = _dg(xn.astype(BF16), win_ref[...])
        h_rkv = hin[:, :3 * c]
        last_rows[part] = (xn[rp - 1:rp], h_rkv[rp - 1:rp])
        last_rows['xn'] = xn
        yield
        if seg_len:
            xn_prev = shift_ref[0]
            h_prev = _dg(xn_prev.astype(BF16), win_ref[:, :3 * c])
            first = (lax.broadcasted_iota(I32, (rp, 1), 0) & (seg_len - 1)) == 0
        else:
            xn_prev, h_prev = (xn_last[...], h_last[...]) if part == 0 else last_rows[part - 1]
            first = lax.broadcasted_iota(I32, (rp, 1), 0) == 0
        dx = jnp.where(first, xn_prev, pltpu.roll(xn, 1, 0)) - xn
        hprev = jnp.where(first, h_prev, pltpu.roll(h_rkv, 1, 0))
        rkv = h_rkv + (hprev - h_rkv) * murkv_ref[...]
        r = rkv[:, :c]
        k = rkv[:, c:2 * c]
        v = rkv[:, 2 * c:]
        l1 = _dg(jnp.concatenate([xn.astype(BF16), dx.astype(BF16)], axis=1), wl1_ref[...])
        lane = lax.broadcasted_iota(I32, l1.shape, 1)
        act = jnp.where(lane < d_w, jnp.tanh(l1), jnp.where(lane < d_w + d_a, l1, _sigmoid(l1))).astype(BF16)
        yield
        za = _dg(act[:, :d_w + d_a], wl2_ref[...])
        a = 0.5 * jnp.tanh(a0_ref[...] + za[:, c:]) + 0.5
        aux_out[0, rs, 0:c] = _dg(act[:, d_w + d_a:], g2_ref[...])
        seg = seg_ref[...]
        kk = k * kk_ref[...]
        kk = kk * jnp.minimum(lax.rsqrt(_seg_sum(kk * kk, seg)), 1e12)
        k2 = k * (1.0 + (a - 1.0) * ka_ref[...])
        wkv_out[0, rs, 0:c] = r
        wkv_out[0, rs, c:2 * c] = (-0.5 * DECAY_SCALE) * jnp.tanh(w0_ref[...] + za[:, :c]) - 0.5 * DECAY_SCALE
        wkv_out[0, rs, 2 * c:3 * c] = k2
        wkv_out[0, rs, 3 * c:4 * c] = v
        wkv_out[0, rs, 4 * c:5 * c] = -kk
        wkv_out[0, rs, 5 * c:6 * c] = kk * a
        yield
        aux_out[0, rs, c:2 * c] = _seg_sum(r * k2 * rk_ref[...], seg) * v
        u = hin[:, 3 * c:4 * c] * (jnp.tanh(hin[:, 4 * c:]) + 1.0)
        if seg_len:
            for b in range(n_seq):
                up_ext[b * stride + CARRY_PAD:(b + 1) * stride, :] = u[b * seg_len:(b + 1) * seg_len]
        else:
            up_ext[pl.ds(CARRY_PAD + part * rp, rp), :] = u
        yield
        first_row = CARRY_PAD - CONV_CARRY + part * rp
        n_out = n_seq * stride - CARRY_PAD if seg_len else rp
        for s in range(SUBLANES):
            span = n_out + (CONV_WIDTH - 1 - s) // SUBLANES * SUBLANES
            shifted[part, s, 0:span, :] = up_ext[pl.ds(first_row + s, span), :]
        z = dwb_ref[...]
        for j in range(CONV_WIDTH):
            s, m = j % SUBLANES, j // SUBLANES
            z = z + dww_ref[j:j + 1, :] * shifted[part, s, m * SUBLANES:m * SUBLANES + n_out, :]
        if seg_len:
            z = jnp.concatenate([z[b * stride:b * stride + seg_len] for b in range(n_seq)], axis=0)
        mu = jnp.mean(z, axis=-1, keepdims=True)
        zc = z - mu
        var = jnp.mean(zc * zc, axis=-1, keepdims=True)
        zh = zc * lax.rsqrt(var + LN_EPS) * clnw_ref[...] + clnb_ref[...]
        aux_out[0, rs, 2 * c:3 * c] = zh * (jnp.tanh(zh) + 1.0)
        yield

    _interleave(stream, n_parts)

    xn_end, h_end = last_rows[n_parts - 1]
    if seg_len:
        xn_tile = last_rows['xn']
        for b in range(n_seq):
            conv_out[b] = up_ext[(b + 1) * stride - CONV_CARRY:(b + 1) * stride, :]
            shift_out[b] = xn_tile[(b + 1) * seg_len - 1:(b + 1) * seg_len]
    else:
        tail = up_ext[pl.ds(tm + CARRY_PAD - CONV_CARRY, CONV_CARRY), :]
        up_ext[CARRY_PAD - CONV_CARRY:CARRY_PAD, :] = tail
        conv_out[0] = tail
        xn_last[...] = xn_end
        h_last[...] = h_end
        shift_out[0] = xn_end


def _mix_in(x, shift0, conv0, wd, tm, n_parts, whole_sequences):
    bsz, t, d = x.shape
    c = wd['w0'].shape[1]
    weights = [wd[n] for n in ('norm_mix', 'w_in', 'mu_rkv', 'w0', 'a0', 'w_lora1', 'w_lora2', 'g2',
                               'k_k', 'k_a', 'r_k', 'seg', 'dw_w', 'dw_b', 'cln_w', 'cln_b')]
    d_w, d_a = wd['lora_dims']
    if whole_sequences:
        tm, seg_len, n_seq, rows_b = bsz * t, t, bsz, 1
        assert t & (t - 1) == 0 and n_parts == 1
        x = x.reshape(1, tm, d)
        shift0 = jnp.repeat(shift0.reshape(bsz, d), t, axis=0).reshape(1, tm, d)
        grid = (1, 1)
        state = lambda rows, w: pl.BlockSpec((n_seq, rows, w), lambda b, i: (0, 0, 0))
        shift_in = pl.BlockSpec((1, tm, d), lambda b, i: (0, 0, 0))
        window_rows = n_seq * (CARRY_PAD + seg_len)
    else:
        seg_len, rows_b = 0, bsz
        grid = (bsz, t // tm)
        state = lambda rows, w: pl.BlockSpec((1, rows, w), lambda b, i: (b, 0, 0))
        shift_in = state(1, d)
        window_rows = tm + CARRY_PAD
    tok = lambda w: pl.BlockSpec((1, tm, w), lambda b, i: (b, i, 0))
    out_tok = lambda w: jax.ShapeDtypeStruct((rows_b, x.shape[1], w), F32)
    wkv_in, aux, shift_new, conv_new = pl.pallas_call(
        functools.partial(_mix_in_kernel, tm=tm, c=c, n_parts=n_parts, d_w=d_w, d_a=d_a, seg_len=seg_len),
        grid=grid,
        in_specs=[tok(d), shift_in, state(CONV_CARRY, c)] + [_full(w.shape) for w in weights],
        out_specs=[tok(6 * c), tok(3 * c), state(1, d), state(CONV_CARRY, c)],
        out_shape=[out_tok(6 * c), out_tok(3 * c), jax.ShapeDtypeStruct((bsz, 1, d), F32),
                   jax.ShapeDtypeStruct((bsz, CONV_CARRY, c), F32)],
        scratch_shapes=[pltpu.VMEM((1, d), F32), pltpu.VMEM((1, 3 * c), F32),
                        pltpu.VMEM((window_rows, c), F32),
                        pltpu.VMEM((n_parts, SUBLANES, (window_rows - CARRY_PAD) // n_parts + CARRY_PAD - SUBLANES, c),
                                   F32)],
        compiler_params=pltpu.CompilerParams(dimension_semantics=("arbitrary", "arbitrary"),
                                             vmem_limit_bytes=VMEM_LIMIT),
        name="mix_in",
    )(x, shift0, conv0, *weights)
    return wkv_in.reshape(bsz, t, 6 * c), aux.reshape(bsz, t, 3 * c), shift_new, conv_new


def _tri_inverse(n_strict, row, col, lg_chunk):
    lg_base = INV_BASE.bit_length() - 1
    same = lambda sh: (row >> sh) == (col >> sh)
    lg0 = min(lg_base, lg_chunk)
    n = row.shape[0]

    def expand(c, lg):
        return jnp.where(same(lg), jnp.concatenate([c] * (n >> lg), axis=0), 0.0).astype(BF16)

    def fold(x, lg):
        b = 1 << lg
        return functools.reduce(lambda u, v: u + v, [x[i:i + b] for i in range(0, n, b)])

    b0 = 1 << lg0
    row_c = lax.broadcasted_iota(I32, (b0, n), 0)
    col_c = lax.broadcasted_iota(I32, (b0, n), 1)
    p_full = [jnp.where(same(lg0), x, 0.0) for x in n_strict]
    p = [fold(x, lg0) for x in p_full]
    t = [jnp.where((col_c & (b0 - 1)) == row_c, 1.0, 0.0) + x for x in p]
    p_full = [x.astype(BF16) for x in p_full]
    for _ in range(lg0 - 1):
        p = [_dg(x.astype(BF16), y) for x, y in zip(p, p_full)]
        p_full = [expand(x, lg0) for x in p]
        t = [x + _dg(x.astype(BF16), y) for x, y in zip(t, p_full)]
    for lg in range(lg0, lg_chunk):
        off_mask = same(lg + 1) & jnp.logical_not(same(lg))
        t_full = [expand(x, lg) for x in t]
        u = [_dg(x.astype(BF16), jnp.where(off_mask, m, 0.0).astype(BF16)) for x, m in zip(t, n_strict)]
        add = [_dg(x.astype(BF16), y) for x, y in zip(u, t_full)]
        even = ((lax.broadcasted_iota(I32, (1 << lg, n), 1) >> lg) & 1) == 0
        t = [jnp.concatenate([jnp.where(even, x, 0.0), jnp.where(even, a, x)], axis=0) for x, a in zip(t, add)]
    return [expand(x, lg_chunk) for x in t]


def _wkv_kernel(in_ref, s0_ref, y_ref, s_out, s_scr, *, tt, n_sub, chunk, chained):
    ti = pl.program_id(1)
    c = in_ref.shape[2] // 6
    n_heads = c // HEAD_SIZE
    field = lambda j, rows: in_ref[0, rows, j * c:(j + 1) * c]
    n_chunks = tt // chunk
    lg_chunk = chunk.bit_length() - 1
    row = lax.broadcasted_iota(I32, (tt, tt), 0)
    col = lax.broadcasted_iota(I32, (tt, tt), 1)
    in_chunk = (row >> lg_chunk) == (col >> lg_chunk)
    tri_incl = in_chunk & (col <= row)
    tri_strict = in_chunk & (col < row)
    m_cum = jnp.where(tri_incl, 1.0, 0.0).astype(BF16)
    row_h = lax.broadcasted_iota(I32, (HEAD_SIZE, HEAD_SIZE), 0)
    col_h = lax.broadcasted_iota(I32, (HEAD_SIZE, HEAD_SIZE), 1)
    eye_h = row_h == col_h

    if chained:
        @pl.when(ti == 0)
        def _():
            s_scr[...] = s0_ref[0]

    rt_all, g_end_all, cols = [], [], {name: [] for name in ('v', 'rt', 'at', 'kt', 'bt', 'bd', 'kd')}
    hsl = [slice(HEAD_SIZE * h, HEAD_SIZE * (h + 1)) for h in range(n_heads)]
    for sub in range(n_sub):
        rows = pl.ds(sub * tt, tt)
        lw_all = field(1, rows)
        k_all = field(2, rows)
        b_all = field(5, rows)
        cum = _mask_dot(m_cum, lw_all)
        tot = jnp.concatenate([jnp.broadcast_to(cum[(ci + 1) * chunk - 1:(ci + 1) * chunk], (chunk, cum.shape[1]))
                               for ci in range(n_chunks)], axis=0)
        e_neg = jnp.exp(-cum)
        e_end = jnp.exp(tot - cum)
        full = dict(v=field(3, rows), rt=field(0, rows) * jnp.exp(cum),
                    at=field(4, rows) * jnp.exp(cum - lw_all), kt=k_all * e_neg, bt=b_all * e_neg,
                    bd=b_all * e_end, kd=k_all * e_end)
        rt_all.append(full['rt'])
        g_end_all.append(jnp.exp(tot))
        for name, z in full.items():
            cols[name] += [z[:, s_].astype(BF16) for s_ in hsl]
    v, rt, at, kt, bt, bd, kd = (cols[name] for name in ('v', 'rt', 'at', 'kt', 'bt', 'bd', 'kd'))

    units = range(n_sub * n_heads)
    mm = [_dg(jnp.concatenate([at[u], rt[u]], axis=0), jnp.concatenate([bt[u], kt[u]], axis=0), NT) for u in units]
    m_ab = [jnp.where(tri_strict, mm[u][:tt, :tt], 0.0) for u in units]
    m_ak = [jnp.where(tri_strict, mm[u][:tt, tt:], 0.0).astype(BF16) for u in units]
    m_rb = [jnp.where(tri_incl, mm[u][tt:, :tt], 0.0).astype(BF16) for u in units]
    m_rk = [jnp.where(tri_incl, mm[u][tt:, tt:], 0.0).astype(BF16) for u in units]
    tinv = _tri_inverse(m_ab, row, col, lg_chunk)
    akv = [_dg(m_ak[u], v[u]).astype(BF16) for u in units]
    w1 = [_dg(tinv[u], at[u]).astype(BF16) for u in units]
    w2 = [_dg(tinv[u], akv[u]).astype(BF16) for u in units]
    q = [(rt_all[u // n_heads][:, hsl[u % n_heads]] + _dg(m_rb[u], w1[u])).astype(BF16) for u in units]
    y0 = [_dg(jnp.concatenate([m_rb[u], m_rk[u]], axis=1), jnp.concatenate([w2[u], v[u]], axis=0)) for u in units]

    heads = range(n_heads)
    if chained:
        s = [s_scr[h] for h in heads]
    for sub in range(n_sub):
        for ci in range(n_chunks):
            cs = slice(ci * chunk, (ci + 1) * chunk)
            seq = sub * n_chunks + ci
            if not chained:
                s = [s0_ref[seq, h] for h in heads]
            g_row = g_end_all[sub][ci * chunk:ci * chunk + 1]
            un = [sub * n_heads + h for h in heads]
            gm = [jnp.where(eye_h, jnp.broadcast_to(g_row[:, hsl[h]], (HEAD_SIZE, HEAD_SIZE)), 0.0)
                  + _dg(bd[un[h]][cs], w1[un[h]][cs], TN) for h in heads]
            hm = [_dg(jnp.concatenate([bd[un[h]][cs], kd[un[h]][cs]], axis=0),
                      jnp.concatenate([w2[un[h]][cs], v[un[h]][cs]], axis=0), TN) for h in heads]
            for h in heads:
                y_ref[0, pl.ds(sub * tt + ci * chunk, chunk), hsl[h]] = _bdot(q[un[h]][cs], s[h]) + y0[un[h]][cs]
            s = [_bdot(gm[h], s[h]) + hm[h] for h in heads]
            if not chained:
                for h in heads:
                    s_out[seq, h] = s[h]
    if chained:
        for h in heads:
            s_scr[h] = s[h]
            s_out[0, h] = s[h]


def _wkv(wkv_in, s0t, tt, n_sub, chunk, chained):
    bsz, t, c6 = wkv_in.shape
    c = c6 // 6
    n_heads = c // HEAD_SIZE
    hs = HEAD_SIZE
    step = tt * n_sub
    if chained:
        grid = (bsz, t // step)
        tok = lambda w: pl.BlockSpec((1, step, w), lambda bi, ti: (bi, ti, 0))
        st = pl.BlockSpec((1, n_heads, hs, hs), lambda bi, ti: (bi, 0, 0, 0))
        y_shape = (bsz, t, c)
    else:
        assert t == chunk and (bsz * t) % step == 0
        n_seq = step // chunk
        grid = (1, bsz * t // step)
        tok = lambda w: pl.BlockSpec((1, step, w), lambda bi, ti: (0, ti, 0))
        st = pl.BlockSpec((n_seq, n_heads, hs, hs), lambda bi, ti: (ti, 0, 0, 0))
        wkv_in = wkv_in.reshape(1, bsz * t, c6)
        y_shape = (1, bsz * t, c)
    y, s_new = pl.pallas_call(
        functools.partial(_wkv_kernel, tt=tt, n_sub=n_sub, chunk=chunk, chained=chained),
        grid=grid,
        in_specs=[tok(c6), st],
        out_specs=[tok(c), st],
        out_shape=[jax.ShapeDtypeStruct(y_shape, F32), jax.ShapeDtypeStruct(s0t.shape, F32)],
        scratch_shapes=[pltpu.VMEM((n_heads, hs, hs), F32)],
        compiler_params=pltpu.CompilerParams(dimension_semantics=("arbitrary",) * 2,
                                             vmem_limit_bytes=VMEM_LIMIT),
        name="wkv",
    )(wkv_in, s0t)
    return y.reshape(bsz, t, c), s_new


def _interleave(make_stream, n_parts):
    for _ in zip(*[make_stream(part) for part in range(n_parts)]):
        pass


def _mix_out_kernel(x_ref, y_ref, aux_ref, lnw_ref, lnb_ref, seg_ref, wout_ref, nffn_ref,
                    wr_ref, h1_out, xn_out, eid_out, eidt_out, gate_out, *, c, n_parts):
    rows_per = x_ref.shape[0] // n_parts

    def stream(part):
        rs = pl.ds(part * rows_per, rows_per)
        seg = seg_ref[...]
        y = y_ref[rs, :]
        inv_n = 1.0 / HEAD_SIZE
        mu = _seg_sum(y, seg) * inv_n
        yield
        yc = y - mu
        var = _seg_sum(yc * yc, seg) * inv_n
        yield
        yn = yc * lax.rsqrt(var + GN_EPS) * lnw_ref[...] + lnb_ref[...]
        ya = (yn + aux_ref[rs, c:2 * c]) * aux_ref[rs, 0:c]
        mix = _dg(ya.astype(BF16), wout_ref[:c, :]) + _dg(aux_ref[rs, 2 * c:3 * c].astype(BF16), wout_ref[c:, :])
        yield
        h1 = x_ref[rs, :] + mix
        h1_out[rs, :] = h1
        xn = _rms(h1, nffn_ref[...])
        xn_out[rs, :] = xn.astype(BF16)
        xh, xl = _split2(xn)
        hi_lo = _dg(xh, wr_ref[...])
        logits = hi_lo[:, :ROUTER_LANES] + (hi_lo[:, ROUTER_LANES:] + _dg(xl, wr_ref[:, :ROUTER_LANES]))
        yield
        lane = lax.broadcasted_iota(I32, logits.shape, 1)
        neg = jnp.float32(-jnp.inf)
        is_g = (lane >= N_EXPERTS) & (lane < N_EXPERTS + N_EXPERT_GROUPS)
        glog = jnp.where(is_g, logits, neg)
        gmax = jnp.max(glog, axis=-1, keepdims=True)
        gsel = jnp.min(jnp.where(glog == gmax, lane, 4 * ROUTER_LANES), axis=-1, keepdims=True) - N_EXPERTS
        gp = 1.0 / jnp.sum(jnp.where(is_g, jnp.exp(glog - gmax), 0.0), axis=-1, keepdims=True)
        in_grp = (lane >= gsel * EXPERTS_PER_GROUP) & (lane < (gsel + 1) * EXPERTS_PER_GROUP)
        elog = jnp.where(in_grp, logits, neg)
        emax = jnp.max(elog, axis=-1, keepdims=True)
        ex = jnp.where(in_grp, jnp.exp(elog - emax), 0.0)
        eprob = ex / jnp.sum(ex, axis=-1, keepdims=True)
        eprob = jnp.where(in_grp, eprob, -1.0)
        yield
        v1 = jnp.max(eprob, axis=-1, keepdims=True)
        i1 = jnp.min(jnp.where(eprob == v1, lane, 4 * ROUTER_LANES), axis=-1, keepdims=True)
        rest = jnp.where(lane == i1, -1.0, eprob)
        v2 = jnp.max(rest, axis=-1, keepdims=True)
        i2 = jnp.min(jnp.where(rest == v2, lane, 4 * ROUTER_LANES), axis=-1, keepdims=True)
        denom = v1 + v2
        eid = jnp.where(lane == 0, i1, jnp.where(lane == 1, i2, 0))
        eid_out[rs, :] = eid
        eidt_out[:, rs] = jnp.transpose(eid)[:SUBLANES]
        gate_out[rs, :] =jnp.where(lane == 0, gp * v1 / denom, jnp.where(lane == 1, gp * v2 / denom, 0.0))
        yield

    _interleave(stream, n_parts)


def _mix_out(x2, y2, aux2, wd, tm, n_parts):
    m, d = x2.shape
    c = y2.shape[1]
    tokd = pl.BlockSpec((tm, d), lambda i: (i, 0))
    tokc = pl.BlockSpec((tm, c), lambda i: (i, 0))
    tokr = pl.BlockSpec((tm, ROUTER_LANES), lambda i: (i, 0))
    weights = [wd[n] for n in ('ln_x_w', 'ln_x_b', 'seg', 'w_out', 'norm_ffn', 'w_router')]
    return pl.pallas_call(
        functools.partial(_mix_out_kernel, c=c, n_parts=n_parts),
        grid=(m // tm,),
        in_specs=[tokd, tokc, pl.BlockSpec((tm, 3 * c), lambda i: (i, 0))] + [_full(w.shape) for w in weights],
        out_specs=[tokd, tokd, tokr, pl.BlockSpec((SUBLANES, tm), lambda i: (0, i)), tokr],
        out_shape=[jax.ShapeDtypeStruct((m, d), F32), jax.ShapeDtypeStruct((m, d), BF16),
                   jax.ShapeDtypeStruct((m, ROUTER_LANES), I32), jax.ShapeDtypeStruct((SUBLANES, m), I32),
                   jax.ShapeDtypeStruct((m, ROUTER_LANES), F32)],
        compiler_params=pltpu.CompilerParams(dimension_semantics=("arbitrary",), vmem_limit_bytes=VMEM_LIMIT),
        name="mix_out",
    )(x2, y2, aux2, *weights)


def _pow2_pieces(count, max_rows, fn):
    off = 0
    rows = max_rows
    while rows >= RUN_ALIGN:
        has = (count & (rows // RUN_ALIGN)) != 0

        @pl.when(has)
        def _(off=off, rows=rows):
            fn(off, rows)
        off = off + jnp.where(has, rows, 0)
        rows //= 2


def _for_each_expert(fn):
    def body(e, carry):
        fn(e)
        return carry
    lax.fori_loop(0, N_EXPERTS, body, 0)


def _pow2_floor(n):
    return 1 << (n.bit_length() - 1)


def _dispatch_kernel(nch_ref, tot_ref, off_ref, loff_ref, tn_ref, toff_ref, nu_ref, *refs, groups, bm, n_blocks):
    n_g = len(groups)
    eid_refs, xn_refs = refs[:n_g], refs[n_g + 1:2 * n_g + 1]
    loffc_ref = refs[n_g]
    xs_out, buf, sem, zbuf, zsem = refs[2 * n_g + 1:]
    i = pl.program_id(0)
    last = pl.num_programs(0) - 1
    max_run = max(tm for tm, _ in groups)

    def start_tile(tile):
        def per_expert(e):
            src0 = loff_ref[tile * N_EXPERTS + e]
            dst0 = off_ref[tile * N_EXPERTS + e]

            def piece(o, rows):
                src = buf.at[tile % 2, pl.ds(pl.multiple_of(src0 + o, RUN_ALIGN), rows)]
                dst = xs_out.at[pl.ds(pl.multiple_of(dst0 + o, RUN_ALIGN), rows)]
                pltpu.make_async_copy(src, dst, sem.at[tile % 2]).start()
            _pow2_pieces(nch_ref[tile * N_EXPERTS + e], max_run, piece)
        _for_each_expert(per_expert)

    def wait_tile(tile):
        def piece(o, rows):
            pltpu.make_async_copy(buf.at[tile % 2, pl.ds(0, rows)], xs_out.at[pl.ds(0, rows)], sem.at[tile % 2]).wait()
        _pow2_pieces(tot_ref[tile], _pow2_floor(buf.shape[1]), piece)

    @pl.when(i >= 2)
    def _():
        wait_tile(i - 2)

    def sort_tile(eid_ref, xn_ref, tm):
        e_rows = eid_ref[...]
        sub = lax.broadcasted_iota(I32, (N_EXPERTS, tm), 0)
        e1 = jnp.where(sub == e_rows[0:1], 1.0, 0.0)
        e2 = jnp.where(sub == e_rows[1:2], 1.0, 0.0)
        before = lax.broadcasted_iota(I32, (tm, tm), 0) < lax.broadcasted_iota(I32, (tm, tm), 1)
        slot = _dg((e1 + e2).astype(BF16), jnp.where(before, 1.0, 0.0).astype(BF16)) + loffc_ref[0]
        l1 = jnp.sum(slot * e1, axis=0, keepdims=True).astype(I32)
        l2 = jnp.sum(slot * e2, axis=0, keepdims=True).astype(I32)
        n_rows = 2 * tm + LOCAL_PAD
        rows = lax.broadcasted_iota(I32, (n_rows, tm), 0)
        perm = jnp.where((rows == l1) | (rows == l2), 1.0, 0.0).astype(BF16)
        buf[i % 2, 0:n_rows, :] = _dg(perm, xn_ref[...]).astype(BF16)

    first = 0
    for g, (tm, n_tiles) in enumerate(groups):
        pl.when((i >= first) & (i < first + n_tiles))(functools.partial(sort_tile, eid_refs[g], xn_refs[g], tm))
        first += n_tiles
    start_tile(i)

    @pl.when(i == 0)
    def _():
        zbuf[...] = jnp.zeros_like(zbuf)
        half = zbuf.shape[0]

        def zero_copy(off, rows):
            return pltpu.make_async_copy(zbuf.at[pl.ds(0, rows)], xs_out.at[pl.ds(off, rows)], zsem)

        def tails(fn):
            _for_each_expert(lambda e: _pow2_pieces(
                tn_ref[e], half, lambda o, rows: fn(pl.multiple_of(toff_ref[e] + o, RUN_ALIGN), rows)))

        def unused_blocks(fn):
            def body(b, carry):
                fn(pl.multiple_of(b * bm, bm), half)
                fn(pl.multiple_of(b * bm + half, half), half)
                return carry
            lax.fori_loop(nu_ref[0], n_blocks, body, 0)

        tails(lambda off, rows: zero_copy(off, rows).start())
        unused_blocks(lambda off, rows: zero_copy(off, rows).start())
        tails(lambda off, rows: zero_copy(off, rows).wait())
        unused_blocks(lambda off, rows: zero_copy(off, rows).wait())

    @pl.when(i == last)
    def _():
        @pl.when(i >= 1)
        def _():
            wait_tile(i - 1)
        wait_tile(i)


def _dispatch(plan, n_used, eids_t, xns, tms, n_blocks, bm):
    d = xns[0].shape[1]
    groups = tuple((tm, xn.shape[0] // tm) for xn, tm in zip(xns, tms))
    firsts = [sum(n for _, n in groups[:g]) for g in range(len(groups))]
    lbuf = 2 * max(tms) + LOCAL_PAD

    def tile_of(g):
        return lambda i: jnp.clip(i - firsts[g], 0, groups[g][1] - 1)

    imap = lambda f: (lambda i, *_: f(i))
    in_specs = ([pl.BlockSpec((SUBLANES, tm), imap(lambda i, g=g: (0, tile_of(g)(i)))) for g, tm in enumerate(tms)] +
                [pl.BlockSpec((1, N_EXPERTS, 1), imap(lambda i: (i, 0, 0)))] +
                [pl.BlockSpec((tm, d), imap(lambda i, g=g: (tile_of(g)(i), 0))) for g, tm in enumerate(tms)])
    return pl.pallas_call(
        functools.partial(_dispatch_kernel, groups=groups, bm=bm, n_blocks=n_blocks),
        grid_spec=pltpu.PrefetchScalarGridSpec(
            num_scalar_prefetch=7,
            grid=(sum(n for _, n in groups),),
            in_specs=in_specs,
            out_specs=pl.BlockSpec(memory_space=pl.ANY),
            scratch_shapes=[pltpu.VMEM((2, lbuf, d), BF16), pltpu.SemaphoreType.DMA((2,)),
                            pltpu.VMEM((bm // 2, d), BF16), pltpu.SemaphoreType.DMA],
        ),
        out_shape=jax.ShapeDtypeStruct((n_blocks * bm, d), BF16),
        compiler_params=pltpu.CompilerParams(dimension_semantics=("arbitrary",), vmem_limit_bytes=VMEM_LIMIT),
        name="moe_dispatch",
    )(plan['nch'], plan['tot'], plan['off'], plan['loff'], plan['tail_n'], plan['tail_off'], n_used, *eids_t,
      plan['loff_col'], *xns)


def _experts_kernel(be_ref, slot_ref, nxt_ref, nu_ref, xs_ref, wg_hbm, wu_hbm, wd_hbm, yb_ref,
                    wg_f, wu_f, wd_f, wg_b, wu_b, wd_b, sem):
    b = pl.program_id(0)

    def weight_copies(e, slot):
        pairs = ((wg_hbm, wg_f), (wu_hbm, wu_f), (wd_hbm, wd_f))
        return [pltpu.make_async_copy(src.at[e], dst.at[slot], sem.at[slot, j]) for j, (src, dst) in enumerate(pairs)]

    @pl.when(b == 0)
    def _():
        for cp in weight_copies(be_ref[0], slot_ref[0]):
            cp.start()

    def gated_mlp(wg, wu, wd):
        xb = xs_ref[...]
        hg = _dg(xb, wg)
        hu = _dg(xb, wu)
        act = (hg * _sigmoid(hg) * hu).astype(BF16)
        yb_ref[...] = _dg(act, wd).astype(BF16)

    active = b < nu_ref[0]
    first = (b == 0) | (be_ref[b] != be_ref[jnp.maximum(b - 1, 0)])

    @pl.when(active & first)
    def _():
        slot = slot_ref[b]
        for cp in weight_copies(be_ref[b], slot):
            cp.wait()

        @pl.when(nxt_ref[b] >= 0)
        def _():
            for cp in weight_copies(nxt_ref[b], 1 - slot):
                cp.start()
        wg, wu, wd = wg_f[slot].astype(BF16), wu_f[slot].astype(BF16), wd_f[slot].astype(BF16)
        wg_b[...] = wg
        wu_b[...] = wu
        wd_b[...] = wd
        gated_mlp(wg, wu, wd)

    @pl.when(active & jnp.logical_not(first))
    def _():
        gated_mlp(wg_b[...], wu_b[...], wd_b[...])

    @pl.when(pl.program_id(0) >= nu_ref[0])
    def _():
        yb_ref[...] = jnp.zeros_like(yb_ref)


def _experts(sched, n_used, xs, wg, wu, wdn, bm):
    p, d = xs.shape
    ff = wg.shape[2]
    n_blocks = p // bm
    return pl.pallas_call(
        _experts_kernel,
        grid_spec=pltpu.PrefetchScalarGridSpec(
            num_scalar_prefetch=4,
            grid=(n_blocks,),
            in_specs=[pl.BlockSpec((bm, d), lambda b, be, sl, nx, nu: (jnp.minimum(b, nu[0] - 1), 0)),
                      pl.BlockSpec(memory_space=pl.ANY), pl.BlockSpec(memory_space=pl.ANY),
                      pl.BlockSpec(memory_space=pl.ANY)],
            out_specs=pl.BlockSpec((bm, d), lambda b, *_: (b, 0)),
            scratch_shapes=[pltpu.VMEM((2, d, ff), F32), pltpu.VMEM((2, d, ff), F32), pltpu.VMEM((2, ff, d), F32),
                            pltpu.VMEM((d, ff), BF16), pltpu.VMEM((d, ff), BF16), pltpu.VMEM((ff, d), BF16),
                            pltpu.SemaphoreType.DMA((2, 3))],
        ),
        out_shape=jax.ShapeDtypeStruct((p, d), BF16),
        compiler_params=pltpu.CompilerParams(dimension_semantics=("arbitrary",), vmem_limit_bytes=VMEM_LIMIT),
        name="moe_experts",
    )(sched['expert'], sched['slot'], sched['next'], n_used, xs, wg, wu, wdn)


def _final_kernel(nch_ref, tot_ref, off_ref, loff_ref, h1_ref, eid_ref, gate_ref, loffr_ref, p_ref, yb_hbm,
                  nple_ref, wpg_ref, wpp_ref, nfin_ref, y_out, buf, sem, *, tm, lbuf, tile0, n_parts):
    i = pl.program_id(0)

    def fetch(tile):
        base = (tile0 + tile) * N_EXPERTS

        def per_expert(e):
            src0 = off_ref[base + e]
            dst0 = loff_ref[base + e]

            def piece(o, rows):
                src = yb_hbm.at[pl.ds(pl.multiple_of(src0 + o, RUN_ALIGN), rows)]
                dst = buf.at[tile % 2, pl.ds(pl.multiple_of(dst0 + o, RUN_ALIGN), rows)]
                pltpu.make_async_copy(src, dst, sem.at[tile % 2]).start()
            _pow2_pieces(nch_ref[base + e], tm, piece)
        _for_each_expert(per_expert)

    def wait_fetch(tile):
        def piece(o, rows):
            pltpu.make_async_copy(yb_hbm.at[pl.ds(0, rows)], buf.at[tile % 2, pl.ds(0, rows)], sem.at[tile % 2]).wait()
        _pow2_pieces(tot_ref[tile0 + tile], _pow2_floor(lbuf), piece)

    @pl.when(i == 0)
    def _():
        buf[...] = jnp.zeros_like(buf)
        fetch(i)

    @pl.when(i + 1 < pl.num_programs(0))
    def _():
        fetch(i + 1)

    eid = eid_ref[...]
    lane = lax.broadcasted_iota(I32, (tm, N_EXPERTS), 1)
    e12 = (jnp.where(lane == eid[:, 0:1], 1.0, 0.0) + jnp.where(lane == eid[:, 1:2], 1.0, 0.0)).astype(BF16)
    rows_per = tm // n_parts
    picks = []
    for part in range(n_parts):
        rs = pl.ds(part * rows_per, rows_per)
        lane_p = lax.broadcasted_iota(I32, (rows_per, N_EXPERTS), 1)
        eid_p = eid_ref[rs, :]
        e1 = jnp.where(lane_p == eid_p[:, 0:1], 1.0, 0.0)
        e2 = jnp.where(lane_p == eid_p[:, 1:2], 1.0, 0.0)
        before = (lax.broadcasted_iota(I32, (rows_per, tm), 1)
                  < lax.broadcasted_iota(I32, (rows_per, tm), 0) + part * rows_per)
        slot = _dg(jnp.where(before, 1.0, 0.0).astype(BF16), e12) + loffr_ref[0]
        l1 = jnp.sum(slot * e1, axis=1, keepdims=True).astype(I32)
        l2 = jnp.sum(slot * e2, axis=1, keepdims=True).astype(I32)
        cols = lax.broadcasted_iota(I32, (rows_per, lbuf), 1)
        gate = gate_ref[rs, :]
        picks.append(jnp.where(cols == l1, gate[:, 0:1], jnp.where(cols == l2, gate[:, 1:2], 0.0)).astype(BF16))

    wait_fetch(i)
    sorted_rows = buf[i % 2]

    def stream(part):
        rs = pl.ds(part * rows_per, rows_per)
        h2 = h1_ref[rs, :] + _dg(picks[part], sorted_rows)
        yield
        gate_in = _rms(h2, nple_ref[...]).astype(BF16)
        pg = _sigmoid(_dg(gate_in, wpg_ref[...]))
        yield
        h3 = h2 + pg * _dg(p_ref[rs, :].astype(BF16), wpp_ref[...])
        y_out[rs, :] = _rms(h3, nfin_ref[...])
        yield

    _interleave(stream, n_parts)


def _final(plan, h1, eid, gate, p2, yb, wd, tm, tile0):
    m, d = h1.shape
    pd = p2.shape[1]
    lbuf = 2 * tm + LOCAL_PAD
    weights = [wd[n] for n in ('norm_ple', 'w_ple_gate', 'w_ple_proj', 'norm_final')]
    imap = lambda f: (lambda i, *_: f(i))
    return pl.pallas_call(
        functools.partial(_final_kernel, tm=tm, lbuf=lbuf, tile0=tile0, n_parts=TOKEN_PARTS),
        grid_spec=pltpu.PrefetchScalarGridSpec(
            num_scalar_prefetch=4,
            grid=(m // tm,),
            in_specs=[pl.BlockSpec((tm, d), imap(lambda i: (i, 0))),
                      pl.BlockSpec((tm, ROUTER_LANES), imap(lambda i: (i, 0))),
                      pl.BlockSpec((tm, ROUTER_LANES), imap(lambda i: (i, 0))),
                      pl.BlockSpec((1, 1, N_EXPERTS), imap(lambda i: (tile0 + i, 0, 0))),
                      pl.BlockSpec((tm, pd), imap(lambda i: (i, 0))),
                      pl.BlockSpec(memory_space=pl.ANY)] +
                     [pl.BlockSpec(w.shape, imap(lambda i, n=len(w.shape): (0,) * n)) for w in weights],
            out_specs=pl.BlockSpec((tm, d), imap(lambda i: (i, 0))),
            scratch_shapes=[pltpu.VMEM((2, lbuf, d), BF16), pltpu.SemaphoreType.DMA((2,))],
        ),
        out_shape=jax.ShapeDtypeStruct((m, d), F32),
        compiler_params=pltpu.CompilerParams(dimension_semantics=("arbitrary",), vmem_limit_bytes=VMEM_LIMIT),
        name="moe_final",
    )(plan['nch'], plan['tot'], plan['off'], plan['loff'], h1, eid, gate, plan['loff_row'], p2, yb, *weights)


def _route_plan(eids, tms, bm):
    experts = jnp.arange(N_EXPERTS, dtype=I32)
    counts = []
    for eid, tm in zip(eids, tms):
        onehot = (eid[:2, :, None] == experts).astype(I32)
        counts.append(onehot.reshape(2, -1, tm, N_EXPERTS).sum(axis=(0, 2)))
    n = jnp.concatenate(counts)
    n_al = (n + RUN_ALIGN - 1) // RUN_ALIGN * RUN_ALIGN
    loff = jnp.cumsum(n_al, axis=1) - n_al
    used = n_al.sum(axis=0)
    region = (used + bm - 1) // bm * bm
    pend = jnp.cumsum(region)
    off = (pend - region)[None, :] + jnp.cumsum(n_al, axis=0) - n_al
    n_assign = sum(2 * e.shape[1] for e in eids)
    n_blocks = -(-(n_assign + (RUN_ALIGN - 1) * N_EXPERTS * n.shape[0] + N_EXPERTS * (bm - 1)) // bm)
    block_start = jnp.arange(n_blocks, dtype=I32) * bm
    block_expert = jnp.minimum(jnp.sum((pend[None, :] <= block_start[:, None]).astype(I32), axis=1), N_EXPERTS - 1)
    plan = dict(nch=(n_al // RUN_ALIGN).reshape(-1).astype(I32), off=off.reshape(-1).astype(I32),
                tot=(n_al.sum(axis=1) // RUN_ALIGN).astype(I32),
                loff=loff.reshape(-1).astype(I32), tail_n=((region - used) // RUN_ALIGN).astype(I32),
                tail_off=(pend - region + used).astype(I32), loff_col=loff.astype(F32)[:, :, None],
                loff_row=loff.astype(F32)[:, None, :])
    n_used = pend[-1] // bm
    prev = jnp.concatenate([jnp.full((1,), -1, I32), block_expert[:-1].astype(I32)])
    slot = (jnp.cumsum((block_expert != prev).astype(I32)) - 1) % 2
    after = pend[block_expert] // bm
    nxt = jnp.where(after < n_used, block_expert[jnp.minimum(after, n_blocks - 1)], -1)
    sched = dict(expert=block_expert.astype(I32), slot=slot.astype(I32), next=nxt.astype(I32))
    return plan, sched, n_used.astype(I32).reshape(1), n_blocks


def _tile_plan(bsz, t, long_sequence):
    rows = bsz * t
    span = t if long_sequence else rows
    wkv_sub = min(WKV_SUBTILE, span)
    return dict(
        mix_in=min(TOKEN_TILE, t), mix_in_parts=MIX_IN_PARTS if (long_sequence and t >= TOKEN_TILE) else 1,
        wkv_sub=wkv_sub, wkv_n_sub=min(WKV_SUBTILES, span // wkv_sub),
        wkv_chunk=min(WKV_CHUNK, t),
        token=min(2 * TOKEN_TILE, rows), token_parts=2 * TOKEN_PARTS if rows >= 2 * TOKEN_TILE else TOKEN_PARTS)


def _layer_front(x, shift0, wkv0, conv0, wd, tiles, long_sequence):
    bsz, t, d = x.shape
    wkv_in, aux, shift_new, conv_new = _mix_in(
        x, shift0, conv0, wd, tiles['mix_in'], tiles['mix_in_parts'], not long_sequence)
    y, s_new = _wkv(wkv_in, jnp.swapaxes(wkv0, -1, -2), tiles['wkv_sub'], tiles['wkv_n_sub'], tiles['wkv_chunk'],
                    long_sequence)
    flat = lambda z: z.reshape(bsz * t, z.shape[-1])
    h1, xn2, eid, eid_t, gate = _mix_out(flat(x), flat(y), flat(aux), wd, tiles['token'], tiles['token_parts'])
    return h1, xn2, (eid, eid_t), gate, shift_new.reshape(bsz, d), jnp.swapaxes(s_new, -1, -2), conv_new


def kernel(x_prompt, x_sample, state_shift, state_wkv, cache_conv, p_prompt, p_sample, norm_mix, w_in, mu_rkv, mu_w, mu_a, mu_g, w0, w1, w2, a0, a1, a2, g1, g2, k_k, k_a, r_k, ln_x_w, ln_x_b, dw_w, dw_b, cln_w, cln_b, w_out, norm_ffn, w_router_group, w_router_expert, w_exp_gate, w_exp_up, w_exp_down, norm_ple, w_ple_gate, w_ple_proj, norm_final):
    depth = norm_mix.shape[0]
    assert depth == 1
    d = x_prompt.shape[-1]
    c = w0.shape[-1]
    row = lambda z: z[0].reshape(1, -1).astype(F32)
    lane = jnp.arange(LANES, dtype=I32) // HEAD_SIZE
    w_router = jnp.concatenate([w_router_expert[0], w_router_group[0],
                                jnp.zeros((d, ROUTER_LANES - N_EXPERTS - N_EXPERT_GROUPS), F32)], axis=1)
    first = jnp.concatenate([w1[0], a1[0], g1[0]], axis=1)
    mixed = jnp.concatenate([mu_w[0][:, None] * w1[0], mu_a[0][:, None] * a1[0], mu_g[0][:, None] * g1[0]], axis=1)
    d_w, d_a = w1.shape[2], a1.shape[2]
    second = jnp.concatenate([jnp.concatenate([w2[0], jnp.zeros((d_w, c), F32)], axis=1),
                              jnp.concatenate([jnp.zeros((d_a, c), F32), a2[0]], axis=1)], axis=0)
    glu_half = jnp.concatenate([jnp.ones((3 * c,), F32), jnp.full((w_in.shape[2] - 3 * c,), 0.5, F32)])
    wd = dict(
        norm_mix=row(norm_mix), w_in=(w_in[0] * glu_half).astype(BF16), mu_rkv=row(mu_rkv),
        w0=0.5 * row(w0), a0=0.5 * row(a0),
        w_lora1=jnp.concatenate([first, mixed], axis=0).astype(BF16), w_lora2=(0.5 * second).astype(BF16),
        g2=g2[0].astype(BF16), lora_dims=(d_w, d_a),
        k_k=row(k_k), k_a=row(k_a), r_k=row(r_k), ln_x_w=row(ln_x_w), ln_x_b=row(ln_x_b),
        seg=(lane[:, None] == lane[None, :]).astype(BF16),
        dw_w=dw_w[0].astype(F32), dw_b=row(dw_b), cln_w=0.5 * row(cln_w), cln_b=0.5 * row(cln_b),
        w_out=w_out[0].astype(BF16), norm_ffn=row(norm_ffn),
        w_router=jnp.concatenate(_split2(w_router), axis=1),
        norm_ple=row(norm_ple), w_ple_gate=w_ple_gate[0].astype(BF16), w_ple_proj=w_ple_proj[0].astype(BF16),
        norm_final=norm_final.reshape(1, -1).astype(F32),
    )
    bp, tp, _ = x_prompt.shape
    bs, ts, _ = x_sample.shape
    mp, ms = bp * tp, bs * ts

    zeros = lambda *s: jnp.zeros(s, F32)
    h1_p, xn_p, eid_p, gate_p, shift_p, wkv_p, conv_p = _layer_front(
        x_prompt, zeros(bp, 1, d), zeros(bp, c // HEAD_SIZE, HEAD_SIZE, HEAD_SIZE), zeros(bp, CONV_CARRY, c),
        wd, _tile_plan(bp, tp, True), True)
    h1_s, xn_s, eid_s, gate_s, shift_s, wkv_s, conv_s = _layer_front(
        x_sample, state_shift[0][:, None, :], state_wkv[0], cache_conv[0],
        wd, _tile_plan(bs, ts, False), False)

    bm = EXPERT_BLOCK
    tr_p, tr_s = min(TOKEN_TILE, mp), min(TOKEN_TILE, ms)
    plan, sched, n_used, n_blocks = _route_plan([eid_p[1], eid_s[1]], [tr_p, tr_s], bm)
    tiles_p = mp // tr_p
    xs = _dispatch(plan, n_used, [eid_p[1], eid_s[1]], [xn_p, xn_s], [tr_p, tr_s], n_blocks, bm)
    yb = _experts(sched, n_used, xs, w_exp_gate[0], w_exp_up[0], w_exp_down[0], bm)
    y_p = _final(plan, h1_p, eid_p[0], gate_p, p_prompt[0].reshape(mp, -1), yb, wd, tr_p, 0)
    y_s = _final(plan, h1_s, eid_s[0], gate_s, p_sample[0].reshape(ms, -1), yb, wd, tr_s, tiles_p)
    return (y_p.reshape(x_prompt.shape), y_s.reshape(x_sample.shape), shift_p[None], wkv_p[None], conv_p[None],
            shift_s[None], wkv_s[None], conv_s[None])
```
